```python
import math
import jax
import jax.numpy as jnp
from jax import lax
import numpy as np

D_MODEL = 1024
BATCH = 2
SEQ = 16384
DEPTH = 2

HEAD_DIM = 64
BRANCH_WIDTH = 256
N_BRANCHES = 4
A_HEADS = BRANCH_WIDTH // HEAD_DIM
DILATED_PATTERNS = ((128, 1), (512, 4), (2048, 16))
ROPE_THETA = 500000.0
ROPE_DIMS = HEAD_DIM // 4
RWKV_HEADS = BRANCH_WIDTH // HEAD_DIM
RWKV_DECAY_LORA = 64
RWKV_ICLR_LORA = 64
RWKV_GATE_LORA = 128
RWKV_GN_EPS = 64e-5
S5_GROUP_CH = 16
S5_GROUPS = BRANCH_WIDTH // S5_GROUP_CH
S5_STATE = 64
GQA_Q_HEADS = BRANCH_WIDTH // HEAD_DIM
GQA_KV_HEADS = 2
GQA_Q_BLOCK = 128
AXIAL_THETA = 10000.0
GRID_W = 64
DENSE_FF = 2816
N_EXPERTS = 8
TOP_K = 2
EXPERT_FF = 3584
MOE_BLOCK = 512
NORM_EPS = 1e-6
NEG_INF = -1e30
IN_A = 3 * BRANCH_WIDTH
IN_B = 3 * BRANCH_WIDTH
IN_C = BRANCH_WIDTH
IN_DQ = GQA_Q_HEADS * HEAD_DIM
IN_DKV = GQA_KV_HEADS * HEAD_DIM
IN_GATES = N_BRANCHES * D_MODEL
IN_TOTAL = IN_A + IN_B + IN_C + IN_DQ + 2 * IN_DKV + IN_GATES
N_DENSE_LAYERS = (DEPTH + 1) // 2
N_MOE_LAYERS = DEPTH // 2

kernel_name = 'hybrid_gated_dilated_rwkv7_s5_gqa_moe'


def rms_norm(x, g, eps=NORM_EPS):
    xf = x.astype(jnp.float32)
    y = xf * lax.rsqrt(jnp.mean(xf * xf, axis=-1, keepdims=True) + eps)
    return (y * g.astype(jnp.float32)).astype(x.dtype)


def rope_angles(pos, n_freq, theta):
    inv_freq = theta ** (-jnp.arange(n_freq, dtype=jnp.float32) / n_freq)
    return pos.astype(jnp.float32)[:, None] * inv_freq[None, :]


def apply_rotary(x, ang):
    n = ang.shape[-1]
    cos = jnp.cos(ang)[None, :, None, :].astype(x.dtype)
    sin = jnp.sin(ang)[None, :, None, :].astype(x.dtype)
    x1, x2, rest = x[..., :n], x[..., n:2 * n], x[..., 2 * n:]
    return jnp.concatenate([x1 * cos - x2 * sin, x2 * cos + x1 * sin, rest], axis=-1)


def shift_prev(x):
    pad = [(0, 0), (1, 0)] + [(0, 0)] * (x.ndim - 2)
    return jnp.pad(x[:, :-1], pad)


def shift_next(x):
    pad = [(0, 0), (0, 1)] + [(0, 0)] * (x.ndim - 2)
    return jnp.pad(x[:, 1:], pad)


def dilated_window_attention(q, k, v, dilation, radius):
    B, S, H, hd = q.shape
    L = S // dilation
    BD = B * dilation
    blk = radius
    nb = -(-L // blk)
    Lp = nb * blk

    def to_sub(t):
        return t.reshape(B, L, dilation, H, hd).swapaxes(1, 2).reshape(BD, L, H, hd)

    qs = jnp.pad(to_sub(q), ((0, 0), (0, Lp - L), (0, 0), (0, 0))).reshape(BD, nb, blk, H, hd)
    kv_pad = ((0, 0), (blk, Lp - L + blk), (0, 0), (0, 0))
    ks = jnp.pad(to_sub(k), kv_pad).reshape(BD, nb + 2, blk, H, hd)
    vs = jnp.pad(to_sub(v), kv_pad).reshape(BD, nb + 2, blk, H, hd)
    kb = jnp.concatenate([ks[:, :-2], ks[:, 1:-1], ks[:, 2:]], axis=2)
    vb = jnp.concatenate([vs[:, :-2], vs[:, 1:-1], vs[:, 2:]], axis=2)

    qpos = jnp.arange(Lp, dtype=jnp.int32).reshape(nb, blk)
    kpos = qpos[:, :1] - blk + jnp.arange(3 * blk, dtype=jnp.int32)[None, :]
    mask = ((jnp.abs(qpos[:, :, None] - kpos[:, None, :]) <= radius)
            & (kpos[:, None, :] >= 0) & (kpos[:, None, :] < L))

    s = jnp.einsum('znqhd,znkhd->znhqk', qs, kb, preferred_element_type=jnp.float32)
    s = jnp.where(mask[None, :, None], s, NEG_INF)
    m = jnp.max(s, axis=-1, keepdims=True)
    p = jnp.exp(s - m)
    l = jnp.sum(p, axis=-1, keepdims=True)
    o = jnp.einsum('znhqk,znkhd->znqhd', (p / l).astype(v.dtype), vb)
    lse = (m + jnp.log(l))[..., 0].swapaxes(2, 3)

    def from_sub(t):
        t = t.reshape((BD, Lp) + t.shape[3:])[:, :L]
        t = t.reshape((B, dilation, L) + t.shape[2:]).swapaxes(1, 2)
        return t.reshape((B, S) + t.shape[3:])

    return from_sub(o), from_sub(lse)


def dilated_attention_mixer(qkv, ang):
    B, S, _ = qkv.shape
    qkv = qkv.reshape(B, S, 3, A_HEADS, HEAD_DIM)
    q = apply_rotary(qkv[:, :, 0], ang) * (HEAD_DIM ** -0.5)
    k = apply_rotary(qkv[:, :, 1], ang)
    v = qkv[:, :, 2]
    outs, lses = [], []
    for window, dilation in DILATED_PATTERNS:
        o, lse = dilated_window_attention(q, k, v, dilation, window // (2 * dilation))
        outs.append(o)
        lses.append(lse)
    wts = jax.nn.softmax(jnp.stack(lses), axis=0)
    o = jnp.einsum('gbsh,gbshd->bshd', wts.astype(v.dtype), jnp.stack(outs))
    return o.reshape(B, S, A_HEADS * HEAD_DIM)


def wkv7_scan(r, decay, k, v, kappa, iclr, reverse):
    B, S, H, N = r.shape
    xs = tuple(jnp.moveaxis(t.astype(jnp.float32), 1, 0)
               for t in (r, decay, k, v, kappa, iclr * kappa))

    def step(state, inp):
        r_t, w_t, k_t, v_t, kap_t, b_t = inp
        s_kap = jnp.einsum('bhvk,bhk->bhv', state, kap_t)
        state = (state * w_t[:, :, None, :]
                 - s_kap[..., None] * b_t[:, :, None, :]
                 + v_t[..., None] * k_t[:, :, None, :])
        return state, jnp.einsum('bhvk,bhk->bhv', state, r_t)

    _, y = lax.scan(step, jnp.zeros((B, H, N, N), jnp.float32), xs, reverse=reverse)
    return jnp.moveaxis(y, 0, 1)


def rwkv7_mixer(rkv, xn, mu_rkv, mu_x, w0, w1, w2, a0, a1, a2, g1, g2, k_k, k_a, r_k, ln_w, ln_b):
    B, S, _ = rkv.shape
    H, N, W = RWKV_HEADS, HEAD_DIM, BRANCH_WIDTH

    def heads(t):
        return t.reshape(B, S, H, N)

    rkv = rkv.reshape(B, S, 3, W)
    rkv = rkv + mu_rkv[0] * (shift_prev(rkv) - rkv) + mu_rkv[1] * (shift_next(rkv) - rkv)
    r, k, v = rkv[:, :, 0], rkv[:, :, 1], rkv[:, :, 2]
    kap = heads(k * k_k).astype(jnp.float32)
    kap = kap * lax.rsqrt(jnp.sum(kap * kap, axis=-1, keepdims=True) + 1e-12)
    y_state = jnp.zeros((B, S, H, N), jnp.float32)
    bonus = jnp.zeros((B, S, H, N), jnp.float32)
    for d, (x_shift, reverse) in enumerate(((shift_prev(xn), False), (shift_next(xn), True))):
        xd = xn + mu_x[d] * (x_shift - xn)
        w_log = -jax.nn.softplus(-(w0[d] + jnp.tanh(xd @ w1[d]) @ w2[d])) - 0.5
        decay = jnp.exp(-jnp.exp(w_log.astype(jnp.float32)))
        iclr = jax.nn.sigmoid(a0[d] + (xd @ a1[d]) @ a2[d])
        k_d = k * (1.0 + (iclr - 1.0) * k_a)
        y_state = y_state + wkv7_scan(heads(r), heads(decay), heads(k_d), heads(v),
                                      kap, heads(iclr), reverse)
        bonus = bonus + (jnp.sum(heads(r * k_d * r_k), axis=-1, keepdims=True).astype(jnp.float32)
                         * heads(v).astype(jnp.float32))
    mu = jnp.mean(y_state, axis=-1, keepdims=True)
    var = jnp.mean(jnp.square(y_state - mu), axis=-1, keepdims=True)
    y = ((y_state - mu) * lax.rsqrt(var + RWKV_GN_EPS)).reshape(B, S, W) * ln_w + ln_b
    y = y + bonus.reshape(B, S, W)
    g = jax.nn.sigmoid(xn @ g1) @ g2
    return (y * g).astype(xn.dtype)


def _complex_affine_combine(e1, e2):
    a1r, a1i, b1r, b1i = e1
    a2r, a2i, b2r, b2i = e2
    return (a2r * a1r - a2i * a1i, a2r * a1i + a2i * a1r,
            a2r * b1r - a2i * b1i + b2r, a2r * b1i + a2i * b1r + b2i)


def s5_mixer(u, a_re, a_im, log_dt, b_re, b_im, c_re, c_im, d_skip, glu_w, glu_b):
    B, S, W = u.shape
    G, P, C = S5_GROUPS, S5_STATE, S5_GROUP_CH
    ug = u.reshape(B, S, G, C).astype(jnp.float32)
    y = jnp.zeros((B, S, G, C), jnp.float32)
    for d in range(2):
        lr = a_re[d].astype(jnp.float32)
        li = a_im[d].astype(jnp.float32)
        dt = jnp.exp(log_dt[d].astype(jnp.float32))[:, None]
        mag = jnp.exp(lr * dt)
        bar_re, bar_im = mag * jnp.cos(li * dt), mag * jnp.sin(li * dt)
        den = lr * lr + li * li
        f_re = ((bar_re - 1.0) * lr + bar_im * li) / den
        f_im = (bar_im * lr - (bar_re - 1.0) * li) / den
        bb_re = f_re[..., None] * b_re - f_im[..., None] * b_im
        bb_im = f_re[..., None] * b_im + f_im[..., None] * b_re
        bu_re = jnp.einsum('bsgc,gpc->bsgp', ug, bb_re)
        bu_im = jnp.einsum('bsgc,gpc->bsgp', ug, bb_im)
        lam_re = jnp.broadcast_to(bar_re, (1, S, G, P))
        lam_im = jnp.broadcast_to(bar_im, (1, S, G, P))
        _, _, x_re, x_im = lax.associative_scan(
            _complex_affine_combine, (lam_re, lam_im, bu_re, bu_im), reverse=(d == 1), axis=1)
        y = (y + jnp.einsum('bsgp,gcp->bsgc', x_re, c_re[d])
             - jnp.einsum('bsgp,gcp->bsgc', x_im, c_im[d]))
    y = y.reshape(B, S, W) + d_skip * u
    z = jax.nn.gelu(y)
    h = z @ glu_w + glu_b
    return (h[..., :W] * jax.nn.sigmoid(h[..., W:])).astype(u.dtype)


def gqa_axial_mixer(q, k, v, q_norm, k_norm, ang):
    B, S, _ = q.shape
    rep = GQA_Q_HEADS // GQA_KV_HEADS
    q = apply_rotary(rms_norm(q.reshape(B, S, GQA_Q_HEADS, HEAD_DIM), q_norm), ang) * (HEAD_DIM ** -0.5)
    k = apply_rotary(rms_norm(k.reshape(B, S, GQA_KV_HEADS, HEAD_DIM), k_norm), ang)
    v = v.reshape(B, S, GQA_KV_HEADS, HEAD_DIM)
    n_blk = S // GQA_Q_BLOCK
    qb = q.reshape(B, n_blk, GQA_Q_BLOCK, GQA_KV_HEADS, rep, HEAD_DIM).swapaxes(0, 1)

    def attend(q_blk):
        s = jnp.einsum('bqgrd,bkgd->bgrqk', q_blk, k, preferred_element_type=jnp.float32)
        p = jax.nn.softmax(s, axis=-1).astype(v.dtype)
        return jnp.einsum('bgrqk,bkgd->bqgrd', p, v)

    o = lax.map(attend, qb).swapaxes(0, 1)
    return o.reshape(B, S, GQA_Q_HEADS * HEAD_DIM)


def hybrid_mixer(xn, w_in, gate_b, w_branch, w_out, ang_rope, ang_axial,
                 rwkv_mu_rkv, rwkv_mu_x, rwkv_w0, rwkv_w1, rwkv_w2, rwkv_a0, rwkv_a1, rwkv_a2,
                 rwkv_g1, rwkv_g2, rwkv_k_k, rwkv_k_a, rwkv_r_k, rwkv_ln_w, rwkv_ln_b,
                 s5_a_re, s5_a_im, s5_log_dt, s5_b_re, s5_b_im, s5_c_re, s5_c_im, s5_d,
                 s5_glu_w, s5_glu_b, gqa_q_norm, gqa_k_norm):
    B, S, _ = xn.shape
    z = xn @ w_in
    bounds = [int(b) for b in np.cumsum([IN_A, IN_B, IN_C, IN_DQ, IN_DKV, IN_DKV])]
    za, zb, zc, zq, zk, zv, zg = jnp.split(z, bounds, axis=-1)
    ya = dilated_attention_mixer(za, ang_rope)
    yb = rwkv7_mixer(zb, xn, rwkv_mu_rkv, rwkv_mu_x, rwkv_w0, rwkv_w1, rwkv_w2, rwkv_a0, rwkv_a1,
                     rwkv_a2, rwkv_g1, rwkv_g2, rwkv_k_k, rwkv_k_a, rwkv_r_k, rwkv_ln_w, rwkv_ln_b)
    yc = s5_mixer(zc, s5_a_re, s5_a_im, s5_log_dt, s5_b_re, s5_b_im, s5_c_re, s5_c_im, s5_d,
                  s5_glu_w, s5_glu_b)
    yd = gqa_axial_mixer(zq, zk, zv, gqa_q_norm, gqa_k_norm, ang_axial)
    gates = jax.nn.sigmoid(zg.reshape(B, S, N_BRANCHES, D_MODEL) + gate_b)
    branches = (ya, yb, yc, yd)
    merged = sum(gates[:, :, i] * (y_i @ w_branch[i]) for i, y_i in enumerate(branches))
    return merged @ w_out


def swiglu(x, w_gate, w_up, w_down):
    return (jax.nn.silu(x @ w_gate) * (x @ w_up)) @ w_down


def moe_swiglu(x, router, w_gate, w_up, w_down):
    B, S, D = x.shape
    xt = x.reshape(B * S, D)
    N = xt.shape[0]
    NK = N * TOP_K
    logits = jnp.einsum('nd,de->ne', xt, router, preferred_element_type=jnp.float32)
    top_logit, top_e = lax.top_k(logits, TOP_K)
    top_w = jax.nn.softmax(top_logit, axis=-1)
    flat_e = top_e.reshape(NK)
    flat_tok = jnp.arange(NK, dtype=jnp.int32) // TOP_K
    flat_w = top_w.reshape(NK)
    order = jnp.argsort(flat_e)
    se = flat_e[order]
    counts = jnp.zeros((N_EXPERTS,), jnp.int32).at[flat_e].add(1)
    padded = (counts + MOE_BLOCK - 1) // MOE_BLOCK * MOE_BLOCK
    pad_end = jnp.cumsum(padded)
    pad_start = pad_end - padded
    start = jnp.cumsum(counts) - counts
    dest = pad_start[se] + jnp.arange(NK, dtype=jnp.int32) - start[se]
    n_blocks = -(-NK // MOE_BLOCK) + N_EXPERTS
    rows = n_blocks * MOE_BLOCK
    row_tok = jnp.full((rows,), N, jnp.int32).at[dest].set(flat_tok[order])
    row_w = jnp.zeros((rows,), jnp.float32).at[dest].set(flat_w[order])
    blk_e = jnp.minimum(jnp.searchsorted(pad_end, jnp.arange(n_blocks, dtype=jnp.int32) * MOE_BLOCK,
                                         side='right'), N_EXPERTS - 1)
    x_pad = jnp.concatenate([xt, jnp.zeros((1, D), xt.dtype)], axis=0)
    xb = x_pad[row_tok].reshape(n_blocks, MOE_BLOCK, D)

    def expert_block(args):
        x_blk, e = args
        return (jax.nn.silu(x_blk @ w_gate[e]) * (x_blk @ w_up[e])) @ w_down[e]

    yb = lax.map(expert_block, (xb, blk_e)).reshape(rows, D)
    y = jnp.zeros((N + 1, D), yb.dtype).at[row_tok].add(yb * row_w[:, None].astype(yb.dtype))
    return y[:N].reshape(B, S, D)


def setup_inputs(seed: int = 0) -> dict:
    key = jax.random.key(seed)
    keys = iter(jax.random.split(key, 64))

    def normal(shape, scale):
        return scale * jax.random.normal(next(keys), shape, jnp.float32)

    def uniform(shape, lo, hi):
        return jax.random.uniform(next(keys), shape, jnp.float32, lo, hi)

    L, D, W = DEPTH, D_MODEL, BRANCH_WIDTH
    G, P, C = S5_GROUPS, S5_STATE, S5_GROUP_CH
    s5_n = jnp.arange(P, dtype=jnp.float32)
    return {
        'x': normal((BATCH, SEQ, D), 1.0),
        'norm_mix_g': 1.0 + normal((L, D), 0.02),
        'w_in': normal((L, D, IN_TOTAL), D ** -0.5),
        'gate_b': normal((L, N_BRANCHES, D), 0.1),
        'w_branch': normal((L, N_BRANCHES, W, D), W ** -0.5),
        'w_out': normal((L, D, D), D ** -0.5),
        'rwkv_mu_rkv': uniform((L, 2, 3, W), 0.0, 0.5),
        'rwkv_mu_x': uniform((L, 2, D), 0.0, 0.5),
        'rwkv_w0': uniform((L, 2, W), -3.0, 1.0),
        'rwkv_w1': normal((L, 2, D, RWKV_DECAY_LORA), D ** -0.5),
        'rwkv_w2': normal((L, 2, RWKV_DECAY_LORA, W), 0.5 * RWKV_DECAY_LORA ** -0.5),
        'rwkv_a0': normal((L, 2, W), 0.5),
        'rwkv_a1': normal((L, 2, D, RWKV_ICLR_LORA), D ** -0.5),
        'rwkv_a2': normal((L, 2, RWKV_ICLR_LORA, W), 0.5 * RWKV_ICLR_LORA ** -0.5),
        'rwkv_g1': normal((L, D, RWKV_GATE_LORA), D ** -0.5),
        'rwkv_g2': normal((L, RWKV_GATE_LORA, W), RWKV_GATE_LORA ** -0.5),
        'rwkv_k_k': 0.85 + normal((L, W), 0.02),
        'rwkv_k_a': 1.0 + normal((L, W), 0.02),
        'rwkv_r_k': normal((L, W), 0.1),
        'rwkv_ln_w': 1.0 + normal((L, W), 0.02),
        'rwkv_ln_b': normal((L, W), 0.02),
        's5_a_re': -0.5 + normal((L, 2, G, P), 0.01),
        's5_a_im': jnp.pi * s5_n + normal((L, 2, G, P), 0.01),
        's5_log_dt': uniform((L, 2, G), math.log(1e-3), math.log(1e-1)),
        's5_b_re': normal((L, G, P, C), (2 * C) ** -0.5),
        's5_b_im': normal((L, G, P, C), (2 * C) ** -0.5),
        's5_c_re': normal((L, 2, G, C, P), P ** -0.5),
        's5_c_im': normal((L, 2, G, C, P), P ** -0.5),
        's5_d': normal((L, W), 1.0),
        's5_glu_w': normal((L, W, 2 * W), W ** -0.5),
        's5_glu_b': normal((L, 2 * W), 0.01),
        'gqa_q_norm': 1.0 + normal((L, HEAD_DIM), 0.02),
        'gqa_k_norm': 1.0 + normal((L, HEAD_DIM), 0.02),
        'norm_ffn_g': 1.0 + normal((L, D), 0.02),
        'dense_w_gate': normal((N_DENSE_LAYERS, D, DENSE_FF), D ** -0.5),
        'dense_w_up': normal((N_DENSE_LAYERS, D, DENSE_FF), D ** -0.5),
        'dense_w_down': normal((N_DENSE_LAYERS, DENSE_FF, D), DENSE_FF ** -0.5),
        'moe_router': normal((N_MOE_LAYERS, D, N_EXPERTS), D ** -0.5),
        'moe_w_gate': normal((N_MOE_LAYERS, N_EXPERTS, D, EXPERT_FF), D ** -0.5),
        'moe_w_up': normal((N_MOE_LAYERS, N_EXPERTS, D, EXPERT_FF), D ** -0.5),
        'moe_w_down': normal((N_MOE_LAYERS, N_EXPERTS, EXPERT_FF, D), EXPERT_FF ** -0.5),
        'final_norm_g': 1.0 + normal((D,), 0.02),
    }


def reference(x, norm_mix_g, w_in, gate_b, w_branch, w_out,
              rwkv_mu_rkv, rwkv_mu_x, rwkv_w0, rwkv_w1, rwkv_w2, rwkv_a0, rwkv_a1, rwkv_a2,
              rwkv_g1, rwkv_g2, rwkv_k_k, rwkv_k_a, rwkv_r_k, rwkv_ln_w, rwkv_ln_b,
              s5_a_re, s5_a_im, s5_log_dt, s5_b_re, s5_b_im, s5_c_re, s5_c_im, s5_d,
              s5_glu_w, s5_glu_b, gqa_q_norm, gqa_k_norm, norm_ffn_g,
              dense_w_gate, dense_w_up, dense_w_down,
              moe_router, moe_w_gate, moe_w_up, moe_w_down, final_norm_g):
    B, S, D = x.shape
    ROWS = S // GRID_W
    t = jnp.arange(S, dtype=jnp.int32)
    ang_rope = rope_angles(t, ROPE_DIMS // 2, ROPE_THETA)
    row = jnp.repeat(jnp.arange(ROWS, dtype=jnp.int32), GRID_W)
    col = jnp.tile(jnp.arange(GRID_W, dtype=jnp.int32), ROWS)
    ang_axial = jnp.concatenate([rope_angles(row, HEAD_DIM // 4, AXIAL_THETA),
                                 rope_angles(col, HEAD_DIM // 4, AXIAL_THETA)], axis=-1)
    for l in range(DEPTH):
        xn = rms_norm(x, norm_mix_g[l])
        x = x + hybrid_mixer(
            xn, w_in[l], gate_b[l], w_branch[l], w_out[l], ang_rope, ang_axial,
            rwkv_mu_rkv[l], rwkv_mu_x[l], rwkv_w0[l], rwkv_w1[l], rwkv_w2[l], rwkv_a0[l],
            rwkv_a1[l], rwkv_a2[l], rwkv_g1[l], rwkv_g2[l], rwkv_k_k[l], rwkv_k_a[l], rwkv_r_k[l],
            rwkv_ln_w[l], rwkv_ln_b[l],
            s5_a_re[l], s5_a_im[l], s5_log_dt[l], s5_b_re[l], s5_b_im[l], s5_c_re[l], s5_c_im[l],
            s5_d[l], s5_glu_w[l], s5_glu_b[l], gqa_q_norm[l], gqa_k_norm[l])
        xn = rms_norm(x, norm_ffn_g[l])
        i = l // 2
        if l % 2 == 0:
            x = x + swiglu(xn, dense_w_gate[i], dense_w_up[i], dense_w_down[i])
        else:
            x = x + moe_swiglu(xn, moe_router[i], moe_w_gate[i], moe_w_up[i], moe_w_down[i])
    return rms_norm(x, final_norm_g)
```

```python
import functools
import math

import jax
import jax.numpy as jnp
from jax import lax
from jax.experimental import pallas as pl
from jax.experimental.pallas import tpu as pltpu

F32 = jnp.float32
BF16 = jnp.bfloat16

D_MODEL = 1024
HEAD_DIM = 64
BRANCH_WIDTH = 256
N_BRANCHES = 4
N_HEADS = BRANCH_WIDTH // HEAD_DIM
DILATED_PATTERNS = ((128, 1), (512, 4), (2048, 16))
ROPE_THETA = 500000.0
ROPE_DIMS = HEAD_DIM // 4
RWKV_GN_EPS = 64e-5
S5_GROUP_CH = 16
S5_GROUPS = BRANCH_WIDTH // S5_GROUP_CH
S5_STATE = 64
GQA_KV_HEADS = 2
AXIAL_THETA = 10000.0
GRID_W = 64
N_EXPERTS = 8
TOP_K = 2
NORM_EPS = 1e-6
NEG_INF = -1e30

IN_A = 3 * BRANCH_WIDTH
IN_B = 3 * BRANCH_WIDTH
IN_C = BRANCH_WIDTH
IN_DQ = BRANCH_WIDTH
IN_DKV = GQA_KV_HEADS * HEAD_DIM
IN_GATES = N_BRANCHES * D_MODEL
OFF_B = IN_A
OFF_C = OFF_B + IN_B
OFF_DQ = OFF_C + IN_C
OFF_DKV = OFF_DQ + IN_DQ
OFF_GATES = OFF_DKV + 2 * IN_DKV
IN_TOTAL = OFF_GATES + IN_GATES

VMEM_LIMIT_BYTES = 56 * 1024 * 1024


def _params(*semantics):
    return pltpu.CompilerParams(dimension_semantics=semantics, vmem_limit_bytes=VMEM_LIMIT_BYTES)


def _bdot(a, b):
    return jnp.dot(a.astype(BF16), b.astype(BF16), preferred_element_type=F32)


def _bdot_nt(a, b):
    return lax.dot_general(a.astype(BF16), b.astype(BF16), (((1,), (1,)), ((), ())),
                           preferred_element_type=F32)


def _rms(x, g):
    return x * lax.rsqrt(jnp.mean(x * x, axis=-1, keepdims=True) + NORM_EPS) * g


def _rms_in_proj_kernel(x_ref, g_ref, w_ref, xn_ref, xnb_ref, z_ref):
    @pl.when(pl.program_id(1) == 0)
    def _():
        y = _rms(x_ref[...], g_ref[...])
        xn_ref[...] = y
        xnb_ref[...] = y.astype(BF16)

    z_ref[...] = jnp.dot(xnb_ref[...], w_ref[...], preferred_element_type=F32)


def rms_in_proj(x, g, w_bf16, tm=1024, tn=768):
    n, d = x.shape
    nout = w_bf16.shape[1]
    tm = min(tm, n)
    return pl.pallas_call(
        _rms_in_proj_kernel,
        grid=(n // tm, nout // tn),
        in_specs=[pl.BlockSpec((tm, d), lambda i, j: (i, 0)),
                  pl.BlockSpec((1, d), lambda i, j: (0, 0)),
                  pl.BlockSpec((d, tn), lambda i, j: (0, j))],
        out_specs=[pl.BlockSpec((tm, d), lambda i, j: (i, 0)),
                   pl.BlockSpec((tm, d), lambda i, j: (i, 0)),
                   pl.BlockSpec((tm, tn), lambda i, j: (i, j))],
        out_shape=[jax.ShapeDtypeStruct((n, d), F32), jax.ShapeDtypeStruct((n, d), BF16),
                   jax.ShapeDtypeStruct((n, nout), F32)],
        compiler_params=_params("parallel", "arbitrary"),
        name="rms_in_proj",
    )(x, g.reshape(1, d), w_bf16)


def _rotary_tables(pos_angles, n_heads):
    s, n = pos_angles.shape
    pad = HEAD_DIM - 2 * n
    cos = jnp.concatenate([jnp.cos(pos_angles), jnp.cos(pos_angles), jnp.ones((s, pad), F32)], axis=-1)
    zeros_n = jnp.zeros((s, n), F32)
    zeros_p = jnp.zeros((s, pad), F32)
    sin_lo = jnp.concatenate([-jnp.sin(pos_angles), zeros_n, zeros_p], axis=-1)
    sin_hi = jnp.concatenate([zeros_n, jnp.sin(pos_angles), zeros_p], axis=-1)
    return tuple(jnp.tile(t, (1, n_heads)) for t in (cos, sin_lo, sin_hi))


def _rotate(x, cos, sin_lo, sin_hi, n):
    width = x.shape[-1]
    from_above = pltpu.roll(x, width - n, 1)
    from_below = pltpu.roll(x, n, 1)
    return x * cos + from_above * sin_lo + from_below * sin_hi


def _head_sum(x, n_heads):
    lane = lax.broadcasted_iota(jnp.int32, x.shape, 1)
    out = jnp.zeros_like(x)
    for h in range(n_heads):
        in_head = (lane >= h * HEAD_DIM) & (lane < (h + 1) * HEAD_DIM)
        s = jnp.sum(jnp.where(in_head, x, 0.0), axis=-1, keepdims=True)
        out = jnp.where(in_head, s, out)
    return out


def _head_rms(x, g, n_heads):
    ms = _head_sum(x * x, n_heads) * (1.0 / HEAD_DIM)
    return x * lax.rsqrt(ms + NORM_EPS) * g


def _qkv_prep_kernel(za_ref, zq_ref, zkv_ref, rc_ref, rl_ref, rh_ref, ac_ref, al_ref, ah_ref,
                     qn_ref, kn_ref, qa_ref, ka_ref, va_ref, qd_ref, kd_ref, vd_ref):
    w = BRANCH_WIDTH
    n_rope = ROPE_DIMS // 2
    n_ax = HEAD_DIM // 2
    za = za_ref[...]
    rc, rl, rh = rc_ref[...], rl_ref[...], rh_ref[...]
    qa = _rotate(za[:, :w], rc, rl, rh, n_rope) * (HEAD_DIM ** -0.5)
    ka = _rotate(za[:, w:2 * w], rc, rl, rh, n_rope)
    va = za[:, 2 * w:]
    ac, al, ah = ac_ref[...], al_ref[...], ah_ref[...]
    qd = _rotate(_head_rms(zq_ref[...], qn_ref[...], N_HEADS), ac, al, ah, n_ax) * (HEAD_DIM ** -0.5)
    zkv = zkv_ref[...]
    kw = GQA_KV_HEADS * HEAD_DIM
    kd = _rotate(_head_rms(zkv[:, :kw], kn_ref[...], GQA_KV_HEADS), ac[:, :kw], al[:, :kw], ah[:, :kw], n_ax)
    vd = zkv[:, kw:]
    for h in range(N_HEADS):
        sl = slice(h * HEAD_DIM, (h + 1) * HEAD_DIM)
        qa_ref[0, h] = qa[:, sl].astype(BF16)
        ka_ref[0, h] = ka[:, sl].astype(BF16)
        va_ref[0, h] = va[:, sl].astype(BF16)
        qd_ref[0, h] = qd[:, sl].astype(BF16)
    for h in range(GQA_KV_HEADS):
        sl = slice(h * HEAD_DIM, (h + 1) * HEAD_DIM)
        kd_ref[0, h] = kd[:, sl].astype(BF16)
        vd_ref[0, h] = vd[:, sl].astype(BF16)


def qkv_prep(z, batch, seq, rope_tabs, axial_tabs, q_norm, k_norm, tm=512):
    tm = min(tm, seq)
    nt = seq // tm
    w = BRANCH_WIDTH
    row = lambda b, i: b * nt + i
    tab_spec = pl.BlockSpec((tm, w), lambda b, i: (i, 0))
    head_out = lambda nh: pl.BlockSpec((1, nh, tm, HEAD_DIM), lambda b, i: (b, 0, i, 0))
    head_shape = lambda nh: jax.ShapeDtypeStruct((batch, nh, seq, HEAD_DIM), BF16)
    return pl.pallas_call(
        _qkv_prep_kernel,
        grid=(batch, nt),
        in_specs=[pl.BlockSpec((tm, IN_A), lambda b, i: (row(b, i), 0)),
                  pl.BlockSpec((tm, w), lambda b, i: (row(b, i), OFF_DQ // w)),
                  pl.BlockSpec((tm, w), lambda b, i: (row(b, i), OFF_DKV // w)),
                  tab_spec, tab_spec, tab_spec, tab_spec, tab_spec, tab_spec,
                  pl.BlockSpec((1, w), lambda b, i: (0, 0)),
                  pl.BlockSpec((1, GQA_KV_HEADS * HEAD_DIM), lambda b, i: (0, 0))],
        out_specs=[head_out(N_HEADS), head_out(N_HEADS), head_out(N_HEADS),
                   head_out(N_HEADS), head_out(GQA_KV_HEADS), head_out(GQA_KV_HEADS)],
        out_shape=[head_shape(N_HEADS), head_shape(N_HEADS), head_shape(N_HEADS),
                   head_shape(N_HEADS), head_shape(GQA_KV_HEADS), head_shape(GQA_KV_HEADS)],
        compiler_params=_params("parallel", "parallel"),
        name="qkv_prep",
    )(z, z, z, *rope_tabs, *axial_tabs,
      jnp.tile(q_norm.reshape(1, HEAD_DIM), (1, N_HEADS)),
      jnp.tile(k_norm.reshape(1, HEAD_DIM), (1, GQA_KV_HEADS)))


A_TQ = 1024
A_SUB = 256
A_RADIUS = 64


def _dilated_windows():
    out = []
    for window, dil in DILATED_PATTERNS:
        halo = -(-(window // 2) // 128) * 128
        out.append((dil, -halo, A_SUB + 2 * halo))
    return out


def _dilated_bias():
    biases = []
    for dil, first, width in _dilated_windows():
        qi = jnp.arange(A_SUB, dtype=jnp.int32)[:, None]
        kj = jnp.arange(width, dtype=jnp.int32)[None, :] + first
        delta = kj - qi
        ok = (jnp.abs(delta) <= A_RADIUS * dil) & ((delta & (dil - 1)) == 0)
        biases.append(jnp.where(ok, 0.0, NEG_INF).astype(F32))
    return biases


def _dilated_attn_kernel(q_ref, kp_ref, kc_ref, kn_ref, vp_ref, vc_ref, vn_ref, b0_ref, b1_ref, b2_ref,
                         o_ref, k3_ref, v3_ref, *, seq, tq):
    i = pl.program_id(2)
    k3_ref[0:tq] = kp_ref[0, 0]
    k3_ref[tq:2 * tq] = kc_ref[0, 0]
    k3_ref[2 * tq:3 * tq] = kn_ref[0, 0]
    v3_ref[0:tq] = vp_ref[0, 0]
    v3_ref[tq:2 * tq] = vc_ref[0, 0]
    v3_ref[2 * tq:3 * tq] = vn_ref[0, 0]
    bias_refs = (b0_ref, b1_ref, b2_ref)
    windows = _dilated_windows()
    for u in range(tq // A_SUB):
        q = q_ref[0, 0, u * A_SUB:(u + 1) * A_SUB, :]
        scores = []
        for (dil, first, width), b_ref in zip(windows, bias_refs):
            start = tq + u * A_SUB + first
            s = _bdot_nt(q, k3_ref[start:start + width, :]) + b_ref[...]
            kpos = (i - 1) * tq + start + lax.broadcasted_iota(jnp.int32, (1, width), 1)
            s = jnp.where((kpos >= 0) & (kpos < seq), s, NEG_INF)
            scores.append((s, start, width))
        m = functools.reduce(jnp.maximum, [jnp.max(s, axis=-1, keepdims=True) for s, _, _ in scores])
        l = jnp.zeros_like(m)
        acc = jnp.zeros((A_SUB, HEAD_DIM), F32)
        for s, start, width in scores:
            p = jnp.exp(s - m)
            l = l + jnp.sum(p, axis=-1, keepdims=True)
            acc = acc + _bdot(p, v3_ref[start:start + width, :])
        o_ref[0, 0, u * A_SUB:(u + 1) * A_SUB, :] = acc / l


def dilated_attention(qa, ka, va):
    batch, nh, seq, hd = qa.shape
    tq = min(A_TQ, seq)
    assert tq == A_TQ, "key halo of radius * max dilation needs full query tiles"
    nt = seq // tq
    cur = pl.BlockSpec((1, 1, tq, hd), lambda b, h, i: (b, h, i, 0))
    prev = pl.BlockSpec((1, 1, tq, hd), lambda b, h, i: (b, h, jnp.maximum(i - 1, 0), 0))
    nxt = pl.BlockSpec((1, 1, tq, hd), lambda b, h, i: (b, h, jnp.minimum(i + 1, nt - 1), 0))
    biases = _dilated_bias()
    bias_specs = [pl.BlockSpec(b.shape, lambda b_, h, i: (0, 0)) for b in biases]
    return pl.pallas_call(
        functools.partial(_dilated_attn_kernel, seq=seq, tq=tq),
        grid=(batch, nh, nt),
        in_specs=[cur, prev, cur, nxt, prev, cur, nxt] + bias_specs,
        out_specs=cur,
        out_shape=jax.ShapeDtypeStruct((batch, nh, seq, hd), F32),
        scratch_shapes=[pltpu.VMEM((3 * tq, hd), BF16), pltpu.VMEM((3 * tq, hd), BF16)],
        compiler_params=_params("parallel", "parallel", "parallel"),
        name="dilated_attention",
    )(qa, ka, ka, ka, va, va, va, *biases)


def _gqa_kernel(q_ref, k_ref, v_ref, o_ref, m_ref, l_ref, acc_ref, *, rep, tq):
    j = pl.program_id(3)

    @pl.when(j == 0)
    def _():
        m_ref[...] = jnp.full(m_ref.shape, NEG_INF, F32)
        l_ref[...] = jnp.zeros(l_ref.shape, F32)
        acc_ref[...] = jnp.zeros(acc_ref.shape, F32)

    q = q_ref[0].reshape(rep * tq, HEAD_DIM)
    s = _bdot_nt(q, k_ref[0, 0])
    m_prev = m_ref[...]
    m_new = jnp.maximum(m_prev, jnp.max(s, axis=-1, keepdims=True))
    alpha = jnp.exp(m_prev - m_new)
    p = jnp.exp(s - m_new)
    l_ref[...] = alpha * l_ref[...] + jnp.sum(p, axis=-1, keepdims=True)
    acc_ref[...] = alpha * acc_ref[...] + _bdot(p, v_ref[0, 0])
    m_ref[...] = m_new

    @pl.when(j == pl.num_programs(3) - 1)
    def _():
        o_ref[0] = (acc_ref[...] / l_ref[...]).reshape(rep, tq, HEAD_DIM)


def gqa_attention(qd, kd, vd, tq=512, tk=1024):
    batch, nh, seq, hd = qd.shape
    ng = kd.shape[1]
    rep = nh // ng
    tq = min(tq, seq)
    tk = min(tk, seq)
    return pl.pallas_call(
        functools.partial(_gqa_kernel, rep=rep, tq=tq),
        grid=(batch, ng, seq // tq, seq // tk),
        in_specs=[pl.BlockSpec((1, rep, tq, hd), lambda b, g, i, j: (b, g, i, 0)),
                  pl.BlockSpec((1, 1, tk, hd), lambda b, g, i, j: (b, g, j, 0)),
                  pl.BlockSpec((1, 1, tk, hd), lambda b, g, i, j: (b, g, j, 0))],
        out_specs=pl.BlockSpec((1, rep, tq, hd), lambda b, g, i, j: (b, g, i, 0)),
        out_shape=jax.ShapeDtypeStruct((batch, nh, seq, hd), F32),
        scratch_shapes=[pltpu.VMEM((rep * tq, 1), F32), pltpu.VMEM((rep * tq, 1), F32),
                        pltpu.VMEM((rep * tq, hd), F32)],
        compiler_params=_params("parallel", "parallel", "parallel", "arbitrary"),
        name="gqa_attention",
    )(qd, kd, vd)


def _matmul_kernel(a_ref, w_ref, o_ref):
    o_ref[...] = jnp.dot(a_ref[...], w_ref[...], preferred_element_type=F32)


def matmul_bf16(a, w, tm, tn):
    n, k = a.shape
    m = w.shape[1]
    tm = min(tm, n)
    return pl.pallas_call(
        _matmul_kernel,
        grid=(n // tm, m // tn),
        in_specs=[pl.BlockSpec((tm, k), lambda i, j: (i, 0)),
                  pl.BlockSpec((k, tn), lambda i, j: (0, j))],
        out_specs=pl.BlockSpec((tm, tn), lambda i, j: (i, j)),
        out_shape=jax.ShapeDtypeStruct((n, m), F32),
        compiler_params=_params("parallel", "arbitrary"),
        name="matmul_bf16",
    )(a, w)


def _sigmoid(x):
    return 1.0 / (1.0 + jnp.exp(-x))


def _softplus(x):
    return jnp.maximum(x, 0.0) + jnp.log(1.0 + jnp.exp(-jnp.abs(x)))


def _shift_rows(x, edge_row, down):
    rows = x.shape[0]
    ridx = lax.broadcasted_iota(jnp.int32, x.shape, 0)
    if down:
        return jnp.where(ridx == 0, edge_row, pltpu.roll(x, 1, 0))
    return jnp.where(ridx == rows - 1, edge_row, pltpu.roll(x, rows - 1, 0))


WKV_FIELDS = 6


def _rwkv_prep_kernel(xn_ref, xp_ref, xq_ref, zb_ref, zp_ref, zq_ref, mux_ref, murkv_ref, lw1_ref, lw2_ref,
                      w0a0_ref, g1_ref, g2_ref, kk_ref, ka_ref, rk_ref,
                      fw_ref, bw_ref, bonus_ref, gate_ref, *, tiles_per_seq):
    w = BRANCH_WIDTH
    i = pl.program_id(0)
    first = (i % tiles_per_seq) == 0
    last = (i % tiles_per_seq) == tiles_per_seq - 1
    xn = xn_ref[...]
    x_shift = (_shift_rows(xn, jnp.where(first, 0.0, xp_ref[7:8, :]), True),
               _shift_rows(xn, jnp.where(last, 0.0, xq_ref[0:1, :]), False))
    zb = zb_ref[...]
    z_prev = _shift_rows(zb, jnp.where(first, 0.0, zp_ref[7:8, :]), True)
    z_next = _shift_rows(zb, jnp.where(last, 0.0, zq_ref[0:1, :]), False)
    mu = murkv_ref[...]
    rkv = zb + mu[0:1] * (z_prev - zb) + mu[1:2] * (z_next - zb)
    r, k, v = rkv[:, :w], rkv[:, w:2 * w], rkv[:, 2 * w:]
    kap = k * kk_ref[...]
    kap = kap * lax.rsqrt(_head_sum(kap * kap, N_HEADS) + 1e-12)
    gate_ref[...] = _bdot(_sigmoid(_bdot(xn, g1_ref[...])), g2_ref[...])
    bonus = jnp.zeros_like(v)
    lora_lane = lax.broadcasted_iota(jnp.int32, (xn.shape[0], lw1_ref.shape[-1]), 1)
    for d, out_ref in enumerate((fw_ref, bw_ref)):
        xd = xn + mux_ref[d:d + 1, :] * (x_shift[d] - xn)
        h = _bdot(xd, lw1_ref[d])
        h = jnp.where(lora_lane < lw1_ref.shape[-1] // 2, jnp.tanh(h), h)
        h = _bdot(h, lw2_ref[d]) + w0a0_ref[d:d + 1, :]
        w_log = -_softplus(-h[:, :w]) - 0.5
        decay = jnp.exp(-jnp.exp(w_log))
        iclr = _sigmoid(h[:, w:])
        k_d = k * (1.0 + (iclr - 1.0) * ka_ref[...])
        bonus = bonus + _head_sum(r * k_d * rk_ref[...], N_HEADS) * v
        for j, field in enumerate((r, decay, k_d, v, kap, iclr * kap)):
            out_ref[:, j * w:(j + 1) * w] = field
    bonus_ref[...] = bonus


def rwkv_prep(xn, z, seq, mu_x, mu_rkv, lw1, lw2, w0a0, g1, g2, k_k, k_a, r_k, tm=512):
    n, d = xn.shape
    w = BRANCH_WIDTH
    tm = min(tm, seq)
    halo = 8
    prev_halo = lambda i: (jnp.maximum(i * (tm // halo) - 1, 0), 0)
    next_halo = lambda i: (jnp.minimum((i + 1) * (tm // halo), n // halo - 1), 0)
    full = lambda a: pl.BlockSpec(a.shape, lambda i: (0,) * a.ndim)
    params = (mu_x, mu_rkv, lw1, lw2, w0a0, g1, g2, k_k.reshape(1, w), k_a.reshape(1, w), r_k.reshape(1, w))
    rows = lambda width: pl.BlockSpec((tm, width), lambda i: (i, 0))
    return pl.pallas_call(
        functools.partial(_rwkv_prep_kernel, tiles_per_seq=seq // tm),
        grid=(n // tm,),
        in_specs=[rows(d), pl.BlockSpec((halo, d), prev_halo), pl.BlockSpec((halo, d), next_halo),
                  pl.BlockSpec((tm, IN_B), lambda i: (i, OFF_B // IN_B)),
                  pl.BlockSpec((halo, IN_B), lambda i: (prev_halo(i)[0], OFF_B // IN_B)),
                  pl.BlockSpec((halo, IN_B), lambda i: (next_halo(i)[0], OFF_B // IN_B))]
                 + [full(p) for p in params],
        out_specs=[rows(WKV_FIELDS * w), rows(WKV_FIELDS * w), rows(w), rows(w)],
        out_shape=[jax.ShapeDtypeStruct((n, WKV_FIELDS * w), F32), jax.ShapeDtypeStruct((n, WKV_FIELDS * w), F32),
                   jax.ShapeDtypeStruct((n, w), F32), jax.ShapeDtypeStruct((n, w), F32)],
        compiler_params=_params("parallel"),
        name="rwkv_prep",
    )(xn, xn, xn, z, z, z, *params)


WKV_UNROLL = 8


def _wkv_step(row, state, ones, eye):
    w = BRANCH_WIDTH
    r, decay, k, v, kap, b = (row[:, j * w:(j + 1) * w] for j in range(WKV_FIELDS))
    s_kap = jnp.dot((state * kap).astype(BF16), ones, preferred_element_type=F32)
    v_col = jnp.dot((eye * v).astype(BF16), ones, preferred_element_type=F32)
    state = state * decay - s_kap * b + v_col * k
    y_full = jnp.dot((state * r).astype(BF16), ones, preferred_element_type=F32)
    return state, jnp.sum(y_full * eye, axis=0, keepdims=True)


def _wkv_scan_kernel(fw_ref, bw_ref, ones_ref, eye_ref, yf_ref, yb_ref, state_ref, *, batch, chunk):
    @pl.when(pl.program_id(0) == 0)
    def _():
        state_ref[...] = jnp.zeros(state_ref.shape, F32)

    ones = ones_ref[...]
    eye = eye_ref[...]
    groups = chunk // WKV_UNROLL

    def body(i, carry):
        base_f = pl.multiple_of(i * WKV_UNROLL, WKV_UNROLL)
        base_b = pl.multiple_of((groups - 1 - i) * WKV_UNROLL, WKV_UNROLL)
        for bi in range(batch):
            sf = state_ref[2 * bi]
            sb = state_ref[2 * bi + 1]
            ys_f, ys_b = [], []
            for j in range(WKV_UNROLL):
                sf, y = _wkv_step(fw_ref[bi, pl.ds(base_f + j, 1), :], sf, ones, eye)
                ys_f.append(y)
                sb, y = _wkv_step(bw_ref[bi, pl.ds(base_b + WKV_UNROLL - 1 - j, 1), :], sb, ones, eye)
                ys_b.append(y)
            yf_ref[bi, pl.ds(base_f, WKV_UNROLL), :] = jnp.concatenate(ys_f, axis=0)
            yb_ref[bi, pl.ds(base_b, WKV_UNROLL), :] = jnp.concatenate(ys_b[::-1], axis=0)
            state_ref[2 * bi] = sf
            state_ref[2 * bi + 1] = sb
        return carry

    lax.fori_loop(0, groups, body, 0)


def wkv_scan(fw, bw, batch, seq, chunk=256):
    w = BRANCH_WIDTH
    chunk = min(chunk, seq)
    nc = seq // chunk
    head_of_lane = jnp.arange(w) // HEAD_DIM
    ones = (head_of_lane[:, None] == head_of_lane[None, :]).astype(BF16)
    eye = (jnp.arange(HEAD_DIM)[:, None] == (jnp.arange(w) % HEAD_DIM)[None, :]).astype(F32)
    in_f = pl.BlockSpec((batch, chunk, WKV_FIELDS * w), lambda c: (0, c, 0))
    in_b = pl.BlockSpec((batch, chunk, WKV_FIELDS * w), lambda c: (0, nc - 1 - c, 0))
    yf, yb = pl.pallas_call(
        functools.partial(_wkv_scan_kernel, batch=batch, chunk=chunk),
        grid=(nc,),
        in_specs=[in_f, in_b, pl.BlockSpec(ones.shape, lambda c: (0, 0)), pl.BlockSpec(eye.shape, lambda c: (0, 0))],
        out_specs=[pl.BlockSpec((batch, chunk, w), lambda c: (0, c, 0)),
                   pl.BlockSpec((batch, chunk, w), lambda c: (0, nc - 1 - c, 0))],
        out_shape=[jax.ShapeDtypeStruct((batch, seq, w), F32), jax.ShapeDtypeStruct((batch, seq, w), F32)],
        scratch_shapes=[pltpu.VMEM((2 * batch, HEAD_DIM, w), F32)],
        compiler_params=_params("arbitrary"),
        name="wkv_scan",
    )(fw.reshape(batch, seq, -1), bw.reshape(batch, seq, -1), ones, eye)
    return yf.reshape(batch * seq, w), yb.reshape(batch * seq, w)


S5_SEGMENTS = 8
S5_LANE_BLOCK = 128
S5_WIDTH = S5_GROUPS * S5_STATE


def _s5_discretize(a_re, a_im, log_dt, b_re, b_im, c_re, c_im):
    g, p, c = S5_GROUPS, S5_STATE, S5_GROUP_CH
    dt = jnp.exp(log_dt)[:, None]
    mag = jnp.exp(a_re * dt)
    bar_re, bar_im = mag * jnp.cos(a_im * dt), mag * jnp.sin(a_im * dt)
    den = a_re * a_re + a_im * a_im
    f_re = ((bar_re - 1.0) * a_re + bar_im * a_im) / den
    f_im = (bar_im * a_re - (bar_re - 1.0) * a_im) / den
    bb_re = f_re[..., None] * b_re - f_im[..., None] * b_im
    bb_im = f_re[..., None] * b_im + f_im[..., None] * b_re
    eye_g = jnp.eye(g, dtype=F32)
    w_in = jnp.concatenate(
        [jnp.einsum('gpc,gh->gchp', bb, eye_g).reshape(g * c, g * p) for bb in (bb_re, bb_im)], axis=1)
    w_out = jnp.concatenate(
        [jnp.einsum('gcp,gh->gphc', cc, eye_g).reshape(g * p, g * c) for cc in (c_re, -c_im)], axis=0)
    return bar_re.reshape(1, g * p), bar_im.reshape(1, g * p), w_in.astype(BF16), w_out.astype(BF16), dt, a_re, a_im


def _s5_powers(a_re, a_im, dt, count, reverse):
    j = jnp.arange(1, count + 1, dtype=F32)
    if reverse:
        j = j[::-1]
    e = j[:, None, None] * (a_re * dt)[None]
    th = j[:, None, None] * (a_im * dt)[None]
    mag = jnp.exp(e)
    return (mag * jnp.cos(th)).reshape(count, -1), (mag * jnp.sin(th)).reshape(count, -1)


def _s5_scan_kernel(u_ref, win_ref, lre_ref, lim_ref, pre_ref, pim_ref, wout_ref, y_ref,
                    xre_ref, xim_ref, cre_ref, cim_ref, *, reverse, chunk):
    seg = chunk // S5_SEGMENTS
    nw = S5_WIDTH

    @pl.when(pl.program_id(1) == 0)
    def _():
        cre_ref[...] = jnp.zeros(cre_ref.shape, F32)
        cim_ref[...] = jnp.zeros(cim_ref.shape, F32)

    nb = nw // S5_LANE_BLOCK
    lanes = lambda c: slice(c * S5_LANE_BLOCK, (c + 1) * S5_LANE_BLOCK)
    bu = jnp.dot(u_ref[...].astype(BF16), win_ref[...], preferred_element_type=F32)
    for c in range(nb):
        xre_ref[c] = bu[:, lanes(c)]
        xim_ref[c] = bu[:, lanes(nb + c)]
    lre = [lre_ref[:, lanes(c)] for c in range(nb)]
    lim = [lim_ref[:, lanes(c)] for c in range(nb)]

    def local_step(s, carry):
        pos = (seg - 1 - s) if reverse else s
        rows = pl.ds(pos, S5_SEGMENTS, stride=seg)
        out = []
        for c in range(nb):
            xr, xi = carry[c]
            nr = lre[c] * xr - lim[c] * xi + xre_ref[c, rows, :]
            ni = lre[c] * xi + lim[c] * xr + xim_ref[c, rows, :]
            xre_ref[c, rows, :] = nr
            xim_ref[c, rows, :] = ni
            out.append((nr, ni))
        return tuple(out)

    zero = jnp.zeros((S5_SEGMENTS, S5_LANE_BLOCK), F32)
    fin = lax.fori_loop(0, seg, local_step, tuple((zero, zero) for _ in range(nb)))

    full_seg = 0 if reverse else seg - 1
    order = range(S5_SEGMENTS - 1, -1, -1) if reverse else range(S5_SEGMENTS)
    for c in range(nb):
        pr, pi = pre_ref[:, lanes(c)], pim_ref[:, lanes(c)]
        pl_re, pl_im = pr[full_seg:full_seg + 1, :], pi[full_seg:full_seg + 1, :]
        ir, ii = cre_ref[:, lanes(c)], cim_ref[:, lanes(c)]
        for j in order:
            rows = slice(j * seg, (j + 1) * seg)
            xre_ref[c, rows, :] = xre_ref[c, rows, :] + (pr * ir - pi * ii)
            xim_ref[c, rows, :] = xim_ref[c, rows, :] + (pr * ii + pi * ir)
            fr, fi = fin[c][0][j:j + 1, :], fin[c][1][j:j + 1, :]
            ir, ii = fr + (pl_re * ir - pl_im * ii), fi + (pl_re * ii + pl_im * ir)
        cre_ref[:, lanes(c)] = ir
        cim_ref[:, lanes(c)] = ii
    x_re = jnp.concatenate([xre_ref[c] for c in range(nb)], axis=1).astype(BF16)
    x_im = jnp.concatenate([xim_ref[c] for c in range(nb)], axis=1).astype(BF16)
    y_ref[...] = (jnp.dot(x_re, wout_ref[:nw, :], preferred_element_type=F32)
                  + jnp.dot(x_im, wout_ref[nw:, :], preferred_element_type=F32))


def s5_scan(z, batch, seq, disc, reverse, chunk=1024):
    lam_re, lam_im, w_in, w_out, dt, a_re, a_im = disc
    w = BRANCH_WIDTH
    chunk = min(chunk, seq)
    nc = seq // chunk
    seg = chunk // S5_SEGMENTS
    pw_re, pw_im = _s5_powers(a_re, a_im, dt, seg, reverse)
    order = (lambda c: nc - 1 - c) if reverse else (lambda c: c)
    full = lambda a: pl.BlockSpec(a.shape, lambda b, c: (0, 0))
    return pl.pallas_call(
        functools.partial(_s5_scan_kernel, reverse=reverse, chunk=chunk),
        grid=(batch, nc),
        in_specs=[pl.BlockSpec((chunk, w), lambda b, c: (b * nc + order(c), OFF_C // w)),
                  full(w_in), full(lam_re), full(lam_im), full(pw_re), full(pw_im), full(w_out)],
        out_specs=pl.BlockSpec((chunk, w), lambda b, c: (b * nc + order(c), 0)),
        out_shape=jax.ShapeDtypeStruct((batch * seq, w), F32),
        scratch_shapes=[pltpu.VMEM((S5_WIDTH // S5_LANE_BLOCK, chunk, S5_LANE_BLOCK), F32),
                        pltpu.VMEM((S5_WIDTH // S5_LANE_BLOCK, chunk, S5_LANE_BLOCK), F32),
                        pltpu.VMEM((1, S5_WIDTH), F32), pltpu.VMEM((1, S5_WIDTH), F32)],
        compiler_params=_params("parallel", "arbitrary"),
        name="s5_scan_bwd" if reverse else "s5_scan_fwd",
    )(z, w_in, lam_re, lam_im, pw_re, pw_im, w_out)


def _gelu_tanh(y):
    return 0.5 * y * (1.0 + jnp.tanh(math.sqrt(2.0 / math.pi) * (y + 0.044715 * (y * y * y))))


def _merge_kernel(x_ref, ya_ref, yd_ref, wf_ref, wb_ref, bonus_ref, rg_ref, sf_ref, sb_ref, u_ref, zg_ref,
                  gb_ref, wbr_ref, wout_ref, lnw_ref, lnb_ref, s5d_ref, gluw_ref, glub_ref, o_ref):
    w = BRANCH_WIDTH
    ya = jnp.concatenate([ya_ref[0, h] for h in range(N_HEADS)], axis=1)
    yd = jnp.concatenate([yd_ref[0, h] for h in range(N_HEADS)], axis=1)
    ys = wf_ref[...] + wb_ref[...]
    cen = ys - _head_sum(ys, N_HEADS) * (1.0 / HEAD_DIM)
    var = _head_sum(cen * cen, N_HEADS) * (1.0 / HEAD_DIM)
    yb = (cen * lax.rsqrt(var + RWKV_GN_EPS) * lnw_ref[...] + lnb_ref[...] + bonus_ref[...]) * rg_ref[...]
    yc = sf_ref[...] + sb_ref[...] + s5d_ref[...] * u_ref[...]
    h = _bdot(_gelu_tanh(yc), gluw_ref[...]) + glub_ref[...]
    yc = h[:, :w] * _sigmoid(h[:, w:])
    merged = jnp.zeros(o_ref.shape, F32)
    for i, y in enumerate((ya, yb, yc, yd)):
        gate = _sigmoid(zg_ref[:, i * D_MODEL:(i + 1) * D_MODEL] + gb_ref[i:i + 1, :])
        merged = merged + gate * _bdot(y, wbr_ref[i])
    o_ref[...] = x_ref[...] + _bdot(merged, wout_ref[...])


def merge_branches(x, ya, yd, wkv_f, wkv_b, bonus, rgate, s5_f, s5_b, z, zg, gate_b, w_branch, w_out,
                   ln_w, ln_b, s5_d, glu_w, glu_b, batch, seq, tm=256):
    w = BRANCH_WIDTH
    d = D_MODEL
    tm = min(tm, seq)
    nt = seq // tm
    rows = lambda width, col=0: pl.BlockSpec((tm, width), lambda b, i: (b * nt + i, col))
    heads = pl.BlockSpec((1, N_HEADS, tm, HEAD_DIM), lambda b, i: (b, 0, i, 0))
    full = lambda a: pl.BlockSpec(a.shape, lambda b, i: (0,) * a.ndim)
    params = (gate_b, w_branch, w_out, ln_w.reshape(1, w), ln_b.reshape(1, w), s5_d.reshape(1, w),
              glu_w, glu_b.reshape(1, 2 * w))
    return pl.pallas_call(
        _merge_kernel,
        grid=(batch, nt),
        in_specs=[rows(d), heads, heads, rows(w), rows(w), rows(w), rows(w), rows(w), rows(w),
                  rows(w, OFF_C // w), rows(N_BRANCHES * d)] + [full(p) for p in params],
        out_specs=rows(d),
        out_shape=jax.ShapeDtypeStruct(x.shape, F32),
        compiler_params=_params("parallel", "parallel"),
        name="merge_branches",
    )(x, ya, yd, wkv_f, wkv_b, bonus, rgate, s5_f, s5_b, z, zg, *params)


def _silu(x):
    return x * _sigmoid(x)


def _dense_ffn_kernel(x_ref, g_ref, wg_ref, wu_ref, wd_ref, o_ref, xn_ref):
    @pl.when(pl.program_id(1) == 0)
    def _():
        x = x_ref[...]
        xn_ref[...] = _rms(x, g_ref[...]).astype(BF16)
        o_ref[...] = x

    xn = xn_ref[...]
    h = (_silu(jnp.dot(xn, wg_ref[...], preferred_element_type=F32))
         * jnp.dot(xn, wu_ref[...], preferred_element_type=F32))
    o_ref[...] += _bdot(h, wd_ref[...])


def dense_ffn(x, g, w_gate, w_up, w_down, tm=1024, tf=1408):
    n, d = x.shape
    ff = w_gate.shape[1]
    tm = min(tm, n)
    return pl.pallas_call(
        _dense_ffn_kernel,
        grid=(n // tm, ff // tf),
        in_specs=[pl.BlockSpec((tm, d), lambda i, f: (i, 0)),
                  pl.BlockSpec((1, d), lambda i, f: (0, 0)),
                  pl.BlockSpec((d, tf), lambda i, f: (0, f)),
                  pl.BlockSpec((d, tf), lambda i, f: (0, f)),
                  pl.BlockSpec((tf, d), lambda i, f: (f, 0))],
        out_specs=pl.BlockSpec((tm, d), lambda i, f: (i, 0)),
        out_shape=jax.ShapeDtypeStruct((n, d), F32),
        scratch_shapes=[pltpu.VMEM((tm, d), BF16)],
        compiler_params=_params("parallel", "arbitrary"),
        name="dense_ffn",
    )(x, g.reshape(1, d), w_gate, w_up, w_down)


def _router_kernel(x_ref, g_ref, rt_ref, xnb_ref, sel_ref, wt_ref):
    xn = _rms(x_ref[...], g_ref[...])
    xnb_ref[...] = xn.astype(BF16)
    logits = lax.dot_general(rt_ref[...], xn, (((1,), (1,)), ((), ())),
                             precision=lax.Precision.HIGHEST, preferred_element_type=F32)
    e = lax.broadcasted_iota(jnp.int32, logits.shape, 0)
    m1 = jnp.max(logits, axis=0, keepdims=True)
    i1 = jnp.min(jnp.where(logits == m1, e, N_EXPERTS), axis=0, keepdims=True)
    rest = jnp.where(e == i1, NEG_INF, logits)
    m2 = jnp.max(rest, axis=0, keepdims=True)
    i2 = jnp.min(jnp.where(rest == m2, e, N_EXPERTS), axis=0, keepdims=True)
    ratio = jnp.exp(m2 - m1)
    w1 = 1.0 / (1.0 + ratio)
    w2 = ratio / (1.0 + ratio)
    sel_ref[...] = jnp.where((e == i1) | (e == i2), 1.0, 0.0)
    wt_ref[...] = jnp.where(e == i1, w1, jnp.where(e == i2, w2, 0.0))


def moe_route(x, g, router, tm=1024):
    n, d = x.shape
    tm = min(tm, n)
    ne = router.shape[1]
    return pl.pallas_call(
        _router_kernel,
        grid=(n // tm,),
        in_specs=[pl.BlockSpec((tm, d), lambda i: (i, 0)),
                  pl.BlockSpec((1, d), lambda i: (0, 0)),
                  pl.BlockSpec((ne, d), lambda i: (0, 0))],
        out_specs=[pl.BlockSpec((tm, d), lambda i: (i, 0)),
                   pl.BlockSpec((ne, tm), lambda i: (0, i)),
                   pl.BlockSpec((ne, tm), lambda i: (0, i))],
        out_shape=[jax.ShapeDtypeStruct((n, d), BF16), jax.ShapeDtypeStruct((ne, n), F32),
                   jax.ShapeDtypeStruct((ne, n), F32)],
        compiler_params=_params("parallel"),
        name="moe_router",
    )(x, g.reshape(1, d), router.T)


MOE_ROWS = 128


def _moe_kernel(x_ref, xnb_ref, sel_ref, wt_ref, wg_ref, wu_ref, wd_ref, o_ref,
                rank_ref, xg_ref, acc_ref, nblk_ref):
    e = pl.program_id(1)
    f = pl.program_id(2)
    nf = pl.num_programs(2)
    tm = x_ref.shape[0]

    @pl.when((e == 0) & (f == 0))
    def _():
        o_ref[...] = x_ref[...]
        before = (lax.broadcasted_iota(jnp.int32, (tm, tm), 0) < lax.broadcasted_iota(jnp.int32, (tm, tm), 1))
        rank_ref[...] = jnp.dot(sel_ref[...].astype(BF16), jnp.where(before, 1.0, 0.0).astype(BF16),
                                preferred_element_type=F32)

    sel_e = sel_ref[pl.ds(e, 1), :]
    rank_e = rank_ref[pl.ds(e, 1), :]

    def one_hot(b):
        slot = (b * MOE_ROWS + lax.broadcasted_iota(jnp.int32, (MOE_ROWS, tm), 0)).astype(F32)
        return jnp.where((rank_e == slot) & (sel_e > 0.0), 1.0, 0.0)

    def rows_of(b):
        return pl.ds(pl.multiple_of(b * MOE_ROWS, MOE_ROWS), MOE_ROWS)

    @pl.when(f == 0)
    def _():
        count = jnp.sum(sel_e).astype(jnp.int32)
        nblk_ref[0] = (count + MOE_ROWS - 1) // MOE_ROWS

        def gather(b, carry):
            xg_ref[rows_of(b), :] = jnp.dot(one_hot(b).astype(BF16), xnb_ref[...],
                                            preferred_element_type=F32).astype(BF16)
            acc_ref[rows_of(b), :] = jnp.zeros((MOE_ROWS, acc_ref.shape[1]), F32)
            return carry

        lax.fori_loop(0, nblk_ref[0], gather, 0)

    nblk = nblk_ref[0]

    def ffn(b, carry):
        xg = xg_ref[rows_of(b), :]
        h = (_silu(jnp.dot(xg, wg_ref[0], preferred_element_type=F32))
             * jnp.dot(xg, wu_ref[0], preferred_element_type=F32))
        acc_ref[rows_of(b), :] += _bdot(h, wd_ref[0])
        return carry

    lax.fori_loop(0, nblk, ffn, 0)

    @pl.when(f == nf - 1)
    def _():
        wt_e = wt_ref[pl.ds(e, 1), :]

        def scatter(b, carry):
            hot = one_hot(b)
            row_w = jnp.sum(hot * wt_e, axis=1, keepdims=True)
            yw = (acc_ref[rows_of(b), :] * row_w).astype(BF16)
            o_ref[...] += lax.dot_general(hot.astype(BF16), yw, (((0,), (0,)), ((), ())),
                                          preferred_element_type=F32)
            return carry

        lax.fori_loop(0, nblk, scatter, 0)


def moe_ffn(x, xnb, sel, wt, w_gate, w_up, w_down, tm=1024, tf=512):
    n, d = x.shape
    ne, _, ff = w_gate.shape
    tm = min(tm, n)
    return pl.pallas_call(
        _moe_kernel,
        grid=(n // tm, ne, ff // tf),
        in_specs=[pl.BlockSpec((tm, d), lambda i, e, f: (i, 0)),
                  pl.BlockSpec((tm, d), lambda i, e, f: (i, 0)),
                  pl.BlockSpec((ne, tm), lambda i, e, f: (0, i)),
                  pl.BlockSpec((ne, tm), lambda i, e, f: (0, i)),
                  pl.BlockSpec((1, d, tf), lambda i, e, f: (e, 0, f)),
                  pl.BlockSpec((1, d, tf), lambda i, e, f: (e, 0, f)),
                  pl.BlockSpec((1, tf, d), lambda i, e, f: (e, f, 0))],
        out_specs=pl.BlockSpec((tm, d), lambda i, e, f: (i, 0)),
        out_shape=jax.ShapeDtypeStruct((n, d), F32),
        scratch_shapes=[pltpu.VMEM((ne, tm), F32), pltpu.VMEM((tm, d), BF16), pltpu.VMEM((tm, d), F32),
                        pltpu.SMEM((1,), jnp.int32)],
        compiler_params=_params("parallel", "arbitrary", "arbitrary"),
        name="moe_ffn",
    )(x, xnb, sel, wt, w_gate, w_up, w_down)


def _rms_kernel(x_ref, g_ref, o_ref):
    o_ref[...] = _rms(x_ref[...], g_ref[...])


def rms_norm(x, g, tm=1024):
    n, d = x.shape
    tm = min(tm, n)
    return pl.pallas_call(
        _rms_kernel,
        grid=(n // tm,),
        in_specs=[pl.BlockSpec((tm, d), lambda i: (i, 0)), pl.BlockSpec((1, d), lambda i: (0, 0))],
        out_specs=pl.BlockSpec((tm, d), lambda i: (i, 0)),
        out_shape=jax.ShapeDtypeStruct((n, d), F32),
        compiler_params=_params("parallel"),
        name="final_rms_norm",
    )(x, g.reshape(1, d))


def _rope_angles(pos, n_freq, theta):
    inv_freq = theta ** (-jnp.arange(n_freq, dtype=F32) / n_freq)
    return pos.astype(F32)[:, None] * inv_freq[None, :]


def kernel(x, norm_mix_g, w_in, gate_b, w_branch, w_out, rwkv_mu_rkv, rwkv_mu_x, rwkv_w0, rwkv_w1, rwkv_w2,
           rwkv_a0, rwkv_a1, rwkv_a2, rwkv_g1, rwkv_g2, rwkv_k_k, rwkv_k_a, rwkv_r_k, rwkv_ln_w, rwkv_ln_b,
           s5_a_re, s5_a_im, s5_log_dt, s5_b_re, s5_b_im, s5_c_re, s5_c_im, s5_d, s5_glu_w, s5_glu_b,
           gqa_q_norm, gqa_k_norm, norm_ffn_g, dense_w_gate, dense_w_up, dense_w_down,
           moe_router, moe_w_gate, moe_w_up, moe_w_down, final_norm_g):
    batch, seq, d = x.shape
    depth = w_in.shape[0]
    n = batch * seq
    t = jnp.arange(seq, dtype=jnp.int32)
    rope_tabs = _rotary_tables(_rope_angles(t, ROPE_DIMS // 2, ROPE_THETA), N_HEADS)
    ang_axial = jnp.concatenate([_rope_angles(t // GRID_W, HEAD_DIM // 4, AXIAL_THETA),
                                 _rope_angles(t % GRID_W, HEAD_DIM // 4, AXIAL_THETA)], axis=-1)
    axial_tabs = _rotary_tables(ang_axial, N_HEADS)
    x = x.reshape(n, d)
    for l in range(depth):
        w_small = w_in[l, :, :OFF_GATES].astype(BF16)
        w_gates = w_in[l, :, OFF_GATES:].astype(BF16)
        xn, xnb, z = rms_in_proj(x, norm_mix_g[l], w_small)
        zg = matmul_bf16(xnb, w_gates, tm=1024, tn=1024)
        qa, ka, va, qd, kd, vd = qkv_prep(z, batch, seq, rope_tabs, axial_tabs, gqa_q_norm[l], gqa_k_norm[l])
        ya = dilated_attention(qa, ka, va)
        yd = gqa_attention(qd, kd, vd)
        lw1 = jnp.concatenate([rwkv_w1[l], rwkv_a1[l]], axis=-1).astype(BF16)
        zeros = jnp.zeros_like(rwkv_w2[l])
        lw2 = jnp.concatenate([jnp.concatenate([rwkv_w2[l], zeros], axis=-1),
                               jnp.concatenate([zeros, rwkv_a2[l]], axis=-1)], axis=1).astype(BF16)
        w0a0 = jnp.concatenate([rwkv_w0[l], rwkv_a0[l]], axis=-1)
        fw, bw, bonus, rgate = rwkv_prep(xn, z, seq, rwkv_mu_x[l], rwkv_mu_rkv[l].reshape(2, IN_B), lw1, lw2, w0a0,
                                         rwkv_g1[l].astype(BF16), rwkv_g2[l].astype(BF16),
                                         rwkv_k_k[l], rwkv_k_a[l], rwkv_r_k[l])
        wkv_f, wkv_b = wkv_scan(fw, bw, batch, seq)
        s5_out = []
        for dr in range(2):
            disc = _s5_discretize(s5_a_re[l, dr], s5_a_im[l, dr], s5_log_dt[l, dr], s5_b_re[l], s5_b_im[l],
                                  s5_c_re[l, dr], s5_c_im[l, dr])
            s5_out.append(s5_scan(z, batch, seq, disc, reverse=(dr == 1)))
        x = merge_branches(x, ya, yd, wkv_f, wkv_b, bonus, rgate, s5_out[0], s5_out[1], z, zg, gate_b[l],
                           w_branch[l].astype(BF16), w_out[l].astype(BF16), rwkv_ln_w[l], rwkv_ln_b[l],
                           s5_d[l], s5_glu_w[l].astype(BF16), s5_glu_b[l], batch, seq)
        i = l // 2
        if l % 2 == 0:
            x = dense_ffn(x, norm_ffn_g[l], dense_w_gate[i].astype(BF16), dense_w_up[i].astype(BF16),
                          dense_w_down[i].astype(BF16))
        else:
            xnb_f, sel, wt = moe_route(x, norm_ffn_g[l], moe_router[i])
            x = moe_ffn(x, xnb_f, sel, wt, moe_w_gate[i].astype(BF16), moe_w_up[i].astype(BF16),
                        moe_w_down[i].astype(BF16))
    return rms_norm(x, final_norm_g).reshape(batch, seq, d)
```

```python
import functools
import math

import jax
import jax.numpy as jnp
from jax import lax
from jax.experimental import pallas as pl
from jax.experimental.pallas import tpu as pltpu

F32 = jnp.float32
BF16 = jnp.bfloat16

D_MODEL = 1024
HEAD_DIM = 64
BRANCH_WIDTH = 256
N_BRANCHES = 4
N_HEADS = BRANCH_WIDTH // HEAD_DIM
DILATED_PATTERNS = ((128, 1), (512, 4), (2048, 16))
ROPE_THETA = 500000.0
ROPE_DIMS = HEAD_DIM // 4
RWKV_GN_EPS = 64e-5
S5_GROUP_CH = 16
S5_GROUPS = BRANCH_WIDTH // S5_GROUP_CH
S5_STATE = 64
GQA_KV_HEADS = 2
AXIAL_THETA = 10000.0
GRID_W = 64
N_EXPERTS = 8
TOP_K = 2
NORM_EPS = 1e-6
NEG_INF = -1e30

IN_A = 3 * BRANCH_WIDTH
IN_B = 3 * BRANCH_WIDTH
IN_C = BRANCH_WIDTH
IN_DQ = BRANCH_WIDTH
IN_DKV = GQA_KV_HEADS * HEAD_DIM
IN_GATES = N_BRANCHES * D_MODEL
OFF_B = IN_A
OFF_C = OFF_B + IN_B
OFF_DQ = OFF_C + IN_C
OFF_DKV = OFF_DQ + IN_DQ
OFF_GATES = OFF_DKV + 2 * IN_DKV
IN_TOTAL = OFF_GATES + IN_GATES

VMEM_LIMIT_BYTES = 56 * 1024 * 1024


def _params(*semantics):
    return pltpu.CompilerParams(dimension_semantics=semantics, vmem_limit_bytes=VMEM_LIMIT_BYTES)


def _bdot(a, b):
    return jnp.dot(a.astype(BF16), b.astype(BF16), preferred_element_type=F32)


def _bdot_nt(a, b):
    return lax.dot_general(a.astype(BF16), b.astype(BF16), (((1,), (1,)), ((), ())),
                           preferred_element_type=F32)


def _rms(x, g):
    return x * lax.rsqrt(jnp.mean(x * x, axis=-1, keepdims=True) + NORM_EPS) * g


def _rms_in_proj_kernel(x_ref, g_ref, w_ref, xn_ref, xnb_ref, z_ref):
    @pl.when(pl.program_id(1) == 0)
    def _():
        y = _rms(x_ref[...], g_ref[...])
        xn_ref[...] = y
        xnb_ref[...] = y.astype(BF16)

    z_ref[...] = jnp.dot(xnb_ref[...], w_ref[...], preferred_element_type=F32)


def rms_in_proj(x, g, w_bf16, tm=1024, tn=768):
    n, d = x.shape
    nout = w_bf16.shape[1]
    tm = min(tm, n)
    return pl.pallas_call(
        _rms_in_proj_kernel,
        grid=(n // tm, nout // tn),
        in_specs=[pl.BlockSpec((tm, d), lambda i, j: (i, 0)),
                  pl.BlockSpec((1, d), lambda i, j: (0, 0)),
                  pl.BlockSpec((d, tn), lambda i, j: (0, j))],
        out_specs=[pl.BlockSpec((tm, d), lambda i, j: (i, 0)),
                   pl.BlockSpec((tm, d), lambda i, j: (i, 0)),
                   pl.BlockSpec((tm, tn), lambda i, j: (i, j))],
        out_shape=[jax.ShapeDtypeStruct((n, d), F32), jax.ShapeDtypeStruct((n, d), BF16),
                   jax.ShapeDtypeStruct((n, nout), F32)],
        compiler_params=_params("parallel", "arbitrary"),
        name="rms_in_proj",
    )(x, g.reshape(1, d), w_bf16)


def _rotary_tables(pos_angles, n_heads):
    s, n = pos_angles.shape
    pad = HEAD_DIM - 2 * n
    cos = jnp.concatenate([jnp.cos(pos_angles), jnp.cos(pos_angles), jnp.ones((s, pad), F32)], axis=-1)
    zeros_n = jnp.zeros((s, n), F32)
    zeros_p = jnp.zeros((s, pad), F32)
    sin_lo = jnp.concatenate([-jnp.sin(pos_angles), zeros_n, zeros_p], axis=-1)
    sin_hi = jnp.concatenate([zeros_n, jnp.sin(pos_angles), zeros_p], axis=-1)
    return tuple(jnp.tile(t, (1, n_heads)) for t in (cos, sin_lo, sin_hi))


def _rotate(x, cos, sin_lo, sin_hi, n):
    width = x.shape[-1]
    from_above = pltpu.roll(x, width - n, 1)
    from_below = pltpu.roll(x, n, 1)
    return x * cos + from_above * sin_lo + from_below * sin_hi


def _head_sum(x, n_heads):
    lane = lax.broadcasted_iota(jnp.int32, x.shape, 1)
    out = jnp.zeros_like(x)
    for h in range(n_heads):
        in_head = (lane >= h * HEAD_DIM) & (lane < (h + 1) * HEAD_DIM)
        s = jnp.sum(jnp.where(in_head, x, 0.0), axis=-1, keepdims=True)
        out = jnp.where(in_head, s, out)
    return out


def _head_rms(x, g, n_heads):
    ms = _head_sum(x * x, n_heads) * (1.0 / HEAD_DIM)
    return x * lax.rsqrt(ms + NORM_EPS) * g


def _qkv_prep_kernel(za_ref, zq_ref, zkv_ref, rc_ref, rl_ref, rh_ref, ac_ref, al_ref, ah_ref,
                     qn_ref, kn_ref, qa_ref, ka_ref, va_ref, qd_ref, kd_ref, vd_ref):
    w = BRANCH_WIDTH
    n_rope = ROPE_DIMS // 2
    n_ax = HEAD_DIM // 2
    za = za_ref[...]
    rc, rl, rh = rc_ref[...], rl_ref[...], rh_ref[...]
    qa = _rotate(za[:, :w], rc, rl, rh, n_rope) * (HEAD_DIM ** -0.5)
    ka = _rotate(za[:, w:2 * w], rc, rl, rh, n_rope)
    va = za[:, 2 * w:]
    ac, al, ah = ac_ref[...], al_ref[...], ah_ref[...]
    qd = _rotate(_head_rms(zq_ref[...], qn_ref[...], N_HEADS), ac, al, ah, n_ax) * (HEAD_DIM ** -0.5)
    zkv = zkv_ref[...]
    kw = GQA_KV_HEADS * HEAD_DIM
    kd = _rotate(_head_rms(zkv[:, :kw], kn_ref[...], GQA_KV_HEADS), ac[:, :kw], al[:, :kw], ah[:, :kw], n_ax)
    vd = zkv[:, kw:]
    for h in range(N_HEADS):
        sl = slice(h * HEAD_DIM, (h + 1) * HEAD_DIM)
        qa_ref[0, h] = qa[:, sl].astype(BF16)
        ka_ref[0, h] = ka[:, sl].astype(BF16)
        va_ref[0, h] = va[:, sl].astype(BF16)
        qd_ref[0, h] = qd[:, sl].astype(BF16)
    for h in range(GQA_KV_HEADS):
        sl = slice(h * HEAD_DIM, (h + 1) * HEAD_DIM)
        kd_ref[0, h] = kd[:, sl].astype(BF16)
        vd_ref[0, h] = vd[:, sl].astype(BF16)


def qkv_prep(z, batch, seq, rope_tabs, axial_tabs, q_norm, k_norm, tm=512):
    tm = min(tm, seq)
    nt = seq // tm
    w = BRANCH_WIDTH
    row = lambda b, i: b * nt + i
    tab_spec = pl.BlockSpec((tm, w), lambda b, i: (i, 0))
    head_out = lambda nh: pl.BlockSpec((1, nh, tm, HEAD_DIM), lambda b, i: (b, 0, i, 0))
    head_shape = lambda nh: jax.ShapeDtypeStruct((batch, nh, seq, HEAD_DIM), BF16)
    return pl.pallas_call(
        _qkv_prep_kernel,
        grid=(batch, nt),
        in_specs=[pl.BlockSpec((tm, IN_A), lambda b, i: (row(b, i), 0)),
                  pl.BlockSpec((tm, w), lambda b, i: (row(b, i), OFF_DQ // w)),
                  pl.BlockSpec((tm, w), lambda b, i: (row(b, i), OFF_DKV // w)),
                  tab_spec, tab_spec, tab_spec, tab_spec, tab_spec, tab_spec,
                  pl.BlockSpec((1, w), lambda b, i: (0, 0)),
                  pl.BlockSpec((1, GQA_KV_HEADS * HEAD_DIM), lambda b, i: (0, 0))],
        out_specs=[head_out(N_HEADS), head_out(N_HEADS), head_out(N_HEADS),
                   head_out(N_HEADS), head_out(GQA_KV_HEADS), head_out(GQA_KV_HEADS)],
        out_shape=[head_shape(N_HEADS), head_shape(N_HEADS), head_shape(N_HEADS),
                   head_shape(N_HEADS), head_shape(GQA_KV_HEADS), head_shape(GQA_KV_HEADS)],
        compiler_params=_params("parallel", "parallel"),
        name="qkv_prep",
    )(z, z, z, *rope_tabs, *axial_tabs,
      jnp.tile(q_norm.reshape(1, HEAD_DIM), (1, N_HEADS)),
      jnp.tile(k_norm.reshape(1, HEAD_DIM), (1, GQA_KV_HEADS)))


A_TQ = 1024
A_SUB = 256
A_RADIUS = 64


def _dilated_windows():
    out = []
    for window, dil in DILATED_PATTERNS:
        halo = -(-(window // 2) // 128) * 128
        out.append((dil, -halo, A_SUB + 2 * halo))
    return out


def _dilated_bias():
    biases = []
    for dil, first, width in _dilated_windows():
        qi = jnp.arange(A_SUB, dtype=jnp.int32)[:, None]
        kj = jnp.arange(width, dtype=jnp.int32)[None, :] + first
        delta = kj - qi
        ok = (jnp.abs(delta) <= A_RADIUS * dil) & ((delta & (dil - 1)) == 0)
        biases.append(jnp.where(ok, 0.0, NEG_INF).astype(F32))
    return biases


def _dilated_attn_kernel(q_ref, kp_ref, kc_ref, kn_ref, vp_ref, vc_ref, vn_ref, b0_ref, b1_ref, b2_ref,
                         o_ref, k3_ref, v3_ref, *, seq, tq):
    i = pl.program_id(2)
    k3_ref[0:tq] = kp_ref[0, 0]
    k3_ref[tq:2 * tq] = kc_ref[0, 0]
    k3_ref[2 * tq:3 * tq] = kn_ref[0, 0]
    v3_ref[0:tq] = vp_ref[0, 0]
    v3_ref[tq:2 * tq] = vc_ref[0, 0]
    v3_ref[2 * tq:3 * tq] = vn_ref[0, 0]
    bias_refs = (b0_ref, b1_ref, b2_ref)
    windows = _dilated_windows()
    for u in range(tq // A_SUB):
        q = q_ref[0, 0, u * A_SUB:(u + 1) * A_SUB, :]
        scores = []
        for (dil, first, width), b_ref in zip(windows, bias_refs):
            start = tq + u * A_SUB + first
            s = _bdot_nt(q, k3_ref[start:start + width, :]) + b_ref[...]
            kpos = (i - 1) * tq + start + lax.broadcasted_iota(jnp.int32, (1, width), 1)
            s = jnp.where((kpos >= 0) & (kpos < seq), s, NEG_INF)
            scores.append((s, start, width))
        m = functools.reduce(jnp.maximum, [jnp.max(s, axis=-1, keepdims=True) for s, _, _ in scores])
        l = jnp.zeros_like(m)
        acc = jnp.zeros((A_SUB, HEAD_DIM), F32)
        for s, start, width in scores:
            p = jnp.exp(s - m)
            l = l + jnp.sum(p, axis=-1, keepdims=True)
            acc = acc + _bdot(p, v3_ref[start:start + width, :])
        o_ref[0, 0, u * A_SUB:(u + 1) * A_SUB, :] = acc / l


def dilated_attention(qa, ka, va):
    batch, nh, seq, hd = qa.shape
    tq = min(A_TQ, seq)
    assert tq == A_TQ, "key halo of radius * max dilation needs full query tiles"
    nt = seq // tq
    cur = pl.BlockSpec((1, 1, tq, hd), lambda b, h, i: (b, h, i, 0))
    prev = pl.BlockSpec((1, 1, tq, hd), lambda b, h, i: (b, h, jnp.maximum(i - 1, 0), 0))
    nxt = pl.BlockSpec((1, 1, tq, hd), lambda b, h, i: (b, h, jnp.minimum(i + 1, nt - 1), 0))
    biases = _dilated_bias()
    bias_specs = [pl.BlockSpec(b.shape, lambda b_, h, i: (0, 0)) for b in biases]
    return pl.pallas_call(
        functools.partial(_dilated_attn_kernel, seq=seq, tq=tq),
        grid=(batch, nh, nt),
        in_specs=[cur, prev, cur, nxt, prev, cur, nxt] + bias_specs,
        out_specs=cur,
        out_shape=jax.ShapeDtypeStruct((batch, nh, seq, hd), F32),
        scratch_shapes=[pltpu.VMEM((3 * tq, hd), BF16), pltpu.VMEM((3 * tq, hd), BF16)],
        compiler_params=_params("parallel", "parallel", "parallel"),
        name="dilated_attention",
    )(qa, ka, ka, ka, va, va, va, *biases)


GQA_SUB = 128


def _gqa_kernel(q_ref, k_ref, v_ref, o_ref, *scratch, rep, tq):
    j = pl.program_id(3)
    n_sub = tq // GQA_SUB
    blocks = [(r, u) for r in range(rep) for u in range(n_sub)]
    m_refs, l_refs, acc_refs = (scratch[i * len(blocks):(i + 1) * len(blocks)] for i in range(3))

    @pl.when(j == 0)
    def _():
        for m_ref, l_ref, acc_ref in zip(m_refs, l_refs, acc_refs):
            m_ref[...] = jnp.full(m_ref.shape, NEG_INF, F32)
            l_ref[...] = jnp.zeros(l_ref.shape, F32)
            acc_ref[...] = jnp.zeros(acc_ref.shape, F32)

    k = k_ref[0, 0]
    v = v_ref[0, 0]
    for (r, u), m_ref, l_ref, acc_ref in zip(blocks, m_refs, l_refs, acc_refs):
        s = _bdot_nt(k, q_ref[0, r, u * GQA_SUB:(u + 1) * GQA_SUB, :])
        m_prev = m_ref[...]
        m_new = jnp.maximum(m_prev, jnp.max(s, axis=0, keepdims=True))
        alpha = jnp.exp(m_prev - m_new)
        p = jnp.exp(s - m_new)
        l_ref[...] = alpha * l_ref[...] + jnp.sum(p, axis=0, keepdims=True)
        pv = lax.dot_general(v, p.astype(BF16), (((0,), (0,)), ((), ())), preferred_element_type=F32)
        acc_ref[...] = alpha * acc_ref[...] + pv
        m_ref[...] = m_new

    @pl.when(j == pl.num_programs(3) - 1)
    def _():
        for (r, u), l_ref, acc_ref in zip(blocks, l_refs, acc_refs):
            o_ref[0, r, :, u * GQA_SUB:(u + 1) * GQA_SUB] = acc_ref[...] / l_ref[...]


def gqa_attention(qd, kd, vd, tq=512, tk=1024):
    batch, nh, seq, hd = qd.shape
    ng = kd.shape[1]
    rep = nh // ng
    tq = min(tq, seq)
    tk = min(tk, seq)
    n_blocks = rep * (tq // GQA_SUB)
    return pl.pallas_call(
        functools.partial(_gqa_kernel, rep=rep, tq=tq),
        grid=(batch, ng, seq // tq, seq // tk),
        in_specs=[pl.BlockSpec((1, rep, tq, hd), lambda b, g, i, j: (b, g, i, 0)),
                  pl.BlockSpec((1, 1, tk, hd), lambda b, g, i, j: (b, g, j, 0)),
                  pl.BlockSpec((1, 1, tk, hd), lambda b, g, i, j: (b, g, j, 0))],
        out_specs=pl.BlockSpec((1, rep, hd, tq), lambda b, g, i, j: (b, g, 0, i)),
        out_shape=jax.ShapeDtypeStruct((batch, nh, hd, seq), F32),
        scratch_shapes=([pltpu.VMEM((1, GQA_SUB), F32)] * (2 * n_blocks)
                        + [pltpu.VMEM((hd, GQA_SUB), F32)] * n_blocks),
        compiler_params=_params("parallel", "parallel", "parallel", "arbitrary"),
        name="gqa_attention",
    )(qd, kd, vd)


def _matmul_kernel(a_ref, w_ref, o_ref):
    o_ref[...] = jnp.dot(a_ref[...], w_ref[...], preferred_element_type=F32)


def matmul_bf16(a, w, tm, tn):
    n, k = a.shape
    m = w.shape[1]
    tm = min(tm, n)
    return pl.pallas_call(
        _matmul_kernel,
        grid=(n // tm, m // tn),
        in_specs=[pl.BlockSpec((tm, k), lambda i, j: (i, 0)),
                  pl.BlockSpec((k, tn), lambda i, j: (0, j))],
        out_specs=pl.BlockSpec((tm, tn), lambda i, j: (i, j)),
        out_shape=jax.ShapeDtypeStruct((n, m), F32),
        compiler_params=_params("parallel", "arbitrary"),
        name="matmul_bf16",
    )(a, w)


def _sigmoid(x):
    return 1.0 / (1.0 + jnp.exp(-x))


def _softplus(x):
    return jnp.maximum(x, 0.0) + jnp.log(1.0 + jnp.exp(-jnp.abs(x)))


def _shift_rows(x, edge_row, down):
    rows = x.shape[0]
    ridx = lax.broadcasted_iota(jnp.int32, x.shape, 0)
    if down:
        return jnp.where(ridx == 0, edge_row, pltpu.roll(x, 1, 0))
    return jnp.where(ridx == rows - 1, edge_row, pltpu.roll(x, rows - 1, 0))


WKV_FIELDS = 6


def _rwkv_prep_kernel(xn_ref, xp_ref, xq_ref, zb_ref, zp_ref, zq_ref, mux_ref, murkv_ref, lw1_ref, lw2_ref,
                      w0a0_ref, g1_ref, g2_ref, kk_ref, ka_ref, rk_ref,
                      fw_ref, bw_ref, bonus_ref, gate_ref, *, tiles_per_seq):
    w = BRANCH_WIDTH
    i = pl.program_id(0)
    first = (i % tiles_per_seq) == 0
    last = (i % tiles_per_seq) == tiles_per_seq - 1
    xn = xn_ref[...]
    x_shift = (_shift_rows(xn, jnp.where(first, 0.0, xp_ref[7:8, :]), True),
               _shift_rows(xn, jnp.where(last, 0.0, xq_ref[0:1, :]), False))
    zb = zb_ref[...]
    z_prev = _shift_rows(zb, jnp.where(first, 0.0, zp_ref[7:8, :]), True)
    z_next = _shift_rows(zb, jnp.where(last, 0.0, zq_ref[0:1, :]), False)
    mu = murkv_ref[...]
    rkv = zb + mu[0:1] * (z_prev - zb) + mu[1:2] * (z_next - zb)
    r, k, v = rkv[:, :w], rkv[:, w:2 * w], rkv[:, 2 * w:]
    kap = k * kk_ref[...]
    kap = kap * lax.rsqrt(_head_sum(kap * kap, N_HEADS) + 1e-12)
    gate_ref[...] = _bdot(_sigmoid(_bdot(xn, g1_ref[...])), g2_ref[...])
    bonus = jnp.zeros_like(v)
    lora_lane = lax.broadcasted_iota(jnp.int32, (xn.shape[0], lw1_ref.shape[-1]), 1)
    for d, out_ref in enumerate((fw_ref, bw_ref)):
        xd = xn + mux_ref[d:d + 1, :] * (x_shift[d] - xn)
        h = _bdot(xd, lw1_ref[d])
        h = jnp.where(lora_lane < lw1_ref.shape[-1] // 2, jnp.tanh(h), h)
        h = _bdot(h, lw2_ref[d]) + w0a0_ref[d:d + 1, :]
        w_log = -_softplus(-h[:, :w]) - 0.5
        decay = jnp.exp(-jnp.exp(w_log))
        iclr = _sigmoid(h[:, w:])
        k_d = k * (1.0 + (iclr - 1.0) * ka_ref[...])
        bonus = bonus + _head_sum(r * k_d * rk_ref[...], N_HEADS) * v
        for j, field in enumerate((r, decay, k_d, v, kap, iclr * kap)):
            out_ref[:, j * w:(j + 1) * w] = field
    bonus_ref[...] = bonus


def rwkv_prep(xn, z, seq, mu_x, mu_rkv, lw1, lw2, w0a0, g1, g2, k_k, k_a, r_k, tm=512):
    n, d = xn.shape
    w = BRANCH_WIDTH
    tm = min(tm, seq)
    halo = 8
    prev_halo = lambda i: (jnp.maximum(i * (tm // halo) - 1, 0), 0)
    next_halo = lambda i: (jnp.minimum((i + 1) * (tm // halo), n // halo - 1), 0)
    full = lambda a: pl.BlockSpec(a.shape, lambda i: (0,) * a.ndim)
    params = (mu_x, mu_rkv, lw1, lw2, w0a0, g1, g2, k_k.reshape(1, w), k_a.reshape(1, w), r_k.reshape(1, w))
    rows = lambda width: pl.BlockSpec((tm, width), lambda i: (i, 0))
    return pl.pallas_call(
        functools.partial(_rwkv_prep_kernel, tiles_per_seq=seq // tm),
        grid=(n // tm,),
        in_specs=[rows(d), pl.BlockSpec((halo, d), prev_halo), pl.BlockSpec((halo, d), next_halo),
                  pl.BlockSpec((tm, IN_B), lambda i: (i, OFF_B // IN_B)),
                  pl.BlockSpec((halo, IN_B), lambda i: (prev_halo(i)[0], OFF_B // IN_B)),
                  pl.BlockSpec((halo, IN_B), lambda i: (next_halo(i)[0], OFF_B // IN_B))]
                 + [full(p) for p in params],
        out_specs=[rows(WKV_FIELDS * w), rows(WKV_FIELDS * w), rows(w), rows(w)],
        out_shape=[jax.ShapeDtypeStruct((n, WKV_FIELDS * w), F32), jax.ShapeDtypeStruct((n, WKV_FIELDS * w), F32),
                   jax.ShapeDtypeStruct((n, w), F32), jax.ShapeDtypeStruct((n, w), F32)],
        compiler_params=_params("parallel"),
        name="rwkv_prep",
    )(xn, xn, xn, z, z, z, *params)


WKV_UNROLL = 8


def _key_sum(state, kap):
    half = 2 * HEAD_DIM
    low = lax.broadcasted_iota(jnp.int32, (state.shape[0], half), 1) < HEAD_DIM
    out = []
    for c in range(state.shape[1] // half):
        sc = state[:, c * half:(c + 1) * half]
        kc = kap[:, c * half:(c + 1) * half]
        first = jnp.sum(sc * jnp.where(low[:1], kc, 0.0), axis=1, keepdims=True)
        second = jnp.sum(sc * jnp.where(low[:1], 0.0, kc), axis=1, keepdims=True)
        out.append(jnp.where(low, first, second))
    return jnp.concatenate(out, axis=1)


def _wkv_scan_kernel(fw_ref, bw_ref, mask_ref, yf_ref, yb_ref, state_ref, *, batch, chunk):
    @pl.when(pl.program_id(0) == 0)
    def _():
        state_ref[...] = jnp.zeros(state_ref.shape, F32)

    w = BRANCH_WIDTH
    u = WKV_UNROLL
    groups = chunk // u
    chains = [(bi, d) for bi in range(batch) for d in range(2)]

    def body(i, carry):
        bases = (pl.multiple_of(i * u, u), pl.multiple_of((groups - 1 - i) * u, u))
        refs = (fw_ref, bw_ref)
        mask = mask_ref[...]
        vk, r8, state, history = {}, {}, {}, {}
        for c in chains:
            bi, d = c
            rows = refs[d][bi, pl.ds(bases[d], u), :]
            r8[c], k8, v8 = rows[:, 0:w], rows[:, 2 * w:3 * w], rows[:, 3 * w:4 * w]
            v_heads = jnp.concatenate([v8[:, h * HEAD_DIM:(h + 1) * HEAD_DIM] for h in range(N_HEADS)], axis=0)
            vk[c] = lax.dot_general(v_heads.astype(BF16), (jnp.tile(k8, (N_HEADS, u)) * mask).astype(BF16),
                                    (((0,), (0,)), ((), ())), preferred_element_type=F32)
            state[c] = state_ref[2 * bi + d]
            history[c] = [None] * u
        for step in range(u):
            for c in chains:
                bi, d = c
                j = u - 1 - step if d == 1 else step
                row = refs[d][bi, pl.ds(bases[d] + j, 1), :]
                decay, kap, b = row[:, w:2 * w], row[:, 4 * w:5 * w], row[:, 5 * w:6 * w]
                state[c] = state[c] * decay - _key_sum(state[c], kap) * b + vk[c][:, j * w:(j + 1) * w]
                history[c][j] = state[c].astype(BF16)
        for c in chains:
            bi, d = c
            state_ref[2 * bi + d] = state[c]
            y_all = _bdot_nt(jnp.tile(r8[c], (N_HEADS, u)) * mask, jnp.concatenate(history[c], axis=1))
            y_ref = (yf_ref, yb_ref)[d]
            for h in range(N_HEADS):
                y_ref[bi, h, pl.ds(bases[d], u), :] = y_all[h * u:(h + 1) * u, :]
        return carry

    lax.fori_loop(0, groups, body, 0)


def wkv_scan(fw, bw, batch, seq, chunk=256):
    w = BRANCH_WIDTH
    u = WKV_UNROLL
    chunk = min(chunk, seq)
    nc = seq // chunk
    row_head, row_step = jnp.arange(N_HEADS * u) // u, jnp.arange(N_HEADS * u) % u
    lane_step, lane_head = jnp.arange(u * w) // w, (jnp.arange(u * w) % w) // HEAD_DIM
    mask = ((row_head[:, None] == lane_head[None, :]) & (row_step[:, None] == lane_step[None, :])).astype(F32)
    in_f = pl.BlockSpec((batch, chunk, WKV_FIELDS * w), lambda c: (0, c, 0))
    in_b = pl.BlockSpec((batch, chunk, WKV_FIELDS * w), lambda c: (0, nc - 1 - c, 0))
    out_shape = jax.ShapeDtypeStruct((batch, N_HEADS, seq, HEAD_DIM), F32)
    return pl.pallas_call(
        functools.partial(_wkv_scan_kernel, batch=batch, chunk=chunk),
        grid=(nc,),
        in_specs=[in_f, in_b, pl.BlockSpec(mask.shape, lambda c: (0, 0))],
        out_specs=[pl.BlockSpec((batch, N_HEADS, chunk, HEAD_DIM), lambda c: (0, 0, c, 0)),
                   pl.BlockSpec((batch, N_HEADS, chunk, HEAD_DIM), lambda c: (0, 0, nc - 1 - c, 0))],
        out_shape=[out_shape, out_shape],
        scratch_shapes=[pltpu.VMEM((2 * batch, HEAD_DIM, w), F32)],
        compiler_params=_params("arbitrary"),
        name="wkv_scan",
    )(fw.reshape(batch, seq, -1), bw.reshape(batch, seq, -1), mask)


S5_SEGMENTS = 8
S5_WIDTH = S5_GROUPS * S5_STATE


def _s5_discretize(a_re, a_im, log_dt, b_re, b_im, c_re, c_im):
    g, p, c = S5_GROUPS, S5_STATE, S5_GROUP_CH
    dt = jnp.exp(log_dt)[:, None]
    mag = jnp.exp(a_re * dt)
    bar_re, bar_im = mag * jnp.cos(a_im * dt), mag * jnp.sin(a_im * dt)
    den = a_re * a_re + a_im * a_im
    f_re = ((bar_re - 1.0) * a_re + bar_im * a_im) / den
    f_im = (bar_im * a_re - (bar_re - 1.0) * a_im) / den
    bb_re = f_re[..., None] * b_re - f_im[..., None] * b_im
    bb_im = f_re[..., None] * b_im + f_im[..., None] * b_re
    eye_g = jnp.eye(g, dtype=F32)
    w_in = jnp.concatenate(
        [jnp.einsum('gpc,gh->gchp', bb, eye_g).reshape(g * c, g * p) for bb in (bb_re, bb_im)], axis=1)
    w_out = jnp.concatenate(
        [jnp.einsum('gcp,gh->gphc', cc, eye_g).reshape(g * p, g * c) for cc in (c_re, -c_im)], axis=0)
    return bar_re.reshape(1, g * p), bar_im.reshape(1, g * p), w_in.astype(BF16), w_out.astype(BF16), dt, a_re, a_im


def _s5_powers(a_re, a_im, dt, count, reverse):
    j = jnp.arange(1, count + 1, dtype=F32)
    if reverse:
        j = j[::-1]
    e = j[:, None, None] * (a_re * dt)[None]
    th = j[:, None, None] * (a_im * dt)[None]
    mag = jnp.exp(e)
    return (mag * jnp.cos(th)).reshape(count, -1), (mag * jnp.sin(th)).reshape(count, -1)


def _s5_scan_kernel(u_ref, win_ref, lre_ref, lim_ref, pre_ref, pim_ref, wout_ref, y_ref,
                    xre_ref, xim_ref, cre_ref, cim_ref, *, reverse, chunk):
    seg = chunk // S5_SEGMENTS
    nw = S5_WIDTH

    @pl.when(pl.program_id(1) == 0)
    def _():
        cre_ref[...] = jnp.zeros(cre_ref.shape, F32)
        cim_ref[...] = jnp.zeros(cim_ref.shape, F32)

    bu = jnp.dot(u_ref[...].astype(BF16), win_ref[...], preferred_element_type=F32)
    xre_ref[...] = bu[:, :nw]
    xim_ref[...] = bu[:, nw:]
    lre = lre_ref[...]
    lim = lim_ref[...]
    group = lambda s: pl.ds(pl.multiple_of(s * S5_SEGMENTS, S5_SEGMENTS), S5_SEGMENTS)

    def local_step(s, carry):
        xr, xi = carry
        rows = group((seg - 1 - s) if reverse else s)
        nr = lre * xr - lim * xi + xre_ref[rows, :]
        ni = lre * xi + lim * xr + xim_ref[rows, :]
        xre_ref[rows, :] = nr
        xim_ref[rows, :] = ni
        return nr, ni

    zero = jnp.zeros((S5_SEGMENTS, nw), F32)
    fin_re, fin_im = lax.fori_loop(0, seg, local_step, (zero, zero))

    full_seg = 0 if reverse else seg - 1
    pl_re, pl_im = pre_ref[full_seg:full_seg + 1, :], pim_ref[full_seg:full_seg + 1, :]
    ir, ii = cre_ref[...], cim_ref[...]
    init_re, init_im = [None] * S5_SEGMENTS, [None] * S5_SEGMENTS
    for j in (range(S5_SEGMENTS - 1, -1, -1) if reverse else range(S5_SEGMENTS)):
        init_re[j], init_im[j] = ir, ii
        fr, fi = fin_re[j:j + 1, :], fin_im[j:j + 1, :]
        ir, ii = fr + (pl_re * ir - pl_im * ii), fi + (pl_re * ii + pl_im * ir)
    cre_ref[...] = ir
    cim_ref[...] = ii
    init_re = jnp.concatenate(init_re, axis=0)
    init_im = jnp.concatenate(init_im, axis=0)

    def correct(s, carry):
        rows = group(s)
        pr, pi = pre_ref[pl.ds(s, 1), :], pim_ref[pl.ds(s, 1), :]
        xre_ref[rows, :] = xre_ref[rows, :] + (pr * init_re - pi * init_im)
        xim_ref[rows, :] = xim_ref[rows, :] + (pr * init_im + pi * init_re)
        return carry

    lax.fori_loop(0, seg, correct, 0)
    y_ref[...] = (jnp.dot(xre_ref[...].astype(BF16), wout_ref[:nw, :], preferred_element_type=F32)
                  + jnp.dot(xim_ref[...].astype(BF16), wout_ref[nw:, :], preferred_element_type=F32))


def _segment_interleave(a, batch, seq, chunk, inverse=False):
    w = a.shape[-1]
    seg = chunk // S5_SEGMENTS
    shape = (batch, seq // chunk, seg, S5_SEGMENTS, w) if inverse else (batch, seq // chunk, S5_SEGMENTS, seg, w)
    return jnp.swapaxes(a.reshape(shape), 2, 3).reshape(batch * seq, w)


def s5_scan(u_interleaved, batch, seq, disc, reverse, chunk):
    lam_re, lam_im, w_in, w_out, dt, a_re, a_im = disc
    w = BRANCH_WIDTH
    nc = seq // chunk
    seg = chunk // S5_SEGMENTS
    pw_re, pw_im = _s5_powers(a_re, a_im, dt, seg, reverse)
    order = (lambda c: nc - 1 - c) if reverse else (lambda c: c)
    full = lambda a: pl.BlockSpec(a.shape, lambda b, c: (0, 0))
    return pl.pallas_call(
        functools.partial(_s5_scan_kernel, reverse=reverse, chunk=chunk),
        grid=(batch, nc),
        in_specs=[pl.BlockSpec((chunk, w), lambda b, c: (b * nc + order(c), 0)),
                  full(w_in), full(lam_re), full(lam_im), full(pw_re), full(pw_im), full(w_out)],
        out_specs=pl.BlockSpec((chunk, w), lambda b, c: (b * nc + order(c), 0)),
        out_shape=jax.ShapeDtypeStruct((batch * seq, w), F32),
        scratch_shapes=[pltpu.VMEM((chunk, S5_WIDTH), F32), pltpu.VMEM((chunk, S5_WIDTH), F32),
                        pltpu.VMEM((1, S5_WIDTH), F32), pltpu.VMEM((1, S5_WIDTH), F32)],
        compiler_params=_params("parallel", "arbitrary"),
        name="s5_scan_bwd" if reverse else "s5_scan_fwd",
    )(u_interleaved, w_in, lam_re, lam_im, pw_re, pw_im, w_out)


def s5_bidirectional(z, batch, seq, discs, chunk=1024):
    chunk = min(chunk, seq)
    u = _segment_interleave(z[:, OFF_C:OFF_C + BRANCH_WIDTH], batch, seq, chunk)
    return [_segment_interleave(s5_scan(u, batch, seq, disc, reverse=(d == 1), chunk=chunk),
                                batch, seq, chunk, inverse=True) for d, disc in enumerate(discs)]


def _gelu_tanh(y):
    return 0.5 * y * (1.0 + jnp.tanh(math.sqrt(2.0 / math.pi) * (y + 0.044715 * (y * y * y))))


def _merge_kernel(x_ref, ya_ref, yd_ref, wf_ref, wb_ref, bonus_ref, rg_ref, sf_ref, sb_ref, u_ref, zg_ref,
                  gb_ref, wbr_ref, wout_ref, lnw_ref, lnb_ref, s5d_ref, gluw_ref, glub_ref, o_ref):
    w = BRANCH_WIDTH
    ya = jnp.concatenate([ya_ref[0, h] for h in range(N_HEADS)], axis=1)
    normed = []
    for h in range(N_HEADS):
        ys = wf_ref[0, h] + wb_ref[0, h]
        cen = ys - jnp.mean(ys, axis=-1, keepdims=True)
        var = jnp.mean(cen * cen, axis=-1, keepdims=True)
        normed.append(cen * lax.rsqrt(var + RWKV_GN_EPS))
    yb = (jnp.concatenate(normed, axis=1) * lnw_ref[...] + lnb_ref[...] + bonus_ref[...]) * rg_ref[...]
    yc = sf_ref[...] + sb_ref[...] + s5d_ref[...] * u_ref[...]
    h = _bdot(_gelu_tanh(yc), gluw_ref[...]) + glub_ref[...]
    yc = h[:, :w] * _sigmoid(h[:, w:])
    proj_d = sum(lax.dot_general(yd_ref[0, h].astype(BF16), wbr_ref[3, h * HEAD_DIM:(h + 1) * HEAD_DIM, :],
                                 (((0,), (0,)), ((), ())), preferred_element_type=F32) for h in range(N_HEADS))
    merged = jnp.zeros(o_ref.shape, F32)
    for i, proj in enumerate((_bdot(ya, wbr_ref[0]), _bdot(yb, wbr_ref[1]), _bdot(yc, wbr_ref[2]), proj_d)):
        gate = _sigmoid(zg_ref[:, i * D_MODEL:(i + 1) * D_MODEL] + gb_ref[i:i + 1, :])
        merged = merged + gate * proj
    o_ref[...] = x_ref[...] + _bdot(merged, wout_ref[...])


def merge_branches(x, ya, yd, wkv_f, wkv_b, bonus, rgate, s5_f, s5_b, z, zg, gate_b, w_branch, w_out,
                   ln_w, ln_b, s5_d, glu_w, glu_b, batch, seq, tm=256):
    w = BRANCH_WIDTH
    d = D_MODEL
    tm = min(tm, seq)
    nt = seq // tm
    rows = lambda width, col=0: pl.BlockSpec((tm, width), lambda b, i: (b * nt + i, col))
    heads = pl.BlockSpec((1, N_HEADS, tm, HEAD_DIM), lambda b, i: (b, 0, i, 0))
    heads_t = pl.BlockSpec((1, N_HEADS, HEAD_DIM, tm), lambda b, i: (b, 0, 0, i))
    full = lambda a: pl.BlockSpec(a.shape, lambda b, i: (0,) * a.ndim)
    params = (gate_b, w_branch, w_out, ln_w.reshape(1, w), ln_b.reshape(1, w), s5_d.reshape(1, w),
              glu_w, glu_b.reshape(1, 2 * w))
    return pl.pallas_call(
        _merge_kernel,
        grid=(batch, nt),
        in_specs=[rows(d), heads, heads_t, heads, heads, rows(w), rows(w), rows(w), rows(w),
                  rows(w, OFF_C // w), rows(N_BRANCHES * d)] + [full(p) for p in params],
        out_specs=rows(d),
        out_shape=jax.ShapeDtypeStruct(x.shape, F32),
        compiler_params=_params("parallel", "parallel"),
        name="merge_branches",
    )(x, ya, yd, wkv_f, wkv_b, bonus, rgate, s5_f, s5_b, z, zg, *params)


def _silu(x):
    return x * _sigmoid(x)


def _dense_ffn_kernel(x_ref, g_ref, wg_ref, wu_ref, wd_ref, o_ref, xn_ref):
    @pl.when(pl.program_id(1) == 0)
    def _():
        x = x_ref[...]
        xn_ref[...] = _rms(x, g_ref[...]).astype(BF16)
        o_ref[...] = x

    xn = xn_ref[...]
    h = (_silu(jnp.dot(xn, wg_ref[...], preferred_element_type=F32))
         * jnp.dot(xn, wu_ref[...], preferred_element_type=F32))
    o_ref[...] += _bdot(h, wd_ref[...])


def dense_ffn(x, g, w_gate, w_up, w_down, tm=1024, tf=1408):
    n, d = x.shape
    ff = w_gate.shape[1]
    tm = min(tm, n)
    return pl.pallas_call(
        _dense_ffn_kernel,
        grid=(n // tm, ff // tf),
        in_specs=[pl.BlockSpec((tm, d), lambda i, f: (i, 0)),
                  pl.BlockSpec((1, d), lambda i, f: (0, 0)),
                  pl.BlockSpec((d, tf), lambda i, f: (0, f)),
                  pl.BlockSpec((d, tf), lambda i, f: (0, f)),
                  pl.BlockSpec((tf, d), lambda i, f: (f, 0))],
        out_specs=pl.BlockSpec((tm, d), lambda i, f: (i, 0)),
        out_shape=jax.ShapeDtypeStruct((n, d), F32),
        scratch_shapes=[pltpu.VMEM((tm, d), BF16)],
        compiler_params=_params("parallel", "arbitrary"),
        name="dense_ffn",
    )(x, g.reshape(1, d), w_gate, w_up, w_down)


def _router_kernel(x_ref, g_ref, rt_ref, xnb_ref, sel_ref, wt_ref):
    xn = _rms(x_ref[...], g_ref[...])
    xnb_ref[...] = xn.astype(BF16)
    logits = lax.dot_general(rt_ref[...], xn, (((1,), (1,)), ((), ())),
                             precision=lax.Precision.HIGHEST, preferred_element_type=F32)
    e = lax.broadcasted_iota(jnp.int32, logits.shape, 0)
    m1 = jnp.max(logits, axis=0, keepdims=True)
    i1 = jnp.min(jnp.where(logits == m1, e, N_EXPERTS), axis=0, keepdims=True)
    rest = jnp.where(e == i1, NEG_INF, logits)
    m2 = jnp.max(rest, axis=0, keepdims=True)
    i2 = jnp.min(jnp.where(rest == m2, e, N_EXPERTS), axis=0, keepdims=True)
    ratio = jnp.exp(m2 - m1)
    w1 = 1.0 / (1.0 + ratio)
    w2 = ratio / (1.0 + ratio)
    sel_ref[...] = jnp.where((e == i1) | (e == i2), 1.0, 0.0)
    wt_ref[...] = jnp.where(e == i1, w1, jnp.where(e == i2, w2, 0.0))


def moe_route(x, g, router, tm=1024):
    n, d = x.shape
    tm = min(tm, n)
    ne = router.shape[1]
    return pl.pallas_call(
        _router_kernel,
        grid=(n // tm,),
        in_specs=[pl.BlockSpec((tm, d), lambda i: (i, 0)),
                  pl.BlockSpec((1, d), lambda i: (0, 0)),
                  pl.BlockSpec((ne, d), lambda i: (0, 0))],
        out_specs=[pl.BlockSpec((tm, d), lambda i: (i, 0)),
                   pl.BlockSpec((ne, tm), lambda i: (0, i)),
                   pl.BlockSpec((ne, tm), lambda i: (0, i))],
        out_shape=[jax.ShapeDtypeStruct((n, d), BF16), jax.ShapeDtypeStruct((ne, n), F32),
                   jax.ShapeDtypeStruct((ne, n), F32)],
        compiler_params=_params("parallel"),
        name="moe_router",
    )(x, g.reshape(1, d), router.T)


MOE_ROWS = 128
MOE_STATIC_BLOCKS = (2, 3)


def _moe_kernel(x_ref, xnb_ref, sel_ref, wt_ref, wg_ref, wu_ref, wd_ref, o_ref,
                rank_ref, xg_ref, acc_ref, nblk_ref):
    e = pl.program_id(1)
    f = pl.program_id(2)
    nf = pl.num_programs(2)
    tm = x_ref.shape[0]

    @pl.when((e == 0) & (f == 0))
    def _():
        o_ref[...] = x_ref[...]
        before = (lax.broadcasted_iota(jnp.int32, (tm, tm), 0) < lax.broadcasted_iota(jnp.int32, (tm, tm), 1))
        rank_ref[...] = jnp.dot(sel_ref[...].astype(BF16), jnp.where(before, 1.0, 0.0).astype(BF16),
                                preferred_element_type=F32)

    sel_e = sel_ref[pl.ds(e, 1), :]
    rank_e = rank_ref[pl.ds(e, 1), :]
    wt_e = wt_ref[pl.ds(e, 1), :]

    @pl.when(f == 0)
    def _():
        count = jnp.sum(sel_e).astype(jnp.int32)
        nblk_ref[0] = (count + MOE_ROWS - 1) // MOE_ROWS

    nblk = nblk_ref[0]

    def process(rows):
        n_rows = rows.stop - rows.start if isinstance(rows, slice) else rows.size
        first = rows.start

        def one_hot():
            slot = (first + lax.broadcasted_iota(jnp.int32, (n_rows, tm), 0)).astype(F32)
            return jnp.where((rank_e == slot) & (sel_e > 0.0), 1.0, 0.0)

        @pl.when(f == 0)
        def _():
            xg_ref[rows, :] = jnp.dot(one_hot().astype(BF16), xnb_ref[...],
                                      preferred_element_type=F32).astype(BF16)
            acc_ref[rows, :] = jnp.zeros((n_rows, acc_ref.shape[1]), F32)

        xg = xg_ref[rows, :]
        h = (_silu(jnp.dot(xg, wg_ref[0], preferred_element_type=F32))
             * jnp.dot(xg, wu_ref[0], preferred_element_type=F32))
        acc_ref[rows, :] += _bdot(h, wd_ref[0])

        @pl.when(f == nf - 1)
        def _():
            hot = one_hot()
            row_w = jnp.sum(hot * wt_e, axis=1, keepdims=True)
            yw = (acc_ref[rows, :] * row_w).astype(BF16)
            o_ref[...] += lax.dot_general(hot.astype(BF16), yw, (((0,), (0,)), ((), ())),
                                          preferred_element_type=F32)

    for n_static in MOE_STATIC_BLOCKS:
        lo = 0 if n_static == MOE_STATIC_BLOCKS[0] else n_static
        hi = n_static if n_static != MOE_STATIC_BLOCKS[-1] else tm // MOE_ROWS
        pl.when((nblk >= lo) & (nblk <= hi))(functools.partial(process, slice(0, n_static * MOE_ROWS)))

    def tail(b, carry):
        process(pl.ds(pl.multiple_of(b * MOE_ROWS, MOE_ROWS), MOE_ROWS))
        return carry

    lax.fori_loop(MOE_STATIC_BLOCKS[-1], nblk, tail, 0)


def moe_ffn(x, xnb, sel, wt, w_gate, w_up, w_down, tm=1024, tf=512):
    n, d = x.shape
    ne, _, ff = w_gate.shape
    tm = min(tm, n)
    return pl.pallas_call(
        _moe_kernel,
        grid=(n // tm, ne, ff // tf),
        in_specs=[pl.BlockSpec((tm, d), lambda i, e, f: (i, 0)),
                  pl.BlockSpec((tm, d), lambda i, e, f: (i, 0)),
                  pl.BlockSpec((ne, tm), lambda i, e, f: (0, i)),
                  pl.BlockSpec((ne, tm), lambda i, e, f: (0, i)),
                  pl.BlockSpec((1, d, tf), lambda i, e, f: (e, 0, f)),
                  pl.BlockSpec((1, d, tf), lambda i, e, f: (e, 0, f)),
                  pl.BlockSpec((1, tf, d), lambda i, e, f: (e, f, 0))],
        out_specs=pl.BlockSpec((tm, d), lambda i, e, f: (i, 0)),
        out_shape=jax.ShapeDtypeStruct((n, d), F32),
        scratch_shapes=[pltpu.VMEM((ne, tm), F32), pltpu.VMEM((tm, d), BF16), pltpu.VMEM((tm, d), F32),
                        pltpu.SMEM((1,), jnp.int32)],
        compiler_params=_params("parallel", "arbitrary", "arbitrary"),
        name="moe_ffn",
    )(x, xnb, sel, wt, w_gate, w_up, w_down)


def _rms_kernel(x_ref, g_ref, o_ref):
    o_ref[...] = _rms(x_ref[...], g_ref[...])


def rms_norm(x, g, tm=1024):
    n, d = x.shape
    tm = min(tm, n)
    return pl.pallas_call(
        _rms_kernel,
        grid=(n // tm,),
        in_specs=[pl.BlockSpec((tm, d), lambda i: (i, 0)), pl.BlockSpec((1, d), lambda i: (0, 0))],
        out_specs=pl.BlockSpec((tm, d), lambda i: (i, 0)),
        out_shape=jax.ShapeDtypeStruct((n, d), F32),
        compiler_params=_params("parallel"),
        name="final_rms_norm",
    )(x, g.reshape(1, d))


def _rope_angles(pos, n_freq, theta):
    inv_freq = theta ** (-jnp.arange(n_freq, dtype=F32) / n_freq)
    return pos.astype(F32)[:, None] * inv_freq[None, :]


def kernel(x, norm_mix_g, w_in, gate_b, w_branch, w_out, rwkv_mu_rkv, rwkv_mu_x, rwkv_w0, rwkv_w1, rwkv_w2,
           rwkv_a0, rwkv_a1, rwkv_a2, rwkv_g1, rwkv_g2, rwkv_k_k, rwkv_k_a, rwkv_r_k, rwkv_ln_w, rwkv_ln_b,
           s5_a_re, s5_a_im, s5_log_dt, s5_b_re, s5_b_im, s5_c_re, s5_c_im, s5_d, s5_glu_w, s5_glu_b,
           gqa_q_norm, gqa_k_norm, norm_ffn_g, dense_w_gate, dense_w_up, dense_w_down,
           moe_router, moe_w_gate, moe_w_up, moe_w_down, final_norm_g):
    batch, seq, d = x.shape
    depth = w_in.shape[0]
    n = batch * seq
    t = jnp.arange(seq, dtype=jnp.int32)
    rope_tabs = _rotary_tables(_rope_angles(t, ROPE_DIMS // 2, ROPE_THETA), N_HEADS)
    ang_axial = jnp.concatenate([_rope_angles(t // GRID_W, HEAD_DIM // 4, AXIAL_THETA),
                                 _rope_angles(t % GRID_W, HEAD_DIM // 4, AXIAL_THETA)], axis=-1)
    axial_tabs = _rotary_tables(ang_axial, N_HEADS)
    x = x.reshape(n, d)
    for l in range(depth):
        w_small = w_in[l, :, :OFF_GATES].astype(BF16)
        w_gates = w_in[l, :, OFF_GATES:].astype(BF16)
        xn, xnb, z = rms_in_proj(x, norm_mix_g[l], w_small)
        zg = matmul_bf16(xnb, w_gates, tm=1024, tn=1024)
        qa, ka, va, qd, kd, vd = qkv_prep(z, batch, seq, rope_tabs, axial_tabs, gqa_q_norm[l], gqa_k_norm[l])
        ya = dilated_attention(qa, ka, va)
        yd = gqa_attention(qd, kd, vd)
        lw1 = jnp.concatenate([rwkv_w1[l], rwkv_a1[l]], axis=-1).astype(BF16)
        zeros = jnp.zeros_like(rwkv_w2[l])
        lw2 = jnp.concatenate([jnp.concatenate([rwkv_w2[l], zeros], axis=-1),
                               jnp.concatenate([zeros, rwkv_a2[l]], axis=-1)], axis=1).astype(BF16)
        w0a0 = jnp.concatenate([rwkv_w0[l], rwkv_a0[l]], axis=-1)
        fw, bw, bonus, rgate = rwkv_prep(xn, z, seq, rwkv_mu_x[l], rwkv_mu_rkv[l].reshape(2, IN_B), lw1, lw2, w0a0,
                                         rwkv_g1[l].astype(BF16), rwkv_g2[l].astype(BF16),
                                         rwkv_k_k[l], rwkv_k_a[l], rwkv_r_k[l])
        wkv_f, wkv_b = wkv_scan(fw, bw, batch, seq)
        discs = [_s5_discretize(s5_a_re[l, dr], s5_a_im[l, dr], s5_log_dt[l, dr], s5_b_re[l], s5_b_im[l],
                                s5_c_re[l, dr], s5_c_im[l, dr]) for dr in range(2)]
        s5_out = s5_bidirectional(z, batch, seq, discs)
        x = merge_branches(x, ya, yd, wkv_f, wkv_b, bonus, rgate, s5_out[0], s5_out[1], z, zg, gate_b[l],
                           w_branch[l].astype(BF16), w_out[l].astype(BF16), rwkv_ln_w[l], rwkv_ln_b[l],
                           s5_d[l], s5_glu_w[l].astype(BF16), s5_glu_b[l], batch, seq)
        i = l // 2
        if l % 2 == 0:
            x = dense_ffn(x, norm_ffn_g[l], dense_w_gate[i].astype(BF16), dense_w_up[i].astype(BF16),
                          dense_w_down[i].astype(BF16))
        else:
            xnb_f, sel, wt = moe_route(x, norm_ffn_g[l], moe_router[i])
            x = moe_ffn(x, xnb_f, sel, wt, moe_w_gate[i].astype(BF16), moe_w_up[i].astype(BF16),
                        moe_w_down[i].astype(BF16))
    return rms_norm(x, final_norm_g).reshape(batch, seq, d)
```

```python
import functools
import math

import jax
import jax.numpy as jnp
from jax import lax
from jax.experimental import pallas as pl
from jax.experimental.pallas import tpu as pltpu

F32 = jnp.float32
BF16 = jnp.bfloat16

D_MODEL = 1024
HEAD_DIM = 64
BRANCH_WIDTH = 256
N_BRANCHES = 4
N_HEADS = BRANCH_WIDTH // HEAD_DIM
DILATED_PATTERNS = ((128, 1), (512, 4), (2048, 16))
ROPE_THETA = 500000.0
ROPE_DIMS = HEAD_DIM // 4
RWKV_GN_EPS = 64e-5
S5_GROUP_CH = 16
S5_GROUPS = BRANCH_WIDTH // S5_GROUP_CH
S5_STATE = 64
GQA_KV_HEADS = 2
AXIAL_THETA = 10000.0
GRID_W = 64
N_EXPERTS = 8
TOP_K = 2
NORM_EPS = 1e-6
NEG_INF = -1e30

IN_A = 3 * BRANCH_WIDTH
IN_B = 3 * BRANCH_WIDTH
IN_C = BRANCH_WIDTH
IN_DQ = BRANCH_WIDTH
IN_DKV = GQA_KV_HEADS * HEAD_DIM
IN_GATES = N_BRANCHES * D_MODEL
OFF_B = IN_A
OFF_C = OFF_B + IN_B
OFF_DQ = OFF_C + IN_C
OFF_DKV = OFF_DQ + IN_DQ
OFF_GATES = OFF_DKV + 2 * IN_DKV
IN_TOTAL = OFF_GATES + IN_GATES

VMEM_LIMIT_BYTES = 56 * 1024 * 1024


def _params(*semantics):
    return pltpu.CompilerParams(dimension_semantics=semantics, vmem_limit_bytes=VMEM_LIMIT_BYTES)


def _bdot(a, b):
    return jnp.dot(a.astype(BF16), b.astype(BF16), preferred_element_type=F32)


def _bdot_nt(a, b):
    return lax.dot_general(a.astype(BF16), b.astype(BF16), (((1,), (1,)), ((), ())),
                           preferred_element_type=F32)


def _rms(x, g):
    return x * lax.rsqrt(jnp.mean(x * x, axis=-1, keepdims=True) + NORM_EPS) * g


def _rms_in_proj_kernel(x_ref, g_ref, w_ref, xn_ref, xnb_ref, z_ref):
    @pl.when(pl.program_id(1) == 0)
    def _():
        y = _rms(x_ref[...], g_ref[...])
        xn_ref[...] = y
        xnb_ref[...] = y.astype(BF16)

    z_ref[...] = jnp.dot(xnb_ref[...], w_ref[...], preferred_element_type=F32)


def rms_in_proj(x, g, w_bf16, tm=1024, tn=768):
    n, d = x.shape
    nout = w_bf16.shape[1]
    tm = min(tm, n)
    return pl.pallas_call(
        _rms_in_proj_kernel,
        grid=(n // tm, nout // tn),
        in_specs=[pl.BlockSpec((tm, d), lambda i, j: (i, 0)),
                  pl.BlockSpec((1, d), lambda i, j: (0, 0)),
                  pl.BlockSpec((d, tn), lambda i, j: (0, j))],
        out_specs=[pl.BlockSpec((tm, d), lambda i, j: (i, 0)),
                   pl.BlockSpec((tm, d), lambda i, j: (i, 0)),
                   pl.BlockSpec((tm, tn), lambda i, j: (i, j))],
        out_shape=[jax.ShapeDtypeStruct((n, d), F32), jax.ShapeDtypeStruct((n, d), BF16),
                   jax.ShapeDtypeStruct((n, nout), F32)],
        compiler_params=_params("parallel", "arbitrary"),
        name="rms_in_proj",
    )(x, g.reshape(1, d), w_bf16)


def _rotary_tables(pos_angles, n_heads):
    s, n = pos_angles.shape
    pad = HEAD_DIM - 2 * n
    cos = jnp.concatenate([jnp.cos(pos_angles), jnp.cos(pos_angles), jnp.ones((s, pad), F32)], axis=-1)
    zeros_n = jnp.zeros((s, n), F32)
    zeros_p = jnp.zeros((s, pad), F32)
    sin_lo = jnp.concatenate([-jnp.sin(pos_angles), zeros_n, zeros_p], axis=-1)
    sin_hi = jnp.concatenate([zeros_n, jnp.sin(pos_angles), zeros_p], axis=-1)
    return tuple(jnp.tile(t, (1, n_heads)) for t in (cos, sin_lo, sin_hi))


def _rotate(x, cos, sin_lo, sin_hi, n):
    width = x.shape[-1]
    from_above = pltpu.roll(x, width - n, 1)
    from_below = pltpu.roll(x, n, 1)
    return x * cos + from_above * sin_lo + from_below * sin_hi


def _head_sum(x, n_heads):
    lane = lax.broadcasted_iota(jnp.int32, x.shape, 1)
    out = jnp.zeros_like(x)
    for h in range(n_heads):
        in_head = (lane >= h * HEAD_DIM) & (lane < (h + 1) * HEAD_DIM)
        s = jnp.sum(jnp.where(in_head, x, 0.0), axis=-1, keepdims=True)
        out = jnp.where(in_head, s, out)
    return out


def _head_rms(x, g, n_heads):
    ms = _head_sum(x * x, n_heads) * (1.0 / HEAD_DIM)
    return x * lax.rsqrt(ms + NORM_EPS) * g


Q_SCALE = HEAD_DIM ** -0.5 * math.log2(math.e)


def _qkv_prep_kernel(za_ref, zq_ref, zkv_ref, rc_ref, rl_ref, rh_ref, ac_ref, al_ref, ah_ref,
                     qn_ref, kn_ref, qa_ref, ka_ref, va_ref, qd_ref, kd_ref, vd_ref):
    w = BRANCH_WIDTH
    n_rope = ROPE_DIMS // 2
    n_ax = HEAD_DIM // 2
    za = za_ref[...]
    rc, rl, rh = rc_ref[...], rl_ref[...], rh_ref[...]
    qa = _rotate(za[:, :w], rc, rl, rh, n_rope) * Q_SCALE
    ka = _rotate(za[:, w:2 * w], rc, rl, rh, n_rope)
    va = za[:, 2 * w:]
    ac, al, ah = ac_ref[...], al_ref[...], ah_ref[...]
    qd = _rotate(_head_rms(zq_ref[...], qn_ref[...], N_HEADS), ac, al, ah, n_ax) * Q_SCALE
    zkv = zkv_ref[...]
    kw = GQA_KV_HEADS * HEAD_DIM
    kd = _rotate(_head_rms(zkv[:, :kw], kn_ref[...], GQA_KV_HEADS), ac[:, :kw], al[:, :kw], ah[:, :kw], n_ax)
    vd = zkv[:, kw:]
    for h in range(N_HEADS):
        sl = slice(h * HEAD_DIM, (h + 1) * HEAD_DIM)
        qa_ref[0, h] = qa[:, sl].astype(BF16)
        ka_ref[0, h] = ka[:, sl].astype(BF16)
        va_ref[0, h] = va[:, sl].astype(BF16)
        qd_ref[0, h] = qd[:, sl].astype(BF16)
    for h in range(GQA_KV_HEADS):
        sl = slice(h * HEAD_DIM, (h + 1) * HEAD_DIM)
        kd_ref[0, h] = kd[:, sl].astype(BF16)
        vd_ref[0, h] = vd[:, sl].astype(BF16)


def qkv_prep(z, batch, seq, rope_tabs, axial_tabs, q_norm, k_norm, tm=512):
    tm = min(tm, seq)
    nt = seq // tm
    w = BRANCH_WIDTH
    row = lambda b, i: b * nt + i
    tab_spec = pl.BlockSpec((tm, w), lambda b, i: (i, 0))
    head_out = lambda nh: pl.BlockSpec((1, nh, tm, HEAD_DIM), lambda b, i: (b, 0, i, 0))
    head_shape = lambda nh: jax.ShapeDtypeStruct((batch, nh, seq, HEAD_DIM), BF16)
    return pl.pallas_call(
        _qkv_prep_kernel,
        grid=(batch, nt),
        in_specs=[pl.BlockSpec((tm, IN_A), lambda b, i: (row(b, i), 0)),
                  pl.BlockSpec((tm, w), lambda b, i: (row(b, i), OFF_DQ // w)),
                  pl.BlockSpec((tm, w), lambda b, i: (row(b, i), OFF_DKV // w)),
                  tab_spec, tab_spec, tab_spec, tab_spec, tab_spec, tab_spec,
                  pl.BlockSpec((1, w), lambda b, i: (0, 0)),
                  pl.BlockSpec((1, GQA_KV_HEADS * HEAD_DIM), lambda b, i: (0, 0))],
        out_specs=[head_out(N_HEADS), head_out(N_HEADS), head_out(N_HEADS),
                   head_out(N_HEADS), head_out(GQA_KV_HEADS), head_out(GQA_KV_HEADS)],
        out_shape=[head_shape(N_HEADS), head_shape(N_HEADS), head_shape(N_HEADS),
                   head_shape(N_HEADS), head_shape(GQA_KV_HEADS), head_shape(GQA_KV_HEADS)],
        compiler_params=_params("parallel", "parallel"),
        name="qkv_prep",
    )(z, z, z, *rope_tabs, *axial_tabs,
      jnp.tile(q_norm.reshape(1, HEAD_DIM), (1, N_HEADS)),
      jnp.tile(k_norm.reshape(1, HEAD_DIM), (1, GQA_KV_HEADS)))


A_TQ = 1024
A_SUB = 256
A_RADIUS = 64


def _dilated_windows():
    out = []
    for window, dil in DILATED_PATTERNS:
        halo = -(-(window // 2) // 128) * 128
        out.append((dil, -halo, A_SUB + 2 * halo))
    return out


def _dilated_bias():
    biases = []
    for dil, first, width in _dilated_windows():
        qi = jnp.arange(A_SUB, dtype=jnp.int32)[:, None]
        kj = jnp.arange(width, dtype=jnp.int32)[None, :] + first
        delta = kj - qi
        ok = (jnp.abs(delta) <= A_RADIUS * dil) & ((delta & (dil - 1)) == 0)
        biases.append(jnp.where(ok, 0.0, NEG_INF).astype(F32))
    return biases


def _dilated_attn_kernel(q_ref, kp_ref, kc_ref, kn_ref, vp_ref, vc_ref, vn_ref, b0_ref, b1_ref, b2_ref,
                         o_ref, k3_ref, v3_ref, *, seq, tq):
    i = pl.program_id(2)
    k3_ref[0:tq] = kp_ref[0, 0]
    k3_ref[tq:2 * tq] = kc_ref[0, 0]
    k3_ref[2 * tq:3 * tq] = kn_ref[0, 0]
    v3_ref[0:tq] = vp_ref[0, 0]
    v3_ref[tq:2 * tq] = vc_ref[0, 0]
    v3_ref[2 * tq:3 * tq] = vn_ref[0, 0]
    bias_refs = (b0_ref, b1_ref, b2_ref)
    windows = _dilated_windows()
    for u in range(tq // A_SUB):
        q = q_ref[0, 0, u * A_SUB:(u + 1) * A_SUB, :]
        scores = []
        for (dil, first, width), b_ref in zip(windows, bias_refs):
            start = tq + u * A_SUB + first
            s = _bdot_nt(q, k3_ref[start:start + width, :]) + b_ref[...]
            kpos = (i - 1) * tq + start + lax.broadcasted_iota(jnp.int32, (1, width), 1)
            s = jnp.where((kpos >= 0) & (kpos < seq), s, NEG_INF)
            scores.append((s, start, width))
        m = functools.reduce(jnp.maximum, [jnp.max(s, axis=-1, keepdims=True) for s, _, _ in scores])
        l = jnp.zeros_like(m)
        acc = jnp.zeros((A_SUB, HEAD_DIM), F32)
        for s, start, width in scores:
            p = jnp.exp2(s - m)
            l = l + jnp.sum(p, axis=-1, keepdims=True)
            acc = acc + _bdot(p, v3_ref[start:start + width, :])
        o_ref[0, 0, u * A_SUB:(u + 1) * A_SUB, :] = acc / l


def dilated_attention(qa, ka, va):
    batch, nh, seq, hd = qa.shape
    tq = min(A_TQ, seq)
    assert tq == A_TQ, "key halo of radius * max dilation needs full query tiles"
    nt = seq // tq
    cur = pl.BlockSpec((1, 1, tq, hd), lambda b, h, i: (b, h, i, 0))
    prev = pl.BlockSpec((1, 1, tq, hd), lambda b, h, i: (b, h, jnp.maximum(i - 1, 0), 0))
    nxt = pl.BlockSpec((1, 1, tq, hd), lambda b, h, i: (b, h, jnp.minimum(i + 1, nt - 1), 0))
    biases = _dilated_bias()
    bias_specs = [pl.BlockSpec(b.shape, lambda b_, h, i: (0, 0)) for b in biases]
    return pl.pallas_call(
        functools.partial(_dilated_attn_kernel, seq=seq, tq=tq),
        grid=(batch, nh, nt),
        in_specs=[cur, prev, cur, nxt, prev, cur, nxt] + bias_specs,
        out_specs=cur,
        out_shape=jax.ShapeDtypeStruct((batch, nh, seq, hd), F32),
        scratch_shapes=[pltpu.VMEM((3 * tq, hd), BF16), pltpu.VMEM((3 * tq, hd), BF16)],
        compiler_params=_params("parallel", "parallel", "parallel"),
        name="dilated_attention",
    )(qa, ka, ka, ka, va, va, va, *biases)


GQA_SUB = 128


def _gqa_kernel(q_ref, k_ref, v_ref, o_ref, *scratch, rep, tq):
    j = pl.program_id(3)
    n_sub = tq // GQA_SUB
    blocks = [(r, u) for r in range(rep) for u in range(n_sub)]
    m_refs, l_refs, acc_refs = (scratch[i * len(blocks):(i + 1) * len(blocks)] for i in range(3))

    @pl.when(j == 0)
    def _():
        for m_ref, l_ref, acc_ref in zip(m_refs, l_refs, acc_refs):
            m_ref[...] = jnp.full(m_ref.shape, NEG_INF, F32)
            l_ref[...] = jnp.zeros(l_ref.shape, F32)
            acc_ref[...] = jnp.zeros(acc_ref.shape, F32)

    k = k_ref[0, 0]
    v = v_ref[0, 0]
    scores = [_bdot_nt(k, q_ref[0, r, u * GQA_SUB:(u + 1) * GQA_SUB, :]) for r, u in blocks]
    for s, m_ref, l_ref, acc_ref in zip(scores, m_refs, l_refs, acc_refs):
        m_prev = m_ref[...]
        m_new = jnp.maximum(m_prev, jnp.max(s, axis=0, keepdims=True))
        alpha = jnp.exp2(m_prev - m_new)
        p = jnp.exp2(s - m_new)
        l_ref[...] = alpha * l_ref[...] + jnp.sum(p, axis=0, keepdims=True)
        pv = lax.dot_general(v, p.astype(BF16), (((0,), (0,)), ((), ())), preferred_element_type=F32)
        acc_ref[...] = alpha * acc_ref[...] + pv
        m_ref[...] = m_new

    @pl.when(j == pl.num_programs(3) - 1)
    def _():
        for (r, u), l_ref, acc_ref in zip(blocks, l_refs, acc_refs):
            o_ref[0, r, :, u * GQA_SUB:(u + 1) * GQA_SUB] = acc_ref[...] / l_ref[...]


def gqa_attention(qd, kd, vd, tq=512, tk=2048):
    batch, nh, seq, hd = qd.shape
    ng = kd.shape[1]
    rep = nh // ng
    tq = min(tq, seq)
    tk = min(tk, seq)
    n_blocks = rep * (tq // GQA_SUB)
    return pl.pallas_call(
        functools.partial(_gqa_kernel, rep=rep, tq=tq),
        grid=(batch, ng, seq // tq, seq // tk),
        in_specs=[pl.BlockSpec((1, rep, tq, hd), lambda b, g, i, j: (b, g, i, 0)),
                  pl.BlockSpec((1, 1, tk, hd), lambda b, g, i, j: (b, g, j, 0)),
                  pl.BlockSpec((1, 1, tk, hd), lambda b, g, i, j: (b, g, j, 0))],
        out_specs=pl.BlockSpec((1, rep, hd, tq), lambda b, g, i, j: (b, g, 0, i)),
        out_shape=jax.ShapeDtypeStruct((batch, nh, hd, seq), F32),
        scratch_shapes=([pltpu.VMEM((1, GQA_SUB), F32)] * (2 * n_blocks)
                        + [pltpu.VMEM((hd, GQA_SUB), F32)] * n_blocks),
        compiler_params=_params("parallel", "parallel", "parallel", "arbitrary"),
        name="gqa_attention",
    )(qd, kd, vd)


def _matmul_kernel(a_ref, w_ref, o_ref):
    o_ref[...] = jnp.dot(a_ref[...], w_ref[...], preferred_element_type=F32)


def matmul_bf16(a, w, tm, tn):
    n, k = a.shape
    m = w.shape[1]
    tm = min(tm, n)
    return pl.pallas_call(
        _matmul_kernel,
        grid=(n // tm, m // tn),
        in_specs=[pl.BlockSpec((tm, k), lambda i, j: (i, 0)),
                  pl.BlockSpec((k, tn), lambda i, j: (0, j))],
        out_specs=pl.BlockSpec((tm, tn), lambda i, j: (i, j)),
        out_shape=jax.ShapeDtypeStruct((n, m), F32),
        compiler_params=_params("parallel", "arbitrary"),
        name="matmul_bf16",
    )(a, w)


def _sigmoid(x):
    return 1.0 / (1.0 + jnp.exp(-x))


def _softplus(x):
    return jnp.maximum(x, 0.0) + jnp.log(1.0 + jnp.exp(-jnp.abs(x)))


def _shift_rows(x, edge_row, down):
    rows = x.shape[0]
    ridx = lax.broadcasted_iota(jnp.int32, x.shape, 0)
    if down:
        return jnp.where(ridx == 0, edge_row, pltpu.roll(x, 1, 0))
    return jnp.where(ridx == rows - 1, edge_row, pltpu.roll(x, rows - 1, 0))


WKV_FIELDS = 6


def _rwkv_prep_kernel(xn_ref, xp_ref, xq_ref, zb_ref, zp_ref, zq_ref, mux_ref, murkv_ref, lw1_ref, lw2_ref,
                      w0a0_ref, g1_ref, g2_ref, kk_ref, ka_ref, rk_ref,
                      fw_ref, bw_ref, bonus_ref, gate_ref, *, tiles_per_seq):
    w = BRANCH_WIDTH
    i = pl.program_id(0)
    first = (i % tiles_per_seq) == 0
    last = (i % tiles_per_seq) == tiles_per_seq - 1
    xn = xn_ref[...]
    x_shift = (_shift_rows(xn, jnp.where(first, 0.0, xp_ref[7:8, :]), True),
               _shift_rows(xn, jnp.where(last, 0.0, xq_ref[0:1, :]), False))
    zb = zb_ref[...]
    z_prev = _shift_rows(zb, jnp.where(first, 0.0, zp_ref[7:8, :]), True)
    z_next = _shift_rows(zb, jnp.where(last, 0.0, zq_ref[0:1, :]), False)
    mu = murkv_ref[...]
    rkv = zb + mu[0:1] * (z_prev - zb) + mu[1:2] * (z_next - zb)
    r, k, v = rkv[:, :w], rkv[:, w:2 * w], rkv[:, 2 * w:]
    kap = k * kk_ref[...]
    kap = kap * lax.rsqrt(_head_sum(kap * kap, N_HEADS) + 1e-12)
    gate_ref[...] = _bdot(_sigmoid(_bdot(xn, g1_ref[...])), g2_ref[...])
    bonus = jnp.zeros_like(v)
    lora_lane = lax.broadcasted_iota(jnp.int32, (xn.shape[0], lw1_ref.shape[-1]), 1)
    for d, out_ref in enumerate((fw_ref, bw_ref)):
        xd = xn + mux_ref[d:d + 1, :] * (x_shift[d] - xn)
        h = _bdot(xd, lw1_ref[d])
        h = jnp.where(lora_lane < lw1_ref.shape[-1] // 2, jnp.tanh(h), h)
        h = _bdot(h, lw2_ref[d]) + w0a0_ref[d:d + 1, :]
        w_log = -_softplus(-h[:, :w]) - 0.5
        decay = jnp.exp(-jnp.exp(w_log))
        iclr = _sigmoid(h[:, w:])
        k_d = k * (1.0 + (iclr - 1.0) * ka_ref[...])
        bonus = bonus + _head_sum(r * k_d * rk_ref[...], N_HEADS) * v
        for j, field in enumerate((r, decay, k_d, v, kap, iclr * kap)):
            out_ref[:, j * w:(j + 1) * w] = field
    bonus_ref[...] = bonus


def rwkv_prep(xn, z, seq, mu_x, mu_rkv, lw1, lw2, w0a0, g1, g2, k_k, k_a, r_k, tm=512):
    n, d = xn.shape
    w = BRANCH_WIDTH
    tm = min(tm, seq)
    halo = 8
    prev_halo = lambda i: (jnp.maximum(i * (tm // halo) - 1, 0), 0)
    next_halo = lambda i: (jnp.minimum((i + 1) * (tm // halo), n // halo - 1), 0)
    full = lambda a: pl.BlockSpec(a.shape, lambda i: (0,) * a.ndim)
    params = (mu_x, mu_rkv, lw1, lw2, w0a0, g1, g2, k_k.reshape(1, w), k_a.reshape(1, w), r_k.reshape(1, w))
    rows = lambda width: pl.BlockSpec((tm, width), lambda i: (i, 0))
    return pl.pallas_call(
        functools.partial(_rwkv_prep_kernel, tiles_per_seq=seq // tm),
        grid=(n // tm,),
        in_specs=[rows(d), pl.BlockSpec((halo, d), prev_halo), pl.BlockSpec((halo, d), next_halo),
                  pl.BlockSpec((tm, IN_B), lambda i: (i, OFF_B // IN_B)),
                  pl.BlockSpec((halo, IN_B), lambda i: (prev_halo(i)[0], OFF_B // IN_B)),
                  pl.BlockSpec((halo, IN_B), lambda i: (next_halo(i)[0], OFF_B // IN_B))]
                 + [full(p) for p in params],
        out_specs=[rows(WKV_FIELDS * w), rows(WKV_FIELDS * w), rows(w), rows(w)],
        out_shape=[jax.ShapeDtypeStruct((n, WKV_FIELDS * w), F32), jax.ShapeDtypeStruct((n, WKV_FIELDS * w), F32),
                   jax.ShapeDtypeStruct((n, w), F32), jax.ShapeDtypeStruct((n, w), F32)],
        compiler_params=_params("parallel"),
        name="rwkv_prep",
    )(xn, xn, xn, z, z, z, *params)


WKV_UNROLL = 8


def _key_sum(state, kap):
    half = 2 * HEAD_DIM
    low = lax.broadcasted_iota(jnp.int32, (state.shape[0], half), 1) < HEAD_DIM
    out = []
    for c in range(state.shape[1] // half):
        sc = state[:, c * half:(c + 1) * half]
        kc = kap[:, c * half:(c + 1) * half]
        first = jnp.sum(sc * jnp.where(low[:1], kc, 0.0), axis=1, keepdims=True)
        second = jnp.sum(sc * jnp.where(low[:1], 0.0, kc), axis=1, keepdims=True)
        out.append(jnp.where(low, first, second))
    return jnp.concatenate(out, axis=1)


def _wkv_scan_kernel(fw_ref, bw_ref, mask_ref, yf_ref, yb_ref, state_ref, *, batch, chunk):
    @pl.when(pl.program_id(0) == 0)
    def _():
        state_ref[...] = jnp.zeros(state_ref.shape, F32)

    w = BRANCH_WIDTH
    u = WKV_UNROLL
    groups = chunk // u
    chains = [(bi, d) for bi in range(batch) for d in range(2)]

    def body(i, carry):
        bases = (pl.multiple_of(i * u, u), pl.multiple_of((groups - 1 - i) * u, u))
        refs = (fw_ref, bw_ref)
        mask = mask_ref[...]
        vk, r8, state, history = {}, {}, {}, {}
        for c in chains:
            bi, d = c
            rows = refs[d][bi, pl.ds(bases[d], u), :]
            r8[c], k8, v8 = rows[:, 0:w], rows[:, 2 * w:3 * w], rows[:, 3 * w:4 * w]
            v_heads = jnp.concatenate([v8[:, h * HEAD_DIM:(h + 1) * HEAD_DIM] for h in range(N_HEADS)], axis=0)
            vk[c] = lax.dot_general(v_heads.astype(BF16), (jnp.tile(k8, (N_HEADS, u)) * mask).astype(BF16),
                                    (((0,), (0,)), ((), ())), preferred_element_type=F32)
            state[c] = state_ref[2 * bi + d]
            history[c] = [None] * u
        for step in range(u):
            for c in chains:
                bi, d = c
                j = u - 1 - step if d == 1 else step
                row = refs[d][bi, pl.ds(bases[d] + j, 1), :]
                decay, kap, b = row[:, w:2 * w], row[:, 4 * w:5 * w], row[:, 5 * w:6 * w]
                state[c] = state[c] * decay - _key_sum(state[c], kap) * b + vk[c][:, j * w:(j + 1) * w]
                history[c][j] = state[c].astype(BF16)
        for c in chains:
            bi, d = c
            state_ref[2 * bi + d] = state[c]
            y_all = _bdot_nt(jnp.tile(r8[c], (N_HEADS, u)) * mask, jnp.concatenate(history[c], axis=1))
            y_ref = (yf_ref, yb_ref)[d]
            for h in range(N_HEADS):
                y_ref[bi, h, pl.ds(bases[d], u), :] = y_all[h * u:(h + 1) * u, :]
        return carry

    lax.fori_loop(0, groups, body, 0)


def wkv_scan(fw, bw, batch, seq, chunk=256):
    w = BRANCH_WIDTH
    u = WKV_UNROLL
    chunk = min(chunk, seq)
    nc = seq // chunk
    row_head, row_step = jnp.arange(N_HEADS * u) // u, jnp.arange(N_HEADS * u) % u
    lane_step, lane_head = jnp.arange(u * w) // w, (jnp.arange(u * w) % w) // HEAD_DIM
    mask = ((row_head[:, None] == lane_head[None, :]) & (row_step[:, None] == lane_step[None, :])).astype(F32)
    in_f = pl.BlockSpec((batch, chunk, WKV_FIELDS * w), lambda c: (0, c, 0))
    in_b = pl.BlockSpec((batch, chunk, WKV_FIELDS * w), lambda c: (0, nc - 1 - c, 0))
    out_shape = jax.ShapeDtypeStruct((batch, N_HEADS, seq, HEAD_DIM), F32)
    return pl.pallas_call(
        functools.partial(_wkv_scan_kernel, batch=batch, chunk=chunk),
        grid=(nc,),
        in_specs=[in_f, in_b, pl.BlockSpec(mask.shape, lambda c: (0, 0))],
        out_specs=[pl.BlockSpec((batch, N_HEADS, chunk, HEAD_DIM), lambda c: (0, 0, c, 0)),
                   pl.BlockSpec((batch, N_HEADS, chunk, HEAD_DIM), lambda c: (0, 0, nc - 1 - c, 0))],
        out_shape=[out_shape, out_shape],
        scratch_shapes=[pltpu.VMEM((2 * batch, HEAD_DIM, w), F32)],
        compiler_params=_params("arbitrary"),
        name="wkv_scan",
    )(fw.reshape(batch, seq, -1), bw.reshape(batch, seq, -1), mask)


S5_SEGMENTS = 8
S5_WIDTH = S5_GROUPS * S5_STATE


def _s5_discretize(a_re, a_im, log_dt, b_re, b_im, c_re, c_im):
    g, p, c = S5_GROUPS, S5_STATE, S5_GROUP_CH
    dt = jnp.exp(log_dt)[:, None]
    mag = jnp.exp(a_re * dt)
    bar_re, bar_im = mag * jnp.cos(a_im * dt), mag * jnp.sin(a_im * dt)
    den = a_re * a_re + a_im * a_im
    f_re = ((bar_re - 1.0) * a_re + bar_im * a_im) / den
    f_im = (bar_im * a_re - (bar_re - 1.0) * a_im) / den
    bb_re = f_re[..., None] * b_re - f_im[..., None] * b_im
    bb_im = f_re[..., None] * b_im + f_im[..., None] * b_re
    eye_g = jnp.eye(g, dtype=F32)
    w_in = jnp.concatenate(
        [jnp.einsum('gpc,gh->gchp', bb, eye_g).reshape(g * c, g * p) for bb in (bb_re, bb_im)], axis=1)
    w_out = jnp.concatenate(
        [jnp.einsum('gcp,gh->gphc', cc, eye_g).reshape(g * p, g * c) for cc in (c_re, -c_im)], axis=0)
    return bar_re.reshape(1, g * p), bar_im.reshape(1, g * p), w_in.astype(BF16), w_out.astype(BF16), dt, a_re, a_im


def _s5_powers(a_re, a_im, dt, count, reverse):
    j = jnp.arange(1, count + 1, dtype=F32)
    if reverse:
        j = j[::-1]
    e = j[:, None, None] * (a_re * dt)[None]
    th = j[:, None, None] * (a_im * dt)[None]
    mag = jnp.exp(e)
    return (mag * jnp.cos(th)).reshape(count, -1), (mag * jnp.sin(th)).reshape(count, -1)


def _s5_scan_kernel(u_ref, win_ref, lre_ref, lim_ref, pre_ref, pim_ref, wout_ref, y_ref,
                    xre_ref, xim_ref, cre_ref, cim_ref, *, reverse, chunk):
    seg = chunk // S5_SEGMENTS
    nw = S5_WIDTH

    @pl.when(pl.program_id(1) == 0)
    def _():
        cre_ref[...] = jnp.zeros(cre_ref.shape, F32)
        cim_ref[...] = jnp.zeros(cim_ref.shape, F32)

    bu = jnp.dot(u_ref[...].astype(BF16), win_ref[...], preferred_element_type=F32)
    xre_ref[...] = bu[:, :nw]
    xim_ref[...] = bu[:, nw:]
    lre = lre_ref[...]
    lim = lim_ref[...]
    group = lambda s: pl.ds(pl.multiple_of(s * S5_SEGMENTS, S5_SEGMENTS), S5_SEGMENTS)

    def local_step(s, carry):
        xr, xi = carry
        rows = group((seg - 1 - s) if reverse else s)
        nr = lre * xr - lim * xi + xre_ref[rows, :]
        ni = lre * xi + lim * xr + xim_ref[rows, :]
        xre_ref[rows, :] = nr
        xim_ref[rows, :] = ni
        return nr, ni

    zero = jnp.zeros((S5_SEGMENTS, nw), F32)
    fin_re, fin_im = lax.fori_loop(0, seg, local_step, (zero, zero))

    full_seg = 0 if reverse else seg - 1
    pl_re, pl_im = pre_ref[full_seg:full_seg + 1, :], pim_ref[full_seg:full_seg + 1, :]
    ir, ii = cre_ref[...], cim_ref[...]
    init_re, init_im = [None] * S5_SEGMENTS, [None] * S5_SEGMENTS
    for j in (range(S5_SEGMENTS - 1, -1, -1) if reverse else range(S5_SEGMENTS)):
        init_re[j], init_im[j] = ir, ii
        fr, fi = fin_re[j:j + 1, :], fin_im[j:j + 1, :]
        ir, ii = fr + (pl_re * ir - pl_im * ii), fi + (pl_re * ii + pl_im * ir)
    cre_ref[...] = ir
    cim_ref[...] = ii
    init_re = jnp.concatenate(init_re, axis=0)
    init_im = jnp.concatenate(init_im, axis=0)

    def correct(s, carry):
        rows = group(s)
        pr, pi = pre_ref[pl.ds(s, 1), :], pim_ref[pl.ds(s, 1), :]
        xre_ref[rows, :] = xre_ref[rows, :] + (pr * init_re - pi * init_im)
        xim_ref[rows, :] = xim_ref[rows, :] + (pr * init_im + pi * init_re)
        return carry

    lax.fori_loop(0, seg, correct, 0)
    y_ref[...] = (jnp.dot(xre_ref[...].astype(BF16), wout_ref[:nw, :], preferred_element_type=F32)
                  + jnp.dot(xim_ref[...].astype(BF16), wout_ref[nw:, :], preferred_element_type=F32))


def _segment_interleave(a, batch, seq, chunk, inverse=False):
    w = a.shape[-1]
    seg = chunk // S5_SEGMENTS
    shape = (batch, seq // chunk, seg, S5_SEGMENTS, w) if inverse else (batch, seq // chunk, S5_SEGMENTS, seg, w)
    return jnp.swapaxes(a.reshape(shape), 2, 3).reshape(batch * seq, w)


def s5_scan(u_interleaved, batch, seq, disc, reverse, chunk):
    lam_re, lam_im, w_in, w_out, dt, a_re, a_im = disc
    w = BRANCH_WIDTH
    nc = seq // chunk
    seg = chunk // S5_SEGMENTS
    pw_re, pw_im = _s5_powers(a_re, a_im, dt, seg, reverse)
    order = (lambda c: nc - 1 - c) if reverse else (lambda c: c)
    full = lambda a: pl.BlockSpec(a.shape, lambda b, c: (0, 0))
    return pl.pallas_call(
        functools.partial(_s5_scan_kernel, reverse=reverse, chunk=chunk),
        grid=(batch, nc),
        in_specs=[pl.BlockSpec((chunk, w), lambda b, c: (b * nc + order(c), 0)),
                  full(w_in), full(lam_re), full(lam_im), full(pw_re), full(pw_im), full(w_out)],
        out_specs=pl.BlockSpec((chunk, w), lambda b, c: (b * nc + order(c), 0)),
        out_shape=jax.ShapeDtypeStruct((batch * seq, w), F32),
        scratch_shapes=[pltpu.VMEM((chunk, S5_WIDTH), F32), pltpu.VMEM((chunk, S5_WIDTH), F32),
                        pltpu.VMEM((1, S5_WIDTH), F32), pltpu.VMEM((1, S5_WIDTH), F32)],
        compiler_params=_params("parallel", "arbitrary"),
        name="s5_scan_bwd" if reverse else "s5_scan_fwd",
    )(u_interleaved, w_in, lam_re, lam_im, pw_re, pw_im, w_out)


def s5_bidirectional(z, batch, seq, discs, chunk=1024):
    chunk = min(chunk, seq)
    u = _segment_interleave(z[:, OFF_C:OFF_C + BRANCH_WIDTH], batch, seq, chunk)
    return [_segment_interleave(s5_scan(u, batch, seq, disc, reverse=(d == 1), chunk=chunk),
                                batch, seq, chunk, inverse=True) for d, disc in enumerate(discs)]


def _gelu_tanh(y):
    return 0.5 * y * (1.0 + jnp.tanh(math.sqrt(2.0 / math.pi) * (y + 0.044715 * (y * y * y))))


def _merge_kernel(x_ref, ya_ref, yd_ref, wf_ref, wb_ref, bonus_ref, rg_ref, sf_ref, sb_ref, u_ref, zg_ref,
                  gb_ref, wbr_ref, wout_ref, lnw_ref, lnb_ref, s5d_ref, gluw_ref, glub_ref, o_ref):
    w = BRANCH_WIDTH
    ya = jnp.concatenate([ya_ref[0, h] for h in range(N_HEADS)], axis=1)
    normed = []
    for h in range(N_HEADS):
        ys = wf_ref[0, h] + wb_ref[0, h]
        cen = ys - jnp.mean(ys, axis=-1, keepdims=True)
        var = jnp.mean(cen * cen, axis=-1, keepdims=True)
        normed.append(cen * lax.rsqrt(var + RWKV_GN_EPS))
    yb = (jnp.concatenate(normed, axis=1) * lnw_ref[...] + lnb_ref[...] + bonus_ref[...]) * rg_ref[...]
    yc = sf_ref[...] + sb_ref[...] + s5d_ref[...] * u_ref[...]
    h = _bdot(_gelu_tanh(yc), gluw_ref[...]) + glub_ref[...]
    yc = h[:, :w] * _sigmoid(h[:, w:])
    proj_d = sum(lax.dot_general(yd_ref[0, h].astype(BF16), wbr_ref[3, h * HEAD_DIM:(h + 1) * HEAD_DIM, :],
                                 (((0,), (0,)), ((), ())), preferred_element_type=F32) for h in range(N_HEADS))
    merged = jnp.zeros(o_ref.shape, F32)
    for i, proj in enumerate((_bdot(ya, wbr_ref[0]), _bdot(yb, wbr_ref[1]), _bdot(yc, wbr_ref[2]), proj_d)):
        gate = _sigmoid(zg_ref[:, i * D_MODEL:(i + 1) * D_MODEL] + gb_ref[i:i + 1, :])
        merged = merged + gate * proj
    o_ref[...] = x_ref[...] + _bdot(merged, wout_ref[...])


def merge_branches(x, ya, yd, wkv_f, wkv_b, bonus, rgate, s5_f, s5_b, z, zg, gate_b, w_branch, w_out,
                   ln_w, ln_b, s5_d, glu_w, glu_b, batch, seq, tm=256):
    w = BRANCH_WIDTH
    d = D_MODEL
    tm = min(tm, seq)
    nt = seq // tm
    rows = lambda width, col=0: pl.BlockSpec((tm, width), lambda b, i: (b * nt + i, col))
    heads = pl.BlockSpec((1, N_HEADS, tm, HEAD_DIM), lambda b, i: (b, 0, i, 0))
    heads_t = pl.BlockSpec((1, N_HEADS, HEAD_DIM, tm), lambda b, i: (b, 0, 0, i))
    full = lambda a: pl.BlockSpec(a.shape, lambda b, i: (0,) * a.ndim)
    params = (gate_b, w_branch, w_out, ln_w.reshape(1, w), ln_b.reshape(1, w), s5_d.reshape(1, w),
              glu_w, glu_b.reshape(1, 2 * w))
    return pl.pallas_call(
        _merge_kernel,
        grid=(batch, nt),
        in_specs=[rows(d), heads, heads_t, heads, heads, rows(w), rows(w), rows(w), rows(w),
                  rows(w, OFF_C // w), rows(N_BRANCHES * d)] + [full(p) for p in params],
        out_specs=rows(d),
        out_shape=jax.ShapeDtypeStruct(x.shape, F32),
        compiler_params=_params("parallel", "parallel"),
        name="merge_branches",
    )(x, ya, yd, wkv_f, wkv_b, bonus, rgate, s5_f, s5_b, z, zg, *params)


def _silu(x):
    return x * _sigmoid(x)


def _dense_ffn_kernel(x_ref, g_ref, wg_ref, wu_ref, wd_ref, o_ref, xn_ref):
    @pl.when(pl.program_id(1) == 0)
    def _():
        x = x_ref[...]
        xn_ref[...] = _rms(x, g_ref[...]).astype(BF16)
        o_ref[...] = x

    xn = xn_ref[...]
    h = (_silu(jnp.dot(xn, wg_ref[...], preferred_element_type=F32))
         * jnp.dot(xn, wu_ref[...], preferred_element_type=F32))
    o_ref[...] += _bdot(h, wd_ref[...])


def dense_ffn(x, g, w_gate, w_up, w_down, tm=1024, tf=1408):
    n, d = x.shape
    ff = w_gate.shape[1]
    tm = min(tm, n)
    return pl.pallas_call(
        _dense_ffn_kernel,
        grid=(n // tm, ff // tf),
        in_specs=[pl.BlockSpec((tm, d), lambda i, f: (i, 0)),
                  pl.BlockSpec((1, d), lambda i, f: (0, 0)),
                  pl.BlockSpec((d, tf), lambda i, f: (0, f)),
                  pl.BlockSpec((d, tf), lambda i, f: (0, f)),
                  pl.BlockSpec((tf, d), lambda i, f: (f, 0))],
        out_specs=pl.BlockSpec((tm, d), lambda i, f: (i, 0)),
        out_shape=jax.ShapeDtypeStruct((n, d), F32),
        scratch_shapes=[pltpu.VMEM((tm, d), BF16)],
        compiler_params=_params("parallel", "arbitrary"),
        name="dense_ffn",
    )(x, g.reshape(1, d), w_gate, w_up, w_down)


def _router_kernel(x_ref, g_ref, rt_ref, xnb_ref, sel_ref, wt_ref):
    xn = _rms(x_ref[...], g_ref[...])
    xnb_ref[...] = xn.astype(BF16)
    logits = lax.dot_general(rt_ref[...], xn, (((1,), (1,)), ((), ())),
                             precision=lax.Precision.HIGHEST, preferred_element_type=F32)
    e = lax.broadcasted_iota(jnp.int32, logits.shape, 0)
    m1 = jnp.max(logits, axis=0, keepdims=True)
    i1 = jnp.min(jnp.where(logits == m1, e, N_EXPERTS), axis=0, keepdims=True)
    rest = jnp.where(e == i1, NEG_INF, logits)
    m2 = jnp.max(rest, axis=0, keepdims=True)
    i2 = jnp.min(jnp.where(rest == m2, e, N_EXPERTS), axis=0, keepdims=True)
    ratio = jnp.exp(m2 - m1)
    w1 = 1.0 / (1.0 + ratio)
    w2 = ratio / (1.0 + ratio)
    sel_ref[...] = jnp.where((e == i1) | (e == i2), 1.0, 0.0)
    wt_ref[...] = jnp.where(e == i1, w1, jnp.where(e == i2, w2, 0.0))


def moe_route(x, g, router, tm=1024):
    n, d = x.shape
    tm = min(tm, n)
    ne = router.shape[1]
    return pl.pallas_call(
        _router_kernel,
        grid=(n // tm,),
        in_specs=[pl.BlockSpec((tm, d), lambda i: (i, 0)),
                  pl.BlockSpec((1, d), lambda i: (0, 0)),
                  pl.BlockSpec((ne, d), lambda i: (0, 0))],
        out_specs=[pl.BlockSpec((tm, d), lambda i: (i, 0)),
                   pl.BlockSpec((ne, tm), lambda i: (0, i)),
                   pl.BlockSpec((ne, tm), lambda i: (0, i))],
        out_shape=[jax.ShapeDtypeStruct((n, d), BF16), jax.ShapeDtypeStruct((ne, n), F32),
                   jax.ShapeDtypeStruct((ne, n), F32)],
        compiler_params=_params("parallel"),
        name="moe_router",
    )(x, g.reshape(1, d), router.T)


MOE_ROWS = 128
MOE_STATIC_BLOCKS = (2, 3)


def _moe_kernel(x_ref, xnb_ref, sel_ref, wt_ref, wg_ref, wu_ref, wd_ref, o_ref,
                rank_ref, xg_ref, acc_ref, nblk_ref):
    e = pl.program_id(1)
    f = pl.program_id(2)
    nf = pl.num_programs(2)
    tm = x_ref.shape[0]

    @pl.when((e == 0) & (f == 0))
    def _():
        o_ref[...] = x_ref[...]
        before = (lax.broadcasted_iota(jnp.int32, (tm, tm), 0) < lax.broadcasted_iota(jnp.int32, (tm, tm), 1))
        rank_ref[...] = jnp.dot(sel_ref[...].astype(BF16), jnp.where(before, 1.0, 0.0).astype(BF16),
                                preferred_element_type=F32)

    sel_e = sel_ref[pl.ds(e, 1), :]
    rank_e = rank_ref[pl.ds(e, 1), :]
    wt_e = wt_ref[pl.ds(e, 1), :]

    @pl.when(f == 0)
    def _():
        count = jnp.sum(sel_e).astype(jnp.int32)
        nblk_ref[0] = (count + MOE_ROWS - 1) // MOE_ROWS

    nblk = nblk_ref[0]

    def process(rows):
        n_rows = rows.stop - rows.start if isinstance(rows, slice) else rows.size
        first = rows.start

        def one_hot():
            slot = (first + lax.broadcasted_iota(jnp.int32, (n_rows, tm), 0)).astype(F32)
            return jnp.where((rank_e == slot) & (sel_e > 0.0), 1.0, 0.0)

        @pl.when(f == 0)
        def _():
            xg_ref[rows, :] = jnp.dot(one_hot().astype(BF16), xnb_ref[...],
                                      preferred_element_type=F32).astype(BF16)
            acc_ref[rows, :] = jnp.zeros((n_rows, acc_ref.shape[1]), F32)

        xg = xg_ref[rows, :]
        h = (_silu(jnp.dot(xg, wg_ref[0], preferred_element_type=F32))
             * jnp.dot(xg, wu_ref[0], preferred_element_type=F32))
        acc_ref[rows, :] += _bdot(h, wd_ref[0])

        @pl.when(f == nf - 1)
        def _():
            hot = one_hot()
            row_w = jnp.sum(hot * wt_e, axis=1, keepdims=True)
            yw = (acc_ref[rows, :] * row_w).astype(BF16)
            o_ref[...] += lax.dot_general(hot.astype(BF16), yw, (((0,), (0,)), ((), ())),
                                          preferred_element_type=F32)

    for n_static in MOE_STATIC_BLOCKS:
        lo = 0 if n_static == MOE_STATIC_BLOCKS[0] else n_static
        hi = n_static if n_static != MOE_STATIC_BLOCKS[-1] else tm // MOE_ROWS
        pl.when((nblk >= lo) & (nblk <= hi))(functools.partial(process, slice(0, n_static * MOE_ROWS)))

    def tail(b, carry):
        process(pl.ds(pl.multiple_of(b * MOE_ROWS, MOE_ROWS), MOE_ROWS))
        return carry

    lax.fori_loop(MOE_STATIC_BLOCKS[-1], nblk, tail, 0)


def moe_ffn(x, xnb, sel, wt, w_gate, w_up, w_down, tm=1024, tf=512):
    n, d = x.shape
    ne, _, ff = w_gate.shape
    tm = min(tm, n)
    return pl.pallas_call(
        _moe_kernel,
        grid=(n // tm, ne, ff // tf),
        in_specs=[pl.BlockSpec((tm, d), lambda i, e, f: (i, 0)),
                  pl.BlockSpec((tm, d), lambda i, e, f: (i, 0)),
                  pl.BlockSpec((ne, tm), lambda i, e, f: (0, i)),
                  pl.BlockSpec((ne, tm), lambda i, e, f: (0, i)),
                  pl.BlockSpec((1, d, tf), lambda i, e, f: (e, 0, f)),
                  pl.BlockSpec((1, d, tf), lambda i, e, f: (e, 0, f)),
                  pl.BlockSpec((1, tf, d), lambda i, e, f: (e, f, 0))],
        out_specs=pl.BlockSpec((tm, d), lambda i, e, f: (i, 0)),
        out_shape=jax.ShapeDtypeStruct((n, d), F32),
        scratch_shapes=[pltpu.VMEM((ne, tm), F32), pltpu.VMEM((tm, d), BF16), pltpu.VMEM((tm, d), F32),
                        pltpu.SMEM((1,), jnp.int32)],
        compiler_params=_params("parallel", "arbitrary", "arbitrary"),
        name="moe_ffn",
    )(x, xnb, sel, wt, w_gate, w_up, w_down)


def _rms_kernel(x_ref, g_ref, o_ref):
    o_ref[...] = _rms(x_ref[...], g_ref[...])


def rms_norm(x, g, tm=1024):
    n, d = x.shape
    tm = min(tm, n)
    return pl.pallas_call(
        _rms_kernel,
        grid=(n // tm,),
        in_specs=[pl.BlockSpec((tm, d), lambda i: (i, 0)), pl.BlockSpec((1, d), lambda i: (0, 0))],
        out_specs=pl.BlockSpec((tm, d), lambda i: (i, 0)),
        out_shape=jax.ShapeDtypeStruct((n, d), F32),
        compiler_params=_params("parallel"),
        name="final_rms_norm",
    )(x, g.reshape(1, d))


def _rope_angles(pos, n_freq, theta):
    inv_freq = theta ** (-jnp.arange(n_freq, dtype=F32) / n_freq)
    return pos.astype(F32)[:, None] * inv_freq[None, :]


def kernel(x, norm_mix_g, w_in, gate_b, w_branch, w_out, rwkv_mu_rkv, rwkv_mu_x, rwkv_w0, rwkv_w1, rwkv_w2,
           rwkv_a0, rwkv_a1, rwkv_a2, rwkv_g1, rwkv_g2, rwkv_k_k, rwkv_k_a, rwkv_r_k, rwkv_ln_w, rwkv_ln_b,
           s5_a_re, s5_a_im, s5_log_dt, s5_b_re, s5_b_im, s5_c_re, s5_c_im, s5_d, s5_glu_w, s5_glu_b,
           gqa_q_norm, gqa_k_norm, norm_ffn_g, dense_w_gate, dense_w_up, dense_w_down,
           moe_router, moe_w_gate, moe_w_up, moe_w_down, final_norm_g):
    batch, seq, d = x.shape
    depth = w_in.shape[0]
    n = batch * seq
    t = jnp.arange(seq, dtype=jnp.int32)
    rope_tabs = _rotary_tables(_rope_angles(t, ROPE_DIMS // 2, ROPE_THETA), N_HEADS)
    ang_axial = jnp.concatenate([_rope_angles(t // GRID_W, HEAD_DIM // 4, AXIAL_THETA),
                                 _rope_angles(t % GRID_W, HEAD_DIM // 4, AXIAL_THETA)], axis=-1)
    axial_tabs = _rotary_tables(ang_axial, N_HEADS)
    x = x.reshape(n, d)
    for l in range(depth):
        w_small = w_in[l, :, :OFF_GATES].astype(BF16)
        w_gates = w_in[l, :, OFF_GATES:].astype(BF16)
        xn, xnb, z = rms_in_proj(x, norm_mix_g[l], w_small)
        zg = matmul_bf16(xnb, w_gates, tm=1024, tn=1024)
        qa, ka, va, qd, kd, vd = qkv_prep(z, batch, seq, rope_tabs, axial_tabs, gqa_q_norm[l], gqa_k_norm[l])
        ya = dilated_attention(qa, ka, va)
        yd = gqa_attention(qd, kd, vd)
        lw1 = jnp.concatenate([rwkv_w1[l], rwkv_a1[l]], axis=-1).astype(BF16)
        zeros = jnp.zeros_like(rwkv_w2[l])
        lw2 = jnp.concatenate([jnp.concatenate([rwkv_w2[l], zeros], axis=-1),
                               jnp.concatenate([zeros, rwkv_a2[l]], axis=-1)], axis=1).astype(BF16)
        w0a0 = jnp.concatenate([rwkv_w0[l], rwkv_a0[l]], axis=-1)
        fw, bw, bonus, rgate = rwkv_prep(xn, z, seq, rwkv_mu_x[l], rwkv_mu_rkv[l].reshape(2, IN_B), lw1, lw2, w0a0,
                                         rwkv_g1[l].astype(BF16), rwkv_g2[l].astype(BF16),
                                         rwkv_k_k[l], rwkv_k_a[l], rwkv_r_k[l])
        wkv_f, wkv_b = wkv_scan(fw, bw, batch, seq)
        discs = [_s5_discretize(s5_a_re[l, dr], s5_a_im[l, dr], s5_log_dt[l, dr], s5_b_re[l], s5_b_im[l],
                                s5_c_re[l, dr], s5_c_im[l, dr]) for dr in range(2)]
        s5_out = s5_bidirectional(z, batch, seq, discs)
        x = merge_branches(x, ya, yd, wkv_f, wkv_b, bonus, rgate, s5_out[0], s5_out[1], z, zg, gate_b[l],
                           w_branch[l].astype(BF16), w_out[l].astype(BF16), rwkv_ln_w[l], rwkv_ln_b[l],
                           s5_d[l], s5_glu_w[l].astype(BF16), s5_glu_b[l], batch, seq)
        i = l // 2
        if l % 2 == 0:
            x = dense_ffn(x, norm_ffn_g[l], dense_w_gate[i].astype(BF16), dense_w_up[i].astype(BF16),
                          dense_w_down[i].astype(BF16))
        else:
            xnb_f, sel, wt = moe_route(x, norm_ffn_g[l], moe_router[i])
            x = moe_ffn(x, xnb_f, sel, wt, moe_w_gate[i].astype(BF16), moe_w_up[i].astype(BF16),
                        moe_w_down[i].astype(BF16))
    return rms_norm(x, final_norm_g).reshape(batch, seq, d)
```

```python
import functools
import math

import jax
import jax.numpy as jnp
from jax import lax
from jax.experimental import pallas as pl
from jax.experimental.pallas import tpu as pltpu

F32 = jnp.float32
BF16 = jnp.bfloat16

D_MODEL = 1024
HEAD_DIM = 64
BRANCH_WIDTH = 256
N_BRANCHES = 4
N_HEADS = BRANCH_WIDTH // HEAD_DIM
DILATED_PATTERNS = ((128, 1), (512, 4), (2048, 16))
ROPE_THETA = 500000.0
ROPE_DIMS = HEAD_DIM // 4
RWKV_GN_EPS = 64e-5
S5_GROUP_CH = 16
S5_GROUPS = BRANCH_WIDTH // S5_GROUP_CH
S5_STATE = 64
GQA_KV_HEADS = 2
AXIAL_THETA = 10000.0
GRID_W = 64
N_EXPERTS = 8
TOP_K = 2
NORM_EPS = 1e-6
NEG_INF = -1e30

IN_A = 3 * BRANCH_WIDTH
IN_B = 3 * BRANCH_WIDTH
IN_C = BRANCH_WIDTH
IN_DQ = BRANCH_WIDTH
IN_DKV = GQA_KV_HEADS * HEAD_DIM
IN_GATES = N_BRANCHES * D_MODEL
OFF_B = IN_A
OFF_C = OFF_B + IN_B
OFF_DQ = OFF_C + IN_C
OFF_DKV = OFF_DQ + IN_DQ
OFF_GATES = OFF_DKV + 2 * IN_DKV
IN_TOTAL = OFF_GATES + IN_GATES

VMEM_LIMIT_BYTES = 56 * 1024 * 1024


def _params(*semantics):
    return pltpu.CompilerParams(dimension_semantics=semantics, vmem_limit_bytes=VMEM_LIMIT_BYTES)


def _bdot(a, b):
    return jnp.dot(a.astype(BF16), b.astype(BF16), preferred_element_type=F32)


def _bdot_nt(a, b):
    return lax.dot_general(a.astype(BF16), b.astype(BF16), (((1,), (1,)), ((), ())),
                           preferred_element_type=F32)


def _rms(x, g):
    return x * lax.rsqrt(jnp.mean(x * x, axis=-1, keepdims=True) + NORM_EPS) * g


def _rms_in_proj_kernel(x_ref, g_ref, w_ref, xn_ref, xnb_ref, z_ref):
    @pl.when(pl.program_id(1) == 0)
    def _():
        y = _rms(x_ref[...], g_ref[...])
        xn_ref[...] = y
        xnb_ref[...] = y.astype(BF16)

    z_ref[...] = jnp.dot(xnb_ref[...], w_ref[...], preferred_element_type=F32)


def rms_in_proj(x, g, w_bf16, tm=1024, tn=768):
    n, d = x.shape
    nout = w_bf16.shape[1]
    tm = min(tm, n)
    return pl.pallas_call(
        _rms_in_proj_kernel,
        grid=(n // tm, nout // tn),
        in_specs=[pl.BlockSpec((tm, d), lambda i, j: (i, 0)),
                  pl.BlockSpec((1, d), lambda i, j: (0, 0)),
                  pl.BlockSpec((d, tn), lambda i, j: (0, j))],
        out_specs=[pl.BlockSpec((tm, d), lambda i, j: (i, 0)),
                   pl.BlockSpec((tm, d), lambda i, j: (i, 0)),
                   pl.BlockSpec((tm, tn), lambda i, j: (i, j))],
        out_shape=[jax.ShapeDtypeStruct((n, d), F32), jax.ShapeDtypeStruct((n, d), BF16),
                   jax.ShapeDtypeStruct((n, nout), F32)],
        compiler_params=_params("parallel", "arbitrary"),
        name="rms_in_proj",
    )(x, g.reshape(1, d), w_bf16)


def _rotary_tables(pos_angles, n_heads):
    s, n = pos_angles.shape
    pad = HEAD_DIM - 2 * n
    cos = jnp.concatenate([jnp.cos(pos_angles), jnp.cos(pos_angles), jnp.ones((s, pad), F32)], axis=-1)
    zeros_n = jnp.zeros((s, n), F32)
    zeros_p = jnp.zeros((s, pad), F32)
    sin_lo = jnp.concatenate([-jnp.sin(pos_angles), zeros_n, zeros_p], axis=-1)
    sin_hi = jnp.concatenate([zeros_n, jnp.sin(pos_angles), zeros_p], axis=-1)
    return tuple(jnp.tile(t, (1, n_heads)) for t in (cos, sin_lo, sin_hi))


def _rotate(x, cos, sin_lo, sin_hi, n):
    width = x.shape[-1]
    from_above = pltpu.roll(x, width - n, 1)
    from_below = pltpu.roll(x, n, 1)
    return x * cos + from_above * sin_lo + from_below * sin_hi


def _head_sum(x, n_heads):
    lane = lax.broadcasted_iota(jnp.int32, x.shape, 1)
    out = jnp.zeros_like(x)
    for h in range(n_heads):
        in_head = (lane >= h * HEAD_DIM) & (lane < (h + 1) * HEAD_DIM)
        s = jnp.sum(jnp.where(in_head, x, 0.0), axis=-1, keepdims=True)
        out = jnp.where(in_head, s, out)
    return out


def _head_rms(x, g, n_heads):
    ms = _head_sum(x * x, n_heads) * (1.0 / HEAD_DIM)
    return x * lax.rsqrt(ms + NORM_EPS) * g


Q_SCALE = HEAD_DIM ** -0.5 * math.log2(math.e)


def _qkv_prep_kernel(za_ref, zq_ref, zkv_ref, rc_ref, rl_ref, rh_ref, ac_ref, al_ref, ah_ref,
                     qn_ref, kn_ref, qa_ref, ka_ref, va_ref, qd_ref, kd_ref, vd_ref):
    w = BRANCH_WIDTH
    n_rope = ROPE_DIMS // 2
    n_ax = HEAD_DIM // 2
    za = za_ref[...]
    rc, rl, rh = rc_ref[...], rl_ref[...], rh_ref[...]
    qa = _rotate(za[:, :w], rc, rl, rh, n_rope) * Q_SCALE
    ka = _rotate(za[:, w:2 * w], rc, rl, rh, n_rope)
    va = za[:, 2 * w:]
    ac, al, ah = ac_ref[...], al_ref[...], ah_ref[...]
    qd = _rotate(_head_rms(zq_ref[...], qn_ref[...], N_HEADS), ac, al, ah, n_ax) * Q_SCALE
    zkv = zkv_ref[...]
    kw = GQA_KV_HEADS * HEAD_DIM
    kd = _rotate(_head_rms(zkv[:, :kw], kn_ref[...], GQA_KV_HEADS), ac[:, :kw], al[:, :kw], ah[:, :kw], n_ax)
    vd = zkv[:, kw:]
    for h in range(N_HEADS):
        sl = slice(h * HEAD_DIM, (h + 1) * HEAD_DIM)
        qa_ref[0, h] = qa[:, sl].astype(BF16)
        ka_ref[0, h] = ka[:, sl].astype(BF16)
        va_ref[0, h] = va[:, sl].astype(BF16)
        qd_ref[0, h] = qd[:, sl].astype(BF16)
    for h in range(GQA_KV_HEADS):
        sl = slice(h * HEAD_DIM, (h + 1) * HEAD_DIM)
        kd_ref[0, h] = kd[:, sl].astype(BF16)
        vd_ref[0, h] = vd[:, sl].astype(BF16)


def qkv_prep(z, batch, seq, rope_tabs, axial_tabs, q_norm, k_norm, tm=512):
    tm = min(tm, seq)
    nt = seq // tm
    w = BRANCH_WIDTH
    row = lambda b, i: b * nt + i
    tab_spec = pl.BlockSpec((tm, w), lambda b, i: (i, 0))
    head_out = lambda nh: pl.BlockSpec((1, nh, tm, HEAD_DIM), lambda b, i: (b, 0, i, 0))
    head_shape = lambda nh: jax.ShapeDtypeStruct((batch, nh, seq, HEAD_DIM), BF16)
    return pl.pallas_call(
        _qkv_prep_kernel,
        grid=(batch, nt),
        in_specs=[pl.BlockSpec((tm, IN_A), lambda b, i: (row(b, i), 0)),
                  pl.BlockSpec((tm, w), lambda b, i: (row(b, i), OFF_DQ // w)),
                  pl.BlockSpec((tm, w), lambda b, i: (row(b, i), OFF_DKV // w)),
                  tab_spec, tab_spec, tab_spec, tab_spec, tab_spec, tab_spec,
                  pl.BlockSpec((1, w), lambda b, i: (0, 0)),
                  pl.BlockSpec((1, GQA_KV_HEADS * HEAD_DIM), lambda b, i: (0, 0))],
        out_specs=[head_out(N_HEADS), head_out(N_HEADS), head_out(N_HEADS),
                   head_out(N_HEADS), head_out(GQA_KV_HEADS), head_out(GQA_KV_HEADS)],
        out_shape=[head_shape(N_HEADS), head_shape(N_HEADS), head_shape(N_HEADS),
                   head_shape(N_HEADS), head_shape(GQA_KV_HEADS), head_shape(GQA_KV_HEADS)],
        compiler_params=_params("parallel", "parallel"),
        name="qkv_prep",
    )(z, z, z, *rope_tabs, *axial_tabs,
      jnp.tile(q_norm.reshape(1, HEAD_DIM), (1, N_HEADS)),
      jnp.tile(k_norm.reshape(1, HEAD_DIM), (1, GQA_KV_HEADS)))


A_TQ = 1024
A_SUB = 256
A_RADIUS = 64


def _dilated_windows():
    out = []
    for window, dil in DILATED_PATTERNS:
        halo = -(-(window // 2) // 128) * 128
        out.append((dil, -halo, A_SUB + 2 * halo))
    return out


def _dilated_bias():
    biases = []
    for dil, first, width in _dilated_windows():
        qi = jnp.arange(A_SUB, dtype=jnp.int32)[:, None]
        kj = jnp.arange(width, dtype=jnp.int32)[None, :] + first
        delta = kj - qi
        ok = (jnp.abs(delta) <= A_RADIUS * dil) & ((delta & (dil - 1)) == 0)
        biases.append(jnp.where(ok, 0.0, NEG_INF).astype(F32))
    return biases


def _dilated_attn_kernel(q_ref, kp_ref, kc_ref, kn_ref, vp_ref, vc_ref, vn_ref, b0_ref, b1_ref, b2_ref,
                         o_ref, k3_ref, v3_ref, *, seq, tq):
    i = pl.program_id(2)
    k3_ref[0:tq] = kp_ref[0, 0]
    k3_ref[tq:2 * tq] = kc_ref[0, 0]
    k3_ref[2 * tq:3 * tq] = kn_ref[0, 0]
    v3_ref[0:tq] = vp_ref[0, 0]
    v3_ref[tq:2 * tq] = vc_ref[0, 0]
    v3_ref[2 * tq:3 * tq] = vn_ref[0, 0]
    bias_refs = (b0_ref, b1_ref, b2_ref)
    windows = _dilated_windows()
    for u in range(tq // A_SUB):
        q = q_ref[0, 0, u * A_SUB:(u + 1) * A_SUB, :]
        scores = []
        for (dil, first, width), b_ref in zip(windows, bias_refs):
            start = tq + u * A_SUB + first
            s = _bdot_nt(q, k3_ref[start:start + width, :]) + b_ref[...]
            kpos = (i - 1) * tq + start + lax.broadcasted_iota(jnp.int32, (1, width), 1)
            s = jnp.where((kpos >= 0) & (kpos < seq), s, NEG_INF)
            scores.append((s, start, width))
        m = functools.reduce(jnp.maximum, [jnp.max(s, axis=-1, keepdims=True) for s, _, _ in scores])
        l = jnp.zeros_like(m)
        acc = jnp.zeros((A_SUB, HEAD_DIM), F32)
        for s, start, width in scores:
            p = jnp.exp2(s - m)
            l = l + jnp.sum(p, axis=-1, keepdims=True)
            acc = acc + _bdot(p, v3_ref[start:start + width, :])
        o_ref[0, 0, u * A_SUB:(u + 1) * A_SUB, :] = acc / l


def dilated_attention(qa, ka, va):
    batch, nh, seq, hd = qa.shape
    tq = min(A_TQ, seq)
    assert tq == A_TQ, "key halo of radius * max dilation needs full query tiles"
    nt = seq // tq
    cur = pl.BlockSpec((1, 1, tq, hd), lambda b, h, i: (b, h, i, 0))
    prev = pl.BlockSpec((1, 1, tq, hd), lambda b, h, i: (b, h, jnp.maximum(i - 1, 0), 0))
    nxt = pl.BlockSpec((1, 1, tq, hd), lambda b, h, i: (b, h, jnp.minimum(i + 1, nt - 1), 0))
    biases = _dilated_bias()
    bias_specs = [pl.BlockSpec(b.shape, lambda b_, h, i: (0, 0)) for b in biases]
    return pl.pallas_call(
        functools.partial(_dilated_attn_kernel, seq=seq, tq=tq),
        grid=(batch, nh, nt),
        in_specs=[cur, prev, cur, nxt, prev, cur, nxt] + bias_specs,
        out_specs=cur,
        out_shape=jax.ShapeDtypeStruct((batch, nh, seq, hd), F32),
        scratch_shapes=[pltpu.VMEM((3 * tq, hd), BF16), pltpu.VMEM((3 * tq, hd), BF16)],
        compiler_params=_params("parallel", "parallel", "parallel"),
        name="dilated_attention",
    )(qa, ka, ka, ka, va, va, va, *biases)


GQA_SUB = 128


def _gqa_kernel(q_ref, k_ref, v_ref, o_ref, *scratch, rep, tq):
    j = pl.program_id(3)
    n_sub = tq // GQA_SUB
    blocks = [(r, u) for r in range(rep) for u in range(n_sub)]
    m_refs, l_refs, acc_refs = (scratch[i * len(blocks):(i + 1) * len(blocks)] for i in range(3))

    @pl.when(j == 0)
    def _():
        for m_ref, l_ref, acc_ref in zip(m_refs, l_refs, acc_refs):
            m_ref[...] = jnp.full(m_ref.shape, NEG_INF, F32)
            l_ref[...] = jnp.zeros(l_ref.shape, F32)
            acc_ref[...] = jnp.zeros(acc_ref.shape, F32)

    k = k_ref[0, 0]
    v = v_ref[0, 0]
    scores = [_bdot_nt(k, q_ref[0, r, u * GQA_SUB:(u + 1) * GQA_SUB, :]) for r, u in blocks]
    for s, m_ref, l_ref, acc_ref in zip(scores, m_refs, l_refs, acc_refs):
        m_prev = m_ref[...]
        m_new = jnp.maximum(m_prev, jnp.max(s, axis=0, keepdims=True))
        alpha = jnp.exp2(m_prev - m_new)
        p = jnp.exp2(s - m_new)
        l_ref[...] = alpha * l_ref[...] + jnp.sum(p, axis=0, keepdims=True)
        pv = lax.dot_general(v, p.astype(BF16), (((0,), (0,)), ((), ())), preferred_element_type=F32)
        acc_ref[...] = alpha * acc_ref[...] + pv
        m_ref[...] = m_new

    @pl.when(j == pl.num_programs(3) - 1)
    def _():
        for (r, u), l_ref, acc_ref in zip(blocks, l_refs, acc_refs):
            o_ref[0, r, :, u * GQA_SUB:(u + 1) * GQA_SUB] = acc_ref[...] / l_ref[...]


def gqa_attention(qd, kd, vd, tq=1024, tk=2048):
    batch, nh, seq, hd = qd.shape
    ng = kd.shape[1]
    rep = nh // ng
    tq = min(tq, seq)
    tk = min(tk, seq)
    n_blocks = rep * (tq // GQA_SUB)
    return pl.pallas_call(
        functools.partial(_gqa_kernel, rep=rep, tq=tq),
        grid=(batch, ng, seq // tq, seq // tk),
        in_specs=[pl.BlockSpec((1, rep, tq, hd), lambda b, g, i, j: (b, g, i, 0)),
                  pl.BlockSpec((1, 1, tk, hd), lambda b, g, i, j: (b, g, j, 0)),
                  pl.BlockSpec((1, 1, tk, hd), lambda b, g, i, j: (b, g, j, 0))],
        out_specs=pl.BlockSpec((1, rep, hd, tq), lambda b, g, i, j: (b, g, 0, i)),
        out_shape=jax.ShapeDtypeStruct((batch, nh, hd, seq), F32),
        scratch_shapes=([pltpu.VMEM((1, GQA_SUB), F32)] * (2 * n_blocks)
                        + [pltpu.VMEM((hd, GQA_SUB), F32)] * n_blocks),
        compiler_params=_params("parallel", "parallel", "parallel", "arbitrary"),
        name="gqa_attention",
    )(qd, kd, vd)


def _matmul_kernel(a_ref, w_ref, o_ref):
    o_ref[...] = jnp.dot(a_ref[...], w_ref[...], preferred_element_type=F32).astype(o_ref.dtype)


def matmul_bf16(a, w, tm, tn, out_dtype=F32):
    n, k = a.shape
    m = w.shape[1]
    tm = min(tm, n)
    return pl.pallas_call(
        _matmul_kernel,
        grid=(n // tm, m // tn),
        in_specs=[pl.BlockSpec((tm, k), lambda i, j: (i, 0)),
                  pl.BlockSpec((k, tn), lambda i, j: (0, j))],
        out_specs=pl.BlockSpec((tm, tn), lambda i, j: (i, j)),
        out_shape=jax.ShapeDtypeStruct((n, m), out_dtype),
        compiler_params=_params("parallel", "arbitrary"),
        name="matmul_bf16",
    )(a, w)


def _sigmoid(x):
    return 1.0 / (1.0 + jnp.exp(-x))


def _softplus(x):
    return jnp.maximum(x, 0.0) + jnp.log(1.0 + jnp.exp(-jnp.abs(x)))


def _shift_rows(x, edge_row, down):
    rows = x.shape[0]
    ridx = lax.broadcasted_iota(jnp.int32, x.shape, 0)
    if down:
        return jnp.where(ridx == 0, edge_row, pltpu.roll(x, 1, 0))
    return jnp.where(ridx == rows - 1, edge_row, pltpu.roll(x, rows - 1, 0))


WKV_FIELDS = 6


def _rwkv_prep_kernel(xn_ref, xp_ref, xq_ref, zb_ref, zp_ref, zq_ref, mux_ref, murkv_ref, lw1_ref, lw2_ref,
                      w0a0_ref, g1_ref, g2_ref, kk_ref, ka_ref, rk_ref,
                      fw_ref, bw_ref, bonus_ref, gate_ref, *, tiles_per_seq):
    w = BRANCH_WIDTH
    i = pl.program_id(0)
    first = (i % tiles_per_seq) == 0
    last = (i % tiles_per_seq) == tiles_per_seq - 1
    xn = xn_ref[...]
    x_shift = (_shift_rows(xn, jnp.where(first, 0.0, xp_ref[7:8, :]), True),
               _shift_rows(xn, jnp.where(last, 0.0, xq_ref[0:1, :]), False))
    zb = zb_ref[...]
    z_prev = _shift_rows(zb, jnp.where(first, 0.0, zp_ref[7:8, :]), True)
    z_next = _shift_rows(zb, jnp.where(last, 0.0, zq_ref[0:1, :]), False)
    mu = murkv_ref[...]
    rkv = zb + mu[0:1] * (z_prev - zb) + mu[1:2] * (z_next - zb)
    r, k, v = rkv[:, :w], rkv[:, w:2 * w], rkv[:, 2 * w:]
    kap = k * kk_ref[...]
    kap = kap * lax.rsqrt(_head_sum(kap * kap, N_HEADS) + 1e-12)
    gate_ref[...] = _bdot(_sigmoid(_bdot(xn, g1_ref[...])), g2_ref[...])
    bonus = jnp.zeros_like(v)
    lora_lane = lax.broadcasted_iota(jnp.int32, (xn.shape[0], lw1_ref.shape[-1]), 1)
    for d, out_ref in enumerate((fw_ref, bw_ref)):
        xd = xn + mux_ref[d:d + 1, :] * (x_shift[d] - xn)
        h = _bdot(xd, lw1_ref[d])
        h = jnp.where(lora_lane < lw1_ref.shape[-1] // 2, jnp.tanh(h), h)
        h = _bdot(h, lw2_ref[d]) + w0a0_ref[d:d + 1, :]
        w_log = -_softplus(-h[:, :w]) - 0.5
        decay = jnp.exp(-jnp.exp(w_log))
        iclr = _sigmoid(h[:, w:])
        k_d = k * (1.0 + (iclr - 1.0) * ka_ref[...])
        bonus = bonus + _head_sum(r * k_d * rk_ref[...], N_HEADS) * v
        for j, field in enumerate((r, decay, k_d, v, kap, iclr * kap)):
            out_ref[:, j * w:(j + 1) * w] = field
    bonus_ref[...] = bonus


def rwkv_prep(xn, z, seq, mu_x, mu_rkv, lw1, lw2, w0a0, g1, g2, k_k, k_a, r_k, tm=512):
    n, d = xn.shape
    w = BRANCH_WIDTH
    tm = min(tm, seq)
    halo = 8
    prev_halo = lambda i: (jnp.maximum(i * (tm // halo) - 1, 0), 0)
    next_halo = lambda i: (jnp.minimum((i + 1) * (tm // halo), n // halo - 1), 0)
    full = lambda a: pl.BlockSpec(a.shape, lambda i: (0,) * a.ndim)
    params = (mu_x, mu_rkv, lw1, lw2, w0a0, g1, g2, k_k.reshape(1, w), k_a.reshape(1, w), r_k.reshape(1, w))
    rows = lambda width: pl.BlockSpec((tm, width), lambda i: (i, 0))
    return pl.pallas_call(
        functools.partial(_rwkv_prep_kernel, tiles_per_seq=seq // tm),
        grid=(n // tm,),
        in_specs=[rows(d), pl.BlockSpec((halo, d), prev_halo), pl.BlockSpec((halo, d), next_halo),
                  pl.BlockSpec((tm, IN_B), lambda i: (i, OFF_B // IN_B)),
                  pl.BlockSpec((halo, IN_B), lambda i: (prev_halo(i)[0], OFF_B // IN_B)),
                  pl.BlockSpec((halo, IN_B), lambda i: (next_halo(i)[0], OFF_B // IN_B))]
                 + [full(p) for p in params],
        out_specs=[rows(WKV_FIELDS * w), rows(WKV_FIELDS * w), rows(w), rows(w)],
        out_shape=[jax.ShapeDtypeStruct((n, WKV_FIELDS * w), F32), jax.ShapeDtypeStruct((n, WKV_FIELDS * w), F32),
                   jax.ShapeDtypeStruct((n, w), F32), jax.ShapeDtypeStruct((n, w), F32)],
        compiler_params=_params("parallel"),
        name="rwkv_prep",
    )(xn, xn, xn, z, z, z, *params)


WKV_UNROLL = 8


def _key_sum(state, kap):
    half = 2 * HEAD_DIM
    low = lax.broadcasted_iota(jnp.int32, (state.shape[0], half), 1) < HEAD_DIM
    out = []
    for c in range(state.shape[1] // half):
        sc = state[:, c * half:(c + 1) * half]
        kc = kap[:, c * half:(c + 1) * half]
        first = jnp.sum(sc * jnp.where(low[:1], kc, 0.0), axis=1, keepdims=True)
        second = jnp.sum(sc * jnp.where(low[:1], 0.0, kc), axis=1, keepdims=True)
        out.append(jnp.where(low, first, second))
    return jnp.concatenate(out, axis=1)


def _wkv_scan_kernel(fw_ref, bw_ref, mask_ref, yf_ref, yb_ref, state_ref, *, batch, chunk):
    @pl.when(pl.program_id(0) == 0)
    def _():
        state_ref[...] = jnp.zeros(state_ref.shape, F32)

    w = BRANCH_WIDTH
    u = WKV_UNROLL
    groups = chunk // u
    chains = [(bi, d) for bi in range(batch) for d in range(2)]

    def body(i, carry):
        bases = (pl.multiple_of(i * u, u), pl.multiple_of((groups - 1 - i) * u, u))
        refs = (fw_ref, bw_ref)
        mask = mask_ref[...]
        vk, r8, state, history = {}, {}, {}, {}
        for c in chains:
            bi, d = c
            rows = refs[d][bi, pl.ds(bases[d], u), :]
            r8[c], k8, v8 = rows[:, 0:w], rows[:, 2 * w:3 * w], rows[:, 3 * w:4 * w]
            v_heads = jnp.concatenate([v8[:, h * HEAD_DIM:(h + 1) * HEAD_DIM] for h in range(N_HEADS)], axis=0)
            vk[c] = lax.dot_general(v_heads.astype(BF16), (jnp.tile(k8, (N_HEADS, u)) * mask).astype(BF16),
                                    (((0,), (0,)), ((), ())), preferred_element_type=F32)
            state[c] = state_ref[2 * bi + d]
            history[c] = [None] * u
        for step in range(u):
            for c in chains:
                bi, d = c
                j = u - 1 - step if d == 1 else step
                row = refs[d][bi, pl.ds(bases[d] + j, 1), :]
                decay, kap, b = row[:, w:2 * w], row[:, 4 * w:5 * w], row[:, 5 * w:6 * w]
                state[c] = state[c] * decay - _key_sum(state[c], kap) * b + vk[c][:, j * w:(j + 1) * w]
                history[c][j] = state[c].astype(BF16)
        for c in chains:
            bi, d = c
            state_ref[2 * bi + d] = state[c]
            y_all = _bdot_nt(jnp.tile(r8[c], (N_HEADS, u)) * mask, jnp.concatenate(history[c], axis=1))
            y_ref = (yf_ref, yb_ref)[d]
            for h in range(N_HEADS):
                y_ref[bi, h, pl.ds(bases[d], u), :] = y_all[h * u:(h + 1) * u, :]
        return carry

    lax.fori_loop(0, groups, body, 0)


def wkv_scan(fw, bw, batch, seq, chunk=256):
    w = BRANCH_WIDTH
    u = WKV_UNROLL
    chunk = min(chunk, seq)
    nc = seq // chunk
    row_head, row_step = jnp.arange(N_HEADS * u) // u, jnp.arange(N_HEADS * u) % u
    lane_step, lane_head = jnp.arange(u * w) // w, (jnp.arange(u * w) % w) // HEAD_DIM
    mask = ((row_head[:, None] == lane_head[None, :]) & (row_step[:, None] == lane_step[None, :])).astype(F32)
    in_f = pl.BlockSpec((batch, chunk, WKV_FIELDS * w), lambda c: (0, c, 0))
    in_b = pl.BlockSpec((batch, chunk, WKV_FIELDS * w), lambda c: (0, nc - 1 - c, 0))
    out_shape = jax.ShapeDtypeStruct((batch, N_HEADS, seq, HEAD_DIM), F32)
    return pl.pallas_call(
        functools.partial(_wkv_scan_kernel, batch=batch, chunk=chunk),
        grid=(nc,),
        in_specs=[in_f, in_b, pl.BlockSpec(mask.shape, lambda c: (0, 0))],
        out_specs=[pl.BlockSpec((batch, N_HEADS, chunk, HEAD_DIM), lambda c: (0, 0, c, 0)),
                   pl.BlockSpec((batch, N_HEADS, chunk, HEAD_DIM), lambda c: (0, 0, nc - 1 - c, 0))],
        out_shape=[out_shape, out_shape],
        scratch_shapes=[pltpu.VMEM((2 * batch, HEAD_DIM, w), F32)],
        compiler_params=_params("arbitrary"),
        name="wkv_scan",
    )(fw.reshape(batch, seq, -1), bw.reshape(batch, seq, -1), mask)


S5_SEGMENTS = 8
S5_WIDTH = S5_GROUPS * S5_STATE


def _s5_discretize(a_re, a_im, log_dt, b_re, b_im, c_re, c_im):
    g, p, c = S5_GROUPS, S5_STATE, S5_GROUP_CH
    dt = jnp.exp(log_dt)[:, None]
    mag = jnp.exp(a_re * dt)
    bar_re, bar_im = mag * jnp.cos(a_im * dt), mag * jnp.sin(a_im * dt)
    den = a_re * a_re + a_im * a_im
    f_re = ((bar_re - 1.0) * a_re + bar_im * a_im) / den
    f_im = (bar_im * a_re - (bar_re - 1.0) * a_im) / den
    bb_re = f_re[..., None] * b_re - f_im[..., None] * b_im
    bb_im = f_re[..., None] * b_im + f_im[..., None] * b_re
    eye_g = jnp.eye(g, dtype=F32)
    w_in = jnp.concatenate(
        [jnp.einsum('gpc,gh->gchp', bb, eye_g).reshape(g * c, g * p) for bb in (bb_re, bb_im)], axis=1)
    w_out = jnp.concatenate(
        [jnp.einsum('gcp,gh->gphc', cc, eye_g).reshape(g * p, g * c) for cc in (c_re, -c_im)], axis=0)
    return bar_re.reshape(1, g * p), bar_im.reshape(1, g * p), w_in.astype(BF16), w_out.astype(BF16), dt, a_re, a_im


def _s5_powers(a_re, a_im, dt, count, reverse):
    j = jnp.arange(1, count + 1, dtype=F32)
    if reverse:
        j = j[::-1]
    e = j[:, None, None] * (a_re * dt)[None]
    th = j[:, None, None] * (a_im * dt)[None]
    mag = jnp.exp(e)
    return (mag * jnp.cos(th)).reshape(count, -1), (mag * jnp.sin(th)).reshape(count, -1)


def _s5_scan_kernel(u_ref, win_ref, lre_ref, lim_ref, pre_ref, pim_ref, wout_ref, y_ref,
                    xre_ref, xim_ref, cre_ref, cim_ref, *, reverse, chunk):
    seg = chunk // S5_SEGMENTS
    nw = S5_WIDTH

    @pl.when(pl.program_id(1) == 0)
    def _():
        cre_ref[...] = jnp.zeros(cre_ref.shape, F32)
        cim_ref[...] = jnp.zeros(cim_ref.shape, F32)

    bu = jnp.dot(u_ref[...].astype(BF16), win_ref[...], preferred_element_type=F32)
    xre_ref[...] = bu[:, :nw]
    xim_ref[...] = bu[:, nw:]
    lre = lre_ref[...]
    lim = lim_ref[...]
    group = lambda s: pl.ds(pl.multiple_of(s * S5_SEGMENTS, S5_SEGMENTS), S5_SEGMENTS)

    def local_step(s, carry):
        xr, xi = carry
        rows = group((seg - 1 - s) if reverse else s)
        nr = lre * xr - lim * xi + xre_ref[rows, :]
        ni = lre * xi + lim * xr + xim_ref[rows, :]
        xre_ref[rows, :] = nr
        xim_ref[rows, :] = ni
        return nr, ni

    zero = jnp.zeros((S5_SEGMENTS, nw), F32)
    fin_re, fin_im = lax.fori_loop(0, seg, local_step, (zero, zero))

    full_seg = 0 if reverse else seg - 1
    pl_re, pl_im = pre_ref[full_seg:full_seg + 1, :], pim_ref[full_seg:full_seg + 1, :]
    ir, ii = cre_ref[...], cim_ref[...]
    init_re, init_im = [None] * S5_SEGMENTS, [None] * S5_SEGMENTS
    for j in (range(S5_SEGMENTS - 1, -1, -1) if reverse else range(S5_SEGMENTS)):
        init_re[j], init_im[j] = ir, ii
        fr, fi = fin_re[j:j + 1, :], fin_im[j:j + 1, :]
        ir, ii = fr + (pl_re * ir - pl_im * ii), fi + (pl_re * ii + pl_im * ir)
    cre_ref[...] = ir
    cim_ref[...] = ii
    init_re = jnp.concatenate(init_re, axis=0)
    init_im = jnp.concatenate(init_im, axis=0)

    def correct(s, carry):
        rows = group(s)
        pr, pi = pre_ref[pl.ds(s, 1), :], pim_ref[pl.ds(s, 1), :]
        xre_ref[rows, :] = xre_ref[rows, :] + (pr * init_re - pi * init_im)
        xim_ref[rows, :] = xim_ref[rows, :] + (pr * init_im + pi * init_re)
        return carry

    lax.fori_loop(0, seg, correct, 0)
    y_ref[...] = (jnp.dot(xre_ref[...].astype(BF16), wout_ref[:nw, :], preferred_element_type=F32)
                  + jnp.dot(xim_ref[...].astype(BF16), wout_ref[nw:, :], preferred_element_type=F32))


def _segment_interleave(a, batch, seq, chunk, inverse=False):
    w = a.shape[-1]
    seg = chunk // S5_SEGMENTS
    shape = (batch, seq // chunk, seg, S5_SEGMENTS, w) if inverse else (batch, seq // chunk, S5_SEGMENTS, seg, w)
    return jnp.swapaxes(a.reshape(shape), 2, 3).reshape(batch * seq, w)


def s5_scan(u_interleaved, batch, seq, disc, reverse, chunk):
    lam_re, lam_im, w_in, w_out, dt, a_re, a_im = disc
    w = BRANCH_WIDTH
    nc = seq // chunk
    seg = chunk // S5_SEGMENTS
    pw_re, pw_im = _s5_powers(a_re, a_im, dt, seg, reverse)
    order = (lambda c: nc - 1 - c) if reverse else (lambda c: c)
    full = lambda a: pl.BlockSpec(a.shape, lambda b, c: (0, 0))
    return pl.pallas_call(
        functools.partial(_s5_scan_kernel, reverse=reverse, chunk=chunk),
        grid=(batch, nc),
        in_specs=[pl.BlockSpec((chunk, w), lambda b, c: (b * nc + order(c), 0)),
                  full(w_in), full(lam_re), full(lam_im), full(pw_re), full(pw_im), full(w_out)],
        out_specs=pl.BlockSpec((chunk, w), lambda b, c: (b * nc + order(c), 0)),
        out_shape=jax.ShapeDtypeStruct((batch * seq, w), F32),
        scratch_shapes=[pltpu.VMEM((chunk, S5_WIDTH), F32), pltpu.VMEM((chunk, S5_WIDTH), F32),
                        pltpu.VMEM((1, S5_WIDTH), F32), pltpu.VMEM((1, S5_WIDTH), F32)],
        compiler_params=_params("parallel", "arbitrary"),
        name="s5_scan_bwd" if reverse else "s5_scan_fwd",
    )(u_interleaved, w_in, lam_re, lam_im, pw_re, pw_im, w_out)


def s5_bidirectional(z, batch, seq, discs, chunk=1024):
    chunk = min(chunk, seq)
    u = _segment_interleave(z[:, OFF_C:OFF_C + BRANCH_WIDTH], batch, seq, chunk)
    return [_segment_interleave(s5_scan(u, batch, seq, disc, reverse=(d == 1), chunk=chunk),
                                batch, seq, chunk, inverse=True) for d, disc in enumerate(discs)]


def _gelu_tanh(y):
    return 0.5 * y * (1.0 + jnp.tanh(math.sqrt(2.0 / math.pi) * (y + 0.044715 * (y * y * y))))


def _merge_kernel(x_ref, ya_ref, yd_ref, wf_ref, wb_ref, bonus_ref, rg_ref, sf_ref, sb_ref, u_ref, zg_ref,
                  gb_ref, wbr_ref, wout_ref, lnw_ref, lnb_ref, s5d_ref, gluw_ref, glub_ref, o_ref):
    w = BRANCH_WIDTH
    ya = jnp.concatenate([ya_ref[0, h] for h in range(N_HEADS)], axis=1)
    normed = []
    for h in range(N_HEADS):
        ys = wf_ref[0, h] + wb_ref[0, h]
        cen = ys - jnp.mean(ys, axis=-1, keepdims=True)
        var = jnp.mean(cen * cen, axis=-1, keepdims=True)
        normed.append(cen * lax.rsqrt(var + RWKV_GN_EPS))
    yb = (jnp.concatenate(normed, axis=1) * lnw_ref[...] + lnb_ref[...] + bonus_ref[...]) * rg_ref[...]
    yc = sf_ref[...] + sb_ref[...] + s5d_ref[...] * u_ref[...]
    h = _bdot(_gelu_tanh(yc), gluw_ref[...]) + glub_ref[...]
    yc = h[:, :w] * _sigmoid(h[:, w:])
    proj_d = sum(lax.dot_general(yd_ref[0, h].astype(BF16), wbr_ref[3, h * HEAD_DIM:(h + 1) * HEAD_DIM, :],
                                 (((0,), (0,)), ((), ())), preferred_element_type=F32) for h in range(N_HEADS))
    merged = jnp.zeros(o_ref.shape, F32)
    for i, proj in enumerate((_bdot(ya, wbr_ref[0]), _bdot(yb, wbr_ref[1]), _bdot(yc, wbr_ref[2]), proj_d)):
        gate = _sigmoid(zg_ref[:, i * D_MODEL:(i + 1) * D_MODEL] + gb_ref[i:i + 1, :])
        merged = merged + gate * proj
    o_ref[...] = x_ref[...] + _bdot(merged, wout_ref[...])


def merge_branches(x, ya, yd, wkv_f, wkv_b, bonus, rgate, s5_f, s5_b, z, zg, gate_b, w_branch, w_out,
                   ln_w, ln_b, s5_d, glu_w, glu_b, batch, seq, tm=512):
    w = BRANCH_WIDTH
    d = D_MODEL
    tm = min(tm, seq)
    nt = seq // tm
    rows = lambda width, col=0: pl.BlockSpec((tm, width), lambda b, i: (b * nt + i, col))
    heads = pl.BlockSpec((1, N_HEADS, tm, HEAD_DIM), lambda b, i: (b, 0, i, 0))
    heads_t = pl.BlockSpec((1, N_HEADS, HEAD_DIM, tm), lambda b, i: (b, 0, 0, i))
    full = lambda a: pl.BlockSpec(a.shape, lambda b, i: (0,) * a.ndim)
    params = (gate_b, w_branch, w_out, ln_w.reshape(1, w), ln_b.reshape(1, w), s5_d.reshape(1, w),
              glu_w, glu_b.reshape(1, 2 * w))
    return pl.pallas_call(
        _merge_kernel,
        grid=(batch, nt),
        in_specs=[rows(d), heads, heads_t, heads, heads, rows(w), rows(w), rows(w), rows(w),
                  rows(w, OFF_C // w), rows(N_BRANCHES * d)] + [full(p) for p in params],
        out_specs=rows(d),
        out_shape=jax.ShapeDtypeStruct(x.shape, F32),
        compiler_params=_params("parallel", "parallel"),
        name="merge_branches",
    )(x, ya, yd, wkv_f, wkv_b, bonus, rgate, s5_f, s5_b, z, zg, *params)


def _silu(x):
    return x * _sigmoid(x)


def _dense_ffn_kernel(x_ref, g_ref, wg_ref, wu_ref, wd_ref, o_ref, xn_ref):
    @pl.when(pl.program_id(1) == 0)
    def _():
        x = x_ref[...]
        xn_ref[...] = _rms(x, g_ref[...]).astype(BF16)
        o_ref[...] = x

    xn = xn_ref[...]
    h = (_silu(jnp.dot(xn, wg_ref[...], preferred_element_type=F32))
         * jnp.dot(xn, wu_ref[...], preferred_element_type=F32))
    o_ref[...] += _bdot(h, wd_ref[...])


def dense_ffn(x, g, w_gate, w_up, w_down, tm=1024, tf=1408):
    n, d = x.shape
    ff = w_gate.shape[1]
    tm = min(tm, n)
    return pl.pallas_call(
        _dense_ffn_kernel,
        grid=(n // tm, ff // tf),
        in_specs=[pl.BlockSpec((tm, d), lambda i, f: (i, 0)),
                  pl.BlockSpec((1, d), lambda i, f: (0, 0)),
                  pl.BlockSpec((d, tf), lambda i, f: (0, f)),
                  pl.BlockSpec((d, tf), lambda i, f: (0, f)),
                  pl.BlockSpec((tf, d), lambda i, f: (f, 0))],
        out_specs=pl.BlockSpec((tm, d), lambda i, f: (i, 0)),
        out_shape=jax.ShapeDtypeStruct((n, d), F32),
        scratch_shapes=[pltpu.VMEM((tm, d), BF16)],
        compiler_params=_params("parallel", "arbitrary"),
        name="dense_ffn",
    )(x, g.reshape(1, d), w_gate, w_up, w_down)


def _router_kernel(x_ref, g_ref, rt_ref, xnb_ref, sel_ref, wt_ref):
    xn = _rms(x_ref[...], g_ref[...])
    xnb_ref[...] = xn.astype(BF16)
    logits = lax.dot_general(rt_ref[...], xn, (((1,), (1,)), ((), ())),
                             precision=lax.Precision.HIGHEST, preferred_element_type=F32)
    e = lax.broadcasted_iota(jnp.int32, logits.shape, 0)
    m1 = jnp.max(logits, axis=0, keepdims=True)
    i1 = jnp.min(jnp.where(logits == m1, e, N_EXPERTS), axis=0, keepdims=True)
    rest = jnp.where(e == i1, NEG_INF, logits)
    m2 = jnp.max(rest, axis=0, keepdims=True)
    i2 = jnp.min(jnp.where(rest == m2, e, N_EXPERTS), axis=0, keepdims=True)
    ratio = jnp.exp(m2 - m1)
    w1 = 1.0 / (1.0 + ratio)
    w2 = ratio / (1.0 + ratio)
    sel_ref[...] = jnp.where((e == i1) | (e == i2), 1.0, 0.0)
    wt_ref[...] = jnp.where(e == i1, w1, jnp.where(e == i2, w2, 0.0))


def moe_route(x, g, router, tm=1024):
    n, d = x.shape
    tm = min(tm, n)
    ne = router.shape[1]
    return pl.pallas_call(
        _router_kernel,
        grid=(n // tm,),
        in_specs=[pl.BlockSpec((tm, d), lambda i: (i, 0)),
                  pl.BlockSpec((1, d), lambda i: (0, 0)),
                  pl.BlockSpec((ne, d), lambda i: (0, 0))],
        out_specs=[pl.BlockSpec((tm, d), lambda i: (i, 0)),
                   pl.BlockSpec((ne, tm), lambda i: (0, i)),
                   pl.BlockSpec((ne, tm), lambda i: (0, i))],
        out_shape=[jax.ShapeDtypeStruct((n, d), BF16), jax.ShapeDtypeStruct((ne, n), F32),
                   jax.ShapeDtypeStruct((ne, n), F32)],
        compiler_params=_params("parallel"),
        name="moe_router",
    )(x, g.reshape(1, d), router.T)


MOE_ROWS = 128
MOE_STATIC_BLOCKS = (2, 3)


def _moe_kernel(x_ref, xnb_ref, sel_ref, wt_ref, wg_ref, wu_ref, wd_ref, o_ref,
                rank_ref, xg_ref, acc_ref, nblk_ref):
    e = pl.program_id(1)
    f = pl.program_id(2)
    nf = pl.num_programs(2)
    tm = x_ref.shape[0]

    @pl.when((e == 0) & (f == 0))
    def _():
        o_ref[...] = x_ref[...]
        before = (lax.broadcasted_iota(jnp.int32, (tm, tm), 0) < lax.broadcasted_iota(jnp.int32, (tm, tm), 1))
        rank_ref[...] = jnp.dot(sel_ref[...].astype(BF16), jnp.where(before, 1.0, 0.0).astype(BF16),
                                preferred_element_type=F32)

    sel_e = sel_ref[pl.ds(e, 1), :]
    rank_e = rank_ref[pl.ds(e, 1), :]
    wt_e = wt_ref[pl.ds(e, 1), :]

    @pl.when(f == 0)
    def _():
        count = jnp.sum(sel_e).astype(jnp.int32)
        nblk_ref[0] = (count + MOE_ROWS - 1) // MOE_ROWS

    nblk = nblk_ref[0]

    def process(rows):
        n_rows = rows.stop - rows.start if isinstance(rows, slice) else rows.size
        first = rows.start

        def one_hot():
            slot = (first + lax.broadcasted_iota(jnp.int32, (n_rows, tm), 0)).astype(F32)
            return jnp.where((rank_e == slot) & (sel_e > 0.0), 1.0, 0.0)

        @pl.when(f == 0)
        def _():
            xg_ref[rows, :] = jnp.dot(one_hot().astype(BF16), xnb_ref[...],
                                      preferred_element_type=F32).astype(BF16)
            acc_ref[rows, :] = jnp.zeros((n_rows, acc_ref.shape[1]), F32)

        xg = xg_ref[rows, :]
        h = (_silu(jnp.dot(xg, wg_ref[0], preferred_element_type=F32))
             * jnp.dot(xg, wu_ref[0], preferred_element_type=F32))
        acc_ref[rows, :] += _bdot(h, wd_ref[0])

        @pl.when(f == nf - 1)
        def _():
            hot = one_hot()
            row_w = jnp.sum(hot * wt_e, axis=1, keepdims=True)
            yw = (acc_ref[rows, :] * row_w).astype(BF16)
            o_ref[...] += lax.dot_general(hot.astype(BF16), yw, (((0,), (0,)), ((), ())),
                                          preferred_element_type=F32)

    for n_static in MOE_STATIC_BLOCKS:
        lo = 0 if n_static == MOE_STATIC_BLOCKS[0] else n_static
        hi = n_static if n_static != MOE_STATIC_BLOCKS[-1] else tm // MOE_ROWS
        pl.when((nblk >= lo) & (nblk <= hi))(functools.partial(process, slice(0, n_static * MOE_ROWS)))

    def tail(b, carry):
        process(pl.ds(pl.multiple_of(b * MOE_ROWS, MOE_ROWS), MOE_ROWS))
        return carry

    lax.fori_loop(MOE_STATIC_BLOCKS[-1], nblk, tail, 0)


def moe_ffn(x, xnb, sel, wt, w_gate, w_up, w_down, tm=1024, tf=896):
    n, d = x.shape
    ne, _, ff = w_gate.shape
    tm = min(tm, n)
    return pl.pallas_call(
        _moe_kernel,
        grid=(n // tm, ne, ff // tf),
        in_specs=[pl.BlockSpec((tm, d), lambda i, e, f: (i, 0)),
                  pl.BlockSpec((tm, d), lambda i, e, f: (i, 0)),
                  pl.BlockSpec((ne, tm), lambda i, e, f: (0, i)),
                  pl.BlockSpec((ne, tm), lambda i, e, f: (0, i)),
                  pl.BlockSpec((1, d, tf), lambda i, e, f: (e, 0, f)),
                  pl.BlockSpec((1, d, tf), lambda i, e, f: (e, 0, f)),
                  pl.BlockSpec((1, tf, d), lambda i, e, f: (e, f, 0))],
        out_specs=pl.BlockSpec((tm, d), lambda i, e, f: (i, 0)),
        out_shape=jax.ShapeDtypeStruct((n, d), F32),
        scratch_shapes=[pltpu.VMEM((ne, tm), F32), pltpu.VMEM((tm, d), BF16), pltpu.VMEM((tm, d), F32),
                        pltpu.SMEM((1,), jnp.int32)],
        compiler_params=_params("parallel", "arbitrary", "arbitrary"),
        name="moe_ffn",
    )(x, xnb, sel, wt, w_gate, w_up, w_down)


def _rms_kernel(x_ref, g_ref, o_ref):
    o_ref[...] = _rms(x_ref[...], g_ref[...])


def rms_norm(x, g, tm=1024):
    n, d = x.shape
    tm = min(tm, n)
    return pl.pallas_call(
        _rms_kernel,
        grid=(n // tm,),
        in_specs=[pl.BlockSpec((tm, d), lambda i: (i, 0)), pl.BlockSpec((1, d), lambda i: (0, 0))],
        out_specs=pl.BlockSpec((tm, d), lambda i: (i, 0)),
        out_shape=jax.ShapeDtypeStruct((n, d), F32),
        compiler_params=_params("parallel"),
        name="final_rms_norm",
    )(x, g.reshape(1, d))


def _rope_angles(pos, n_freq, theta):
    inv_freq = theta ** (-jnp.arange(n_freq, dtype=F32) / n_freq)
    return pos.astype(F32)[:, None] * inv_freq[None, :]


def kernel(x, norm_mix_g, w_in, gate_b, w_branch, w_out, rwkv_mu_rkv, rwkv_mu_x, rwkv_w0, rwkv_w1, rwkv_w2,
           rwkv_a0, rwkv_a1, rwkv_a2, rwkv_g1, rwkv_g2, rwkv_k_k, rwkv_k_a, rwkv_r_k, rwkv_ln_w, rwkv_ln_b,
           s5_a_re, s5_a_im, s5_log_dt, s5_b_re, s5_b_im, s5_c_re, s5_c_im, s5_d, s5_glu_w, s5_glu_b,
           gqa_q_norm, gqa_k_norm, norm_ffn_g, dense_w_gate, dense_w_up, dense_w_down,
           moe_router, moe_w_gate, moe_w_up, moe_w_down, final_norm_g):
    batch, seq, d = x.shape
    depth = w_in.shape[0]
    n = batch * seq
    t = jnp.arange(seq, dtype=jnp.int32)
    rope_tabs = _rotary_tables(_rope_angles(t, ROPE_DIMS // 2, ROPE_THETA), N_HEADS)
    ang_axial = jnp.concatenate([_rope_angles(t // GRID_W, HEAD_DIM // 4, AXIAL_THETA),
                                 _rope_angles(t % GRID_W, HEAD_DIM // 4, AXIAL_THETA)], axis=-1)
    axial_tabs = _rotary_tables(ang_axial, N_HEADS)
    x = x.reshape(n, d)
    for l in range(depth):
        w_small = w_in[l, :, :OFF_GATES].astype(BF16)
        w_gates = w_in[l, :, OFF_GATES:].astype(BF16)
        xn, xnb, z = rms_in_proj(x, norm_mix_g[l], w_small)
        zg = matmul_bf16(xnb, w_gates, tm=1024, tn=1024, out_dtype=BF16)
        qa, ka, va, qd, kd, vd = qkv_prep(z, batch, seq, rope_tabs, axial_tabs, gqa_q_norm[l], gqa_k_norm[l])
        ya = dilated_attention(qa, ka, va)
        yd = gqa_attention(qd, kd, vd)
        lw1 = jnp.concatenate([rwkv_w1[l], rwkv_a1[l]], axis=-1).astype(BF16)
        zeros = jnp.zeros_like(rwkv_w2[l])
        lw2 = jnp.concatenate([jnp.concatenate([rwkv_w2[l], zeros], axis=-1),
                               jnp.concatenate([zeros, rwkv_a2[l]], axis=-1)], axis=1).astype(BF16)
        w0a0 = jnp.concatenate([rwkv_w0[l], rwkv_a0[l]], axis=-1)
        fw, bw, bonus, rgate = rwkv_prep(xn, z, seq, rwkv_mu_x[l], rwkv_mu_rkv[l].reshape(2, IN_B), lw1, lw2, w0a0,
                                         rwkv_g1[l].astype(BF16), rwkv_g2[l].astype(BF16),
                                         rwkv_k_k[l], rwkv_k_a[l], rwkv_r_k[l])
        wkv_f, wkv_b = wkv_scan(fw, bw, batch, seq)
        discs = [_s5_discretize(s5_a_re[l, dr], s5_a_im[l, dr], s5_log_dt[l, dr], s5_b_re[l], s5_b_im[l],
                                s5_c_re[l, dr], s5_c_im[l, dr]) for dr in range(2)]
        s5_out = s5_bidirectional(z, batch, seq, discs)
        x = merge_branches(x, ya, yd, wkv_f, wkv_b, bonus, rgate, s5_out[0], s5_out[1], z, zg, gate_b[l],
                           w_branch[l].astype(BF16), w_out[l].astype(BF16), rwkv_ln_w[l], rwkv_ln_b[l],
                           s5_d[l], s5_glu_w[l].astype(BF16), s5_glu_b[l], batch, seq)
        i = l // 2
        if l % 2 == 0:
            x = dense_ffn(x, norm_ffn_g[l], dense_w_gate[i].astype(BF16), dense_w_up[i].astype(BF16),
                          dense_w_down[i].astype(BF16))
        else:
            xnb_f, sel, wt = moe_route(x, norm_ffn_g[l], moe_router[i])
            x = moe_ffn(x, xnb_f, sel, wt, moe_w_gate[i].astype(BF16), moe_w_up[i].astype(BF16),
                        moe_w_down[i].astype(BF16))
    return rms_norm(x, final_norm_g).reshape(batch, seq, d)
```

```python
import functools
import math

import jax
import jax.numpy as jnp
from jax import lax
from jax.experimental import pallas as pl
from jax.experimental.pallas import tpu as pltpu

F32 = jnp.float32
BF16 = jnp.bfloat16

D_MODEL = 1024
HEAD_DIM = 64
BRANCH_WIDTH = 256
N_BRANCHES = 4
N_HEADS = BRANCH_WIDTH // HEAD_DIM
DILATED_PATTERNS = ((128, 1), (512, 4), (2048, 16))
ROPE_THETA = 500000.0
ROPE_DIMS = HEAD_DIM // 4
RWKV_GN_EPS = 64e-5
S5_GROUP_CH = 16
S5_GROUPS = BRANCH_WIDTH // S5_GROUP_CH
S5_STATE = 64
GQA_KV_HEADS = 2
AXIAL_THETA = 10000.0
GRID_W = 64
N_EXPERTS = 8
TOP_K = 2
NORM_EPS = 1e-6
NEG_INF = -1e30

IN_A = 3 * BRANCH_WIDTH
IN_B = 3 * BRANCH_WIDTH
IN_C = BRANCH_WIDTH
IN_DQ = BRANCH_WIDTH
IN_DKV = GQA_KV_HEADS * HEAD_DIM
IN_GATES = N_BRANCHES * D_MODEL
OFF_B = IN_A
OFF_C = OFF_B + IN_B
OFF_DQ = OFF_C + IN_C
OFF_DKV = OFF_DQ + IN_DQ
OFF_GATES = OFF_DKV + 2 * IN_DKV
IN_TOTAL = OFF_GATES + IN_GATES

VMEM_LIMIT_BYTES = 56 * 1024 * 1024


def _params(*semantics):
    return pltpu.CompilerParams(dimension_semantics=semantics, vmem_limit_bytes=VMEM_LIMIT_BYTES)


def _bdot(a, b):
    return jnp.dot(a.astype(BF16), b.astype(BF16), preferred_element_type=F32)


def _bdot_nt(a, b):
    return lax.dot_general(a.astype(BF16), b.astype(BF16), (((1,), (1,)), ((), ())),
                           preferred_element_type=F32)


def _rms(x, g):
    return x * lax.rsqrt(jnp.mean(x * x, axis=-1, keepdims=True) + NORM_EPS) * g


def _rms_in_proj_kernel(x_ref, g_ref, w_ref, xn_ref, xnb_ref, z_ref):
    @pl.when(pl.program_id(1) == 0)
    def _():
        y = _rms(x_ref[...], g_ref[...])
        xn_ref[...] = y
        xnb_ref[...] = y.astype(BF16)

    z_ref[...] = jnp.dot(xnb_ref[...], w_ref[...], preferred_element_type=F32)


def rms_in_proj(x, g, w_bf16, tm=1024, tn=768):
    n, d = x.shape
    nout = w_bf16.shape[1]
    tm = min(tm, n)
    return pl.pallas_call(
        _rms_in_proj_kernel,
        grid=(n // tm, nout // tn),
        in_specs=[pl.BlockSpec((tm, d), lambda i, j: (i, 0)),
                  pl.BlockSpec((1, d), lambda i, j: (0, 0)),
                  pl.BlockSpec((d, tn), lambda i, j: (0, j))],
        out_specs=[pl.BlockSpec((tm, d), lambda i, j: (i, 0)),
                   pl.BlockSpec((tm, d), lambda i, j: (i, 0)),
                   pl.BlockSpec((tm, tn), lambda i, j: (i, j))],
        out_shape=[jax.ShapeDtypeStruct((n, d), F32), jax.ShapeDtypeStruct((n, d), BF16),
                   jax.ShapeDtypeStruct((n, nout), F32)],
        compiler_params=_params("parallel", "arbitrary"),
        name="rms_in_proj",
    )(x, g.reshape(1, d), w_bf16)


def _rotary_tables(pos_angles, n_heads):
    s, n = pos_angles.shape
    pad = HEAD_DIM - 2 * n
    cos = jnp.concatenate([jnp.cos(pos_angles), jnp.cos(pos_angles), jnp.ones((s, pad), F32)], axis=-1)
    zeros_n = jnp.zeros((s, n), F32)
    zeros_p = jnp.zeros((s, pad), F32)
    sin_lo = jnp.concatenate([-jnp.sin(pos_angles), zeros_n, zeros_p], axis=-1)
    sin_hi = jnp.concatenate([zeros_n, jnp.sin(pos_angles), zeros_p], axis=-1)
    return tuple(jnp.tile(t, (1, n_heads)) for t in (cos, sin_lo, sin_hi))


def _rotate(x, cos, sin_lo, sin_hi, n):
    width = x.shape[-1]
    from_above = pltpu.roll(x, width - n, 1)
    from_below = pltpu.roll(x, n, 1)
    return x * cos + from_above * sin_lo + from_below * sin_hi


def _head_sum(x, n_heads):
    lane = lax.broadcasted_iota(jnp.int32, x.shape, 1)
    out = jnp.zeros_like(x)
    for h in range(n_heads):
        in_head = (lane >= h * HEAD_DIM) & (lane < (h + 1) * HEAD_DIM)
        s = jnp.sum(jnp.where(in_head, x, 0.0), axis=-1, keepdims=True)
        out = jnp.where(in_head, s, out)
    return out


def _head_rms(x, g, n_heads):
    ms = _head_sum(x * x, n_heads) * (1.0 / HEAD_DIM)
    return x * lax.rsqrt(ms + NORM_EPS) * g


Q_SCALE = HEAD_DIM ** -0.5 * math.log2(math.e)


def _qkv_prep_kernel(za_ref, zq_ref, zkv_ref, rc_ref, rl_ref, rh_ref, ac_ref, al_ref, ah_ref,
                     qn_ref, kn_ref, qa_ref, ka_ref, va_ref, qd_ref, kd_ref, vd_ref):
    w = BRANCH_WIDTH
    n_rope = ROPE_DIMS // 2
    n_ax = HEAD_DIM // 2
    za = za_ref[...]
    rc, rl, rh = rc_ref[...], rl_ref[...], rh_ref[...]
    qa = _rotate(za[:, :w], rc, rl, rh, n_rope) * Q_SCALE
    ka = _rotate(za[:, w:2 * w], rc, rl, rh, n_rope)
    va = za[:, 2 * w:]
    ac, al, ah = ac_ref[...], al_ref[...], ah_ref[...]
    qd = _rotate(_head_rms(zq_ref[...], qn_ref[...], N_HEADS), ac, al, ah, n_ax) * Q_SCALE
    zkv = zkv_ref[...]
    kw = GQA_KV_HEADS * HEAD_DIM
    kd = _rotate(_head_rms(zkv[:, :kw], kn_ref[...], GQA_KV_HEADS), ac[:, :kw], al[:, :kw], ah[:, :kw], n_ax)
    vd = zkv[:, kw:]
    for h in range(N_HEADS):
        sl = slice(h * HEAD_DIM, (h + 1) * HEAD_DIM)
        qa_ref[0, h] = qa[:, sl].astype(BF16)
        ka_ref[0, h] = ka[:, sl].astype(BF16)
        va_ref[0, h] = va[:, sl].astype(BF16)
        qd_ref[0, h] = qd[:, sl].astype(BF16)
    for h in range(GQA_KV_HEADS):
        sl = slice(h * HEAD_DIM, (h + 1) * HEAD_DIM)
        kd_ref[0, h] = kd[:, sl].astype(BF16)
        vd_ref[0, h] = vd[:, sl].astype(BF16)


def qkv_prep(z, batch, seq, rope_tabs, axial_tabs, q_norm, k_norm, tm=512):
    tm = min(tm, seq)
    nt = seq // tm
    w = BRANCH_WIDTH
    row = lambda b, i: b * nt + i
    tab_spec = pl.BlockSpec((tm, w), lambda b, i: (i, 0))
    head_out = lambda nh: pl.BlockSpec((1, nh, tm, HEAD_DIM), lambda b, i: (b, 0, i, 0))
    head_shape = lambda nh: jax.ShapeDtypeStruct((batch, nh, seq, HEAD_DIM), BF16)
    return pl.pallas_call(
        _qkv_prep_kernel,
        grid=(batch, nt),
        in_specs=[pl.BlockSpec((tm, IN_A), lambda b, i: (row(b, i), 0)),
                  pl.BlockSpec((tm, w), lambda b, i: (row(b, i), OFF_DQ // w)),
                  pl.BlockSpec((tm, w), lambda b, i: (row(b, i), OFF_DKV // w)),
                  tab_spec, tab_spec, tab_spec, tab_spec, tab_spec, tab_spec,
                  pl.BlockSpec((1, w), lambda b, i: (0, 0)),
                  pl.BlockSpec((1, GQA_KV_HEADS * HEAD_DIM), lambda b, i: (0, 0))],
        out_specs=[head_out(N_HEADS), head_out(N_HEADS), head_out(N_HEADS),
                   head_out(N_HEADS), head_out(GQA_KV_HEADS), head_out(GQA_KV_HEADS)],
        out_shape=[head_shape(N_HEADS), head_shape(N_HEADS), head_shape(N_HEADS),
                   head_shape(N_HEADS), head_shape(GQA_KV_HEADS), head_shape(GQA_KV_HEADS)],
        compiler_params=_params("parallel", "parallel"),
        name="qkv_prep",
    )(z, z, z, *rope_tabs, *axial_tabs,
      jnp.tile(q_norm.reshape(1, HEAD_DIM), (1, N_HEADS)),
      jnp.tile(k_norm.reshape(1, HEAD_DIM), (1, GQA_KV_HEADS)))


A_TQ = 1024
A_SUB = 256
A_RADIUS = 64


def _dilated_windows():
    out = []
    for window, dil in DILATED_PATTERNS:
        halo = -(-(window // 2) // 128) * 128
        out.append((dil, -halo, A_SUB + 2 * halo))
    return out


def _dilated_bias():
    biases = []
    for dil, first, width in _dilated_windows():
        qi = jnp.arange(A_SUB, dtype=jnp.int32)[:, None]
        kj = jnp.arange(width, dtype=jnp.int32)[None, :] + first
        delta = kj - qi
        ok = (jnp.abs(delta) <= A_RADIUS * dil) & ((delta & (dil - 1)) == 0)
        biases.append(jnp.where(ok, 0.0, NEG_INF).astype(F32))
    return biases


def _dilated_attn_kernel(q_ref, kp_ref, kc_ref, kn_ref, vp_ref, vc_ref, vn_ref, b0_ref, b1_ref, b2_ref,
                         o_ref, k3_ref, v3_ref, *, seq, tq):
    i = pl.program_id(2)
    k3_ref[0:tq] = kp_ref[0, 0]
    k3_ref[tq:2 * tq] = kc_ref[0, 0]
    k3_ref[2 * tq:3 * tq] = kn_ref[0, 0]
    v3_ref[0:tq] = vp_ref[0, 0]
    v3_ref[tq:2 * tq] = vc_ref[0, 0]
    v3_ref[2 * tq:3 * tq] = vn_ref[0, 0]
    bias_refs = (b0_ref, b1_ref, b2_ref)
    windows = _dilated_windows()
    for u in range(tq // A_SUB):
        q = q_ref[0, 0, u * A_SUB:(u + 1) * A_SUB, :]
        scores = []
        for (dil, first, width), b_ref in zip(windows, bias_refs):
            start = tq + u * A_SUB + first
            s = _bdot_nt(q, k3_ref[start:start + width, :]) + b_ref[...]
            kpos = (i - 1) * tq + start + lax.broadcasted_iota(jnp.int32, (1, width), 1)
            s = jnp.where((kpos >= 0) & (kpos < seq), s, NEG_INF)
            scores.append((s, start, width))
        m = functools.reduce(jnp.maximum, [jnp.max(s, axis=-1, keepdims=True) for s, _, _ in scores])
        l = jnp.zeros_like(m)
        acc = jnp.zeros((A_SUB, HEAD_DIM), F32)
        for s, start, width in scores:
            p = jnp.exp2(s - m)
            l = l + jnp.sum(p, axis=-1, keepdims=True)
            acc = acc + _bdot(p, v3_ref[start:start + width, :])
        o_ref[0, 0, u * A_SUB:(u + 1) * A_SUB, :] = acc / l


def dilated_attention(qa, ka, va):
    batch, nh, seq, hd = qa.shape
    tq = min(A_TQ, seq)
    assert tq == A_TQ, "key halo of radius * max dilation needs full query tiles"
    nt = seq // tq
    cur = pl.BlockSpec((1, 1, tq, hd), lambda b, h, i: (b, h, i, 0))
    prev = pl.BlockSpec((1, 1, tq, hd), lambda b, h, i: (b, h, jnp.maximum(i - 1, 0), 0))
    nxt = pl.BlockSpec((1, 1, tq, hd), lambda b, h, i: (b, h, jnp.minimum(i + 1, nt - 1), 0))
    biases = _dilated_bias()
    bias_specs = [pl.BlockSpec(b.shape, lambda b_, h, i: (0, 0)) for b in biases]
    return pl.pallas_call(
        functools.partial(_dilated_attn_kernel, seq=seq, tq=tq),
        grid=(batch, nh, nt),
        in_specs=[cur, prev, cur, nxt, prev, cur, nxt] + bias_specs,
        out_specs=cur,
        out_shape=jax.ShapeDtypeStruct((batch, nh, seq, hd), F32),
        scratch_shapes=[pltpu.VMEM((3 * tq, hd), BF16), pltpu.VMEM((3 * tq, hd), BF16)],
        compiler_params=_params("parallel", "parallel", "parallel"),
        name="dilated_attention",
    )(qa, ka, ka, ka, va, va, va, *biases)


GQA_SUB = 128


def _gqa_kernel(q_ref, k_ref, v_ref, o_ref, *scratch, rep, tq):
    j = pl.program_id(3)
    n_sub = tq // GQA_SUB
    blocks = [(r, u) for r in range(rep) for u in range(n_sub)]
    m_refs, l_refs, acc_refs = (scratch[i * len(blocks):(i + 1) * len(blocks)] for i in range(3))

    @pl.when(j == 0)
    def _():
        for m_ref, l_ref, acc_ref in zip(m_refs, l_refs, acc_refs):
            m_ref[...] = jnp.full(m_ref.shape, NEG_INF, F32)
            l_ref[...] = jnp.zeros(l_ref.shape, F32)
            acc_ref[...] = jnp.zeros(acc_ref.shape, F32)

    k = k_ref[0, 0]
    v = v_ref[0, 0]
    scores = [_bdot_nt(k, q_ref[0, r, u * GQA_SUB:(u + 1) * GQA_SUB, :]) for r, u in blocks]
    for s, m_ref, l_ref, acc_ref in zip(scores, m_refs, l_refs, acc_refs):
        m_prev = m_ref[...]
        m_new = jnp.maximum(m_prev, jnp.max(s, axis=0, keepdims=True))
        alpha = jnp.exp2(m_prev - m_new)
        p = jnp.exp2(s - m_new)
        l_ref[...] = alpha * l_ref[...] + jnp.sum(p, axis=0, keepdims=True)
        pv = lax.dot_general(v, p.astype(BF16), (((0,), (0,)), ((), ())), preferred_element_type=F32)
        acc_ref[...] = alpha * acc_ref[...] + pv
        m_ref[...] = m_new

    @pl.when(j == pl.num_programs(3) - 1)
    def _():
        for (r, u), l_ref, acc_ref in zip(blocks, l_refs, acc_refs):
            o_ref[0, r, :, u * GQA_SUB:(u + 1) * GQA_SUB] = acc_ref[...] / l_ref[...]


def gqa_attention(qd, kd, vd, tq=1024, tk=2048):
    batch, nh, seq, hd = qd.shape
    ng = kd.shape[1]
    rep = nh // ng
    tq = min(tq, seq)
    tk = min(tk, seq)
    n_blocks = rep * (tq // GQA_SUB)
    return pl.pallas_call(
        functools.partial(_gqa_kernel, rep=rep, tq=tq),
        grid=(batch, ng, seq // tq, seq // tk),
        in_specs=[pl.BlockSpec((1, rep, tq, hd), lambda b, g, i, j: (b, g, i, 0)),
                  pl.BlockSpec((1, 1, tk, hd), lambda b, g, i, j: (b, g, j, 0)),
                  pl.BlockSpec((1, 1, tk, hd), lambda b, g, i, j: (b, g, j, 0))],
        out_specs=pl.BlockSpec((1, rep, hd, tq), lambda b, g, i, j: (b, g, 0, i)),
        out_shape=jax.ShapeDtypeStruct((batch, nh, hd, seq), F32),
        scratch_shapes=([pltpu.VMEM((1, GQA_SUB), F32)] * (2 * n_blocks)
                        + [pltpu.VMEM((hd, GQA_SUB), F32)] * n_blocks),
        compiler_params=_params("parallel", "parallel", "parallel", "arbitrary"),
        name="gqa_attention",
    )(qd, kd, vd)


def _matmul_kernel(a_ref, w_ref, o_ref):
    o_ref[...] = jnp.dot(a_ref[...], w_ref[...], preferred_element_type=F32).astype(o_ref.dtype)


def matmul_bf16(a, w, tm, tn, out_dtype=F32):
    n, k = a.shape
    m = w.shape[1]
    tm = min(tm, n)
    return pl.pallas_call(
        _matmul_kernel,
        grid=(n // tm, m // tn),
        in_specs=[pl.BlockSpec((tm, k), lambda i, j: (i, 0)),
                  pl.BlockSpec((k, tn), lambda i, j: (0, j))],
        out_specs=pl.BlockSpec((tm, tn), lambda i, j: (i, j)),
        out_shape=jax.ShapeDtypeStruct((n, m), out_dtype),
        compiler_params=_params("parallel", "arbitrary"),
        name="matmul_bf16",
    )(a, w)


def _sigmoid(x):
    return 1.0 / (1.0 + jnp.exp(-x))


def _softplus(x):
    return jnp.maximum(x, 0.0) + jnp.log(1.0 + jnp.exp(-jnp.abs(x)))


def _shift_rows(x, edge_row, down):
    rows = x.shape[0]
    ridx = lax.broadcasted_iota(jnp.int32, x.shape, 0)
    if down:
        return jnp.where(ridx == 0, edge_row, pltpu.roll(x, 1, 0))
    return jnp.where(ridx == rows - 1, edge_row, pltpu.roll(x, rows - 1, 0))


WKV_FIELDS = 6


def _rwkv_prep_kernel(xn_ref, xp_ref, xq_ref, zb_ref, zp_ref, zq_ref, mux_ref, murkv_ref, lw1_ref, lw2_ref,
                      w0a0_ref, g1_ref, g2_ref, kk_ref, ka_ref, rk_ref,
                      fw_ref, bw_ref, bonus_ref, gate_ref, *, tiles_per_seq):
    w = BRANCH_WIDTH
    i = pl.program_id(0)
    first = (i % tiles_per_seq) == 0
    last = (i % tiles_per_seq) == tiles_per_seq - 1
    xn = xn_ref[...]
    x_shift = (_shift_rows(xn, jnp.where(first, 0.0, xp_ref[7:8, :]), True),
               _shift_rows(xn, jnp.where(last, 0.0, xq_ref[0:1, :]), False))
    zb = zb_ref[...]
    z_prev = _shift_rows(zb, jnp.where(first, 0.0, zp_ref[7:8, :]), True)
    z_next = _shift_rows(zb, jnp.where(last, 0.0, zq_ref[0:1, :]), False)
    mu = murkv_ref[...]
    rkv = zb + mu[0:1] * (z_prev - zb) + mu[1:2] * (z_next - zb)
    r, k, v = rkv[:, :w], rkv[:, w:2 * w], rkv[:, 2 * w:]
    kap = k * kk_ref[...]
    kap = kap * lax.rsqrt(_head_sum(kap * kap, N_HEADS) + 1e-12)
    gate_ref[...] = _bdot(_sigmoid(_bdot(xn, g1_ref[...])), g2_ref[...])
    bonus = jnp.zeros_like(v)
    lora_lane = lax.broadcasted_iota(jnp.int32, (xn.shape[0], lw1_ref.shape[-1]), 1)
    for d, out_ref in enumerate((fw_ref, bw_ref)):
        xd = xn + mux_ref[d:d + 1, :] * (x_shift[d] - xn)
        h = _bdot(xd, lw1_ref[d])
        h = jnp.where(lora_lane < lw1_ref.shape[-1] // 2, jnp.tanh(h), h)
        h = _bdot(h, lw2_ref[d]) + w0a0_ref[d:d + 1, :]
        w_log = -_softplus(-h[:, :w]) - 0.5
        log_decay = -jnp.exp(w_log)
        iclr = _sigmoid(h[:, w:])
        k_d = k * (1.0 + (iclr - 1.0) * ka_ref[...])
        bonus = bonus + _head_sum(r * k_d * rk_ref[...], N_HEADS) * v
        for j, field in enumerate((r, log_decay, k_d, v, kap, iclr * kap)):
            out_ref[:, j * w:(j + 1) * w] = field
    bonus_ref[...] = bonus


def rwkv_prep(xn, z, seq, mu_x, mu_rkv, lw1, lw2, w0a0, g1, g2, k_k, k_a, r_k, tm=512):
    n, d = xn.shape
    w = BRANCH_WIDTH
    tm = min(tm, seq)
    halo = 8
    prev_halo = lambda i: (jnp.maximum(i * (tm // halo) - 1, 0), 0)
    next_halo = lambda i: (jnp.minimum((i + 1) * (tm // halo), n // halo - 1), 0)
    full = lambda a: pl.BlockSpec(a.shape, lambda i: (0,) * a.ndim)
    params = (mu_x, mu_rkv, lw1, lw2, w0a0, g1, g2, k_k.reshape(1, w), k_a.reshape(1, w), r_k.reshape(1, w))
    rows = lambda width: pl.BlockSpec((tm, width), lambda i: (i, 0))
    return pl.pallas_call(
        functools.partial(_rwkv_prep_kernel, tiles_per_seq=seq // tm),
        grid=(n // tm,),
        in_specs=[rows(d), pl.BlockSpec((halo, d), prev_halo), pl.BlockSpec((halo, d), next_halo),
                  pl.BlockSpec((tm, IN_B), lambda i: (i, OFF_B // IN_B)),
                  pl.BlockSpec((halo, IN_B), lambda i: (prev_halo(i)[0], OFF_B // IN_B)),
                  pl.BlockSpec((halo, IN_B), lambda i: (next_halo(i)[0], OFF_B // IN_B))]
                 + [full(p) for p in params],
        out_specs=[rows(WKV_FIELDS * w), rows(WKV_FIELDS * w), rows(w), rows(w)],
        out_shape=[jax.ShapeDtypeStruct((n, WKV_FIELDS * w), F32), jax.ShapeDtypeStruct((n, WKV_FIELDS * w), F32),
                   jax.ShapeDtypeStruct((n, w), F32), jax.ShapeDtypeStruct((n, w), F32)],
        compiler_params=_params("parallel"),
        name="rwkv_prep",
    )(xn, xn, xn, z, z, z, *params)


WKV_CHUNK = 64


def _wkv_constants(reverse):
    c, nh = WKV_CHUNK, N_HEADS
    t = jnp.arange(c)
    before = (t[None, :] > t[:, None]) if reverse else (t[None, :] < t[:, None])
    incl = before | (t[None, :] == t[:, None])
    per_head = lambda m: jnp.kron(jnp.eye(nh, dtype=F32), m.astype(F32))
    return incl.astype(F32), per_head(before), per_head(incl)


def _wkv_chunk(x, s_bd, tri, m_strict, m_incl, head_rows, s_mask, last_row):
    c, w, nh = WKV_CHUNK, BRANCH_WIDTH, N_HEADS
    r, logw, k, v, kap, b = (x[:, j * w:(j + 1) * w] for j in range(WKV_FIELDS))
    g = jnp.dot(tri, logw, precision=lax.Precision.HIGHEST, preferred_element_type=F32)
    gam = jnp.exp(g)
    g_inv = jnp.exp(-g)
    kap_t, r_t = kap * jnp.exp(g - logw), r * gam
    k_t, b_t = k * g_inv, b * g_inv

    def per_head(a):
        return jnp.concatenate([jnp.where(head_rows[h:h + 1, :] > 0.0, a, 0.0) for h in range(nh)], axis=0)

    def wide(m):
        return m[0:c] + m[c:2 * c] + m[2 * c:3 * c] + m[3 * c:4 * c]

    gram = _bdot_nt(jnp.concatenate([per_head(kap_t), per_head(r_t)], axis=0),
                    jnp.concatenate([per_head(k_t), per_head(b_t)], axis=0))
    hc = nh * c
    kk, n = gram[:hc, :hc] * m_strict, gram[:hc, hc:] * m_strict
    rk, rb = gram[hc:, :hc] * m_incl, gram[hc:, hc:] * m_incl
    eye = jnp.where(lax.broadcasted_iota(jnp.int32, (hc, hc), 0) == lax.broadcasted_iota(jnp.int32, (hc, hc), 1),
                    1.0, 0.0)
    inv = eye - n
    power = _bdot(n, n)
    levels = c.bit_length() - 2
    for level in range(levels):
        inv = inv + _bdot(inv, power)
        if level + 1 < levels:
            power = _bdot(power, power)
    from_state = _bdot_nt(jnp.concatenate([kap_t, r_t], axis=0), s_bd)
    v_heads = per_head(v)
    u = _bdot(wide(inv), per_head(from_state[:c] + _bdot(wide(kk), v_heads)))
    y = from_state[c:] + _bdot(jnp.concatenate([wide(rk), -wide(rb)], axis=1),
                               jnp.concatenate([v_heads, per_head(u)], axis=0))
    update = lax.dot_general(jnp.concatenate([v, u], axis=0).astype(BF16),
                             jnp.concatenate([k_t, -b_t], axis=0).astype(BF16),
                             (((0,), (0,)), ((), ())), preferred_element_type=F32)
    s_new = (s_bd + update * s_mask) * gam[last_row:last_row + 1, :]
    return y, s_new


def _wkv_chunked_kernel(fw_ref, bw_ref, trif_ref, msf_ref, mif_ref, trib_ref, msb_ref, mib_ref, hr_ref, sm_ref,
                        yf_ref, yb_ref, state_ref, *, batch, rows):
    @pl.when(pl.program_id(0) == 0)
    def _():
        state_ref[...] = jnp.zeros(state_ref.shape, F32)

    c = WKV_CHUNK
    n_chunks = rows // c
    head_rows = hr_ref[...]
    s_mask = sm_ref[...]
    consts = ((trif_ref[...], msf_ref[...], mif_ref[...]), (trib_ref[...], msb_ref[...], mib_ref[...]))
    refs, y_refs = (fw_ref, bw_ref), (yf_ref, yb_ref)

    def body(i, carry):
        bases = (pl.multiple_of(i * c, c), pl.multiple_of((n_chunks - 1 - i) * c, c))
        for bi in range(batch):
            for d in range(2):
                x = refs[d][bi, pl.ds(bases[d], c), :]
                y, s_new = _wkv_chunk(x, state_ref[2 * bi + d], *consts[d], head_rows, s_mask,
                                      last_row=(0 if d == 1 else c - 1))
                y_refs[d][bi, pl.ds(bases[d], c), :] = y
                state_ref[2 * bi + d] = s_new
        return carry

    lax.fori_loop(0, n_chunks, body, 0)


def wkv_chunked(fw, bw, batch, seq, rows=256):
    w = BRANCH_WIDTH
    rows = min(rows, seq)
    nb = seq // rows
    head_of_lane = jnp.arange(w) // HEAD_DIM
    head_rows = (jnp.arange(8)[:, None] == head_of_lane[None, :]).astype(F32)
    s_mask = (head_of_lane[:, None] == head_of_lane[None, :]).astype(F32)
    consts = [*_wkv_constants(False), *_wkv_constants(True), head_rows, s_mask]
    in_f = pl.BlockSpec((batch, rows, WKV_FIELDS * w), lambda c: (0, c, 0))
    in_b = pl.BlockSpec((batch, rows, WKV_FIELDS * w), lambda c: (0, nb - 1 - c, 0))
    out_shape = jax.ShapeDtypeStruct((batch, seq, w), F32)
    yf, yb = pl.pallas_call(
        functools.partial(_wkv_chunked_kernel, batch=batch, rows=rows),
        grid=(nb,),
        in_specs=[in_f, in_b] + [pl.BlockSpec(a.shape, lambda c: (0, 0)) for a in consts],
        out_specs=[pl.BlockSpec((batch, rows, w), lambda c: (0, c, 0)),
                   pl.BlockSpec((batch, rows, w), lambda c: (0, nb - 1 - c, 0))],
        out_shape=[out_shape, out_shape],
        scratch_shapes=[pltpu.VMEM((2 * batch, w, w), F32)],
        compiler_params=_params("arbitrary"),
        name="wkv_chunked",
    )(fw.reshape(batch, seq, -1), bw.reshape(batch, seq, -1), *consts)
    return yf.reshape(batch * seq, w), yb.reshape(batch * seq, w)


S5_SEGMENTS = 8
S5_WIDTH = S5_GROUPS * S5_STATE


def _s5_discretize(a_re, a_im, log_dt, b_re, b_im, c_re, c_im):
    g, p, c = S5_GROUPS, S5_STATE, S5_GROUP_CH
    dt = jnp.exp(log_dt)[:, None]
    mag = jnp.exp(a_re * dt)
    bar_re, bar_im = mag * jnp.cos(a_im * dt), mag * jnp.sin(a_im * dt)
    den = a_re * a_re + a_im * a_im
    f_re = ((bar_re - 1.0) * a_re + bar_im * a_im) / den
    f_im = (bar_im * a_re - (bar_re - 1.0) * a_im) / den
    bb_re = f_re[..., None] * b_re - f_im[..., None] * b_im
    bb_im = f_re[..., None] * b_im + f_im[..., None] * b_re
    eye_g = jnp.eye(g, dtype=F32)
    w_in = jnp.concatenate(
        [jnp.einsum('gpc,gh->gchp', bb, eye_g).reshape(g * c, g * p) for bb in (bb_re, bb_im)], axis=1)
    w_out = jnp.concatenate(
        [jnp.einsum('gcp,gh->gphc', cc, eye_g).reshape(g * p, g * c) for cc in (c_re, -c_im)], axis=0)
    return bar_re.reshape(1, g * p), bar_im.reshape(1, g * p), w_in.astype(BF16), w_out.astype(BF16), dt, a_re, a_im


def _s5_powers(a_re, a_im, dt, count, reverse):
    j = jnp.arange(1, count + 1, dtype=F32)
    if reverse:
        j = j[::-1]
    e = j[:, None, None] * (a_re * dt)[None]
    th = j[:, None, None] * (a_im * dt)[None]
    mag = jnp.exp(e)
    return (mag * jnp.cos(th)).reshape(count, -1), (mag * jnp.sin(th)).reshape(count, -1)


def _s5_scan_kernel(u_ref, win_ref, lre_ref, lim_ref, pre_ref, pim_ref, wout_ref, y_ref,
                    xre_ref, xim_ref, cre_ref, cim_ref, *, reverse, chunk):
    seg = chunk // S5_SEGMENTS
    nw = S5_WIDTH

    @pl.when(pl.program_id(1) == 0)
    def _():
        cre_ref[...] = jnp.zeros(cre_ref.shape, F32)
        cim_ref[...] = jnp.zeros(cim_ref.shape, F32)

    bu = jnp.dot(u_ref[...].astype(BF16), win_ref[...], preferred_element_type=F32)
    xre_ref[...] = bu[:, :nw]
    xim_ref[...] = bu[:, nw:]
    lre = lre_ref[...]
    lim = lim_ref[...]
    group = lambda s: pl.ds(pl.multiple_of(s * S5_SEGMENTS, S5_SEGMENTS), S5_SEGMENTS)

    def local_step(s, carry):
        xr, xi = carry
        rows = group((seg - 1 - s) if reverse else s)
        nr = lre * xr - lim * xi + xre_ref[rows, :]
        ni = lre * xi + lim * xr + xim_ref[rows, :]
        xre_ref[rows, :] = nr
        xim_ref[rows, :] = ni
        return nr, ni

    zero = jnp.zeros((S5_SEGMENTS, nw), F32)
    fin_re, fin_im = lax.fori_loop(0, seg, local_step, (zero, zero))

    full_seg = 0 if reverse else seg - 1
    pl_re, pl_im = pre_ref[full_seg:full_seg + 1, :], pim_ref[full_seg:full_seg + 1, :]
    ir, ii = cre_ref[...], cim_ref[...]
    init_re, init_im = [None] * S5_SEGMENTS, [None] * S5_SEGMENTS
    for j in (range(S5_SEGMENTS - 1, -1, -1) if reverse else range(S5_SEGMENTS)):
        init_re[j], init_im[j] = ir, ii
        fr, fi = fin_re[j:j + 1, :], fin_im[j:j + 1, :]
        ir, ii = fr + (pl_re * ir - pl_im * ii), fi + (pl_re * ii + pl_im * ir)
    cre_ref[...] = ir
    cim_ref[...] = ii
    init_re = jnp.concatenate(init_re, axis=0)
    init_im = jnp.concatenate(init_im, axis=0)

    def correct(s, carry):
        rows = group(s)
        pr, pi = pre_ref[pl.ds(s, 1), :], pim_ref[pl.ds(s, 1), :]
        xre_ref[rows, :] = xre_ref[rows, :] + (pr * init_re - pi * init_im)
        xim_ref[rows, :] = xim_ref[rows, :] + (pr * init_im + pi * init_re)
        return carry

    lax.fori_loop(0, seg, correct, 0)
    y_ref[...] = (jnp.dot(xre_ref[...].astype(BF16), wout_ref[:nw, :], preferred_element_type=F32)
                  + jnp.dot(xim_ref[...].astype(BF16), wout_ref[nw:, :], preferred_element_type=F32))


def _segment_interleave(a, batch, seq, chunk, inverse=False):
    w = a.shape[-1]
    seg = chunk // S5_SEGMENTS
    shape = (batch, seq // chunk, seg, S5_SEGMENTS, w) if inverse else (batch, seq // chunk, S5_SEGMENTS, seg, w)
    return jnp.swapaxes(a.reshape(shape), 2, 3).reshape(batch * seq, w)


def s5_scan(u_interleaved, batch, seq, disc, reverse, chunk):
    lam_re, lam_im, w_in, w_out, dt, a_re, a_im = disc
    w = BRANCH_WIDTH
    nc = seq // chunk
    seg = chunk // S5_SEGMENTS
    pw_re, pw_im = _s5_powers(a_re, a_im, dt, seg, reverse)
    order = (lambda c: nc - 1 - c) if reverse else (lambda c: c)
    full = lambda a: pl.BlockSpec(a.shape, lambda b, c: (0, 0))
    return pl.pallas_call(
        functools.partial(_s5_scan_kernel, reverse=reverse, chunk=chunk),
        grid=(batch, nc),
        in_specs=[pl.BlockSpec((chunk, w), lambda b, c: (b * nc + order(c), 0)),
                  full(w_in), full(lam_re), full(lam_im), full(pw_re), full(pw_im), full(w_out)],
        out_specs=pl.BlockSpec((chunk, w), lambda b, c: (b * nc + order(c), 0)),
        out_shape=jax.ShapeDtypeStruct((batch * seq, w), F32),
        scratch_shapes=[pltpu.VMEM((chunk, S5_WIDTH), F32), pltpu.VMEM((chunk, S5_WIDTH), F32),
                        pltpu.VMEM((1, S5_WIDTH), F32), pltpu.VMEM((1, S5_WIDTH), F32)],
        compiler_params=_params("parallel", "arbitrary"),
        name="s5_scan_bwd" if reverse else "s5_scan_fwd",
    )(u_interleaved, w_in, lam_re, lam_im, pw_re, pw_im, w_out)


def s5_bidirectional(z, batch, seq, discs, chunk=1024):
    chunk = min(chunk, seq)
    u = _segment_interleave(z[:, OFF_C:OFF_C + BRANCH_WIDTH], batch, seq, chunk)
    return [_segment_interleave(s5_scan(u, batch, seq, disc, reverse=(d == 1), chunk=chunk),
                                batch, seq, chunk, inverse=True) for d, disc in enumerate(discs)]


def _gelu_tanh(y):
    return 0.5 * y * (1.0 + jnp.tanh(math.sqrt(2.0 / math.pi) * (y + 0.044715 * (y * y * y))))


def _merge_kernel(x_ref, ya_ref, yd_ref, wf_ref, wb_ref, bonus_ref, rg_ref, sf_ref, sb_ref, u_ref, zg_ref,
                  gb_ref, wbr_ref, wout_ref, lnw_ref, lnb_ref, s5d_ref, gluw_ref, glub_ref, o_ref):
    w = BRANCH_WIDTH
    ya = jnp.concatenate([ya_ref[0, h] for h in range(N_HEADS)], axis=1)
    ys = wf_ref[...] + wb_ref[...]
    cen = ys - _head_sum(ys, N_HEADS) * (1.0 / HEAD_DIM)
    var = _head_sum(cen * cen, N_HEADS) * (1.0 / HEAD_DIM)
    yb = (cen * lax.rsqrt(var + RWKV_GN_EPS) * lnw_ref[...] + lnb_ref[...] + bonus_ref[...]) * rg_ref[...]
    yc = sf_ref[...] + sb_ref[...] + s5d_ref[...] * u_ref[...]
    h = _bdot(_gelu_tanh(yc), gluw_ref[...]) + glub_ref[...]
    yc = h[:, :w] * _sigmoid(h[:, w:])
    proj_d = sum(lax.dot_general(yd_ref[0, h].astype(BF16), wbr_ref[3, h * HEAD_DIM:(h + 1) * HEAD_DIM, :],
                                 (((0,), (0,)), ((), ())), preferred_element_type=F32) for h in range(N_HEADS))
    merged = jnp.zeros(o_ref.shape, F32)
    for i, proj in enumerate((_bdot(ya, wbr_ref[0]), _bdot(yb, wbr_ref[1]), _bdot(yc, wbr_ref[2]), proj_d)):
        gate = _sigmoid(zg_ref[:, i * D_MODEL:(i + 1) * D_MODEL] + gb_ref[i:i + 1, :])
        merged = merged + gate * proj
    o_ref[...] = x_ref[...] + _bdot(merged, wout_ref[...])


def merge_branches(x, ya, yd, wkv_f, wkv_b, bonus, rgate, s5_f, s5_b, z, zg, gate_b, w_branch, w_out,
                   ln_w, ln_b, s5_d, glu_w, glu_b, batch, seq, tm=512):
    w = BRANCH_WIDTH
    d = D_MODEL
    tm = min(tm, seq)
    nt = seq // tm
    rows = lambda width, col=0: pl.BlockSpec((tm, width), lambda b, i: (b * nt + i, col))
    heads = pl.BlockSpec((1, N_HEADS, tm, HEAD_DIM), lambda b, i: (b, 0, i, 0))
    heads_t = pl.BlockSpec((1, N_HEADS, HEAD_DIM, tm), lambda b, i: (b, 0, 0, i))
    full = lambda a: pl.BlockSpec(a.shape, lambda b, i: (0,) * a.ndim)
    params = (gate_b, w_branch, w_out, ln_w.reshape(1, w), ln_b.reshape(1, w), s5_d.reshape(1, w),
              glu_w, glu_b.reshape(1, 2 * w))
    return pl.pallas_call(
        _merge_kernel,
        grid=(batch, nt),
        in_specs=[rows(d), heads, heads_t, rows(w), rows(w), rows(w), rows(w), rows(w), rows(w),
                  rows(w, OFF_C // w), rows(N_BRANCHES * d)] + [full(p) for p in params],
        out_specs=rows(d),
        out_shape=jax.ShapeDtypeStruct(x.shape, F32),
        compiler_params=_params("parallel", "parallel"),
        name="merge_branches",
    )(x, ya, yd, wkv_f, wkv_b, bonus, rgate, s5_f, s5_b, z, zg, *params)


def _silu(x):
    return x * _sigmoid(x)


def _dense_ffn_kernel(x_ref, g_ref, wg_ref, wu_ref, wd_ref, o_ref, xn_ref):
    @pl.when(pl.program_id(1) == 0)
    def _():
        x = x_ref[...]
        xn_ref[...] = _rms(x, g_ref[...]).astype(BF16)
        o_ref[...] = x

    xn = xn_ref[...]
    h = (_silu(jnp.dot(xn, wg_ref[...], preferred_element_type=F32))
         * jnp.dot(xn, wu_ref[...], preferred_element_type=F32))
    o_ref[...] += _bdot(h, wd_ref[...])


def dense_ffn(x, g, w_gate, w_up, w_down, tm=1024, tf=1408):
    n, d = x.shape
    ff = w_gate.shape[1]
    tm = min(tm, n)
    return pl.pallas_call(
        _dense_ffn_kernel,
        grid=(n // tm, ff // tf),
        in_specs=[pl.BlockSpec((tm, d), lambda i, f: (i, 0)),
                  pl.BlockSpec((1, d), lambda i, f: (0, 0)),
                  pl.BlockSpec((d, tf), lambda i, f: (0, f)),
                  pl.BlockSpec((d, tf), lambda i, f: (0, f)),
                  pl.BlockSpec((tf, d), lambda i, f: (f, 0))],
        out_specs=pl.BlockSpec((tm, d), lambda i, f: (i, 0)),
        out_shape=jax.ShapeDtypeStruct((n, d), F32),
        scratch_shapes=[pltpu.VMEM((tm, d), BF16)],
        compiler_params=_params("parallel", "arbitrary"),
        name="dense_ffn",
    )(x, g.reshape(1, d), w_gate, w_up, w_down)


def _router_kernel(x_ref, g_ref, rt_ref, xnb_ref, sel_ref, wt_ref):
    xn = _rms(x_ref[...], g_ref[...])
    xnb_ref[...] = xn.astype(BF16)
    logits = lax.dot_general(rt_ref[...], xn, (((1,), (1,)), ((), ())),
                             precision=lax.Precision.HIGHEST, preferred_element_type=F32)
    e = lax.broadcasted_iota(jnp.int32, logits.shape, 0)
    m1 = jnp.max(logits, axis=0, keepdims=True)
    i1 = jnp.min(jnp.where(logits == m1, e, N_EXPERTS), axis=0, keepdims=True)
    rest = jnp.where(e == i1, NEG_INF, logits)
    m2 = jnp.max(rest, axis=0, keepdims=True)
    i2 = jnp.min(jnp.where(rest == m2, e, N_EXPERTS), axis=0, keepdims=True)
    ratio = jnp.exp(m2 - m1)
    w1 = 1.0 / (1.0 + ratio)
    w2 = ratio / (1.0 + ratio)
    sel_ref[...] = jnp.where((e == i1) | (e == i2), 1.0, 0.0)
    wt_ref[...] = jnp.where(e == i1, w1, jnp.where(e == i2, w2, 0.0))


def moe_route(x, g, router, tm=1024):
    n, d = x.shape
    tm = min(tm, n)
    ne = router.shape[1]
    return pl.pallas_call(
        _router_kernel,
        grid=(n // tm,),
        in_specs=[pl.BlockSpec((tm, d), lambda i: (i, 0)),
                  pl.BlockSpec((1, d), lambda i: (0, 0)),
                  pl.BlockSpec((ne, d), lambda i: (0, 0))],
        out_specs=[pl.BlockSpec((tm, d), lambda i: (i, 0)),
                   pl.BlockSpec((ne, tm), lambda i: (0, i)),
                   pl.BlockSpec((ne, tm), lambda i: (0, i))],
        out_shape=[jax.ShapeDtypeStruct((n, d), BF16), jax.ShapeDtypeStruct((ne, n), F32),
                   jax.ShapeDtypeStruct((ne, n), F32)],
        compiler_params=_params("parallel"),
        name="moe_router",
    )(x, g.reshape(1, d), router.T)


MOE_ROWS = 128
MOE_STATIC_BLOCKS = (2, 3)


def _moe_kernel(x_ref, xnb_ref, sel_ref, wt_ref, wg_ref, wu_ref, wd_ref, o_ref,
                rank_ref, xg_ref, acc_ref, nblk_ref):
    e = pl.program_id(1)
    f = pl.program_id(2)
    nf = pl.num_programs(2)
    tm = x_ref.shape[0]

    @pl.when((e == 0) & (f == 0))
    def _():
        o_ref[...] = x_ref[...]
        before = (lax.broadcasted_iota(jnp.int32, (tm, tm), 0) < lax.broadcasted_iota(jnp.int32, (tm, tm), 1))
        rank_ref[...] = jnp.dot(sel_ref[...].astype(BF16), jnp.where(before, 1.0, 0.0).astype(BF16),
                                preferred_element_type=F32)

    sel_e = sel_ref[pl.ds(e, 1), :]
    rank_e = rank_ref[pl.ds(e, 1), :]
    wt_e = wt_ref[pl.ds(e, 1), :]

    @pl.when(f == 0)
    def _():
        count = jnp.sum(sel_e).astype(jnp.int32)
        nblk_ref[0] = (count + MOE_ROWS - 1) // MOE_ROWS

    nblk = nblk_ref[0]

    def process(rows):
        n_rows = rows.stop - rows.start if isinstance(rows, slice) else rows.size
        first = rows.start

        def one_hot():
            slot = (first + lax.broadcasted_iota(jnp.int32, (n_rows, tm), 0)).astype(F32)
            return jnp.where((rank_e == slot) & (sel_e > 0.0), 1.0, 0.0)

        @pl.when(f == 0)
        def _():
            xg_ref[rows, :] = jnp.dot(one_hot().astype(BF16), xnb_ref[...],
                                      preferred_element_type=F32).astype(BF16)
            acc_ref[rows, :] = jnp.zeros((n_rows, acc_ref.shape[1]), F32)

        xg = xg_ref[rows, :]
        h = (_silu(jnp.dot(xg, wg_ref[0], preferred_element_type=F32))
             * jnp.dot(xg, wu_ref[0], preferred_element_type=F32))
        acc_ref[rows, :] += _bdot(h, wd_ref[0])

        @pl.when(f == nf - 1)
        def _():
            hot = one_hot()
            row_w = jnp.sum(hot * wt_e, axis=1, keepdims=True)
            yw = (acc_ref[rows, :] * row_w).astype(BF16)
            o_ref[...] += lax.dot_general(hot.astype(BF16), yw, (((0,), (0,)), ((), ())),
                                          preferred_element_type=F32)

    for n_static in MOE_STATIC_BLOCKS:
        lo = 0 if n_static == MOE_STATIC_BLOCKS[0] else n_static
        hi = n_static if n_static != MOE_STATIC_BLOCKS[-1] else tm // MOE_ROWS
        pl.when((nblk >= lo) & (nblk <= hi))(functools.partial(process, slice(0, n_static * MOE_ROWS)))

    def tail(b, carry):
        process(pl.ds(pl.multiple_of(b * MOE_ROWS, MOE_ROWS), MOE_ROWS))
        return carry

    lax.fori_loop(MOE_STATIC_BLOCKS[-1], nblk, tail, 0)


def moe_ffn(x, xnb, sel, wt, w_gate, w_up, w_down, tm=1024, tf=896):
    n, d = x.shape
    ne, _, ff = w_gate.shape
    tm = min(tm, n)
    return pl.pallas_call(
        _moe_kernel,
        grid=(n // tm, ne, ff // tf),
        in_specs=[pl.BlockSpec((tm, d), lambda i, e, f: (i, 0)),
                  pl.BlockSpec((tm, d), lambda i, e, f: (i, 0)),
                  pl.BlockSpec((ne, tm), lambda i, e, f: (0, i)),
                  pl.BlockSpec((ne, tm), lambda i, e, f: (0, i)),
                  pl.BlockSpec((1, d, tf), lambda i, e, f: (e, 0, f)),
                  pl.BlockSpec((1, d, tf), lambda i, e, f: (e, 0, f)),
                  pl.BlockSpec((1, tf, d), lambda i, e, f: (e, f, 0))],
        out_specs=pl.BlockSpec((tm, d), lambda i, e, f: (i, 0)),
        out_shape=jax.ShapeDtypeStruct((n, d), F32),
        scratch_shapes=[pltpu.VMEM((ne, tm), F32), pltpu.VMEM((tm, d), BF16), pltpu.VMEM((tm, d), F32),
                        pltpu.SMEM((1,), jnp.int32)],
        compiler_params=_params("parallel", "arbitrary", "arbitrary"),
        name="moe_ffn",
    )(x, xnb, sel, wt, w_gate, w_up, w_down)


def _rms_kernel(x_ref, g_ref, o_ref):
    o_ref[...] = _rms(x_ref[...], g_ref[...])


def rms_norm(x, g, tm=1024):
    n, d = x.shape
    tm = min(tm, n)
    return pl.pallas_call(
        _rms_kernel,
        grid=(n // tm,),
        in_specs=[pl.BlockSpec((tm, d), lambda i: (i, 0)), pl.BlockSpec((1, d), lambda i: (0, 0))],
        out_specs=pl.BlockSpec((tm, d), lambda i: (i, 0)),
        out_shape=jax.ShapeDtypeStruct((n, d), F32),
        compiler_params=_params("parallel"),
        name="final_rms_norm",
    )(x, g.reshape(1, d))


def _rope_angles(pos, n_freq, theta):
    inv_freq = theta ** (-jnp.arange(n_freq, dtype=F32) / n_freq)
    return pos.astype(F32)[:, None] * inv_freq[None, :]


def kernel(x, norm_mix_g, w_in, gate_b, w_branch, w_out, rwkv_mu_rkv, rwkv_mu_x, rwkv_w0, rwkv_w1, rwkv_w2,
           rwkv_a0, rwkv_a1, rwkv_a2, rwkv_g1, rwkv_g2, rwkv_k_k, rwkv_k_a, rwkv_r_k, rwkv_ln_w, rwkv_ln_b,
           s5_a_re, s5_a_im, s5_log_dt, s5_b_re, s5_b_im, s5_c_re, s5_c_im, s5_d, s5_glu_w, s5_glu_b,
           gqa_q_norm, gqa_k_norm, norm_ffn_g, dense_w_gate, dense_w_up, dense_w_down,
           moe_router, moe_w_gate, moe_w_up, moe_w_down, final_norm_g):
    batch, seq, d = x.shape
    depth = w_in.shape[0]
    n = batch * seq
    t = jnp.arange(seq, dtype=jnp.int32)
    rope_tabs = _rotary_tables(_rope_angles(t, ROPE_DIMS // 2, ROPE_THETA), N_HEADS)
    ang_axial = jnp.concatenate([_rope_angles(t // GRID_W, HEAD_DIM // 4, AXIAL_THETA),
                                 _rope_angles(t % GRID_W, HEAD_DIM // 4, AXIAL_THETA)], axis=-1)
    axial_tabs = _rotary_tables(ang_axial, N_HEADS)
    x = x.reshape(n, d)
    for l in range(depth):
        w_small = w_in[l, :, :OFF_GATES].astype(BF16)
        w_gates = w_in[l, :, OFF_GATES:].astype(BF16)
        xn, xnb, z = rms_in_proj(x, norm_mix_g[l], w_small)
        zg = matmul_bf16(xnb, w_gates, tm=1024, tn=1024, out_dtype=BF16)
        qa, ka, va, qd, kd, vd = qkv_prep(z, batch, seq, rope_tabs, axial_tabs, gqa_q_norm[l], gqa_k_norm[l])
        ya = dilated_attention(qa, ka, va)
        yd = gqa_attention(qd, kd, vd)
        lw1 = jnp.concatenate([rwkv_w1[l], rwkv_a1[l]], axis=-1).astype(BF16)
        zeros = jnp.zeros_like(rwkv_w2[l])
        lw2 = jnp.concatenate([jnp.concatenate([rwkv_w2[l], zeros], axis=-1),
                               jnp.concatenate([zeros, rwkv_a2[l]], axis=-1)], axis=1).astype(BF16)
        w0a0 = jnp.concatenate([rwkv_w0[l], rwkv_a0[l]], axis=-1)
        fw, bw, bonus, rgate = rwkv_prep(xn, z, seq, rwkv_mu_x[l], rwkv_mu_rkv[l].reshape(2, IN_B), lw1, lw2, w0a0,
                                         rwkv_g1[l].astype(BF16), rwkv_g2[l].astype(BF16),
                                         rwkv_k_k[l], rwkv_k_a[l], rwkv_r_k[l])
        wkv_f, wkv_b = wkv_chunked(fw, bw, batch, seq)
        discs = [_s5_discretize(s5_a_re[l, dr], s5_a_im[l, dr], s5_log_dt[l, dr], s5_b_re[l], s5_b_im[l],
                                s5_c_re[l, dr], s5_c_im[l, dr]) for dr in range(2)]
        s5_out = s5_bidirectional(z, batch, seq, discs)
        x = merge_branches(x, ya, yd, wkv_f, wkv_b, bonus, rgate, s5_out[0], s5_out[1], z, zg, gate_b[l],
                           w_branch[l].astype(BF16), w_out[l].astype(BF16), rwkv_ln_w[l], rwkv_ln_b[l],
                           s5_d[l], s5_glu_w[l].astype(BF16), s5_glu_b[l], batch, seq)
        i = l // 2
        if l % 2 == 0:
            x = dense_ffn(x, norm_ffn_g[l], dense_w_gate[i].astype(BF16), dense_w_up[i].astype(BF16),
                          dense_w_down[i].astype(BF16))
        else:
            xnb_f, sel, wt = moe_route(x, norm_ffn_g[l], moe_router[i])
            x = moe_ffn(x, xnb_f, sel, wt, moe_w_gate[i].astype(BF16), moe_w_up[i].astype(BF16),
                        moe_w_down[i].astype(BF16))
    return rms_norm(x, final_norm_g).reshape(batch, seq, d)
```

```python
import functools
import math

import jax
import jax.numpy as jnp
from jax import lax
from jax.experimental import pallas as pl
from jax.experimental.pallas import tpu as pltpu

F32 = jnp.float32
BF16 = jnp.bfloat16

D_MODEL = 1024
HEAD_DIM = 64
BRANCH_WIDTH = 256
N_BRANCHES = 4
N_HEADS = BRANCH_WIDTH // HEAD_DIM
DILATED_PATTERNS = ((128, 1), (512, 4), (2048, 16))
ROPE_THETA = 500000.0
ROPE_DIMS = HEAD_DIM // 4
RWKV_GN_EPS = 64e-5
S5_GROUP_CH = 16
S5_GROUPS = BRANCH_WIDTH // S5_GROUP_CH
S5_STATE = 64
GQA_KV_HEADS = 2
AXIAL_THETA = 10000.0
GRID_W = 64
N_EXPERTS = 8
TOP_K = 2
NORM_EPS = 1e-6
NEG_INF = -1e30

IN_A = 3 * BRANCH_WIDTH
IN_B = 3 * BRANCH_WIDTH
IN_C = BRANCH_WIDTH
IN_DQ = BRANCH_WIDTH
IN_DKV = GQA_KV_HEADS * HEAD_DIM
IN_GATES = N_BRANCHES * D_MODEL
OFF_B = IN_A
OFF_C = OFF_B + IN_B
OFF_DQ = OFF_C + IN_C
OFF_DKV = OFF_DQ + IN_DQ
OFF_GATES = OFF_DKV + 2 * IN_DKV
IN_TOTAL = OFF_GATES + IN_GATES

VMEM_LIMIT_BYTES = 56 * 1024 * 1024


def _params(*semantics):
    return pltpu.CompilerParams(dimension_semantics=semantics, vmem_limit_bytes=VMEM_LIMIT_BYTES)


def _bdot(a, b):
    return jnp.dot(a.astype(BF16), b.astype(BF16), preferred_element_type=F32)


def _bdot_nt(a, b):
    return lax.dot_general(a.astype(BF16), b.astype(BF16), (((1,), (1,)), ((), ())),
                           preferred_element_type=F32)


def _rms(x, g):
    return x * lax.rsqrt(jnp.mean(x * x, axis=-1, keepdims=True) + NORM_EPS) * g


def _rms_in_proj_kernel(x_ref, g_ref, w_ref, xn_ref, xnb_ref, z_ref):
    @pl.when(pl.program_id(1) == 0)
    def _():
        y = _rms(x_ref[...], g_ref[...])
        xn_ref[...] = y
        xnb_ref[...] = y.astype(BF16)

    z_ref[...] = jnp.dot(xnb_ref[...], w_ref[...], preferred_element_type=F32)


def rms_in_proj(x, g, w_bf16, tm=1024, tn=768):
    n, d = x.shape
    nout = w_bf16.shape[1]
    tm = min(tm, n)
    return pl.pallas_call(
        _rms_in_proj_kernel,
        grid=(n // tm, nout // tn),
        in_specs=[pl.BlockSpec((tm, d), lambda i, j: (i, 0)),
                  pl.BlockSpec((1, d), lambda i, j: (0, 0)),
                  pl.BlockSpec((d, tn), lambda i, j: (0, j))],
        out_specs=[pl.BlockSpec((tm, d), lambda i, j: (i, 0)),
                   pl.BlockSpec((tm, d), lambda i, j: (i, 0)),
                   pl.BlockSpec((tm, tn), lambda i, j: (i, j))],
        out_shape=[jax.ShapeDtypeStruct((n, d), F32), jax.ShapeDtypeStruct((n, d), BF16),
                   jax.ShapeDtypeStruct((n, nout), F32)],
        compiler_params=_params("parallel", "arbitrary"),
        name="rms_in_proj",
    )(x, g.reshape(1, d), w_bf16)


def _rotary_tables(pos_angles, n_heads):
    s, n = pos_angles.shape
    pad = HEAD_DIM - 2 * n
    cos = jnp.concatenate([jnp.cos(pos_angles), jnp.cos(pos_angles), jnp.ones((s, pad), F32)], axis=-1)
    zeros_n = jnp.zeros((s, n), F32)
    zeros_p = jnp.zeros((s, pad), F32)
    sin_lo = jnp.concatenate([-jnp.sin(pos_angles), zeros_n, zeros_p], axis=-1)
    sin_hi = jnp.concatenate([zeros_n, jnp.sin(pos_angles), zeros_p], axis=-1)
    return tuple(jnp.tile(t, (1, n_heads)) for t in (cos, sin_lo, sin_hi))


def _rotate(x, cos, sin_lo, sin_hi, n):
    width = x.shape[-1]
    from_above = pltpu.roll(x, width - n, 1)
    from_below = pltpu.roll(x, n, 1)
    return x * cos + from_above * sin_lo + from_below * sin_hi


def _head_sum(x, n_heads):
    lane = lax.broadcasted_iota(jnp.int32, x.shape, 1)
    out = jnp.zeros_like(x)
    for h in range(n_heads):
        in_head = (lane >= h * HEAD_DIM) & (lane < (h + 1) * HEAD_DIM)
        s = jnp.sum(jnp.where(in_head, x, 0.0), axis=-1, keepdims=True)
        out = jnp.where(in_head, s, out)
    return out


def _head_rms(x, g, n_heads):
    ms = _head_sum(x * x, n_heads) * (1.0 / HEAD_DIM)
    return x * lax.rsqrt(ms + NORM_EPS) * g


Q_SCALE = HEAD_DIM ** -0.5 * math.log2(math.e)


def _qkv_prep_kernel(za_ref, zq_ref, zkv_ref, rc_ref, rl_ref, rh_ref, ac_ref, al_ref, ah_ref,
                     qn_ref, kn_ref, qa_ref, ka_ref, va_ref, qd_ref, kd_ref, vd_ref):
    w = BRANCH_WIDTH
    n_rope = ROPE_DIMS // 2
    n_ax = HEAD_DIM // 2
    za = za_ref[...]
    rc, rl, rh = rc_ref[...], rl_ref[...], rh_ref[...]
    qa = _rotate(za[:, :w], rc, rl, rh, n_rope) * Q_SCALE
    ka = _rotate(za[:, w:2 * w], rc, rl, rh, n_rope)
    va = za[:, 2 * w:]
    ac, al, ah = ac_ref[...], al_ref[...], ah_ref[...]
    qd = _rotate(_head_rms(zq_ref[...], qn_ref[...], N_HEADS), ac, al, ah, n_ax) * Q_SCALE
    zkv = zkv_ref[...]
    kw = GQA_KV_HEADS * HEAD_DIM
    kd = _rotate(_head_rms(zkv[:, :kw], kn_ref[...], GQA_KV_HEADS), ac[:, :kw], al[:, :kw], ah[:, :kw], n_ax)
    vd = zkv[:, kw:]
    for h in range(N_HEADS):
        sl = slice(h * HEAD_DIM, (h + 1) * HEAD_DIM)
        qa_ref[0, h] = qa[:, sl].astype(BF16)
        ka_ref[0, h] = ka[:, sl].astype(BF16)
        va_ref[0, h] = va[:, sl].astype(BF16)
        qd_ref[0, h] = qd[:, sl].astype(BF16)
    for h in range(GQA_KV_HEADS):
        sl = slice(h * HEAD_DIM, (h + 1) * HEAD_DIM)
        kd_ref[0, h] = kd[:, sl].astype(BF16)
        vd_ref[0, h] = vd[:, sl].astype(BF16)


def qkv_prep(z, batch, seq, rope_tabs, axial_tabs, q_norm, k_norm, tm=512):
    tm = min(tm, seq)
    nt = seq // tm
    w = BRANCH_WIDTH
    row = lambda b, i: b * nt + i
    tab_spec = pl.BlockSpec((tm, w), lambda b, i: (i, 0))
    head_out = lambda nh: pl.BlockSpec((1, nh, tm, HEAD_DIM), lambda b, i: (b, 0, i, 0))
    head_shape = lambda nh: jax.ShapeDtypeStruct((batch, nh, seq, HEAD_DIM), BF16)
    return pl.pallas_call(
        _qkv_prep_kernel,
        grid=(batch, nt),
        in_specs=[pl.BlockSpec((tm, IN_A), lambda b, i: (row(b, i), 0)),
                  pl.BlockSpec((tm, w), lambda b, i: (row(b, i), OFF_DQ // w)),
                  pl.BlockSpec((tm, w), lambda b, i: (row(b, i), OFF_DKV // w)),
                  tab_spec, tab_spec, tab_spec, tab_spec, tab_spec, tab_spec,
                  pl.BlockSpec((1, w), lambda b, i: (0, 0)),
                  pl.BlockSpec((1, GQA_KV_HEADS * HEAD_DIM), lambda b, i: (0, 0))],
        out_specs=[head_out(N_HEADS), head_out(N_HEADS), head_out(N_HEADS),
                   head_out(N_HEADS), head_out(GQA_KV_HEADS), head_out(GQA_KV_HEADS)],
        out_shape=[head_shape(N_HEADS), head_shape(N_HEADS), head_shape(N_HEADS),
                   head_shape(N_HEADS), head_shape(GQA_KV_HEADS), head_shape(GQA_KV_HEADS)],
        compiler_params=_params("parallel", "parallel"),
        name="qkv_prep",
    )(z, z, z, *rope_tabs, *axial_tabs,
      jnp.tile(q_norm.reshape(1, HEAD_DIM), (1, N_HEADS)),
      jnp.tile(k_norm.reshape(1, HEAD_DIM), (1, GQA_KV_HEADS)))


A_TQ = 1024
A_SUB = 256
A_RADIUS = 64


def _dilated_windows():
    out = []
    for window, dil in DILATED_PATTERNS:
        halo = -(-(window // 2) // 128) * 128
        out.append((dil, -halo, A_SUB + 2 * halo))
    return out


def _dilated_bias():
    biases = []
    for dil, first, width in _dilated_windows():
        qi = jnp.arange(A_SUB, dtype=jnp.int32)[:, None]
        kj = jnp.arange(width, dtype=jnp.int32)[None, :] + first
        delta = kj - qi
        ok = (jnp.abs(delta) <= A_RADIUS * dil) & ((delta & (dil - 1)) == 0)
        biases.append(jnp.where(ok, 0.0, NEG_INF).astype(F32))
    return biases


def _dilated_attn_kernel(q_ref, kp_ref, kc_ref, kn_ref, vp_ref, vc_ref, vn_ref, b0_ref, b1_ref, b2_ref,
                         o_ref, k3_ref, v3_ref, *, seq, tq):
    i = pl.program_id(2)
    k3_ref[0:tq] = kp_ref[0, 0]
    k3_ref[tq:2 * tq] = kc_ref[0, 0]
    k3_ref[2 * tq:3 * tq] = kn_ref[0, 0]
    v3_ref[0:tq] = vp_ref[0, 0]
    v3_ref[tq:2 * tq] = vc_ref[0, 0]
    v3_ref[2 * tq:3 * tq] = vn_ref[0, 0]
    bias_refs = (b0_ref, b1_ref, b2_ref)
    windows = _dilated_windows()
    for u in range(tq // A_SUB):
        q = q_ref[0, 0, u * A_SUB:(u + 1) * A_SUB, :]
        scores = []
        for (dil, first, width), b_ref in zip(windows, bias_refs):
            start = tq + u * A_SUB + first
            s = _bdot_nt(q, k3_ref[start:start + width, :]) + b_ref[...]
            kpos = (i - 1) * tq + start + lax.broadcasted_iota(jnp.int32, (1, width), 1)
            s = jnp.where((kpos >= 0) & (kpos < seq), s, NEG_INF)
            scores.append((s, start, width))
        m = functools.reduce(jnp.maximum, [jnp.max(s, axis=-1, keepdims=True) for s, _, _ in scores])
        l = jnp.zeros_like(m)
        acc = jnp.zeros((A_SUB, HEAD_DIM), F32)
        for s, start, width in scores:
            p = jnp.exp2(s - m)
            l = l + jnp.sum(p, axis=-1, keepdims=True)
            acc = acc + _bdot(p, v3_ref[start:start + width, :])
        o_ref[0, 0, u * A_SUB:(u + 1) * A_SUB, :] = acc / l


def dilated_attention(qa, ka, va):
    batch, nh, seq, hd = qa.shape
    tq = min(A_TQ, seq)
    assert tq == A_TQ, "key halo of radius * max dilation needs full query tiles"
    nt = seq // tq
    cur = pl.BlockSpec((1, 1, tq, hd), lambda b, h, i: (b, h, i, 0))
    prev = pl.BlockSpec((1, 1, tq, hd), lambda b, h, i: (b, h, jnp.maximum(i - 1, 0), 0))
    nxt = pl.BlockSpec((1, 1, tq, hd), lambda b, h, i: (b, h, jnp.minimum(i + 1, nt - 1), 0))
    biases = _dilated_bias()
    bias_specs = [pl.BlockSpec(b.shape, lambda b_, h, i: (0, 0)) for b in biases]
    return pl.pallas_call(
        functools.partial(_dilated_attn_kernel, seq=seq, tq=tq),
        grid=(batch, nh, nt),
        in_specs=[cur, prev, cur, nxt, prev, cur, nxt] + bias_specs,
        out_specs=cur,
        out_shape=jax.ShapeDtypeStruct((batch, nh, seq, hd), F32),
        scratch_shapes=[pltpu.VMEM((3 * tq, hd), BF16), pltpu.VMEM((3 * tq, hd), BF16)],
        compiler_params=_params("parallel", "parallel", "parallel"),
        name="dilated_attention",
    )(qa, ka, ka, ka, va, va, va, *biases)


GQA_SUB = 128


def _gqa_kernel(q_ref, k_ref, v_ref, o_ref, *scratch, rep, tq):
    j = pl.program_id(3)
    n_sub = tq // GQA_SUB
    blocks = [(r, u) for r in range(rep) for u in range(n_sub)]
    m_refs, l_refs, acc_refs = (scratch[i * len(blocks):(i + 1) * len(blocks)] for i in range(3))

    @pl.when(j == 0)
    def _():
        for m_ref, l_ref, acc_ref in zip(m_refs, l_refs, acc_refs):
            m_ref[...] = jnp.full(m_ref.shape, NEG_INF, F32)
            l_ref[...] = jnp.zeros(l_ref.shape, F32)
            acc_ref[...] = jnp.zeros(acc_ref.shape, F32)

    k = k_ref[0, 0]
    v = v_ref[0, 0]
    scores = [_bdot_nt(k, q_ref[0, r, u * GQA_SUB:(u + 1) * GQA_SUB, :]) for r, u in blocks]
    for s, m_ref, l_ref, acc_ref in zip(scores, m_refs, l_refs, acc_refs):
        m_prev = m_ref[...]
        m_new = jnp.maximum(m_prev, jnp.max(s, axis=0, keepdims=True))
        alpha = jnp.exp2(m_prev - m_new)
        p = jnp.exp2(s - m_new)
        l_ref[...] = alpha * l_ref[...] + jnp.sum(p, axis=0, keepdims=True)
        pv = lax.dot_general(v, p.astype(BF16), (((0,), (0,)), ((), ())), preferred_element_type=F32)
        acc_ref[...] = alpha * acc_ref[...] + pv
        m_ref[...] = m_new

    @pl.when(j == pl.num_programs(3) - 1)
    def _():
        for (r, u), l_ref, acc_ref in zip(blocks, l_refs, acc_refs):
            o_ref[0, r, :, u * GQA_SUB:(u + 1) * GQA_SUB] = acc_ref[...] / l_ref[...]


def gqa_attention(qd, kd, vd, tq=1024, tk=2048):
    batch, nh, seq, hd = qd.shape
    ng = kd.shape[1]
    rep = nh // ng
    tq = min(tq, seq)
    tk = min(tk, seq)
    n_blocks = rep * (tq // GQA_SUB)
    return pl.pallas_call(
        functools.partial(_gqa_kernel, rep=rep, tq=tq),
        grid=(batch, ng, seq // tq, seq // tk),
        in_specs=[pl.BlockSpec((1, rep, tq, hd), lambda b, g, i, j: (b, g, i, 0)),
                  pl.BlockSpec((1, 1, tk, hd), lambda b, g, i, j: (b, g, j, 0)),
                  pl.BlockSpec((1, 1, tk, hd), lambda b, g, i, j: (b, g, j, 0))],
        out_specs=pl.BlockSpec((1, rep, hd, tq), lambda b, g, i, j: (b, g, 0, i)),
        out_shape=jax.ShapeDtypeStruct((batch, nh, hd, seq), F32),
        scratch_shapes=([pltpu.VMEM((1, GQA_SUB), F32)] * (2 * n_blocks)
                        + [pltpu.VMEM((hd, GQA_SUB), F32)] * n_blocks),
        compiler_params=_params("parallel", "parallel", "parallel", "arbitrary"),
        name="gqa_attention",
    )(qd, kd, vd)


def _matmul_kernel(a_ref, w_ref, o_ref):
    o_ref[...] = jnp.dot(a_ref[...], w_ref[...], preferred_element_type=F32).astype(o_ref.dtype)


def matmul_bf16(a, w, tm, tn, out_dtype=F32):
    n, k = a.shape
    m = w.shape[1]
    tm = min(tm, n)
    return pl.pallas_call(
        _matmul_kernel,
        grid=(n // tm, m // tn),
        in_specs=[pl.BlockSpec((tm, k), lambda i, j: (i, 0)),
                  pl.BlockSpec((k, tn), lambda i, j: (0, j))],
        out_specs=pl.BlockSpec((tm, tn), lambda i, j: (i, j)),
        out_shape=jax.ShapeDtypeStruct((n, m), out_dtype),
        compiler_params=_params("parallel", "arbitrary"),
        name="matmul_bf16",
    )(a, w)


def _sigmoid(x):
    return 1.0 / (1.0 + jnp.exp(-x))


def _softplus(x):
    return jnp.maximum(x, 0.0) + jnp.log(1.0 + jnp.exp(-jnp.abs(x)))


def _shift_rows(x, edge_row, down):
    rows = x.shape[0]
    ridx = lax.broadcasted_iota(jnp.int32, x.shape, 0)
    if down:
        return jnp.where(ridx == 0, edge_row, pltpu.roll(x, 1, 0))
    return jnp.where(ridx == rows - 1, edge_row, pltpu.roll(x, rows - 1, 0))


WKV_FIELDS = 6


def _rwkv_prep_kernel(xn_ref, xp_ref, xq_ref, zb_ref, zp_ref, zq_ref, mux_ref, murkv_ref, lw1_ref, lw2_ref,
                      w0a0_ref, g1_ref, g2_ref, kk_ref, ka_ref, rk_ref,
                      fw_ref, bw_ref, bonus_ref, gate_ref, *, tiles_per_seq):
    w = BRANCH_WIDTH
    i = pl.program_id(0)
    first = (i % tiles_per_seq) == 0
    last = (i % tiles_per_seq) == tiles_per_seq - 1
    xn = xn_ref[...]
    x_shift = (_shift_rows(xn, jnp.where(first, 0.0, xp_ref[7:8, :]), True),
               _shift_rows(xn, jnp.where(last, 0.0, xq_ref[0:1, :]), False))
    zb = zb_ref[...]
    z_prev = _shift_rows(zb, jnp.where(first, 0.0, zp_ref[7:8, :]), True)
    z_next = _shift_rows(zb, jnp.where(last, 0.0, zq_ref[0:1, :]), False)
    mu = murkv_ref[...]
    rkv = zb + mu[0:1] * (z_prev - zb) + mu[1:2] * (z_next - zb)
    r, k, v = rkv[:, :w], rkv[:, w:2 * w], rkv[:, 2 * w:]
    kap = k * kk_ref[...]
    kap = kap * lax.rsqrt(_head_sum(kap * kap, N_HEADS) + 1e-12)
    gate_ref[...] = _bdot(_sigmoid(_bdot(xn, g1_ref[...])), g2_ref[...])
    bonus = jnp.zeros_like(v)
    lora_lane = lax.broadcasted_iota(jnp.int32, (xn.shape[0], lw1_ref.shape[-1]), 1)
    for d, out_ref in enumerate((fw_ref, bw_ref)):
        xd = xn + mux_ref[d:d + 1, :] * (x_shift[d] - xn)
        h = _bdot(xd, lw1_ref[d])
        h = jnp.where(lora_lane < lw1_ref.shape[-1] // 2, jnp.tanh(h), h)
        h = _bdot(h, lw2_ref[d]) + w0a0_ref[d:d + 1, :]
        w_log = -_softplus(-h[:, :w]) - 0.5
        log_decay = -jnp.exp(w_log)
        iclr = _sigmoid(h[:, w:])
        k_d = k * (1.0 + (iclr - 1.0) * ka_ref[...])
        bonus = bonus + _head_sum(r * k_d * rk_ref[...], N_HEADS) * v
        for j, field in enumerate((r, log_decay, k_d, v, kap, iclr * kap)):
            out_ref[:, j * w:(j + 1) * w] = field
    bonus_ref[...] = bonus


def rwkv_prep(xn, z, seq, mu_x, mu_rkv, lw1, lw2, w0a0, g1, g2, k_k, k_a, r_k, tm=512):
    n, d = xn.shape
    w = BRANCH_WIDTH
    tm = min(tm, seq)
    halo = 8
    prev_halo = lambda i: (jnp.maximum(i * (tm // halo) - 1, 0), 0)
    next_halo = lambda i: (jnp.minimum((i + 1) * (tm // halo), n // halo - 1), 0)
    full = lambda a: pl.BlockSpec(a.shape, lambda i: (0,) * a.ndim)
    params = (mu_x, mu_rkv, lw1, lw2, w0a0, g1, g2, k_k.reshape(1, w), k_a.reshape(1, w), r_k.reshape(1, w))
    rows = lambda width: pl.BlockSpec((tm, width), lambda i: (i, 0))
    return pl.pallas_call(
        functools.partial(_rwkv_prep_kernel, tiles_per_seq=seq // tm),
        grid=(n // tm,),
        in_specs=[rows(d), pl.BlockSpec((halo, d), prev_halo), pl.BlockSpec((halo, d), next_halo),
                  pl.BlockSpec((tm, IN_B), lambda i: (i, OFF_B // IN_B)),
                  pl.BlockSpec((halo, IN_B), lambda i: (prev_halo(i)[0], OFF_B // IN_B)),
                  pl.BlockSpec((halo, IN_B), lambda i: (next_halo(i)[0], OFF_B // IN_B))]
                 + [full(p) for p in params],
        out_specs=[rows(WKV_FIELDS * w), rows(WKV_FIELDS * w), rows(w), rows(w)],
        out_shape=[jax.ShapeDtypeStruct((n, WKV_FIELDS * w), F32), jax.ShapeDtypeStruct((n, WKV_FIELDS * w), F32),
                   jax.ShapeDtypeStruct((n, w), F32), jax.ShapeDtypeStruct((n, w), F32)],
        compiler_params=_params("parallel"),
        name="rwkv_prep",
    )(xn, xn, xn, z, z, z, *params)


WKV_CHUNK = 64


def _wkv_constants(reverse):
    c, nh = WKV_CHUNK, N_HEADS
    t = jnp.arange(c)
    before = (t[None, :] > t[:, None]) if reverse else (t[None, :] < t[:, None])
    incl = before | (t[None, :] == t[:, None])
    per_head = lambda m: jnp.kron(jnp.eye(nh, dtype=F32), m.astype(F32))
    return incl.astype(F32), per_head(before), per_head(incl)


def _cumulative_log_decay(tri, logws):
    w = logws[0].shape[1]
    cat = jnp.concatenate(logws, axis=1)
    hi = cat.astype(BF16)
    lo = (cat - hi.astype(F32)).astype(BF16)
    tri = tri.astype(BF16)
    g = jnp.dot(tri, hi, preferred_element_type=F32) + jnp.dot(tri, lo, preferred_element_type=F32)
    return [g[:, i * w:(i + 1) * w] for i in range(len(logws))]


def _wkv_chunk(x, g, s_bd, m_strict, m_incl, head_rows, s_mask, last_row):
    c, w, nh = WKV_CHUNK, BRANCH_WIDTH, N_HEADS
    r, logw, k, v, kap, b = (x[:, j * w:(j + 1) * w] for j in range(WKV_FIELDS))
    gam = jnp.exp(g)
    g_inv = jnp.exp(-g)
    kap_t, r_t = kap * jnp.exp(g - logw), r * gam
    k_t, b_t = k * g_inv, b * g_inv

    def per_head(a):
        return jnp.concatenate([jnp.where(head_rows[h:h + 1, :] > 0.0, a, 0.0) for h in range(nh)], axis=0)

    def wide(m):
        return m[0:c] + m[c:2 * c] + m[2 * c:3 * c] + m[3 * c:4 * c]

    gram = _bdot_nt(jnp.concatenate([per_head(kap_t), per_head(r_t)], axis=0),
                    jnp.concatenate([per_head(k_t), per_head(b_t), s_bd], axis=0))
    hc = nh * c
    kk, n = gram[:hc, :hc] * m_strict, gram[:hc, hc:2 * hc] * m_strict
    rk, rb = gram[hc:, :hc] * m_incl, gram[hc:, hc:2 * hc] * m_incl
    from_state = (wide(gram[:hc, 2 * hc:]), wide(gram[hc:, 2 * hc:]))
    eye = jnp.where(lax.broadcasted_iota(jnp.int32, (hc, hc), 0) == lax.broadcasted_iota(jnp.int32, (hc, hc), 1),
                    1.0, 0.0)
    inv = eye - n
    power = _bdot(n, n)
    levels = c.bit_length() - 2
    for level in range(levels):
        inv = inv + _bdot(inv, power)
        if level + 1 < levels:
            power = _bdot(power, power)
    v_heads = per_head(v)
    u = _bdot(wide(inv), per_head(from_state[0] + _bdot(wide(kk), v_heads)))
    y = from_state[1] + _bdot(jnp.concatenate([wide(rk), -wide(rb)], axis=1),
                              jnp.concatenate([v_heads, per_head(u)], axis=0))
    update = lax.dot_general(jnp.concatenate([v, u], axis=0).astype(BF16),
                             jnp.concatenate([k_t, -b_t], axis=0).astype(BF16),
                             (((0,), (0,)), ((), ())), preferred_element_type=F32)
    s_new = (s_bd + update * s_mask) * gam[last_row:last_row + 1, :]
    return y, s_new


def _wkv_chunked_kernel(fw_ref, bw_ref, trif_ref, msf_ref, mif_ref, trib_ref, msb_ref, mib_ref, hr_ref, sm_ref,
                        yf_ref, yb_ref, state_ref, *, batch, rows):
    @pl.when(pl.program_id(0) == 0)
    def _():
        state_ref[...] = jnp.zeros(state_ref.shape, F32)

    c = WKV_CHUNK
    n_chunks = rows // c
    head_rows = hr_ref[...]
    s_mask = sm_ref[...]
    tris = (trif_ref[...], trib_ref[...])
    masks = ((msf_ref[...], mif_ref[...]), (msb_ref[...], mib_ref[...]))
    refs, y_refs = (fw_ref, bw_ref), (yf_ref, yb_ref)
    w = BRANCH_WIDTH

    def body(i, carry):
        bases = (pl.multiple_of(i * c, c), pl.multiple_of((n_chunks - 1 - i) * c, c))
        for d in range(2):
            xs = [refs[d][bi, pl.ds(bases[d], c), :] for bi in range(batch)]
            gs = _cumulative_log_decay(tris[d], [x[:, w:2 * w] for x in xs])
            for bi in range(batch):
                y, s_new = _wkv_chunk(xs[bi], gs[bi], state_ref[2 * bi + d], *masks[d], head_rows, s_mask,
                                      last_row=(0 if d == 1 else c - 1))
                y_refs[d][bi, pl.ds(bases[d], c), :] = y
                state_ref[2 * bi + d] = s_new
        return carry

    lax.fori_loop(0, n_chunks, body, 0)


def wkv_chunked(fw, bw, batch, seq, rows=256):
    w = BRANCH_WIDTH
    rows = min(rows, seq)
    nb = seq // rows
    head_of_lane = jnp.arange(w) // HEAD_DIM
    head_rows = (jnp.arange(8)[:, None] == head_of_lane[None, :]).astype(F32)
    s_mask = (head_of_lane[:, None] == head_of_lane[None, :]).astype(F32)
    consts = [*_wkv_constants(False), *_wkv_constants(True), head_rows, s_mask]
    in_f = pl.BlockSpec((batch, rows, WKV_FIELDS * w), lambda c: (0, c, 0))
    in_b = pl.BlockSpec((batch, rows, WKV_FIELDS * w), lambda c: (0, nb - 1 - c, 0))
    out_shape = jax.ShapeDtypeStruct((batch, seq, w), F32)
    yf, yb = pl.pallas_call(
        functools.partial(_wkv_chunked_kernel, batch=batch, rows=rows),
        grid=(nb,),
        in_specs=[in_f, in_b] + [pl.BlockSpec(a.shape, lambda c: (0, 0)) for a in consts],
        out_specs=[pl.BlockSpec((batch, rows, w), lambda c: (0, c, 0)),
                   pl.BlockSpec((batch, rows, w), lambda c: (0, nb - 1 - c, 0))],
        out_shape=[out_shape, out_shape],
        scratch_shapes=[pltpu.VMEM((2 * batch, w, w), F32)],
        compiler_params=_params("arbitrary"),
        name="wkv_chunked",
    )(fw.reshape(batch, seq, -1), bw.reshape(batch, seq, -1), *consts)
    return yf.reshape(batch * seq, w), yb.reshape(batch * seq, w)


S5_SEGMENTS = 8
S5_WIDTH = S5_GROUPS * S5_STATE


def _s5_discretize(a_re, a_im, log_dt, b_re, b_im, c_re, c_im):
    g, p, c = S5_GROUPS, S5_STATE, S5_GROUP_CH
    dt = jnp.exp(log_dt)[:, None]
    mag = jnp.exp(a_re * dt)
    bar_re, bar_im = mag * jnp.cos(a_im * dt), mag * jnp.sin(a_im * dt)
    den = a_re * a_re + a_im * a_im
    f_re = ((bar_re - 1.0) * a_re + bar_im * a_im) / den
    f_im = (bar_im * a_re - (bar_re - 1.0) * a_im) / den
    bb_re = f_re[..., None] * b_re - f_im[..., None] * b_im
    bb_im = f_re[..., None] * b_im + f_im[..., None] * b_re
    eye_g = jnp.eye(g, dtype=F32)
    w_in = jnp.concatenate(
        [jnp.einsum('gpc,gh->gchp', bb, eye_g).reshape(g * c, g * p) for bb in (bb_re, bb_im)], axis=1)
    w_out = jnp.concatenate(
        [jnp.einsum('gcp,gh->gphc', cc, eye_g).reshape(g * p, g * c) for cc in (c_re, -c_im)], axis=0)
    return bar_re.reshape(1, g * p), bar_im.reshape(1, g * p), w_in.astype(BF16), w_out.astype(BF16), dt, a_re, a_im


def _s5_powers(a_re, a_im, dt, count, reverse):
    j = jnp.arange(1, count + 1, dtype=F32)
    if reverse:
        j = j[::-1]
    e = j[:, None, None] * (a_re * dt)[None]
    th = j[:, None, None] * (a_im * dt)[None]
    mag = jnp.exp(e)
    return (mag * jnp.cos(th)).reshape(count, -1), (mag * jnp.sin(th)).reshape(count, -1)


def _s5_scan_kernel(u_ref, win_ref, lre_ref, lim_ref, pre_ref, pim_ref, wout_ref, y_ref,
                    xre_ref, xim_ref, cre_ref, cim_ref, *, reverse, chunk):
    seg = chunk // S5_SEGMENTS
    nw = S5_WIDTH

    @pl.when(pl.program_id(1) == 0)
    def _():
        cre_ref[...] = jnp.zeros(cre_ref.shape, F32)
        cim_ref[...] = jnp.zeros(cim_ref.shape, F32)

    bu = jnp.dot(u_ref[...].astype(BF16), win_ref[...], preferred_element_type=F32)
    xre_ref[...] = bu[:, :nw]
    xim_ref[...] = bu[:, nw:]
    lre = lre_ref[...]
    lim = lim_ref[...]
    group = lambda s: pl.ds(pl.multiple_of(s * S5_SEGMENTS, S5_SEGMENTS), S5_SEGMENTS)

    def local_step(s, carry):
        xr, xi = carry
        rows = group((seg - 1 - s) if reverse else s)
        nr = lre * xr - lim * xi + xre_ref[rows, :]
        ni = lre * xi + lim * xr + xim_ref[rows, :]
        xre_ref[rows, :] = nr
        xim_ref[rows, :] = ni
        return nr, ni

    zero = jnp.zeros((S5_SEGMENTS, nw), F32)
    fin_re, fin_im = lax.fori_loop(0, seg, local_step, (zero, zero))

    full_seg = 0 if reverse else seg - 1
    pl_re, pl_im = pre_ref[full_seg:full_seg + 1, :], pim_ref[full_seg:full_seg + 1, :]
    ir, ii = cre_ref[...], cim_ref[...]
    init_re, init_im = [None] * S5_SEGMENTS, [None] * S5_SEGMENTS
    for j in (range(S5_SEGMENTS - 1, -1, -1) if reverse else range(S5_SEGMENTS)):
        init_re[j], init_im[j] = ir, ii
        fr, fi = fin_re[j:j + 1, :], fin_im[j:j + 1, :]
        ir, ii = fr + (pl_re * ir - pl_im * ii), fi + (pl_re * ii + pl_im * ir)
    cre_ref[...] = ir
    cim_ref[...] = ii
    init_re = jnp.concatenate(init_re, axis=0)
    init_im = jnp.concatenate(init_im, axis=0)

    def correct(s, carry):
        rows = group(s)
        pr, pi = pre_ref[pl.ds(s, 1), :], pim_ref[pl.ds(s, 1), :]
        xre_ref[rows, :] = xre_ref[rows, :] + (pr * init_re - pi * init_im)
        xim_ref[rows, :] = xim_ref[rows, :] + (pr * init_im + pi * init_re)
        return carry

    lax.fori_loop(0, seg, correct, 0)
    y_ref[...] = (jnp.dot(xre_ref[...].astype(BF16), wout_ref[:nw, :], preferred_element_type=F32)
                  + jnp.dot(xim_ref[...].astype(BF16), wout_ref[nw:, :], preferred_element_type=F32))


def _segment_interleave(a, batch, seq, chunk, inverse=False):
    w = a.shape[-1]
    seg = chunk // S5_SEGMENTS
    shape = (batch, seq // chunk, seg, S5_SEGMENTS, w) if inverse else (batch, seq // chunk, S5_SEGMENTS, seg, w)
    return jnp.swapaxes(a.reshape(shape), 2, 3).reshape(batch * seq, w)


def s5_scan(u_interleaved, batch, seq, disc, reverse, chunk):
    lam_re, lam_im, w_in, w_out, dt, a_re, a_im = disc
    w = BRANCH_WIDTH
    nc = seq // chunk
    seg = chunk // S5_SEGMENTS
    pw_re, pw_im = _s5_powers(a_re, a_im, dt, seg, reverse)
    order = (lambda c: nc - 1 - c) if reverse else (lambda c: c)
    full = lambda a: pl.BlockSpec(a.shape, lambda b, c: (0, 0))
    return pl.pallas_call(
        functools.partial(_s5_scan_kernel, reverse=reverse, chunk=chunk),
        grid=(batch, nc),
        in_specs=[pl.BlockSpec((chunk, w), lambda b, c: (b * nc + order(c), 0)),
                  full(w_in), full(lam_re), full(lam_im), full(pw_re), full(pw_im), full(w_out)],
        out_specs=pl.BlockSpec((chunk, w), lambda b, c: (b * nc + order(c), 0)),
        out_shape=jax.ShapeDtypeStruct((batch * seq, w), F32),
        scratch_shapes=[pltpu.VMEM((chunk, S5_WIDTH), F32), pltpu.VMEM((chunk, S5_WIDTH), F32),
                        pltpu.VMEM((1, S5_WIDTH), F32), pltpu.VMEM((1, S5_WIDTH), F32)],
        compiler_params=_params("parallel", "arbitrary"),
        name="s5_scan_bwd" if reverse else "s5_scan_fwd",
    )(u_interleaved, w_in, lam_re, lam_im, pw_re, pw_im, w_out)


def s5_bidirectional(z, batch, seq, discs, chunk=1024):
    chunk = min(chunk, seq)
    u = _segment_interleave(z[:, OFF_C:OFF_C + BRANCH_WIDTH], batch, seq, chunk)
    return [_segment_interleave(s5_scan(u, batch, seq, disc, reverse=(d == 1), chunk=chunk),
                                batch, seq, chunk, inverse=True) for d, disc in enumerate(discs)]


def _gelu_tanh(y):
    return 0.5 * y * (1.0 + jnp.tanh(math.sqrt(2.0 / math.pi) * (y + 0.044715 * (y * y * y))))


def _merge_kernel(x_ref, ya_ref, yd_ref, wf_ref, wb_ref, bonus_ref, rg_ref, sf_ref, sb_ref, u_ref, zg_ref,
                  gb_ref, wbr_ref, wout_ref, lnw_ref, lnb_ref, s5d_ref, gluw_ref, glub_ref, o_ref):
    w = BRANCH_WIDTH
    ya = jnp.concatenate([ya_ref[0, h] for h in range(N_HEADS)], axis=1)
    ys = wf_ref[...] + wb_ref[...]
    cen = ys - _head_sum(ys, N_HEADS) * (1.0 / HEAD_DIM)
    var = _head_sum(cen * cen, N_HEADS) * (1.0 / HEAD_DIM)
    yb = (cen * lax.rsqrt(var + RWKV_GN_EPS) * lnw_ref[...] + lnb_ref[...] + bonus_ref[...]) * rg_ref[...]
    yc = sf_ref[...] + sb_ref[...] + s5d_ref[...] * u_ref[...]
    h = _bdot(_gelu_tanh(yc), gluw_ref[...]) + glub_ref[...]
    yc = h[:, :w] * _sigmoid(h[:, w:])
    proj_d = sum(lax.dot_general(yd_ref[0, h].astype(BF16), wbr_ref[3, h * HEAD_DIM:(h + 1) * HEAD_DIM, :],
                                 (((0,), (0,)), ((), ())), preferred_element_type=F32) for h in range(N_HEADS))
    merged = jnp.zeros(o_ref.shape, F32)
    for i, proj in enumerate((_bdot(ya, wbr_ref[0]), _bdot(yb, wbr_ref[1]), _bdot(yc, wbr_ref[2]), proj_d)):
        gate = _sigmoid(zg_ref[:, i * D_MODEL:(i + 1) * D_MODEL] + gb_ref[i:i + 1, :])
        merged = merged + gate * proj
    o_ref[...] = x_ref[...] + _bdot(merged, wout_ref[...])


def merge_branches(x, ya, yd, wkv_f, wkv_b, bonus, rgate, s5_f, s5_b, z, zg, gate_b, w_branch, w_out,
                   ln_w, ln_b, s5_d, glu_w, glu_b, batch, seq, tm=512):
    w = BRANCH_WIDTH
    d = D_MODEL
    tm = min(tm, seq)
    nt = seq // tm
    rows = lambda width, col=0: pl.BlockSpec((tm, width), lambda b, i: (b * nt + i, col))
    heads = pl.BlockSpec((1, N_HEADS, tm, HEAD_DIM), lambda b, i: (b, 0, i, 0))
    heads_t = pl.BlockSpec((1, N_HEADS, HEAD_DIM, tm), lambda b, i: (b, 0, 0, i))
    full = lambda a: pl.BlockSpec(a.shape, lambda b, i: (0,) * a.ndim)
    params = (gate_b, w_branch, w_out, ln_w.reshape(1, w), ln_b.reshape(1, w), s5_d.reshape(1, w),
              glu_w, glu_b.reshape(1, 2 * w))
    return pl.pallas_call(
        _merge_kernel,
        grid=(batch, nt),
        in_specs=[rows(d), heads, heads_t, rows(w), rows(w), rows(w), rows(w), rows(w), rows(w),
                  rows(w, OFF_C // w), rows(N_BRANCHES * d)] + [full(p) for p in params],
        out_specs=rows(d),
        out_shape=jax.ShapeDtypeStruct(x.shape, F32),
        compiler_params=_params("parallel", "parallel"),
        name="merge_branches",
    )(x, ya, yd, wkv_f, wkv_b, bonus, rgate, s5_f, s5_b, z, zg, *params)


def _silu(x):
    return x * _sigmoid(x)


def _dense_ffn_kernel(x_ref, g_ref, wg_ref, wu_ref, wd_ref, o_ref, xn_ref):
    @pl.when(pl.program_id(1) == 0)
    def _():
        x = x_ref[...]
        xn_ref[...] = _rms(x, g_ref[...]).astype(BF16)
        o_ref[...] = x

    xn = xn_ref[...]
    h = (_silu(jnp.dot(xn, wg_ref[...], preferred_element_type=F32))
         * jnp.dot(xn, wu_ref[...], preferred_element_type=F32))
    o_ref[...] += _bdot(h, wd_ref[...])


def dense_ffn(x, g, w_gate, w_up, w_down, tm=1024, tf=1408):
    n, d = x.shape
    ff = w_gate.shape[1]
    tm = min(tm, n)
    return pl.pallas_call(
        _dense_ffn_kernel,
        grid=(n // tm, ff // tf),
        in_specs=[pl.BlockSpec((tm, d), lambda i, f: (i, 0)),
                  pl.BlockSpec((1, d), lambda i, f: (0, 0)),
                  pl.BlockSpec((d, tf), lambda i, f: (0, f)),
                  pl.BlockSpec((d, tf), lambda i, f: (0, f)),
                  pl.BlockSpec((tf, d), lambda i, f: (f, 0))],
        out_specs=pl.BlockSpec((tm, d), lambda i, f: (i, 0)),
        out_shape=jax.ShapeDtypeStruct((n, d), F32),
        scratch_shapes=[pltpu.VMEM((tm, d), BF16)],
        compiler_params=_params("parallel", "arbitrary"),
        name="dense_ffn",
    )(x, g.reshape(1, d), w_gate, w_up, w_down)


def _router_kernel(x_ref, g_ref, rt_ref, xnb_ref, sel_ref, wt_ref):
    xn = _rms(x_ref[...], g_ref[...])
    xnb_ref[...] = xn.astype(BF16)
    logits = lax.dot_general(rt_ref[...], xn, (((1,), (1,)), ((), ())),
                             precision=lax.Precision.HIGHEST, preferred_element_type=F32)
    e = lax.broadcasted_iota(jnp.int32, logits.shape, 0)
    m1 = jnp.max(logits, axis=0, keepdims=True)
    i1 = jnp.min(jnp.where(logits == m1, e, N_EXPERTS), axis=0, keepdims=True)
    rest = jnp.where(e == i1, NEG_INF, logits)
    m2 = jnp.max(rest, axis=0, keepdims=True)
    i2 = jnp.min(jnp.where(rest == m2, e, N_EXPERTS), axis=0, keepdims=True)
    ratio = jnp.exp(m2 - m1)
    w1 = 1.0 / (1.0 + ratio)
    w2 = ratio / (1.0 + ratio)
    sel_ref[...] = jnp.where((e == i1) | (e == i2), 1.0, 0.0)
    wt_ref[...] = jnp.where(e == i1, w1, jnp.where(e == i2, w2, 0.0))


def moe_route(x, g, router, tm=1024):
    n, d = x.shape
    tm = min(tm, n)
    ne = router.shape[1]
    return pl.pallas_call(
        _router_kernel,
        grid=(n // tm,),
        in_specs=[pl.BlockSpec((tm, d), lambda i: (i, 0)),
                  pl.BlockSpec((1, d), lambda i: (0, 0)),
                  pl.BlockSpec((ne, d), lambda i: (0, 0))],
        out_specs=[pl.BlockSpec((tm, d), lambda i: (i, 0)),
                   pl.BlockSpec((ne, tm), lambda i: (0, i)),
                   pl.BlockSpec((ne, tm), lambda i: (0, i))],
        out_shape=[jax.ShapeDtypeStruct((n, d), BF16), jax.ShapeDtypeStruct((ne, n), F32),
                   jax.ShapeDtypeStruct((ne, n), F32)],
        compiler_params=_params("parallel"),
        name="moe_router",
    )(x, g.reshape(1, d), router.T)


MOE_ROWS = 32
MOE_STATIC_BLOCKS = (8, 9, 10)


def _moe_kernel(x_ref, xnb_ref, sel_ref, wt_ref, wg_ref, wu_ref, wd_ref, o_ref,
                rank_ref, xg_ref, acc_ref, nblk_ref):
    e = pl.program_id(1)
    f = pl.program_id(2)
    nf = pl.num_programs(2)
    tm = x_ref.shape[0]

    @pl.when((e == 0) & (f == 0))
    def _():
        o_ref[...] = x_ref[...]
        before = (lax.broadcasted_iota(jnp.int32, (tm, tm), 0) < lax.broadcasted_iota(jnp.int32, (tm, tm), 1))
        rank_ref[...] = jnp.dot(sel_ref[...].astype(BF16), jnp.where(before, 1.0, 0.0).astype(BF16),
                                preferred_element_type=F32)

    sel_e = sel_ref[pl.ds(e, 1), :]
    rank_e = rank_ref[pl.ds(e, 1), :]
    wt_e = wt_ref[pl.ds(e, 1), :]

    @pl.when(f == 0)
    def _():
        count = jnp.sum(sel_e).astype(jnp.int32)
        nblk_ref[0] = (count + MOE_ROWS - 1) // MOE_ROWS

    nblk = nblk_ref[0]

    def process(rows):
        n_rows = rows.stop - rows.start if isinstance(rows, slice) else rows.size
        first = rows.start

        def one_hot():
            slot = (first + lax.broadcasted_iota(jnp.int32, (n_rows, tm), 0)).astype(F32)
            return jnp.where((rank_e == slot) & (sel_e > 0.0), 1.0, 0.0)

        @pl.when(f == 0)
        def _():
            xg_ref[rows, :] = jnp.dot(one_hot().astype(BF16), xnb_ref[...],
                                      preferred_element_type=F32).astype(BF16)
            acc_ref[rows, :] = jnp.zeros((n_rows, acc_ref.shape[1]), F32)

        xg = xg_ref[rows, :]
        h = (_silu(jnp.dot(xg, wg_ref[0], preferred_element_type=F32))
             * jnp.dot(xg, wu_ref[0], preferred_element_type=F32))
        acc_ref[rows, :] += _bdot(h, wd_ref[0])

        @pl.when(f == nf - 1)
        def _():
            hot = one_hot()
            row_w = jnp.sum(hot * wt_e, axis=1, keepdims=True)
            yw = (acc_ref[rows, :] * row_w).astype(BF16)
            o_ref[...] += lax.dot_general(hot.astype(BF16), yw, (((0,), (0,)), ((), ())),
                                          preferred_element_type=F32)

    for n_static in MOE_STATIC_BLOCKS:
        lo = 0 if n_static == MOE_STATIC_BLOCKS[0] else n_static
        hi = n_static if n_static != MOE_STATIC_BLOCKS[-1] else tm // MOE_ROWS
        pl.when((nblk >= lo) & (nblk <= hi))(functools.partial(process, slice(0, n_static * MOE_ROWS)))

    def tail(b, carry):
        process(pl.ds(pl.multiple_of(b * MOE_ROWS, MOE_ROWS), MOE_ROWS))
        return carry

    lax.fori_loop(MOE_STATIC_BLOCKS[-1], nblk, tail, 0)


def moe_ffn(x, xnb, sel, wt, w_gate, w_up, w_down, tm=1024, tf=896):
    n, d = x.shape
    ne, _, ff = w_gate.shape
    tm = min(tm, n)
    return pl.pallas_call(
        _moe_kernel,
        grid=(n // tm, ne, ff // tf),
        in_specs=[pl.BlockSpec((tm, d), lambda i, e, f: (i, 0)),
                  pl.BlockSpec((tm, d), lambda i, e, f: (i, 0)),
                  pl.BlockSpec((ne, tm), lambda i, e, f: (0, i)),
                  pl.BlockSpec((ne, tm), lambda i, e, f: (0, i)),
                  pl.BlockSpec((1, d, tf), lambda i, e, f: (e, 0, f)),
                  pl.BlockSpec((1, d, tf), lambda i, e, f: (e, 0, f)),
                  pl.BlockSpec((1, tf, d), lambda i, e, f: (e, f, 0))],
        out_specs=pl.BlockSpec((tm, d), lambda i, e, f: (i, 0)),
        out_shape=jax.ShapeDtypeStruct((n, d), F32),
        scratch_shapes=[pltpu.VMEM((ne, tm), F32), pltpu.VMEM((tm, d), BF16), pltpu.VMEM((tm, d), F32),
                        pltpu.SMEM((1,), jnp.int32)],
        compiler_params=_params("parallel", "arbitrary", "arbitrary"),
        name="moe_ffn",
    )(x, xnb, sel, wt, w_gate, w_up, w_down)


def _rms_kernel(x_ref, g_ref, o_ref):
    o_ref[...] = _rms(x_ref[...], g_ref[...])


def rms_norm(x, g, tm=1024):
    n, d = x.shape
    tm = min(tm, n)
    return pl.pallas_call(
        _rms_kernel,
        grid=(n // tm,),
        in_specs=[pl.BlockSpec((tm, d), lambda i: (i, 0)), pl.BlockSpec((1, d), lambda i: (0, 0))],
        out_specs=pl.BlockSpec((tm, d), lambda i: (i, 0)),
        out_shape=jax.ShapeDtypeStruct((n, d), F32),
        compiler_params=_params("parallel"),
        name="final_rms_norm",
    )(x, g.reshape(1, d))


def _rope_angles(pos, n_freq, theta):
    inv_freq = theta ** (-jnp.arange(n_freq, dtype=F32) / n_freq)
    return pos.astype(F32)[:, None] * inv_freq[None, :]


def kernel(x, norm_mix_g, w_in, gate_b, w_branch, w_out, rwkv_mu_rkv, rwkv_mu_x, rwkv_w0, rwkv_w1, rwkv_w2,
           rwkv_a0, rwkv_a1, rwkv_a2, rwkv_g1, rwkv_g2, rwkv_k_k, rwkv_k_a, rwkv_r_k, rwkv_ln_w, rwkv_ln_b,
           s5_a_re, s5_a_im, s5_log_dt, s5_b_re, s5_b_im, s5_c_re, s5_c_im, s5_d, s5_glu_w, s5_glu_b,
           gqa_q_norm, gqa_k_norm, norm_ffn_g, dense_w_gate, dense_w_up, dense_w_down,
           moe_router, moe_w_gate, moe_w_up, moe_w_down, final_norm_g):
    batch, seq, d = x.shape
    depth = w_in.shape[0]
    n = batch * seq
    t = jnp.arange(seq, dtype=jnp.int32)
    rope_tabs = _rotary_tables(_rope_angles(t, ROPE_DIMS // 2, ROPE_THETA), N_HEADS)
    ang_axial = jnp.concatenate([_rope_angles(t // GRID_W, HEAD_DIM // 4, AXIAL_THETA),
                                 _rope_angles(t % GRID_W, HEAD_DIM // 4, AXIAL_THETA)], axis=-1)
    axial_tabs = _rotary_tables(ang_axial, N_HEADS)
    x = x.reshape(n, d)
    for l in range(depth):
        w_small = w_in[l, :, :OFF_GATES].astype(BF16)
        w_gates = w_in[l, :, OFF_GATES:].astype(BF16)
        xn, xnb, z = rms_in_proj(x, norm_mix_g[l], w_small)
        zg = matmul_bf16(xnb, w_gates, tm=1024, tn=1024, out_dtype=BF16)
        qa, ka, va, qd, kd, vd = qkv_prep(z, batch, seq, rope_tabs, axial_tabs, gqa_q_norm[l], gqa_k_norm[l])
        ya = dilated_attention(qa, ka, va)
        yd = gqa_attention(qd, kd, vd)
        lw1 = jnp.concatenate([rwkv_w1[l], rwkv_a1[l]], axis=-1).astype(BF16)
        zeros = jnp.zeros_like(rwkv_w2[l])
        lw2 = jnp.concatenate([jnp.concatenate([rwkv_w2[l], zeros], axis=-1),
                               jnp.concatenate([zeros, rwkv_a2[l]], axis=-1)], axis=1).astype(BF16)
        w0a0 = jnp.concatenate([rwkv_w0[l], rwkv_a0[l]], axis=-1)
        fw, bw, bonus, rgate = rwkv_prep(xn, z, seq, rwkv_mu_x[l], rwkv_mu_rkv[l].reshape(2, IN_B), lw1, lw2, w0a0,
                                         rwkv_g1[l].astype(BF16), rwkv_g2[l].astype(BF16),
                                         rwkv_k_k[l], rwkv_k_a[l], rwkv_r_k[l])
        wkv_f, wkv_b = wkv_chunked(fw, bw, batch, seq)
        discs = [_s5_discretize(s5_a_re[l, dr], s5_a_im[l, dr], s5_log_dt[l, dr], s5_b_re[l], s5_b_im[l],
                                s5_c_re[l, dr], s5_c_im[l, dr]) for dr in range(2)]
        s5_out = s5_bidirectional(z, batch, seq, discs)
        x = merge_branches(x, ya, yd, wkv_f, wkv_b, bonus, rgate, s5_out[0], s5_out[1], z, zg, gate_b[l],
                           w_branch[l].astype(BF16), w_out[l].astype(BF16), rwkv_ln_w[l], rwkv_ln_b[l],
                           s5_d[l], s5_glu_w[l].astype(BF16), s5_glu_b[l], batch, seq)
        i = l // 2
        if l % 2 == 0:
            x = dense_ffn(x, norm_ffn_g[l], dense_w_gate[i].astype(BF16), dense_w_up[i].astype(BF16),
                          dense_w_down[i].astype(BF16))
        else:
            xnb_f, sel, wt = moe_route(x, norm_ffn_g[l], moe_router[i])
            x = moe_ffn(x, xnb_f, sel, wt, moe_w_gate[i].astype(BF16), moe_w_up[i].astype(BF16),
                        moe_w_down[i].astype(BF16))
    return rms_norm(x, final_norm_g).reshape(batch, seq, d)
```

```python
import functools
import math

import jax
import jax.numpy as jnp
from jax import lax
from jax.experimental import pallas as pl
from jax.experimental.pallas import tpu as pltpu

F32 = jnp.float32
BF16 = jnp.bfloat16

D_MODEL = 1024
HEAD_DIM = 64
BRANCH_WIDTH = 256
N_BRANCHES = 4
N_HEADS = BRANCH_WIDTH // HEAD_DIM
DILATED_PATTERNS = ((128, 1), (512, 4), (2048, 16))
ROPE_THETA = 500000.0
ROPE_DIMS = HEAD_DIM // 4
RWKV_GN_EPS = 64e-5
S5_GROUP_CH = 16
S5_GROUPS = BRANCH_WIDTH // S5_GROUP_CH
S5_STATE = 64
GQA_KV_HEADS = 2
AXIAL_THETA = 10000.0
GRID_W = 64
N_EXPERTS = 8
TOP_K = 2
NORM_EPS = 1e-6
NEG_INF = -1e30

IN_A = 3 * BRANCH_WIDTH
IN_B = 3 * BRANCH_WIDTH
IN_C = BRANCH_WIDTH
IN_DQ = BRANCH_WIDTH
IN_DKV = GQA_KV_HEADS * HEAD_DIM
IN_GATES = N_BRANCHES * D_MODEL
OFF_B = IN_A
OFF_C = OFF_B + IN_B
OFF_DQ = OFF_C + IN_C
OFF_DKV = OFF_DQ + IN_DQ
OFF_GATES = OFF_DKV + 2 * IN_DKV
IN_TOTAL = OFF_GATES + IN_GATES

VMEM_LIMIT_BYTES = 56 * 1024 * 1024


def _params(*semantics):
    return pltpu.CompilerParams(dimension_semantics=semantics, vmem_limit_bytes=VMEM_LIMIT_BYTES)


def _bdot(a, b):
    return jnp.dot(a.astype(BF16), b.astype(BF16), preferred_element_type=F32)


def _bdot_nt(a, b):
    return lax.dot_general(a.astype(BF16), b.astype(BF16), (((1,), (1,)), ((), ())),
                           preferred_element_type=F32)


def _rms(x, g):
    return x * lax.rsqrt(jnp.mean(x * x, axis=-1, keepdims=True) + NORM_EPS) * g


def _rms_in_proj_kernel(x_ref, g_ref, w_ref, xn_ref, xnb_ref, z_ref):
    @pl.when(pl.program_id(1) == 0)
    def _():
        y = _rms(x_ref[...], g_ref[...])
        xn_ref[...] = y
        xnb_ref[...] = y.astype(BF16)

    z_ref[...] = jnp.dot(xnb_ref[...], w_ref[...], preferred_element_type=F32)


def rms_in_proj(x, g, w_bf16, tm=1024, tn=768):
    n, d = x.shape
    nout = w_bf16.shape[1]
    tm = min(tm, n)
    return pl.pallas_call(
        _rms_in_proj_kernel,
        grid=(n // tm, nout // tn),
        in_specs=[pl.BlockSpec((tm, d), lambda i, j: (i, 0)),
                  pl.BlockSpec((1, d), lambda i, j: (0, 0)),
                  pl.BlockSpec((d, tn), lambda i, j: (0, j))],
        out_specs=[pl.BlockSpec((tm, d), lambda i, j: (i, 0)),
                   pl.BlockSpec((tm, d), lambda i, j: (i, 0)),
                   pl.BlockSpec((tm, tn), lambda i, j: (i, j))],
        out_shape=[jax.ShapeDtypeStruct((n, d), F32), jax.ShapeDtypeStruct((n, d), BF16),
                   jax.ShapeDtypeStruct((n, nout), F32)],
        compiler_params=_params("parallel", "arbitrary"),
        name="rms_in_proj",
    )(x, g.reshape(1, d), w_bf16)


def _rotary_tables(pos_angles, n_heads):
    s, n = pos_angles.shape
    pad = HEAD_DIM - 2 * n
    cos = jnp.concatenate([jnp.cos(pos_angles), jnp.cos(pos_angles), jnp.ones((s, pad), F32)], axis=-1)
    zeros_n = jnp.zeros((s, n), F32)
    zeros_p = jnp.zeros((s, pad), F32)
    sin_lo = jnp.concatenate([-jnp.sin(pos_angles), zeros_n, zeros_p], axis=-1)
    sin_hi = jnp.concatenate([zeros_n, jnp.sin(pos_angles), zeros_p], axis=-1)
    return tuple(jnp.tile(t, (1, n_heads)) for t in (cos, sin_lo, sin_hi))


def _rotate(x, cos, sin_lo, sin_hi, n):
    width = x.shape[-1]
    from_above = pltpu.roll(x, width - n, 1)
    from_below = pltpu.roll(x, n, 1)
    return x * cos + from_above * sin_lo + from_below * sin_hi


def _head_sum(x, n_heads):
    lane = lax.broadcasted_iota(jnp.int32, x.shape, 1)
    out = jnp.zeros_like(x)
    for h in range(n_heads):
        in_head = (lane >= h * HEAD_DIM) & (lane < (h + 1) * HEAD_DIM)
        s = jnp.sum(jnp.where(in_head, x, 0.0), axis=-1, keepdims=True)
        out = jnp.where(in_head, s, out)
    return out


def _head_rms(x, g, n_heads):
    ms = _head_sum(x * x, n_heads) * (1.0 / HEAD_DIM)
    return x * lax.rsqrt(ms + NORM_EPS) * g


Q_SCALE = HEAD_DIM ** -0.5 * math.log2(math.e)


def _qkv_prep_kernel(za_ref, zq_ref, zkv_ref, rc_ref, rl_ref, rh_ref, ac_ref, al_ref, ah_ref,
                     qn_ref, kn_ref, qa_ref, ka_ref, va_ref, qd_ref, kd_ref, vd_ref):
    w = BRANCH_WIDTH
    n_rope = ROPE_DIMS // 2
    n_ax = HEAD_DIM // 2
    za = za_ref[...]
    rc, rl, rh = rc_ref[...], rl_ref[...], rh_ref[...]
    qa = _rotate(za[:, :w], rc, rl, rh, n_rope) * Q_SCALE
    ka = _rotate(za[:, w:2 * w], rc, rl, rh, n_rope)
    va = za[:, 2 * w:]
    ac, al, ah = ac_ref[...], al_ref[...], ah_ref[...]
    qd = _rotate(_head_rms(zq_ref[...], qn_ref[...], N_HEADS), ac, al, ah, n_ax) * Q_SCALE
    zkv = zkv_ref[...]
    kw = GQA_KV_HEADS * HEAD_DIM
    kd = _rotate(_head_rms(zkv[:, :kw], kn_ref[...], GQA_KV_HEADS), ac[:, :kw], al[:, :kw], ah[:, :kw], n_ax)
    vd = zkv[:, kw:]
    for h in range(N_HEADS):
        sl = slice(h * HEAD_DIM, (h + 1) * HEAD_DIM)
        qa_ref[0, h] = qa[:, sl].astype(BF16)
        ka_ref[0, h] = ka[:, sl].astype(BF16)
        va_ref[0, h] = va[:, sl].astype(BF16)
        qd_ref[0, h] = qd[:, sl].astype(BF16)
    for h in range(GQA_KV_HEADS):
        sl = slice(h * HEAD_DIM, (h + 1) * HEAD_DIM)
        kd_ref[0, h] = kd[:, sl].astype(BF16)
        vd_ref[0, h] = vd[:, sl].astype(BF16)


def qkv_prep(z, batch, seq, rope_tabs, axial_tabs, q_norm, k_norm, tm=512):
    tm = min(tm, seq)
    nt = seq // tm
    w = BRANCH_WIDTH
    row = lambda b, i: b * nt + i
    tab_spec = pl.BlockSpec((tm, w), lambda b, i: (i, 0))
    head_out = lambda nh: pl.BlockSpec((1, nh, tm, HEAD_DIM), lambda b, i: (b, 0, i, 0))
    head_shape = lambda nh: jax.ShapeDtypeStruct((batch, nh, seq, HEAD_DIM), BF16)
    return pl.pallas_call(
        _qkv_prep_kernel,
        grid=(batch, nt),
        in_specs=[pl.BlockSpec((tm, IN_A), lambda b, i: (row(b, i), 0)),
                  pl.BlockSpec((tm, w), lambda b, i: (row(b, i), OFF_DQ // w)),
                  pl.BlockSpec((tm, w), lambda b, i: (row(b, i), OFF_DKV // w)),
                  tab_spec, tab_spec, tab_spec, tab_spec, tab_spec, tab_spec,
                  pl.BlockSpec((1, w), lambda b, i: (0, 0)),
                  pl.BlockSpec((1, GQA_KV_HEADS * HEAD_DIM), lambda b, i: (0, 0))],
        out_specs=[head_out(N_HEADS), head_out(N_HEADS), head_out(N_HEADS),
                   head_out(N_HEADS), head_out(GQA_KV_HEADS), head_out(GQA_KV_HEADS)],
        out_shape=[head_shape(N_HEADS), head_shape(N_HEADS), head_shape(N_HEADS),
                   head_shape(N_HEADS), head_shape(GQA_KV_HEADS), head_shape(GQA_KV_HEADS)],
        compiler_params=_params("parallel", "parallel"),
        name="qkv_prep",
    )(z, z, z, *rope_tabs, *axial_tabs,
      jnp.tile(q_norm.reshape(1, HEAD_DIM), (1, N_HEADS)),
      jnp.tile(k_norm.reshape(1, HEAD_DIM), (1, GQA_KV_HEADS)))


A_TQ = 1024
A_SUB = 256
A_RADIUS = 64


def _dilated_windows():
    out = []
    for window, dil in DILATED_PATTERNS:
        halo = -(-(window // 2) // 128) * 128
        out.append((dil, -halo, A_SUB + 2 * halo))
    return out


def _dilated_bias():
    biases = []
    for dil, first, width in _dilated_windows():
        qi = jnp.arange(A_SUB, dtype=jnp.int32)[:, None]
        kj = jnp.arange(width, dtype=jnp.int32)[None, :] + first
        delta = kj - qi
        ok = (jnp.abs(delta) <= A_RADIUS * dil) & ((delta & (dil - 1)) == 0)
        biases.append(jnp.where(ok, 0.0, NEG_INF).astype(F32))
    return biases


def _dilated_attn_kernel(q_ref, kp_ref, kc_ref, kn_ref, vp_ref, vc_ref, vn_ref, b0_ref, b1_ref, b2_ref,
                         o_ref, k3_ref, v3_ref, *, seq, tq):
    i = pl.program_id(2)
    k3_ref[0:tq] = kp_ref[0, 0]
    k3_ref[tq:2 * tq] = kc_ref[0, 0]
    k3_ref[2 * tq:3 * tq] = kn_ref[0, 0]
    v3_ref[0:tq] = vp_ref[0, 0]
    v3_ref[tq:2 * tq] = vc_ref[0, 0]
    v3_ref[2 * tq:3 * tq] = vn_ref[0, 0]
    bias_refs = (b0_ref, b1_ref, b2_ref)
    windows = _dilated_windows()
    for u in range(tq // A_SUB):
        q = q_ref[0, 0, u * A_SUB:(u + 1) * A_SUB, :]
        scores = []
        for (dil, first, width), b_ref in zip(windows, bias_refs):
            start = tq + u * A_SUB + first
            s = _bdot_nt(q, k3_ref[start:start + width, :]) + b_ref[...]
            kpos = (i - 1) * tq + start + lax.broadcasted_iota(jnp.int32, (1, width), 1)
            s = jnp.where((kpos >= 0) & (kpos < seq), s, NEG_INF)
            scores.append((s, start, width))
        m = functools.reduce(jnp.maximum, [jnp.max(s, axis=-1, keepdims=True) for s, _, _ in scores])
        l = jnp.zeros_like(m)
        acc = jnp.zeros((A_SUB, HEAD_DIM), F32)
        for s, start, width in scores:
            p = jnp.exp2(s - m)
            l = l + jnp.sum(p, axis=-1, keepdims=True)
            acc = acc + _bdot(p, v3_ref[start:start + width, :])
        o_ref[0, 0, u * A_SUB:(u + 1) * A_SUB, :] = acc / l


def dilated_attention(qa, ka, va):
    batch, nh, seq, hd = qa.shape
    tq = min(A_TQ, seq)
    assert tq == A_TQ, "key halo of radius * max dilation needs full query tiles"
    nt = seq // tq
    cur = pl.BlockSpec((1, 1, tq, hd), lambda b, h, i: (b, h, i, 0))
    prev = pl.BlockSpec((1, 1, tq, hd), lambda b, h, i: (b, h, jnp.maximum(i - 1, 0), 0))
    nxt = pl.BlockSpec((1, 1, tq, hd), lambda b, h, i: (b, h, jnp.minimum(i + 1, nt - 1), 0))
    biases = _dilated_bias()
    bias_specs = [pl.BlockSpec(b.shape, lambda b_, h, i: (0, 0)) for b in biases]
    return pl.pallas_call(
        functools.partial(_dilated_attn_kernel, seq=seq, tq=tq),
        grid=(batch, nh, nt),
        in_specs=[cur, prev, cur, nxt, prev, cur, nxt] + bias_specs,
        out_specs=cur,
        out_shape=jax.ShapeDtypeStruct((batch, nh, seq, hd), F32),
        scratch_shapes=[pltpu.VMEM((3 * tq, hd), BF16), pltpu.VMEM((3 * tq, hd), BF16)],
        compiler_params=_params("parallel", "parallel", "parallel"),
        name="dilated_attention",
    )(qa, ka, ka, ka, va, va, va, *biases)


GQA_SUB = 128


def _gqa_kernel(q_ref, k_ref, v_ref, o_ref, *scratch, rep, tq):
    j = pl.program_id(3)
    n_sub = tq // GQA_SUB
    blocks = [(r, u) for r in range(rep) for u in range(n_sub)]
    m_refs, l_refs, acc_refs = (scratch[i * len(blocks):(i + 1) * len(blocks)] for i in range(3))

    @pl.when(j == 0)
    def _():
        for m_ref, l_ref, acc_ref in zip(m_refs, l_refs, acc_refs):
            m_ref[...] = jnp.full(m_ref.shape, NEG_INF, F32)
            l_ref[...] = jnp.zeros(l_ref.shape, F32)
            acc_ref[...] = jnp.zeros(acc_ref.shape, F32)

    k = k_ref[0, 0]
    v = v_ref[0, 0]
    scores = [_bdot_nt(k, q_ref[0, r, u * GQA_SUB:(u + 1) * GQA_SUB, :]) for r, u in blocks]
    for s, m_ref, l_ref, acc_ref in zip(scores, m_refs, l_refs, acc_refs):
        m_prev = m_ref[...]
        m_new = jnp.maximum(m_prev, jnp.max(s, axis=0, keepdims=True))
        alpha = jnp.exp2(m_prev - m_new)
        p = jnp.exp2(s - m_new)
        l_ref[...] = alpha * l_ref[...] + jnp.sum(p, axis=0, keepdims=True)
        pv = lax.dot_general(v, p.astype(BF16), (((0,), (0,)), ((), ())), preferred_element_type=F32)
        acc_ref[...] = alpha * acc_ref[...] + pv
        m_ref[...] = m_new

    @pl.when(j == pl.num_programs(3) - 1)
    def _():
        for (r, u), l_ref, acc_ref in zip(blocks, l_refs, acc_refs):
            o_ref[0, r, :, u * GQA_SUB:(u + 1) * GQA_SUB] = acc_ref[...] / l_ref[...]


def gqa_attention(qd, kd, vd, tq=1024, tk=2048):
    batch, nh, seq, hd = qd.shape
    ng = kd.shape[1]
    rep = nh // ng
    tq = min(tq, seq)
    tk = min(tk, seq)
    n_blocks = rep * (tq // GQA_SUB)
    return pl.pallas_call(
        functools.partial(_gqa_kernel, rep=rep, tq=tq),
        grid=(batch, ng, seq // tq, seq // tk),
        in_specs=[pl.BlockSpec((1, rep, tq, hd), lambda b, g, i, j: (b, g, i, 0)),
                  pl.BlockSpec((1, 1, tk, hd), lambda b, g, i, j: (b, g, j, 0)),
                  pl.BlockSpec((1, 1, tk, hd), lambda b, g, i, j: (b, g, j, 0))],
        out_specs=pl.BlockSpec((1, rep, hd, tq), lambda b, g, i, j: (b, g, 0, i)),
        out_shape=jax.ShapeDtypeStruct((batch, nh, hd, seq), F32),
        scratch_shapes=([pltpu.VMEM((1, GQA_SUB), F32)] * (2 * n_blocks)
                        + [pltpu.VMEM((hd, GQA_SUB), F32)] * n_blocks),
        compiler_params=_params("parallel", "parallel", "parallel", "arbitrary"),
        name="gqa_attention",
    )(qd, kd, vd)


def _matmul_kernel(a_ref, w_ref, o_ref):
    o_ref[...] = jnp.dot(a_ref[...], w_ref[...], preferred_element_type=F32).astype(o_ref.dtype)


def matmul_bf16(a, w, tm, tn, out_dtype=F32):
    n, k = a.shape
    m = w.shape[1]
    tm = min(tm, n)
    return pl.pallas_call(
        _matmul_kernel,
        grid=(n // tm, m // tn),
        in_specs=[pl.BlockSpec((tm, k), lambda i, j: (i, 0)),
                  pl.BlockSpec((k, tn), lambda i, j: (0, j))],
        out_specs=pl.BlockSpec((tm, tn), lambda i, j: (i, j)),
        out_shape=jax.ShapeDtypeStruct((n, m), out_dtype),
        compiler_params=_params("parallel", "arbitrary"),
        name="matmul_bf16",
    )(a, w)


def _sigmoid(x):
    return 1.0 / (1.0 + jnp.exp(-x))


def _softplus(x):
    return jnp.maximum(x, 0.0) + jnp.log(1.0 + jnp.exp(-jnp.abs(x)))


def _shift_rows(x, edge_row, down):
    rows = x.shape[0]
    ridx = lax.broadcasted_iota(jnp.int32, x.shape, 0)
    if down:
        return jnp.where(ridx == 0, edge_row, pltpu.roll(x, 1, 0))
    return jnp.where(ridx == rows - 1, edge_row, pltpu.roll(x, rows - 1, 0))


WKV_FIELDS = 6


def _rwkv_prep_kernel(xn_ref, xp_ref, xq_ref, zb_ref, zp_ref, zq_ref, mux_ref, murkv_ref, lw1_ref, lw2_ref,
                      w0a0_ref, g1_ref, g2_ref, kk_ref, ka_ref, rk_ref,
                      fw_ref, bw_ref, bonus_ref, gate_ref, *, tiles_per_seq):
    w = BRANCH_WIDTH
    i = pl.program_id(0)
    first = (i % tiles_per_seq) == 0
    last = (i % tiles_per_seq) == tiles_per_seq - 1
    xn = xn_ref[...]
    x_shift = (_shift_rows(xn, jnp.where(first, 0.0, xp_ref[7:8, :]), True),
               _shift_rows(xn, jnp.where(last, 0.0, xq_ref[0:1, :]), False))
    zb = zb_ref[...]
    z_prev = _shift_rows(zb, jnp.where(first, 0.0, zp_ref[7:8, :]), True)
    z_next = _shift_rows(zb, jnp.where(last, 0.0, zq_ref[0:1, :]), False)
    mu = murkv_ref[...]
    rkv = zb + mu[0:1] * (z_prev - zb) + mu[1:2] * (z_next - zb)
    r, k, v = rkv[:, :w], rkv[:, w:2 * w], rkv[:, 2 * w:]
    kap = k * kk_ref[...]
    kap = kap * lax.rsqrt(_head_sum(kap * kap, N_HEADS) + 1e-12)
    gate_ref[...] = _bdot(_sigmoid(_bdot(xn, g1_ref[...])), g2_ref[...])
    bonus = jnp.zeros_like(v)
    lora_lane = lax.broadcasted_iota(jnp.int32, (xn.shape[0], lw1_ref.shape[-1]), 1)
    for d, out_ref in enumerate((fw_ref, bw_ref)):
        xd = xn + mux_ref[d:d + 1, :] * (x_shift[d] - xn)
        h = _bdot(xd, lw1_ref[d])
        h = jnp.where(lora_lane < lw1_ref.shape[-1] // 2, jnp.tanh(h), h)
        h = _bdot(h, lw2_ref[d]) + w0a0_ref[d:d + 1, :]
        w_log = -_softplus(-h[:, :w]) - 0.5
        log_decay = -jnp.exp(w_log)
        iclr = _sigmoid(h[:, w:])
        k_d = k * (1.0 + (iclr - 1.0) * ka_ref[...])
        bonus = bonus + _head_sum(r * k_d * rk_ref[...], N_HEADS) * v
        for j, field in enumerate((r, log_decay, k_d, v, kap, iclr * kap)):
            out_ref[:, j * w:(j + 1) * w] = field
    bonus_ref[...] = bonus


def rwkv_prep(xn, z, seq, mu_x, mu_rkv, lw1, lw2, w0a0, g1, g2, k_k, k_a, r_k, tm=512):
    n, d = xn.shape
    w = BRANCH_WIDTH
    tm = min(tm, seq)
    halo = 8
    prev_halo = lambda i: (jnp.maximum(i * (tm // halo) - 1, 0), 0)
    next_halo = lambda i: (jnp.minimum((i + 1) * (tm // halo), n // halo - 1), 0)
    full = lambda a: pl.BlockSpec(a.shape, lambda i: (0,) * a.ndim)
    params = (mu_x, mu_rkv, lw1, lw2, w0a0, g1, g2, k_k.reshape(1, w), k_a.reshape(1, w), r_k.reshape(1, w))
    rows = lambda width: pl.BlockSpec((tm, width), lambda i: (i, 0))
    return pl.pallas_call(
        functools.partial(_rwkv_prep_kernel, tiles_per_seq=seq // tm),
        grid=(n // tm,),
        in_specs=[rows(d), pl.BlockSpec((halo, d), prev_halo), pl.BlockSpec((halo, d), next_halo),
                  pl.BlockSpec((tm, IN_B), lambda i: (i, OFF_B // IN_B)),
                  pl.BlockSpec((halo, IN_B), lambda i: (prev_halo(i)[0], OFF_B // IN_B)),
                  pl.BlockSpec((halo, IN_B), lambda i: (next_halo(i)[0], OFF_B // IN_B))]
                 + [full(p) for p in params],
        out_specs=[rows(WKV_FIELDS * w), rows(WKV_FIELDS * w), rows(w), rows(w)],
        out_shape=[jax.ShapeDtypeStruct((n, WKV_FIELDS * w), F32), jax.ShapeDtypeStruct((n, WKV_FIELDS * w), F32),
                   jax.ShapeDtypeStruct((n, w), F32), jax.ShapeDtypeStruct((n, w), F32)],
        compiler_params=_params("parallel"),
        name="rwkv_prep",
    )(xn, xn, xn, z, z, z, *params)


WKV_CHUNK = 64


def _wkv_constants(reverse):
    c, nh = WKV_CHUNK, N_HEADS
    t = jnp.arange(c)
    before = (t[None, :] > t[:, None]) if reverse else (t[None, :] < t[:, None])
    incl = before | (t[None, :] == t[:, None])
    per_head = lambda m: jnp.kron(jnp.eye(nh, dtype=F32), m.astype(F32))
    return incl.astype(F32), per_head(before), per_head(incl)


def _cumulative_log_decay(tri, logws):
    w = logws[0].shape[1]
    cat = jnp.concatenate(logws, axis=1)
    hi = cat.astype(BF16)
    lo = (cat - hi.astype(F32)).astype(BF16)
    tri = tri.astype(BF16)
    g = jnp.dot(tri, hi, preferred_element_type=F32) + jnp.dot(tri, lo, preferred_element_type=F32)
    return [g[:, i * w:(i + 1) * w] for i in range(len(logws))]


def _wkv_chunks(chains, head_rows, s_mask):
    c, w, nh = WKV_CHUNK, BRANCH_WIDTH, N_HEADS
    hc = nh * c
    every = range(len(chains))

    def per_head(a):
        return jnp.concatenate([jnp.where(head_rows[h:h + 1, :] > 0.0, a, 0.0) for h in range(nh)], axis=0)

    def wide(m):
        return m[0:c] + m[c:2 * c] + m[2 * c:3 * c] + m[3 * c:4 * c]

    gram_lhs, gram_rhs, gam, vs, k_ts, b_ts = [], [], [], [], [], []
    for x, g, s_bd, _, _, _ in chains:
        r, logw, k, v, kap, b = (x[:, j * w:(j + 1) * w] for j in range(WKV_FIELDS))
        gam.append(jnp.exp(g))
        g_inv = jnp.exp(-g)
        kap_t, r_t = kap * jnp.exp(g - logw), r * gam[-1]
        k_ts.append(k * g_inv)
        b_ts.append(b * g_inv)
        vs.append(v)
        gram_lhs.append(jnp.concatenate([per_head(kap_t), per_head(r_t)], axis=0))
        gram_rhs.append(jnp.concatenate([per_head(k_ts[-1]), per_head(b_ts[-1]), s_bd], axis=0))
    gram = [_bdot_nt(gram_lhs[i], gram_rhs[i]) for i in every]
    kk = [gram[i][:hc, :hc] * chains[i][3] for i in every]
    n = [gram[i][:hc, hc:2 * hc] * chains[i][3] for i in every]
    rk = [gram[i][hc:, :hc] * chains[i][4] for i in every]
    rb = [gram[i][hc:, hc:2 * hc] * chains[i][4] for i in every]
    from_kap = [wide(gram[i][:hc, 2 * hc:]) for i in every]
    from_r = [wide(gram[i][hc:, 2 * hc:]) for i in every]
    eye = jnp.where(lax.broadcasted_iota(jnp.int32, (hc, hc), 0) == lax.broadcasted_iota(jnp.int32, (hc, hc), 1),
                    1.0, 0.0)
    inv = [eye - n[i] for i in every]
    power = [_bdot(n[i], n[i]) for i in every]
    levels = c.bit_length() - 2
    for level in range(levels):
        inv = [inv[i] + _bdot(inv[i], power[i]) for i in every]
        if level + 1 < levels:
            power = [_bdot(power[i], power[i]) for i in every]
    v_heads = [per_head(v) for v in vs]
    kk_v = [_bdot(wide(kk[i]), v_heads[i]) for i in every]
    u = [_bdot(wide(inv[i]), per_head(from_kap[i] + kk_v[i])) for i in every]
    y = [from_r[i] + _bdot(jnp.concatenate([wide(rk[i]), -wide(rb[i])], axis=1),
                           jnp.concatenate([v_heads[i], per_head(u[i])], axis=0)) for i in every]
    update = [lax.dot_general(jnp.concatenate([vs[i], u[i]], axis=0).astype(BF16),
                              jnp.concatenate([k_ts[i], -b_ts[i]], axis=0).astype(BF16),
                              (((0,), (0,)), ((), ())), preferred_element_type=F32) for i in every]
    s_new = [(chains[i][2] + update[i] * s_mask) * gam[i][chains[i][5]:chains[i][5] + 1, :] for i in every]
    return y, s_new


def _wkv_chunked_kernel(fw_ref, bw_ref, trif_ref, msf_ref, mif_ref, trib_ref, msb_ref, mib_ref, hr_ref, sm_ref,
                        yf_ref, yb_ref, state_ref, *, batch, rows):
    @pl.when(pl.program_id(0) == 0)
    def _():
        state_ref[...] = jnp.zeros(state_ref.shape, F32)

    c = WKV_CHUNK
    n_chunks = rows // c
    head_rows = hr_ref[...]
    s_mask = sm_ref[...]
    tris = (trif_ref[...], trib_ref[...])
    masks = ((msf_ref[...], mif_ref[...]), (msb_ref[...], mib_ref[...]))
    refs, y_refs = (fw_ref, bw_ref), (yf_ref, yb_ref)
    w = BRANCH_WIDTH

    def body(i, carry):
        bases = (pl.multiple_of(i * c, c), pl.multiple_of((n_chunks - 1 - i) * c, c))
        chains, where = [], []
        for d in range(2):
            xs = [refs[d][bi, pl.ds(bases[d], c), :] for bi in range(batch)]
            gs = _cumulative_log_decay(tris[d], [x[:, w:2 * w] for x in xs])
            for bi in range(batch):
                chains.append((xs[bi], gs[bi], state_ref[2 * bi + d], *masks[d], 0 if d == 1 else c - 1))
                where.append((bi, d))
        ys, states = _wkv_chunks(chains, head_rows, s_mask)
        for (bi, d), y, s_new in zip(where, ys, states):
            y_refs[d][bi, pl.ds(bases[d], c), :] = y
            state_ref[2 * bi + d] = s_new
        return carry

    lax.fori_loop(0, n_chunks, body, 0)


def wkv_chunked(fw, bw, batch, seq, rows=256):
    w = BRANCH_WIDTH
    rows = min(rows, seq)
    nb = seq // rows
    head_of_lane = jnp.arange(w) // HEAD_DIM
    head_rows = (jnp.arange(8)[:, None] == head_of_lane[None, :]).astype(F32)
    s_mask = (head_of_lane[:, None] == head_of_lane[None, :]).astype(F32)
    consts = [*_wkv_constants(False), *_wkv_constants(True), head_rows, s_mask]
    in_f = pl.BlockSpec((batch, rows, WKV_FIELDS * w), lambda c: (0, c, 0))
    in_b = pl.BlockSpec((batch, rows, WKV_FIELDS * w), lambda c: (0, nb - 1 - c, 0))
    out_shape = jax.ShapeDtypeStruct((batch, seq, w), F32)
    yf, yb = pl.pallas_call(
        functools.partial(_wkv_chunked_kernel, batch=batch, rows=rows),
        grid=(nb,),
        in_specs=[in_f, in_b] + [pl.BlockSpec(a.shape, lambda c: (0, 0)) for a in consts],
        out_specs=[pl.BlockSpec((batch, rows, w), lambda c: (0, c, 0)),
                   pl.BlockSpec((batch, rows, w), lambda c: (0, nb - 1 - c, 0))],
        out_shape=[out_shape, out_shape],
        scratch_shapes=[pltpu.VMEM((2 * batch, w, w), F32)],
        compiler_params=_params("arbitrary"),
        name="wkv_chunked",
    )(fw.reshape(batch, seq, -1), bw.reshape(batch, seq, -1), *consts)
    return yf.reshape(batch * seq, w), yb.reshape(batch * seq, w)


S5_SEGMENTS = 8
S5_WIDTH = S5_GROUPS * S5_STATE


def _s5_discretize(a_re, a_im, log_dt, b_re, b_im, c_re, c_im):
    g, p, c = S5_GROUPS, S5_STATE, S5_GROUP_CH
    dt = jnp.exp(log_dt)[:, None]
    mag = jnp.exp(a_re * dt)
    bar_re, bar_im = mag * jnp.cos(a_im * dt), mag * jnp.sin(a_im * dt)
    den = a_re * a_re + a_im * a_im
    f_re = ((bar_re - 1.0) * a_re + bar_im * a_im) / den
    f_im = (bar_im * a_re - (bar_re - 1.0) * a_im) / den
    bb_re = f_re[..., None] * b_re - f_im[..., None] * b_im
    bb_im = f_re[..., None] * b_im + f_im[..., None] * b_re
    eye_g = jnp.eye(g, dtype=F32)
    w_in = jnp.concatenate(
        [jnp.einsum('gpc,gh->gchp', bb, eye_g).reshape(g * c, g * p) for bb in (bb_re, bb_im)], axis=1)
    w_out = jnp.concatenate(
        [jnp.einsum('gcp,gh->gphc', cc, eye_g).reshape(g * p, g * c) for cc in (c_re, -c_im)], axis=0)
    return bar_re.reshape(1, g * p), bar_im.reshape(1, g * p), w_in.astype(BF16), w_out.astype(BF16), dt, a_re, a_im


def _s5_powers(a_re, a_im, dt, count, reverse):
    j = jnp.arange(1, count + 1, dtype=F32)
    if reverse:
        j = j[::-1]
    e = j[:, None, None] * (a_re * dt)[None]
    th = j[:, None, None] * (a_im * dt)[None]
    mag = jnp.exp(e)
    return (mag * jnp.cos(th)).reshape(count, -1), (mag * jnp.sin(th)).reshape(count, -1)


def _s5_scan_kernel(u_ref, win_ref, lre_ref, lim_ref, pre_ref, pim_ref, wout_ref, y_ref,
                    xre_ref, xim_ref, cre_ref, cim_ref, *, reverse, chunk):
    seg = chunk // S5_SEGMENTS
    nw = S5_WIDTH

    @pl.when(pl.program_id(1) == 0)
    def _():
        cre_ref[...] = jnp.zeros(cre_ref.shape, F32)
        cim_ref[...] = jnp.zeros(cim_ref.shape, F32)

    bu = jnp.dot(u_ref[...].astype(BF16), win_ref[...], preferred_element_type=F32)
    xre_ref[...] = bu[:, :nw]
    xim_ref[...] = bu[:, nw:]
    lre = lre_ref[...]
    lim = lim_ref[...]
    group = lambda s: pl.ds(pl.multiple_of(s * S5_SEGMENTS, S5_SEGMENTS), S5_SEGMENTS)

    def local_step(s, carry):
        xr, xi = carry
        rows = group((seg - 1 - s) if reverse else s)
        nr = lre * xr - lim * xi + xre_ref[rows, :]
        ni = lre * xi + lim * xr + xim_ref[rows, :]
        xre_ref[rows, :] = nr
        xim_ref[rows, :] = ni
        return nr, ni

    zero = jnp.zeros((S5_SEGMENTS, nw), F32)
    fin_re, fin_im = lax.fori_loop(0, seg, local_step, (zero, zero))

    full_seg = 0 if reverse else seg - 1
    pl_re, pl_im = pre_ref[full_seg:full_seg + 1, :], pim_ref[full_seg:full_seg + 1, :]
    ir, ii = cre_ref[...], cim_ref[...]
    init_re, init_im = [None] * S5_SEGMENTS, [None] * S5_SEGMENTS
    for j in (range(S5_SEGMENTS - 1, -1, -1) if reverse else range(S5_SEGMENTS)):
        init_re[j], init_im[j] = ir, ii
        fr, fi = fin_re[j:j + 1, :], fin_im[j:j + 1, :]
        ir, ii = fr + (pl_re * ir - pl_im * ii), fi + (pl_re * ii + pl_im * ir)
    cre_ref[...] = ir
    cim_ref[...] = ii
    init_re = jnp.concatenate(init_re, axis=0)
    init_im = jnp.concatenate(init_im, axis=0)

    def correct(s, carry):
        rows = group(s)
        pr, pi = pre_ref[pl.ds(s, 1), :], pim_ref[pl.ds(s, 1), :]
        xre_ref[rows, :] = xre_ref[rows, :] + (pr * init_re - pi * init_im)
        xim_ref[rows, :] = xim_ref[rows, :] + (pr * init_im + pi * init_re)
        return carry

    lax.fori_loop(0, seg, correct, 0)
    y_ref[...] = (jnp.dot(xre_ref[...].astype(BF16), wout_ref[:nw, :], preferred_element_type=F32)
                  + jnp.dot(xim_ref[...].astype(BF16), wout_ref[nw:, :], preferred_element_type=F32))


def _segment_interleave(a, batch, seq, chunk, inverse=False):
    w = a.shape[-1]
    seg = chunk // S5_SEGMENTS
    shape = (batch, seq // chunk, seg, S5_SEGMENTS, w) if inverse else (batch, seq // chunk, S5_SEGMENTS, seg, w)
    return jnp.swapaxes(a.reshape(shape), 2, 3).reshape(batch * seq, w)


def s5_scan(u_interleaved, batch, seq, disc, reverse, chunk):
    lam_re, lam_im, w_in, w_out, dt, a_re, a_im = disc
    w = BRANCH_WIDTH
    nc = seq // chunk
    seg = chunk // S5_SEGMENTS
    pw_re, pw_im = _s5_powers(a_re, a_im, dt, seg, reverse)
    order = (lambda c: nc - 1 - c) if reverse else (lambda c: c)
    full = lambda a: pl.BlockSpec(a.shape, lambda b, c: (0, 0))
    return pl.pallas_call(
        functools.partial(_s5_scan_kernel, reverse=reverse, chunk=chunk),
        grid=(batch, nc),
        in_specs=[pl.BlockSpec((chunk, w), lambda b, c: (b * nc + order(c), 0)),
                  full(w_in), full(lam_re), full(lam_im), full(pw_re), full(pw_im), full(w_out)],
        out_specs=pl.BlockSpec((chunk, w), lambda b, c: (b * nc + order(c), 0)),
        out_shape=jax.ShapeDtypeStruct((batch * seq, w), F32),
        scratch_shapes=[pltpu.VMEM((chunk, S5_WIDTH), F32), pltpu.VMEM((chunk, S5_WIDTH), F32),
                        pltpu.VMEM((1, S5_WIDTH), F32), pltpu.VMEM((1, S5_WIDTH), F32)],
        compiler_params=_params("parallel", "arbitrary"),
        name="s5_scan_bwd" if reverse else "s5_scan_fwd",
    )(u_interleaved, w_in, lam_re, lam_im, pw_re, pw_im, w_out)


def s5_bidirectional(z, batch, seq, discs, chunk=1024):
    chunk = min(chunk, seq)
    u = _segment_interleave(z[:, OFF_C:OFF_C + BRANCH_WIDTH], batch, seq, chunk)
    return [_segment_interleave(s5_scan(u, batch, seq, disc, reverse=(d == 1), chunk=chunk),
                                batch, seq, chunk, inverse=True) for d, disc in enumerate(discs)]


def _gelu_tanh(y):
    return 0.5 * y * (1.0 + jnp.tanh(math.sqrt(2.0 / math.pi) * (y + 0.044715 * (y * y * y))))


def _merge_kernel(x_ref, ya_ref, yd_ref, wf_ref, wb_ref, bonus_ref, rg_ref, sf_ref, sb_ref, u_ref, zg_ref,
                  gb_ref, wbr_ref, wout_ref, lnw_ref, lnb_ref, s5d_ref, gluw_ref, glub_ref, o_ref):
    w = BRANCH_WIDTH
    ya = jnp.concatenate([ya_ref[0, h] for h in range(N_HEADS)], axis=1)
    ys = wf_ref[...] + wb_ref[...]
    cen = ys - _head_sum(ys, N_HEADS) * (1.0 / HEAD_DIM)
    var = _head_sum(cen * cen, N_HEADS) * (1.0 / HEAD_DIM)
    yb = (cen * lax.rsqrt(var + RWKV_GN_EPS) * lnw_ref[...] + lnb_ref[...] + bonus_ref[...]) * rg_ref[...]
    yc = sf_ref[...] + sb_ref[...] + s5d_ref[...] * u_ref[...]
    h = _bdot(_gelu_tanh(yc), gluw_ref[...]) + glub_ref[...]
    yc = h[:, :w] * _sigmoid(h[:, w:])
    proj_d = sum(lax.dot_general(yd_ref[0, h].astype(BF16), wbr_ref[3, h * HEAD_DIM:(h + 1) * HEAD_DIM, :],
                                 (((0,), (0,)), ((), ())), preferred_element_type=F32) for h in range(N_HEADS))
    merged = jnp.zeros(o_ref.shape, F32)
    for i, proj in enumerate((_bdot(ya, wbr_ref[0]), _bdot(yb, wbr_ref[1]), _bdot(yc, wbr_ref[2]), proj_d)):
        gate = _sigmoid(zg_ref[:, i * D_MODEL:(i + 1) * D_MODEL] + gb_ref[i:i + 1, :])
        merged = merged + gate * proj
    o_ref[...] = x_ref[...] + _bdot(merged, wout_ref[...])


def merge_branches(x, ya, yd, wkv_f, wkv_b, bonus, rgate, s5_f, s5_b, z, zg, gate_b, w_branch, w_out,
                   ln_w, ln_b, s5_d, glu_w, glu_b, batch, seq, tm=512):
    w = BRANCH_WIDTH
    d = D_MODEL
    tm = min(tm, seq)
    nt = seq // tm
    rows = lambda width, col=0: pl.BlockSpec((tm, width), lambda b, i: (b * nt + i, col))
    heads = pl.BlockSpec((1, N_HEADS, tm, HEAD_DIM), lambda b, i: (b, 0, i, 0))
    heads_t = pl.BlockSpec((1, N_HEADS, HEAD_DIM, tm), lambda b, i: (b, 0, 0, i))
    full = lambda a: pl.BlockSpec(a.shape, lambda b, i: (0,) * a.ndim)
    params = (gate_b, w_branch, w_out, ln_w.reshape(1, w), ln_b.reshape(1, w), s5_d.reshape(1, w),
              glu_w, glu_b.reshape(1, 2 * w))
    return pl.pallas_call(
        _merge_kernel,
        grid=(batch, nt),
        in_specs=[rows(d), heads, heads_t, rows(w), rows(w), rows(w), rows(w), rows(w), rows(w),
                  rows(w, OFF_C // w), rows(N_BRANCHES * d)] + [full(p) for p in params],
        out_specs=rows(d),
        out_shape=jax.ShapeDtypeStruct(x.shape, F32),
        compiler_params=_params("parallel", "parallel"),
        name="merge_branches",
    )(x, ya, yd, wkv_f, wkv_b, bonus, rgate, s5_f, s5_b, z, zg, *params)


def _silu(x):
    return x * _sigmoid(x)


def _dense_ffn_kernel(x_ref, g_ref, wg_ref, wu_ref, wd_ref, o_ref, xn_ref):
    @pl.when(pl.program_id(1) == 0)
    def _():
        x = x_ref[...]
        xn_ref[...] = _rms(x, g_ref[...]).astype(BF16)
        o_ref[...] = x

    xn = xn_ref[...]
    h = (_silu(jnp.dot(xn, wg_ref[...], preferred_element_type=F32))
         * jnp.dot(xn, wu_ref[...], preferred_element_type=F32))
    o_ref[...] += _bdot(h, wd_ref[...])


def dense_ffn(x, g, w_gate, w_up, w_down, tm=1024, tf=1408):
    n, d = x.shape
    ff = w_gate.shape[1]
    tm = min(tm, n)
    return pl.pallas_call(
        _dense_ffn_kernel,
        grid=(n // tm, ff // tf),
        in_specs=[pl.BlockSpec((tm, d), lambda i, f: (i, 0)),
                  pl.BlockSpec((1, d), lambda i, f: (0, 0)),
                  pl.BlockSpec((d, tf), lambda i, f: (0, f)),
                  pl.BlockSpec((d, tf), lambda i, f: (0, f)),
                  pl.BlockSpec((tf, d), lambda i, f: (f, 0))],
        out_specs=pl.BlockSpec((tm, d), lambda i, f: (i, 0)),
        out_shape=jax.ShapeDtypeStruct((n, d), F32),
        scratch_shapes=[pltpu.VMEM((tm, d), BF16)],
        compiler_params=_params("parallel", "arbitrary"),
        name="dense_ffn",
    )(x, g.reshape(1, d), w_gate, w_up, w_down)


def _router_kernel(x_ref, g_ref, rt_ref, xnb_ref, sel_ref, wt_ref):
    xn = _rms(x_ref[...], g_ref[...])
    xnb_ref[...] = xn.astype(BF16)
    logits = lax.dot_general(rt_ref[...], xn, (((1,), (1,)), ((), ())),
                             precision=lax.Precision.HIGHEST, preferred_element_type=F32)
    e = lax.broadcasted_iota(jnp.int32, logits.shape, 0)
    m1 = jnp.max(logits, axis=0, keepdims=True)
    i1 = jnp.min(jnp.where(logits == m1, e, N_EXPERTS), axis=0, keepdims=True)
    rest = jnp.where(e == i1, NEG_INF, logits)
    m2 = jnp.max(rest, axis=0, keepdims=True)
    i2 = jnp.min(jnp.where(rest == m2, e, N_EXPERTS), axis=0, keepdims=True)
    ratio = jnp.exp(m2 - m1)
    w1 = 1.0 / (1.0 + ratio)
    w2 = ratio / (1.0 + ratio)
    sel_ref[...] = jnp.where((e == i1) | (e == i2), 1.0, 0.0)
    wt_ref[...] = jnp.where(e == i1, w1, jnp.where(e == i2, w2, 0.0))


def moe_route(x, g, router, tm=1024):
    n, d = x.shape
    tm = min(tm, n)
    ne = router.shape[1]
    return pl.pallas_call(
        _router_kernel,
        grid=(n // tm,),
        in_specs=[pl.BlockSpec((tm, d), lambda i: (i, 0)),
                  pl.BlockSpec((1, d), lambda i: (0, 0)),
                  pl.BlockSpec((ne, d), lambda i: (0, 0))],
        out_specs=[pl.BlockSpec((tm, d), lambda i: (i, 0)),
                   pl.BlockSpec((ne, tm), lambda i: (0, i)),
                   pl.BlockSpec((ne, tm), lambda i: (0, i))],
        out_shape=[jax.ShapeDtypeStruct((n, d), BF16), jax.ShapeDtypeStruct((ne, n), F32),
                   jax.ShapeDtypeStruct((ne, n), F32)],
        compiler_params=_params("parallel"),
        name="moe_router",
    )(x, g.reshape(1, d), router.T)


MOE_ROWS = 32
MOE_STATIC_BLOCKS = (8, 9, 10)


def _moe_kernel(x_ref, xnb_ref, sel_ref, wt_ref, wg_ref, wu_ref, wd_ref, o_ref,
                rank_ref, xg_ref, acc_ref, nblk_ref):
    e = pl.program_id(1)
    f = pl.program_id(2)
    nf = pl.num_programs(2)
    tm = x_ref.shape[0]

    @pl.when((e == 0) & (f == 0))
    def _():
        o_ref[...] = x_ref[...]
        before = (lax.broadcasted_iota(jnp.int32, (tm, tm), 0) < lax.broadcasted_iota(jnp.int32, (tm, tm), 1))
        rank_ref[...] = jnp.dot(sel_ref[...].astype(BF16), jnp.where(before, 1.0, 0.0).astype(BF16),
                                preferred_element_type=F32)

    sel_e = sel_ref[pl.ds(e, 1), :]
    rank_e = rank_ref[pl.ds(e, 1), :]
    wt_e = wt_ref[pl.ds(e, 1), :]

    @pl.when(f == 0)
    def _():
        count = jnp.sum(sel_e).astype(jnp.int32)
        nblk_ref[0] = (count + MOE_ROWS - 1) // MOE_ROWS

    nblk = nblk_ref[0]

    def process(rows):
        n_rows = rows.stop - rows.start if isinstance(rows, slice) else rows.size
        first = rows.start

        def one_hot():
            slot = (first + lax.broadcasted_iota(jnp.int32, (n_rows, tm), 0)).astype(F32)
            return jnp.where((rank_e == slot) & (sel_e > 0.0), 1.0, 0.0)

        @pl.when(f == 0)
        def _():
            xg_ref[rows, :] = jnp.dot(one_hot().astype(BF16), xnb_ref[...],
                                      preferred_element_type=F32).astype(BF16)
            acc_ref[rows, :] = jnp.zeros((n_rows, acc_ref.shape[1]), F32)

        xg = xg_ref[rows, :]
        h = (_silu(jnp.dot(xg, wg_ref[0], preferred_element_type=F32))
             * jnp.dot(xg, wu_ref[0], preferred_element_type=F32))
        acc_ref[rows, :] += _bdot(h, wd_ref[0])

        @pl.when(f == nf - 1)
        def _():
            hot = one_hot()
            row_w = jnp.sum(hot * wt_e, axis=1, keepdims=True)
            yw = (acc_ref[rows, :] * row_w).astype(BF16)
            o_ref[...] += lax.dot_general(hot.astype(BF16), yw, (((0,), (0,)), ((), ())),
                                          preferred_element_type=F32)

    for n_static in MOE_STATIC_BLOCKS:
        lo = 0 if n_static == MOE_STATIC_BLOCKS[0] else n_static
        hi = n_static if n_static != MOE_STATIC_BLOCKS[-1] else tm // MOE_ROWS
        pl.when((nblk >= lo) & (nblk <= hi))(functools.partial(process, slice(0, n_static * MOE_ROWS)))

    def tail(b, carry):
        process(pl.ds(pl.multiple_of(b * MOE_ROWS, MOE_ROWS), MOE_ROWS))
        return carry

    lax.fori_loop(MOE_STATIC_BLOCKS[-1], nblk, tail, 0)


def moe_ffn(x, xnb, sel, wt, w_gate, w_up, w_down, tm=1024, tf=896):
    n, d = x.shape
    ne, _, ff = w_gate.shape
    tm = min(tm, n)
    return pl.pallas_call(
        _moe_kernel,
        grid=(n // tm, ne, ff // tf),
        in_specs=[pl.BlockSpec((tm, d), lambda i, e, f: (i, 0)),
                  pl.BlockSpec((tm, d), lambda i, e, f: (i, 0)),
                  pl.BlockSpec((ne, tm), lambda i, e, f: (0, i)),
                  pl.BlockSpec((ne, tm), lambda i, e, f: (0, i)),
                  pl.BlockSpec((1, d, tf), lambda i, e, f: (e, 0, f)),
                  pl.BlockSpec((1, d, tf), lambda i, e, f: (e, 0, f)),
                  pl.BlockSpec((1, tf, d), lambda i, e, f: (e, f, 0))],
        out_specs=pl.BlockSpec((tm, d), lambda i, e, f: (i, 0)),
        out_shape=jax.ShapeDtypeStruct((n, d), F32),
        scratch_shapes=[pltpu.VMEM((ne, tm), F32), pltpu.VMEM((tm, d), BF16), pltpu.VMEM((tm, d), F32),
                        pltpu.SMEM((1,), jnp.int32)],
        compiler_params=_params("parallel", "arbitrary", "arbitrary"),
        name="moe_ffn",
    )(x, xnb, sel, wt, w_gate, w_up, w_down)


def _rms_kernel(x_ref, g_ref, o_ref):
    o_ref[...] = _rms(x_ref[...], g_ref[...])


def rms_norm(x, g, tm=1024):
    n, d = x.shape
    tm = min(tm, n)
    return pl.pallas_call(
        _rms_kernel,
        grid=(n // tm,),
        in_specs=[pl.BlockSpec((tm, d), lambda i: (i, 0)), pl.BlockSpec((1, d), lambda i: (0, 0))],
        out_specs=pl.BlockSpec((tm, d), lambda i: (i, 0)),
        out_shape=jax.ShapeDtypeStruct((n, d), F32),
        compiler_params=_params("parallel"),
        name="final_rms_norm",
    )(x, g.reshape(1, d))


def _rope_angles(pos, n_freq, theta):
    inv_freq = theta ** (-jnp.arange(n_freq, dtype=F32) / n_freq)
    return pos.astype(F32)[:, None] * inv_freq[None, :]


def kernel(x, norm_mix_g, w_in, gate_b, w_branch, w_out, rwkv_mu_rkv, rwkv_mu_x, rwkv_w0, rwkv_w1, rwkv_w2,
           rwkv_a0, rwkv_a1, rwkv_a2, rwkv_g1, rwkv_g2, rwkv_k_k, rwkv_k_a, rwkv_r_k, rwkv_ln_w, rwkv_ln_b,
           s5_a_re, s5_a_im, s5_log_dt, s5_b_re, s5_b_im, s5_c_re, s5_c_im, s5_d, s5_glu_w, s5_glu_b,
           gqa_q_norm, gqa_k_norm, norm_ffn_g, dense_w_gate, dense_w_up, dense_w_down,
           moe_router, moe_w_gate, moe_w_up, moe_w_down, final_norm_g):
    batch, seq, d = x.shape
    depth = w_in.shape[0]
    n = batch * seq
    t = jnp.arange(seq, dtype=jnp.int32)
    rope_tabs = _rotary_tables(_rope_angles(t, ROPE_DIMS // 2, ROPE_THETA), N_HEADS)
    ang_axial = jnp.concatenate([_rope_angles(t // GRID_W, HEAD_DIM // 4, AXIAL_THETA),
                                 _rope_angles(t % GRID_W, HEAD_DIM // 4, AXIAL_THETA)], axis=-1)
    axial_tabs = _rotary_tables(ang_axial, N_HEADS)
    x = x.reshape(n, d)
    for l in range(depth):
        w_small = w_in[l, :, :OFF_GATES].astype(BF16)
        w_gates = w_in[l, :, OFF_GATES:].astype(BF16)
        xn, xnb, z = rms_in_proj(x, norm_mix_g[l], w_small)
        zg = matmul_bf16(xnb, w_gates, tm=1024, tn=1024, out_dtype=BF16)
        qa, ka, va, qd, kd, vd = qkv_prep(z, batch, seq, rope_tabs, axial_tabs, gqa_q_norm[l], gqa_k_norm[l])
        ya = dilated_attention(qa, ka, va)
        yd = gqa_attention(qd, kd, vd)
        lw1 = jnp.concatenate([rwkv_w1[l], rwkv_a1[l]], axis=-1).astype(BF16)
        zeros = jnp.zeros_like(rwkv_w2[l])
        lw2 = jnp.concatenate([jnp.concatenate([rwkv_w2[l], zeros], axis=-1),
                               jnp.concatenate([zeros, rwkv_a2[l]], axis=-1)], axis=1).astype(BF16)
        w0a0 = jnp.concatenate([rwkv_w0[l], rwkv_a0[l]], axis=-1)
        fw, bw, bonus, rgate = rwkv_prep(xn, z, seq, rwkv_mu_x[l], rwkv_mu_rkv[l].reshape(2, IN_B), lw1, lw2, w0a0,
                                         rwkv_g1[l].astype(BF16), rwkv_g2[l].astype(BF16),
                                         rwkv_k_k[l], rwkv_k_a[l], rwkv_r_k[l])
        wkv_f, wkv_b = wkv_chunked(fw, bw, batch, seq)
        discs = [_s5_discretize(s5_a_re[l, dr], s5_a_im[l, dr], s5_log_dt[l, dr], s5_b_re[l], s5_b_im[l],
                                s5_c_re[l, dr], s5_c_im[l, dr]) for dr in range(2)]
        s5_out = s5_bidirectional(z, batch, seq, discs)
        x = merge_branches(x, ya, yd, wkv_f, wkv_b, bonus, rgate, s5_out[0], s5_out[1], z, zg, gate_b[l],
                           w_branch[l].astype(BF16), w_out[l].astype(BF16), rwkv_ln_w[l], rwkv_ln_b[l],
                           s5_d[l], s5_glu_w[l].astype(BF16), s5_glu_b[l], batch, seq)
        i = l // 2
        if l % 2 == 0:
            x = dense_ffn(x, norm_ffn_g[l], dense_w_gate[i].astype(BF16), dense_w_up[i].astype(BF16),
                          dense_w_down[i].astype(BF16))
        else:
            xnb_f, sel, wt = moe_route(x, norm_ffn_g[l], moe_router[i])
            x = moe_ffn(x, xnb_f, sel, wt, moe_w_gate[i].astype(BF16), moe_w_up[i].astype(BF16),
                        moe_w_down[i].astype(BF16))
    return rms_norm(x, final_norm_g).reshape(batch, seq, d)
```

```python
import functools
import math

import jax
import jax.numpy as jnp
from jax import lax
from jax.experimental import pallas as pl
from jax.experimental.pallas import tpu as pltpu

F32 = jnp.float32
BF16 = jnp.bfloat16

D_MODEL = 1024
HEAD_DIM = 64
BRANCH_WIDTH = 256
N_BRANCHES = 4
N_HEADS = BRANCH_WIDTH // HEAD_DIM
DILATED_PATTERNS = ((128, 1), (512, 4), (2048, 16))
ROPE_THETA = 500000.0
ROPE_DIMS = HEAD_DIM // 4
RWKV_GN_EPS = 64e-5
S5_GROUP_CH = 16
S5_GROUPS = BRANCH_WIDTH // S5_GROUP_CH
S5_STATE = 64
GQA_KV_HEADS = 2
AXIAL_THETA = 10000.0
GRID_W = 64
N_EXPERTS = 8
TOP_K = 2
NORM_EPS = 1e-6
NEG_INF = -1e30

IN_A = 3 * BRANCH_WIDTH
IN_B = 3 * BRANCH_WIDTH
IN_C = BRANCH_WIDTH
IN_DQ = BRANCH_WIDTH
IN_DKV = GQA_KV_HEADS * HEAD_DIM
IN_GATES = N_BRANCHES * D_MODEL
OFF_B = IN_A
OFF_C = OFF_B + IN_B
OFF_DQ = OFF_C + IN_C
OFF_DKV = OFF_DQ + IN_DQ
OFF_GATES = OFF_DKV + 2 * IN_DKV
IN_TOTAL = OFF_GATES + IN_GATES

VMEM_LIMIT_BYTES = 56 * 1024 * 1024


def _params(*semantics):
    return pltpu.CompilerParams(dimension_semantics=semantics, vmem_limit_bytes=VMEM_LIMIT_BYTES)


def _bdot(a, b):
    return jnp.dot(a.astype(BF16), b.astype(BF16), preferred_element_type=F32)


def _bdot_nt(a, b):
    return lax.dot_general(a.astype(BF16), b.astype(BF16), (((1,), (1,)), ((), ())),
                           preferred_element_type=F32)


def _rms(x, g):
    return x * lax.rsqrt(jnp.mean(x * x, axis=-1, keepdims=True) + NORM_EPS) * g


def _rms_in_proj_kernel(x_ref, g_ref, w_ref, xn_ref, xnb_ref, z_ref):
    @pl.when(pl.program_id(1) == 0)
    def _():
        y = _rms(x_ref[...], g_ref[...])
        xn_ref[...] = y
        xnb_ref[...] = y.astype(BF16)

    z_ref[...] = jnp.dot(xnb_ref[...], w_ref[...], preferred_element_type=F32)


def rms_in_proj(x, g, w_bf16, tm=1024, tn=768):
    n, d = x.shape
    nout = w_bf16.shape[1]
    tm = min(tm, n)
    return pl.pallas_call(
        _rms_in_proj_kernel,
        grid=(n // tm, nout // tn),
        in_specs=[pl.BlockSpec((tm, d), lambda i, j: (i, 0)),
                  pl.BlockSpec((1, d), lambda i, j: (0, 0)),
                  pl.BlockSpec((d, tn), lambda i, j: (0, j))],
        out_specs=[pl.BlockSpec((tm, d), lambda i, j: (i, 0)),
                   pl.BlockSpec((tm, d), lambda i, j: (i, 0)),
                   pl.BlockSpec((tm, tn), lambda i, j: (i, j))],
        out_shape=[jax.ShapeDtypeStruct((n, d), F32), jax.ShapeDtypeStruct((n, d), BF16),
                   jax.ShapeDtypeStruct((n, nout), F32)],
        compiler_params=_params("parallel", "arbitrary"),
        name="rms_in_proj",
    )(x, g.reshape(1, d), w_bf16)


def _rotary_tables(pos_angles, n_heads):
    s, n = pos_angles.shape
    pad = HEAD_DIM - 2 * n
    cos = jnp.concatenate([jnp.cos(pos_angles), jnp.cos(pos_angles), jnp.ones((s, pad), F32)], axis=-1)
    zeros_n = jnp.zeros((s, n), F32)
    zeros_p = jnp.zeros((s, pad), F32)
    sin_lo = jnp.concatenate([-jnp.sin(pos_angles), zeros_n, zeros_p], axis=-1)
    sin_hi = jnp.concatenate([zeros_n, jnp.sin(pos_angles), zeros_p], axis=-1)
    return tuple(jnp.tile(t, (1, n_heads)) for t in (cos, sin_lo, sin_hi))


def _rotate(x, cos, sin_lo, sin_hi, n):
    width = x.shape[-1]
    from_above = pltpu.roll(x, width - n, 1)
    from_below = pltpu.roll(x, n, 1)
    return x * cos + from_above * sin_lo + from_below * sin_hi


def _head_sum(x, n_heads):
    lane = lax.broadcasted_iota(jnp.int32, x.shape, 1)
    out = jnp.zeros_like(x)
    for h in range(n_heads):
        in_head = (lane >= h * HEAD_DIM) & (lane < (h + 1) * HEAD_DIM)
        s = jnp.sum(jnp.where(in_head, x, 0.0), axis=-1, keepdims=True)
        out = jnp.where(in_head, s, out)
    return out


def _head_rms(x, g, n_heads):
    ms = _head_sum(x * x, n_heads) * (1.0 / HEAD_DIM)
    return x * lax.rsqrt(ms + NORM_EPS) * g


Q_SCALE = HEAD_DIM ** -0.5 * math.log2(math.e)


def _qkv_prep_kernel(za_ref, zq_ref, zkv_ref, rc_ref, rl_ref, rh_ref, ac_ref, al_ref, ah_ref,
                     qn_ref, kn_ref, qa_ref, ka_ref, va_ref, qd_ref, kd_ref, vd_ref):
    w = BRANCH_WIDTH
    n_rope = ROPE_DIMS // 2
    n_ax = HEAD_DIM // 2
    za = za_ref[...]
    rc, rl, rh = rc_ref[...], rl_ref[...], rh_ref[...]
    qa = _rotate(za[:, :w], rc, rl, rh, n_rope) * Q_SCALE
    ka = _rotate(za[:, w:2 * w], rc, rl, rh, n_rope)
    va = za[:, 2 * w:]
    ac, al, ah = ac_ref[...], al_ref[...], ah_ref[...]
    qd = _rotate(_head_rms(zq_ref[...], qn_ref[...], N_HEADS), ac, al, ah, n_ax) * Q_SCALE
    zkv = zkv_ref[...]
    kw = GQA_KV_HEADS * HEAD_DIM
    kd = _rotate(_head_rms(zkv[:, :kw], kn_ref[...], GQA_KV_HEADS), ac[:, :kw], al[:, :kw], ah[:, :kw], n_ax)
    vd = zkv[:, kw:]
    for h in range(N_HEADS):
        sl = slice(h * HEAD_DIM, (h + 1) * HEAD_DIM)
        qa_ref[0, h] = qa[:, sl].astype(BF16)
        ka_ref[0, h] = ka[:, sl].astype(BF16)
        va_ref[0, h] = va[:, sl].astype(BF16)
        qd_ref[0, h] = qd[:, sl].astype(BF16)
    for h in range(GQA_KV_HEADS):
        sl = slice(h * HEAD_DIM, (h + 1) * HEAD_DIM)
        kd_ref[0, h] = kd[:, sl].astype(BF16)
        vd_ref[0, h] = vd[:, sl].astype(BF16)


def qkv_prep(z, batch, seq, rope_tabs, axial_tabs, q_norm, k_norm, tm=512):
    tm = min(tm, seq)
    nt = seq // tm
    w = BRANCH_WIDTH
    row = lambda b, i: b * nt + i
    tab_spec = pl.BlockSpec((tm, w), lambda b, i: (i, 0))
    head_out = lambda nh: pl.BlockSpec((1, nh, tm, HEAD_DIM), lambda b, i: (b, 0, i, 0))
    head_shape = lambda nh: jax.ShapeDtypeStruct((batch, nh, seq, HEAD_DIM), BF16)
    return pl.pallas_call(
        _qkv_prep_kernel,
        grid=(batch, nt),
        in_specs=[pl.BlockSpec((tm, IN_A), lambda b, i: (row(b, i), 0)),
                  pl.BlockSpec((tm, w), lambda b, i: (row(b, i), OFF_DQ // w)),
                  pl.BlockSpec((tm, w), lambda b, i: (row(b, i), OFF_DKV // w)),
                  tab_spec, tab_spec, tab_spec, tab_spec, tab_spec, tab_spec,
                  pl.BlockSpec((1, w), lambda b, i: (0, 0)),
                  pl.BlockSpec((1, GQA_KV_HEADS * HEAD_DIM), lambda b, i: (0, 0))],
        out_specs=[head_out(N_HEADS), head_out(N_HEADS), head_out(N_HEADS),
                   head_out(N_HEADS), head_out(GQA_KV_HEADS), head_out(GQA_KV_HEADS)],
        out_shape=[head_shape(N_HEADS), head_shape(N_HEADS), head_shape(N_HEADS),
                   head_shape(N_HEADS), head_shape(GQA_KV_HEADS), head_shape(GQA_KV_HEADS)],
        compiler_params=_params("parallel", "parallel"),
        name="qkv_prep",
    )(z, z, z, *rope_tabs, *axial_tabs,
      jnp.tile(q_norm.reshape(1, HEAD_DIM), (1, N_HEADS)),
      jnp.tile(k_norm.reshape(1, HEAD_DIM), (1, GQA_KV_HEADS)))


A_TQ = 1024
A_SUB = 256
A_RADIUS = 64
DILATED_GROUPS = ((1, ((128, 1), (512, 4))), (16, ((128, 1),)))
assert sorted(w * g for g, ps in DILATED_GROUPS for w, _ in ps) == sorted(w for w, _ in DILATED_PATTERNS)
assert all(w // (2 * d) == A_RADIUS for w, d in DILATED_PATTERNS)


def _window_geometry(patterns, sub):
    out = []
    for window, dil in patterns:
        halo = -(-(window // 2) // 128) * 128
        out.append((dil, -halo, sub + 2 * halo))
    return out


def _window_bias(patterns, sub):
    biases = []
    for dil, first, width in _window_geometry(patterns, sub):
        qi = jnp.arange(sub, dtype=jnp.int32)[:, None]
        kj = jnp.arange(width, dtype=jnp.int32)[None, :] + first
        delta = kj - qi
        ok = (jnp.abs(delta) <= A_RADIUS * dil) & ((delta & (dil - 1)) == 0)
        biases.append(jnp.where(ok, 0.0, NEG_INF).astype(F32))
    return biases


def _window_attn_kernel(q_ref, kp_ref, kc_ref, kn_ref, vp_ref, vc_ref, vn_ref, *rest, seq, tq, sub, windows):
    bias_refs = rest[:len(windows)]
    o_ref, k3_ref, v3_ref = rest[len(windows):]
    i = pl.program_id(2)
    k3_ref[0:tq] = kp_ref[0, 0]
    k3_ref[tq:2 * tq] = kc_ref[0, 0]
    k3_ref[2 * tq:3 * tq] = kn_ref[0, 0]
    v3_ref[0:tq] = vp_ref[0, 0]
    v3_ref[tq:2 * tq] = vc_ref[0, 0]
    v3_ref[2 * tq:3 * tq] = vn_ref[0, 0]
    for u in range(tq // sub):
        q = q_ref[0, 0, u * sub:(u + 1) * sub, :]
        scores = []
        for (dil, first, width), b_ref in zip(windows, bias_refs):
            start = tq + u * sub + first
            s = _bdot_nt(q, k3_ref[start:start + width, :]) + b_ref[...]
            kpos = (i - 1) * tq + start + lax.broadcasted_iota(jnp.int32, (1, width), 1)
            s = jnp.where((kpos >= 0) & (kpos < seq), s, NEG_INF)
            scores.append((s, start, width))
        m = functools.reduce(jnp.maximum, [jnp.max(s, axis=-1, keepdims=True) for s, _, _ in scores])
        l = jnp.zeros_like(m)
        acc = jnp.zeros((sub, HEAD_DIM), F32)
        for s, start, width in scores:
            p = jnp.exp2(s - m)
            l = l + jnp.sum(p, axis=-1, keepdims=True)
            acc = acc + _bdot(p, v3_ref[start:start + width, :])
        log_den = jnp.broadcast_to(m + jnp.log2(l), (sub, HEAD_DIM))
        o_ref[0, 0, u * sub:(u + 1) * sub, :] = jnp.concatenate([acc / l, log_den], axis=1)


def window_attention(q, k, v, patterns):
    batch, nh, seq, hd = q.shape
    tq = min(A_TQ, seq)
    sub = min(A_SUB, tq)
    windows = _window_geometry(patterns, sub)
    assert all(-first <= tq for _, first, _ in windows), "the key halo must fit in one neighbouring tile"
    nt = seq // tq
    cur = pl.BlockSpec((1, 1, tq, hd), lambda b, h, i: (b, h, i, 0))
    prev = pl.BlockSpec((1, 1, tq, hd), lambda b, h, i: (b, h, jnp.maximum(i - 1, 0), 0))
    nxt = pl.BlockSpec((1, 1, tq, hd), lambda b, h, i: (b, h, jnp.minimum(i + 1, nt - 1), 0))
    biases = _window_bias(patterns, sub)
    bias_specs = [pl.BlockSpec(b.shape, lambda b_, h, i: (0, 0)) for b in biases]
    return pl.pallas_call(
        functools.partial(_window_attn_kernel, seq=seq, tq=tq, sub=sub, windows=windows),
        grid=(batch, nh, nt),
        in_specs=[cur, prev, cur, nxt, prev, cur, nxt] + bias_specs,
        out_specs=pl.BlockSpec((1, 1, tq, 2 * hd), lambda b, h, i: (b, h, i, 0)),
        out_shape=jax.ShapeDtypeStruct((batch, nh, seq, 2 * hd), F32),
        scratch_shapes=[pltpu.VMEM((3 * tq, hd), BF16), pltpu.VMEM((3 * tq, hd), BF16)],
        compiler_params=_params("parallel", "parallel", "parallel"),
        name="window_attention_x%d" % len(patterns),
    )(q, k, k, k, v, v, v, *biases)


def _to_residues(a, stride):
    b, h, s, w = a.shape
    return a.reshape(b, h, s // stride, stride, w).transpose(0, 3, 1, 2, 4).reshape(b * stride, h, s // stride, w)


def _from_residues(a, stride):
    bs, h, l, w = a.shape
    return a.reshape(bs // stride, stride, h, l, w).transpose(0, 2, 3, 1, 4).reshape(bs // stride, h, l * stride, w)


def dilated_attention(qa, ka, va):
    parts = []
    for stride, patterns in DILATED_GROUPS:
        if stride == 1:
            parts.append(window_attention(qa, ka, va, patterns))
        else:
            part = window_attention(*(_to_residues(a, stride) for a in (qa, ka, va)), patterns)
            parts.append(_from_residues(part, stride))
    return parts


GQA_SUB = 128
GQA_LOOKAHEAD = 4


def _gqa_kernel(q_ref, k_ref, v_ref, o_ref, *scratch, rep, tq):
    j = pl.program_id(3)
    n_sub = tq // GQA_SUB
    blocks = [(r, u) for r in range(rep) for u in range(n_sub)]
    m_refs, l_refs, acc_refs = (scratch[i * len(blocks):(i + 1) * len(blocks)] for i in range(3))

    @pl.when(j == 0)
    def _():
        for m_ref, l_ref, acc_ref in zip(m_refs, l_refs, acc_refs):
            m_ref[...] = jnp.full(m_ref.shape, NEG_INF, F32)
            l_ref[...] = jnp.zeros(l_ref.shape, F32)
            acc_ref[...] = jnp.zeros(acc_ref.shape, F32)

    k = k_ref[0, 0]
    v = v_ref[0, 0]
    def score(block):
        r, u = block
        return _bdot_nt(k, q_ref[0, r, u * GQA_SUB:(u + 1) * GQA_SUB, :])

    scores = [score(b) for b in blocks[:GQA_LOOKAHEAD]]
    for i, (m_ref, l_ref, acc_ref) in enumerate(zip(m_refs, l_refs, acc_refs)):
        s = scores[i]
        if i + GQA_LOOKAHEAD < len(blocks):
            scores.append(score(blocks[i + GQA_LOOKAHEAD]))
        m_prev = m_ref[...]
        m_new = jnp.maximum(m_prev, jnp.max(s, axis=0, keepdims=True))
        alpha = jnp.exp2(m_prev - m_new)
        p = jnp.exp2(s - m_new)
        l_ref[...] = alpha * l_ref[...] + jnp.sum(p, axis=0, keepdims=True)
        pv = lax.dot_general(v, p.astype(BF16), (((0,), (0,)), ((), ())), preferred_element_type=F32)
        acc_ref[...] = alpha * acc_ref[...] + pv
        m_ref[...] = m_new

    @pl.when(j == pl.num_programs(3) - 1)
    def _():
        for (r, u), l_ref, acc_ref in zip(blocks, l_refs, acc_refs):
            o_ref[0, r, :, u * GQA_SUB:(u + 1) * GQA_SUB] = acc_ref[...] / l_ref[...]


def gqa_attention(qd, kd, vd, tq=1024, tk=2048):
    batch, nh, seq, hd = qd.shape
    ng = kd.shape[1]
    rep = nh // ng
    tq = min(tq, seq)
    tk = min(tk, seq)
    n_blocks = rep * (tq // GQA_SUB)
    return pl.pallas_call(
        functools.partial(_gqa_kernel, rep=rep, tq=tq),
        grid=(batch, ng, seq // tq, seq // tk),
        in_specs=[pl.BlockSpec((1, rep, tq, hd), lambda b, g, i, j: (b, g, i, 0)),
                  pl.BlockSpec((1, 1, tk, hd), lambda b, g, i, j: (b, g, j, 0)),
                  pl.BlockSpec((1, 1, tk, hd), lambda b, g, i, j: (b, g, j, 0))],
        out_specs=pl.BlockSpec((1, rep, hd, tq), lambda b, g, i, j: (b, g, 0, i)),
        out_shape=jax.ShapeDtypeStruct((batch, nh, hd, seq), F32),
        scratch_shapes=([pltpu.VMEM((1, GQA_SUB), F32)] * (2 * n_blocks)
                        + [pltpu.VMEM((hd, GQA_SUB), F32)] * n_blocks),
        compiler_params=_params("parallel", "parallel", "parallel", "arbitrary"),
        name="gqa_attention",
    )(qd, kd, vd)


def _matmul_kernel(a_ref, w_ref, o_ref):
    o_ref[...] = jnp.dot(a_ref[...], w_ref[...], preferred_element_type=F32).astype(o_ref.dtype)


def matmul_bf16(a, w, tm, tn, out_dtype=F32):
    n, k = a.shape
    m = w.shape[1]
    tm = min(tm, n)
    return pl.pallas_call(
        _matmul_kernel,
        grid=(n // tm, m // tn),
        in_specs=[pl.BlockSpec((tm, k), lambda i, j: (i, 0)),
                  pl.BlockSpec((k, tn), lambda i, j: (0, j))],
        out_specs=pl.BlockSpec((tm, tn), lambda i, j: (i, j)),
        out_shape=jax.ShapeDtypeStruct((n, m), out_dtype),
        compiler_params=_params("parallel", "arbitrary"),
        name="matmul_bf16",
    )(a, w)


def _sigmoid(x):
    return 1.0 / (1.0 + jnp.exp(-x))


def _softplus(x):
    return jnp.maximum(x, 0.0) + jnp.log(1.0 + jnp.exp(-jnp.abs(x)))


def _shift_rows(x, edge_row, down):
    rows = x.shape[0]
    ridx = lax.broadcasted_iota(jnp.int32, x.shape, 0)
    if down:
        return jnp.where(ridx == 0, edge_row, pltpu.roll(x, 1, 0))
    return jnp.where(ridx == rows - 1, edge_row, pltpu.roll(x, rows - 1, 0))


WKV_FIELDS = 6


def _rwkv_prep_kernel(xn_ref, xp_ref, xq_ref, zb_ref, zp_ref, zq_ref, mux_ref, murkv_ref, lw1_ref, lw2_ref,
                      w0a0_ref, g1_ref, g2_ref, kk_ref, ka_ref, rk_ref,
                      fw_ref, bw_ref, bonus_ref, gate_ref, *, tiles_per_seq):
    w = BRANCH_WIDTH
    i = pl.program_id(0)
    first = (i % tiles_per_seq) == 0
    last = (i % tiles_per_seq) == tiles_per_seq - 1
    xn = xn_ref[...]
    x_shift = (_shift_rows(xn, jnp.where(first, 0.0, xp_ref[7:8, :]), True),
               _shift_rows(xn, jnp.where(last, 0.0, xq_ref[0:1, :]), False))
    zb = zb_ref[...]
    z_prev = _shift_rows(zb, jnp.where(first, 0.0, zp_ref[7:8, :]), True)
    z_next = _shift_rows(zb, jnp.where(last, 0.0, zq_ref[0:1, :]), False)
    mu = murkv_ref[...]
    rkv = zb + mu[0:1] * (z_prev - zb) + mu[1:2] * (z_next - zb)
    r, k, v = rkv[:, :w], rkv[:, w:2 * w], rkv[:, 2 * w:]
    kap = k * kk_ref[...]
    kap = kap * lax.rsqrt(_head_sum(kap * kap, N_HEADS) + 1e-12)
    gate_ref[...] = _bdot(_sigmoid(_bdot(xn, g1_ref[...])), g2_ref[...])
    bonus = jnp.zeros_like(v)
    lora_lane = lax.broadcasted_iota(jnp.int32, (xn.shape[0], lw1_ref.shape[-1]), 1)
    for d, out_ref in enumerate((fw_ref, bw_ref)):
        xd = xn + mux_ref[d:d + 1, :] * (x_shift[d] - xn)
        h = _bdot(xd, lw1_ref[d])
        h = jnp.where(lora_lane < lw1_ref.shape[-1] // 2, jnp.tanh(h), h)
        h = _bdot(h, lw2_ref[d]) + w0a0_ref[d:d + 1, :]
        w_log = -_softplus(-h[:, :w]) - 0.5
        log_decay = -jnp.exp(w_log)
        iclr = _sigmoid(h[:, w:])
        k_d = k * (1.0 + (iclr - 1.0) * ka_ref[...])
        bonus = bonus + _head_sum(r * k_d * rk_ref[...], N_HEADS) * v
        for j, field in enumerate((r, log_decay, k_d, v, kap, iclr * kap)):
            out_ref[:, j * w:(j + 1) * w] = field
    bonus_ref[...] = bonus


def rwkv_prep(xn, z, seq, mu_x, mu_rkv, lw1, lw2, w0a0, g1, g2, k_k, k_a, r_k, tm=512):
    n, d = xn.shape
    w = BRANCH_WIDTH
    tm = min(tm, seq)
    halo = 8
    prev_halo = lambda i: (jnp.maximum(i * (tm // halo) - 1, 0), 0)
    next_halo = lambda i: (jnp.minimum((i + 1) * (tm // halo), n // halo - 1), 0)
    full = lambda a: pl.BlockSpec(a.shape, lambda i: (0,) * a.ndim)
    params = (mu_x, mu_rkv, lw1, lw2, w0a0, g1, g2, k_k.reshape(1, w), k_a.reshape(1, w), r_k.reshape(1, w))
    rows = lambda width: pl.BlockSpec((tm, width), lambda i: (i, 0))
    return pl.pallas_call(
        functools.partial(_rwkv_prep_kernel, tiles_per_seq=seq // tm),
        grid=(n // tm,),
        in_specs=[rows(d), pl.BlockSpec((halo, d), prev_halo), pl.BlockSpec((halo, d), next_halo),
                  pl.BlockSpec((tm, IN_B), lambda i: (i, OFF_B // IN_B)),
                  pl.BlockSpec((halo, IN_B), lambda i: (prev_halo(i)[0], OFF_B // IN_B)),
                  pl.BlockSpec((halo, IN_B), lambda i: (next_halo(i)[0], OFF_B // IN_B))]
                 + [full(p) for p in params],
        out_specs=[rows(WKV_FIELDS * w), rows(WKV_FIELDS * w), rows(w), rows(w)],
        out_shape=[jax.ShapeDtypeStruct((n, WKV_FIELDS * w), F32), jax.ShapeDtypeStruct((n, WKV_FIELDS * w), F32),
                   jax.ShapeDtypeStruct((n, w), F32), jax.ShapeDtypeStruct((n, w), F32)],
        compiler_params=_params("parallel"),
        name="rwkv_prep",
    )(xn, xn, xn, z, z, z, *params)


WKV_CHUNK = 64


def _wkv_constants(reverse):
    c, nh = WKV_CHUNK, N_HEADS
    t = jnp.arange(c)
    before = (t[None, :] > t[:, None]) if reverse else (t[None, :] < t[:, None])
    incl = before | (t[None, :] == t[:, None])
    per_head = lambda m: jnp.kron(jnp.eye(nh, dtype=F32), m.astype(F32))
    return incl.astype(F32), per_head(before), per_head(incl)


def _cumulative_log_decay(tri, logws):
    w = logws[0].shape[1]
    cat = jnp.concatenate(logws, axis=1)
    hi = cat.astype(BF16)
    lo = (cat - hi.astype(F32)).astype(BF16)
    tri = tri.astype(BF16)
    g = jnp.dot(tri, hi, preferred_element_type=F32) + jnp.dot(tri, lo, preferred_element_type=F32)
    return [g[:, i * w:(i + 1) * w] for i in range(len(logws))]


def _wkv_chunks(chains, head_rows, s_mask):
    c, w, nh = WKV_CHUNK, BRANCH_WIDTH, N_HEADS
    hc = nh * c
    every = range(len(chains))

    def per_head(a):
        return jnp.concatenate([jnp.where(head_rows[h:h + 1, :] > 0.0, a, 0.0) for h in range(nh)], axis=0)

    def wide(m):
        return m[0:c] + m[c:2 * c] + m[2 * c:3 * c] + m[3 * c:4 * c]

    gram_lhs, gram_rhs, gam, vs, k_ts, b_ts = [], [], [], [], [], []
    for x, g, s_bd, _, _, _ in chains:
        r, logw, k, v, kap, b = (x[:, j * w:(j + 1) * w] for j in range(WKV_FIELDS))
        gam.append(jnp.exp(g))
        g_inv = jnp.exp(-g)
        kap_t, r_t = kap * jnp.exp(g - logw), r * gam[-1]
        k_ts.append(k * g_inv)
        b_ts.append(b * g_inv)
        vs.append(v)
        gram_lhs.append(jnp.concatenate([per_head(kap_t), per_head(r_t)], axis=0))
        gram_rhs.append(jnp.concatenate([per_head(k_ts[-1]), per_head(b_ts[-1]), s_bd], axis=0))
    gram = [_bdot_nt(gram_lhs[i], gram_rhs[i]) for i in every]
    kk = [gram[i][:hc, :hc] * chains[i][3] for i in every]
    n = [gram[i][:hc, hc:2 * hc] * chains[i][3] for i in every]
    rk = [gram[i][hc:, :hc] * chains[i][4] for i in every]
    rb = [gram[i][hc:, hc:2 * hc] * chains[i][4] for i in every]
    from_kap = [wide(gram[i][:hc, 2 * hc:]) for i in every]
    from_r = [wide(gram[i][hc:, 2 * hc:]) for i in every]
    eye = jnp.where(lax.broadcasted_iota(jnp.int32, (hc, hc), 0) == lax.broadcasted_iota(jnp.int32, (hc, hc), 1),
                    1.0, 0.0)
    inv = [eye - n[i] for i in every]
    power = [_bdot(n[i], n[i]) for i in every]
    levels = c.bit_length() - 2
    for level in range(levels):
        inv = [inv[i] + _bdot(inv[i], power[i]) for i in every]
        if level + 1 < levels:
            power = [_bdot(power[i], power[i]) for i in every]
    v_heads = [per_head(v) for v in vs]
    kk_v = [_bdot(wide(kk[i]), v_heads[i]) for i in every]
    u = [_bdot(wide(inv[i]), per_head(from_kap[i] + kk_v[i])) for i in every]
    y = [from_r[i] + _bdot(jnp.concatenate([wide(rk[i]), -wide(rb[i])], axis=1),
                           jnp.concatenate([v_heads[i], per_head(u[i])], axis=0)) for i in every]
    update = [lax.dot_general(jnp.concatenate([vs[i], u[i]], axis=0).astype(BF16),
                              jnp.concatenate([k_ts[i], -b_ts[i]], axis=0).astype(BF16),
                              (((0,), (0,)), ((), ())), preferred_element_type=F32) for i in every]
    s_new = [(chains[i][2] + update[i] * s_mask) * gam[i][chains[i][5]:chains[i][5] + 1, :] for i in every]
    return y, s_new


def _wkv_chunked_kernel(fw_ref, bw_ref, trif_ref, msf_ref, mif_ref, trib_ref, msb_ref, mib_ref, hr_ref, sm_ref,
                        yf_ref, yb_ref, state_ref, *, batch, rows):
    @pl.when(pl.program_id(0) == 0)
    def _():
        state_ref[...] = jnp.zeros(state_ref.shape, F32)

    c = WKV_CHUNK
    n_chunks = rows // c
    head_rows = hr_ref[...]
    s_mask = sm_ref[...]
    tris = (trif_ref[...], trib_ref[...])
    masks = ((msf_ref[...], mif_ref[...]), (msb_ref[...], mib_ref[...]))
    refs, y_refs = (fw_ref, bw_ref), (yf_ref, yb_ref)
    w = BRANCH_WIDTH

    def body(i, carry):
        bases = (pl.multiple_of(i * c, c), pl.multiple_of((n_chunks - 1 - i) * c, c))
        chains, where = [], []
        for d in range(2):
            xs = [refs[d][bi, pl.ds(bases[d], c), :] for bi in range(batch)]
            gs = _cumulative_log_decay(tris[d], [x[:, w:2 * w] for x in xs])
            for bi in range(batch):
                chains.append((xs[bi], gs[bi], state_ref[2 * bi + d], *masks[d], 0 if d == 1 else c - 1))
                where.append((bi, d))
        ys, states = _wkv_chunks(chains, head_rows, s_mask)
        for (bi, d), y, s_new in zip(where, ys, states):
            y_refs[d][bi, pl.ds(bases[d], c), :] = y
            state_ref[2 * bi + d] = s_new
        return carry

    lax.fori_loop(0, n_chunks, body, 0)


def wkv_chunked(fw, bw, batch, seq, rows=256):
    w = BRANCH_WIDTH
    rows = min(rows, seq)
    nb = seq // rows
    head_of_lane = jnp.arange(w) // HEAD_DIM
    head_rows = (jnp.arange(8)[:, None] == head_of_lane[None, :]).astype(F32)
    s_mask = (head_of_lane[:, None] == head_of_lane[None, :]).astype(F32)
    consts = [*_wkv_constants(False), *_wkv_constants(True), head_rows, s_mask]
    in_f = pl.BlockSpec((batch, rows, WKV_FIELDS * w), lambda c: (0, c, 0))
    in_b = pl.BlockSpec((batch, rows, WKV_FIELDS * w), lambda c: (0, nb - 1 - c, 0))
    out_shape = jax.ShapeDtypeStruct((batch, seq, w), F32)
    yf, yb = pl.pallas_call(
        functools.partial(_wkv_chunked_kernel, batch=batch, rows=rows),
        grid=(nb,),
        in_specs=[in_f, in_b] + [pl.BlockSpec(a.shape, lambda c: (0, 0)) for a in consts],
        out_specs=[pl.BlockSpec((batch, rows, w), lambda c: (0, c, 0)),
                   pl.BlockSpec((batch, rows, w), lambda c: (0, nb - 1 - c, 0))],
        out_shape=[out_shape, out_shape],
        scratch_shapes=[pltpu.VMEM((2 * batch, w, w), F32)],
        compiler_params=_params("arbitrary"),
        name="wkv_chunked",
    )(fw.reshape(batch, seq, -1), bw.reshape(batch, seq, -1), *consts)
    return yf.reshape(batch * seq, w), yb.reshape(batch * seq, w)


S5_SEGMENTS = 8
S5_WIDTH = S5_GROUPS * S5_STATE


def _s5_discretize(a_re, a_im, log_dt, b_re, b_im, c_re, c_im):
    g, p, c = S5_GROUPS, S5_STATE, S5_GROUP_CH
    dt = jnp.exp(log_dt)[:, None]
    mag = jnp.exp(a_re * dt)
    bar_re, bar_im = mag * jnp.cos(a_im * dt), mag * jnp.sin(a_im * dt)
    den = a_re * a_re + a_im * a_im
    f_re = ((bar_re - 1.0) * a_re + bar_im * a_im) / den
    f_im = (bar_im * a_re - (bar_re - 1.0) * a_im) / den
    bb_re = f_re[..., None] * b_re - f_im[..., None] * b_im
    bb_im = f_re[..., None] * b_im + f_im[..., None] * b_re
    eye_g = jnp.eye(g, dtype=F32)
    w_in = jnp.concatenate(
        [jnp.einsum('gpc,gh->gchp', bb, eye_g).reshape(g * c, g * p) for bb in (bb_re, bb_im)], axis=1)
    w_out = jnp.concatenate(
        [jnp.einsum('gcp,gh->gphc', cc, eye_g).reshape(g * p, g * c) for cc in (c_re, -c_im)], axis=0)
    return bar_re.reshape(1, g * p), bar_im.reshape(1, g * p), w_in.astype(BF16), w_out.astype(BF16), dt, a_re, a_im


def _s5_powers(a_re, a_im, dt, count, reverse):
    j = jnp.arange(1, count + 1, dtype=F32)
    if reverse:
        j = j[::-1]
    e = j[:, None, None] * (a_re * dt)[None]
    th = j[:, None, None] * (a_im * dt)[None]
    mag = jnp.exp(e)
    return (mag * jnp.cos(th)).reshape(count, -1), (mag * jnp.sin(th)).reshape(count, -1)


def _s5_scan_kernel(u_ref, win_ref, lre_ref, lim_ref, pre_ref, pim_ref, wout_ref, y_ref,
                    xre_ref, xim_ref, cre_ref, cim_ref, *, reverse, chunk):
    seg = chunk // S5_SEGMENTS
    nw = S5_WIDTH

    @pl.when(pl.program_id(1) == 0)
    def _():
        cre_ref[...] = jnp.zeros(cre_ref.shape, F32)
        cim_ref[...] = jnp.zeros(cim_ref.shape, F32)

    bu = jnp.dot(u_ref[...].astype(BF16), win_ref[...], preferred_element_type=F32)
    xre_ref[...] = bu[:, :nw]
    xim_ref[...] = bu[:, nw:]
    lre = lre_ref[...]
    lim = lim_ref[...]
    group = lambda s: pl.ds(pl.multiple_of(s * S5_SEGMENTS, S5_SEGMENTS), S5_SEGMENTS)

    def local_step(s, carry):
        xr, xi = carry
        rows = group((seg - 1 - s) if reverse else s)
        nr = lre * xr - lim * xi + xre_ref[rows, :]
        ni = lre * xi + lim * xr + xim_ref[rows, :]
        xre_ref[rows, :] = nr
        xim_ref[rows, :] = ni
        return nr, ni

    zero = jnp.zeros((S5_SEGMENTS, nw), F32)
    fin_re, fin_im = lax.fori_loop(0, seg, local_step, (zero, zero))

    full_seg = 0 if reverse else seg - 1
    pl_re, pl_im = pre_ref[full_seg:full_seg + 1, :], pim_ref[full_seg:full_seg + 1, :]
    ir, ii = cre_ref[...], cim_ref[...]
    init_re, init_im = [None] * S5_SEGMENTS, [None] * S5_SEGMENTS
    for j in (range(S5_SEGMENTS - 1, -1, -1) if reverse else range(S5_SEGMENTS)):
        init_re[j], init_im[j] = ir, ii
        fr, fi = fin_re[j:j + 1, :], fin_im[j:j + 1, :]
        ir, ii = fr + (pl_re * ir - pl_im * ii), fi + (pl_re * ii + pl_im * ir)
    cre_ref[...] = ir
    cim_ref[...] = ii
    init_re = jnp.concatenate(init_re, axis=0)
    init_im = jnp.concatenate(init_im, axis=0)

    def correct(s, carry):
        rows = group(s)
        pr, pi = pre_ref[pl.ds(s, 1), :], pim_ref[pl.ds(s, 1), :]
        xre_ref[rows, :] = xre_ref[rows, :] + (pr * init_re - pi * init_im)
        xim_ref[rows, :] = xim_ref[rows, :] + (pr * init_im + pi * init_re)
        return carry

    lax.fori_loop(0, seg, correct, 0)
    y_ref[...] = (jnp.dot(xre_ref[...].astype(BF16), wout_ref[:nw, :], preferred_element_type=F32)
                  + jnp.dot(xim_ref[...].astype(BF16), wout_ref[nw:, :], preferred_element_type=F32))


def _segment_interleave(a, batch, seq, chunk, inverse=False):
    w = a.shape[-1]
    seg = chunk // S5_SEGMENTS
    shape = (batch, seq // chunk, seg, S5_SEGMENTS, w) if inverse else (batch, seq // chunk, S5_SEGMENTS, seg, w)
    return jnp.swapaxes(a.reshape(shape), 2, 3).reshape(batch * seq, w)


def s5_scan(u_interleaved, batch, seq, disc, reverse, chunk):
    lam_re, lam_im, w_in, w_out, dt, a_re, a_im = disc
    w = BRANCH_WIDTH
    nc = seq // chunk
    seg = chunk // S5_SEGMENTS
    pw_re, pw_im = _s5_powers(a_re, a_im, dt, seg, reverse)
    order = (lambda c: nc - 1 - c) if reverse else (lambda c: c)
    full = lambda a: pl.BlockSpec(a.shape, lambda b, c: (0, 0))
    return pl.pallas_call(
        functools.partial(_s5_scan_kernel, reverse=reverse, chunk=chunk),
        grid=(batch, nc),
        in_specs=[pl.BlockSpec((chunk, w), lambda b, c: (b * nc + order(c), 0)),
                  full(w_in), full(lam_re), full(lam_im), full(pw_re), full(pw_im), full(w_out)],
        out_specs=pl.BlockSpec((chunk, w), lambda b, c: (b * nc + order(c), 0)),
        out_shape=jax.ShapeDtypeStruct((batch * seq, w), F32),
        scratch_shapes=[pltpu.VMEM((chunk, S5_WIDTH), F32), pltpu.VMEM((chunk, S5_WIDTH), F32),
                        pltpu.VMEM((1, S5_WIDTH), F32), pltpu.VMEM((1, S5_WIDTH), F32)],
        compiler_params=_params("parallel", "arbitrary"),
        name="s5_scan_bwd" if reverse else "s5_scan_fwd",
    )(u_interleaved, w_in, lam_re, lam_im, pw_re, pw_im, w_out)


def s5_bidirectional(z, batch, seq, discs, chunk=1024):
    chunk = min(chunk, seq)
    u = _segment_interleave(z[:, OFF_C:OFF_C + BRANCH_WIDTH], batch, seq, chunk)
    return [_segment_interleave(s5_scan(u, batch, seq, disc, reverse=(d == 1), chunk=chunk),
                                batch, seq, chunk, inverse=True) for d, disc in enumerate(discs)]


def _gelu_tanh(y):
    return 0.5 * y * (1.0 + jnp.tanh(math.sqrt(2.0 / math.pi) * (y + 0.044715 * (y * y * y))))


def _merge_kernel(x_ref, *refs, n_parts):
    ya_refs = refs[:n_parts]
    (yd_ref, wf_ref, wb_ref, bonus_ref, rg_ref, sf_ref, sb_ref, u_ref, zg_ref,
     gb_ref, wbr_ref, wout_ref, lnw_ref, lnb_ref, s5d_ref, gluw_ref, glub_ref, o_ref) = refs[n_parts:]
    w = BRANCH_WIDTH
    heads = []
    for h in range(N_HEADS):
        parts = [ref[0, h] for ref in ya_refs]
        log_den = [p[:, HEAD_DIM:HEAD_DIM + 1] for p in parts]
        top = functools.reduce(jnp.maximum, log_den)
        share = [jnp.exp2(ld - top) for ld in log_den]
        heads.append(sum(s * p[:, :HEAD_DIM] for s, p in zip(share, parts)) / sum(share))
    ya = jnp.concatenate(heads, axis=1)
    ys = wf_ref[...] + wb_ref[...]
    cen = ys - _head_sum(ys, N_HEADS) * (1.0 / HEAD_DIM)
    var = _head_sum(cen * cen, N_HEADS) * (1.0 / HEAD_DIM)
    yb = (cen * lax.rsqrt(var + RWKV_GN_EPS) * lnw_ref[...] + lnb_ref[...] + bonus_ref[...]) * rg_ref[...]
    yc = sf_ref[...] + sb_ref[...] + s5d_ref[...] * u_ref[...]
    h = _bdot(_gelu_tanh(yc), gluw_ref[...]) + glub_ref[...]
    yc = h[:, :w] * _sigmoid(h[:, w:])
    proj_d = sum(lax.dot_general(yd_ref[0, h].astype(BF16), wbr_ref[3, h * HEAD_DIM:(h + 1) * HEAD_DIM, :],
                                 (((0,), (0,)), ((), ())), preferred_element_type=F32) for h in range(N_HEADS))
    merged = jnp.zeros(o_ref.shape, F32)
    for i, proj in enumerate((_bdot(ya, wbr_ref[0]), _bdot(yb, wbr_ref[1]), _bdot(yc, wbr_ref[2]), proj_d)):
        gate = _sigmoid(zg_ref[:, i * D_MODEL:(i + 1) * D_MODEL] + gb_ref[i:i + 1, :])
        merged = merged + gate * proj
    o_ref[...] = x_ref[...] + _bdot(merged, wout_ref[...])


def merge_branches(x, ya_parts, yd, wkv_f, wkv_b, bonus, rgate, s5_f, s5_b, z, zg, gate_b, w_branch, w_out,
                   ln_w, ln_b, s5_d, glu_w, glu_b, batch, seq, tm=512):
    w = BRANCH_WIDTH
    d = D_MODEL
    tm = min(tm, seq)
    nt = seq // tm
    rows = lambda width, col=0: pl.BlockSpec((tm, width), lambda b, i: (b * nt + i, col))
    heads = pl.BlockSpec((1, N_HEADS, tm, 2 * HEAD_DIM), lambda b, i: (b, 0, i, 0))
    heads_t = pl.BlockSpec((1, N_HEADS, HEAD_DIM, tm), lambda b, i: (b, 0, 0, i))
    full = lambda a: pl.BlockSpec(a.shape, lambda b, i: (0,) * a.ndim)
    params = (gate_b, w_branch, w_out, ln_w.reshape(1, w), ln_b.reshape(1, w), s5_d.reshape(1, w),
              glu_w, glu_b.reshape(1, 2 * w))
    return pl.pallas_call(
        functools.partial(_merge_kernel, n_parts=len(ya_parts)),
        grid=(batch, nt),
        in_specs=[rows(d)] + [heads] * len(ya_parts)
                 + [heads_t, rows(w), rows(w), rows(w), rows(w), rows(w), rows(w),
                    rows(w, OFF_C // w), rows(N_BRANCHES * d)] + [full(p) for p in params],
        out_specs=rows(d),
        out_shape=jax.ShapeDtypeStruct(x.shape, F32),
        compiler_params=_params("parallel", "parallel"),
        name="merge_branches",
    )(x, *ya_parts, yd, wkv_f, wkv_b, bonus, rgate, s5_f, s5_b, z, zg, *params)


def _silu(x):
    return x * _sigmoid(x)


def _swiglu_tile(x, wg_ref, wu_ref, wd_ref):
    h = (_silu(jnp.dot(x, wg_ref[...], preferred_element_type=F32))
         * jnp.dot(x, wu_ref[...], preferred_element_type=F32))
    return _bdot(h, wd_ref[...])


def _dense_ffn_kernel(x_ref, g_ref, wg_ref, wu_ref, wd_ref, o_ref, xn_ref):
    @pl.when(pl.program_id(1) == 0)
    def _():
        x = x_ref[...]
        xn_ref[...] = _rms(x, g_ref[...]).astype(BF16)
        o_ref[...] = x

    o_ref[...] += _swiglu_tile(xn_ref[...], wg_ref, wu_ref, wd_ref)


def dense_ffn(x, g, w_gate, w_up, w_down, tm=1024, tf=1408):
    n, d = x.shape
    ff = w_gate.shape[1]
    tm = min(tm, n)
    return pl.pallas_call(
        _dense_ffn_kernel,
        grid=(n // tm, ff // tf),
        in_specs=[pl.BlockSpec((tm, d), lambda i, f: (i, 0)),
                  pl.BlockSpec((1, d), lambda i, f: (0, 0)),
                  pl.BlockSpec((d, tf), lambda i, f: (0, f)),
                  pl.BlockSpec((d, tf), lambda i, f: (0, f)),
                  pl.BlockSpec((tf, d), lambda i, f: (f, 0))],
        out_specs=pl.BlockSpec((tm, d), lambda i, f: (i, 0)),
        out_shape=jax.ShapeDtypeStruct((n, d), F32),
        scratch_shapes=[pltpu.VMEM((tm, d), BF16)],
        compiler_params=_params("parallel", "arbitrary"),
        name="dense_ffn",
    )(x, g.reshape(1, d), w_gate, w_up, w_down)


def _router_kernel(x_ref, g_ref, rt_ref, xnb_ref, sel_ref, wt_ref):
    xn = _rms(x_ref[...], g_ref[...])
    xnb_ref[...] = xn.astype(BF16)
    logits = lax.dot_general(rt_ref[...], xn, (((1,), (1,)), ((), ())),
                             precision=lax.Precision.HIGHEST, preferred_element_type=F32)
    e = lax.broadcasted_iota(jnp.int32, logits.shape, 0)
    m1 = jnp.max(logits, axis=0, keepdims=True)
    i1 = jnp.min(jnp.where(logits == m1, e, N_EXPERTS), axis=0, keepdims=True)
    rest = jnp.where(e == i1, NEG_INF, logits)
    m2 = jnp.max(rest, axis=0, keepdims=True)
    i2 = jnp.min(jnp.where(rest == m2, e, N_EXPERTS), axis=0, keepdims=True)
    ratio = jnp.exp(m2 - m1)
    w1 = 1.0 / (1.0 + ratio)
    w2 = ratio / (1.0 + ratio)
    sel_ref[...] = jnp.where((e == i1) | (e == i2), 1.0, 0.0)
    wt_ref[...] = jnp.where(e == i1, w1, jnp.where(e == i2, w2, 0.0))


def moe_route(x, g, router, tm=1024):
    n, d = x.shape
    tm = min(tm, n)
    ne = router.shape[1]
    return pl.pallas_call(
        _router_kernel,
        grid=(n // tm,),
        in_specs=[pl.BlockSpec((tm, d), lambda i: (i, 0)),
                  pl.BlockSpec((1, d), lambda i: (0, 0)),
                  pl.BlockSpec((ne, d), lambda i: (0, 0))],
        out_specs=[pl.BlockSpec((tm, d), lambda i: (i, 0)),
                   pl.BlockSpec((ne, tm), lambda i: (0, i)),
                   pl.BlockSpec((ne, tm), lambda i: (0, i))],
        out_shape=[jax.ShapeDtypeStruct((n, d), BF16), jax.ShapeDtypeStruct((ne, n), F32),
                   jax.ShapeDtypeStruct((ne, n), F32)],
        compiler_params=_params("parallel"),
        name="moe_router",
    )(x, g.reshape(1, d), router.T)


MOE_ROWS = 32
MOE_STATIC_BLOCKS = (8, 9, 10)


def _moe_kernel(x_ref, xnb_ref, sel_ref, wt_ref, wg_ref, wu_ref, wd_ref, o_ref,
                rank_ref, xg_ref, acc_ref, nblk_ref):
    e = pl.program_id(1)
    f = pl.program_id(2)
    nf = pl.num_programs(2)
    tm = x_ref.shape[0]

    @pl.when((e == 0) & (f == 0))
    def _():
        o_ref[...] = x_ref[...]
        before = (lax.broadcasted_iota(jnp.int32, (tm, tm), 0) < lax.broadcasted_iota(jnp.int32, (tm, tm), 1))
        rank_ref[...] = jnp.dot(sel_ref[...].astype(BF16), jnp.where(before, 1.0, 0.0).astype(BF16),
                                preferred_element_type=F32)

    sel_e = sel_ref[pl.ds(e, 1), :]
    rank_e = rank_ref[pl.ds(e, 1), :]
    wt_e = wt_ref[pl.ds(e, 1), :]

    @pl.when(f == 0)
    def _():
        count = jnp.sum(sel_e).astype(jnp.int32)
        nblk_ref[0] = (count + MOE_ROWS - 1) // MOE_ROWS

    nblk = nblk_ref[0]

    def process(rows):
        n_rows = rows.stop - rows.start if isinstance(rows, slice) else rows.size
        first = rows.start

        def one_hot():
            slot = (first + lax.broadcasted_iota(jnp.int32, (n_rows, tm), 0)).astype(F32)
            return jnp.where((rank_e == slot) & (sel_e > 0.0), 1.0, 0.0)

        @pl.when(f == 0)
        def _():
            xg_ref[rows, :] = jnp.dot(one_hot().astype(BF16), xnb_ref[...],
                                      preferred_element_type=F32).astype(BF16)
            acc_ref[rows, :] = jnp.zeros((n_rows, acc_ref.shape[1]), F32)

        xg = xg_ref[rows, :]
        acc_ref[rows, :] += _swiglu_tile(xg, wg_ref.at[0], wu_ref.at[0], wd_ref.at[0])

        @pl.when(f == nf - 1)
        def _():
            hot = one_hot()
            row_w = jnp.sum(hot * wt_e, axis=1, keepdims=True)
            yw = (acc_ref[rows, :] * row_w).astype(BF16)
            o_ref[...] += lax.dot_general(hot.astype(BF16), yw, (((0,), (0,)), ((), ())),
                                          preferred_element_type=F32)

    for n_static in MOE_STATIC_BLOCKS:
        lo = 0 if n_static == MOE_STATIC_BLOCKS[0] else n_static
        hi = n_static if n_static != MOE_STATIC_BLOCKS[-1] else tm // MOE_ROWS
        pl.when((nblk >= lo) & (nblk <= hi))(functools.partial(process, slice(0, n_static * MOE_ROWS)))

    def tail(b, carry):
        process(pl.ds(pl.multiple_of(b * MOE_ROWS, MOE_ROWS), MOE_ROWS))
        return carry

    lax.fori_loop(MOE_STATIC_BLOCKS[-1], nblk, tail, 0)


def moe_ffn(x, xnb, sel, wt, w_gate, w_up, w_down, tm=1024, tf=896):
    n, d = x.shape
    ne, _, ff = w_gate.shape
    tm = min(tm, n)
    return pl.pallas_call(
        _moe_kernel,
        grid=(n // tm, ne, ff // tf),
        in_specs=[pl.BlockSpec((tm, d), lambda i, e, f: (i, 0)),
                  pl.BlockSpec((tm, d), lambda i, e, f: (i, 0)),
                  pl.BlockSpec((ne, tm), lambda i, e, f: (0, i)),
                  pl.BlockSpec((ne, tm), lambda i, e, f: (0, i)),
                  pl.BlockSpec((1, d, tf), lambda i, e, f: (e, 0, f)),
                  pl.BlockSpec((1, d, tf), lambda i, e, f: (e, 0, f)),
                  pl.BlockSpec((1, tf, d), lambda i, e, f: (e, f, 0))],
        out_specs=pl.BlockSpec((tm, d), lambda i, e, f: (i, 0)),
        out_shape=jax.ShapeDtypeStruct((n, d), F32),
        scratch_shapes=[pltpu.VMEM((ne, tm), F32), pltpu.VMEM((tm, d), BF16), pltpu.VMEM((tm, d), F32),
                        pltpu.SMEM((1,), jnp.int32)],
        compiler_params=_params("parallel", "arbitrary", "arbitrary"),
        name="moe_ffn",
    )(x, xnb, sel, wt, w_gate, w_up, w_down)


def _rms_kernel(x_ref, g_ref, o_ref):
    o_ref[...] = _rms(x_ref[...], g_ref[...])


def rms_norm(x, g, tm=1024):
    n, d = x.shape
    tm = min(tm, n)
    return pl.pallas_call(
        _rms_kernel,
        grid=(n // tm,),
        in_specs=[pl.BlockSpec((tm, d), lambda i: (i, 0)), pl.BlockSpec((1, d), lambda i: (0, 0))],
        out_specs=pl.BlockSpec((tm, d), lambda i: (i, 0)),
        out_shape=jax.ShapeDtypeStruct((n, d), F32),
        compiler_params=_params("parallel"),
        name="final_rms_norm",
    )(x, g.reshape(1, d))


def _rope_angles(pos, n_freq, theta):
    inv_freq = theta ** (-jnp.arange(n_freq, dtype=F32) / n_freq)
    return pos.astype(F32)[:, None] * inv_freq[None, :]


def kernel(x, norm_mix_g, w_in, gate_b, w_branch, w_out, rwkv_mu_rkv, rwkv_mu_x, rwkv_w0, rwkv_w1, rwkv_w2,
           rwkv_a0, rwkv_a1, rwkv_a2, rwkv_g1, rwkv_g2, rwkv_k_k, rwkv_k_a, rwkv_r_k, rwkv_ln_w, rwkv_ln_b,
           s5_a_re, s5_a_im, s5_log_dt, s5_b_re, s5_b_im, s5_c_re, s5_c_im, s5_d, s5_glu_w, s5_glu_b,
           gqa_q_norm, gqa_k_norm, norm_ffn_g, dense_w_gate, dense_w_up, dense_w_down,
           moe_router, moe_w_gate, moe_w_up, moe_w_down, final_norm_g):
    batch, seq, d = x.shape
    depth = w_in.shape[0]
    n = batch * seq
    t = jnp.arange(seq, dtype=jnp.int32)
    rope_tabs = _rotary_tables(_rope_angles(t, ROPE_DIMS // 2, ROPE_THETA), N_HEADS)
    ang_axial = jnp.concatenate([_rope_angles(t // GRID_W, HEAD_DIM // 4, AXIAL_THETA),
                                 _rope_angles(t % GRID_W, HEAD_DIM // 4, AXIAL_THETA)], axis=-1)
    axial_tabs = _rotary_tables(ang_axial, N_HEADS)
    x = x.reshape(n, d)
    for l in range(depth):
        w_small = w_in[l, :, :OFF_GATES].astype(BF16)
        w_gates = w_in[l, :, OFF_GATES:].astype(BF16)
        xn, xnb, z = rms_in_proj(x, norm_mix_g[l], w_small)
        zg = matmul_bf16(xnb, w_gates, tm=1024, tn=1024, out_dtype=BF16)
        qa, ka, va, qd, kd, vd = qkv_prep(z, batch, seq, rope_tabs, axial_tabs, gqa_q_norm[l], gqa_k_norm[l])
        ya = dilated_attention(qa, ka, va)
        yd = gqa_attention(qd, kd, vd)
        lw1 = jnp.concatenate([rwkv_w1[l], rwkv_a1[l]], axis=-1).astype(BF16)
        zeros = jnp.zeros_like(rwkv_w2[l])
        lw2 = jnp.concatenate([jnp.concatenate([rwkv_w2[l], zeros], axis=-1),
                               jnp.concatenate([zeros, rwkv_a2[l]], axis=-1)], axis=1).astype(BF16)
        w0a0 = jnp.concatenate([rwkv_w0[l], rwkv_a0[l]], axis=-1)
        fw, bw, bonus, rgate = rwkv_prep(xn, z, seq, rwkv_mu_x[l], rwkv_mu_rkv[l].reshape(2, IN_B), lw1, lw2, w0a0,
                                         rwkv_g1[l].astype(BF16), rwkv_g2[l].astype(BF16),
                                         rwkv_k_k[l], rwkv_k_a[l], rwkv_r_k[l])
        wkv_f, wkv_b = wkv_chunked(fw, bw, batch, seq)
        discs = [_s5_discretize(s5_a_re[l, dr], s5_a_im[l, dr], s5_log_dt[l, dr], s5_b_re[l], s5_b_im[l],
                                s5_c_re[l, dr], s5_c_im[l, dr]) for dr in range(2)]
        s5_out = s5_bidirectional(z, batch, seq, discs)
        x = merge_branches(x, ya, yd, wkv_f, wkv_b, bonus, rgate, s5_out[0], s5_out[1], z, zg, gate_b[l],
                           w_branch[l].astype(BF16), w_out[l].astype(BF16), rwkv_ln_w[l], rwkv_ln_b[l],
                           s5_d[l], s5_glu_w[l].astype(BF16), s5_glu_b[l], batch, seq)
        i = l // 2
        if l % 2 == 0:
            x = dense_ffn(x, norm_ffn_g[l], dense_w_gate[i].astype(BF16), dense_w_up[i].astype(BF16),
                          dense_w_down[i].astype(BF16))
        else:
            xnb_f, sel, wt = moe_route(x, norm_ffn_g[l], moe_router[i])
            x = moe_ffn(x, xnb_f, sel, wt, moe_w_gate[i].astype(BF16), moe_w_up[i].astype(BF16),
                        moe_w_down[i].astype(BF16))
    return rms_norm(x, final_norm_g).reshape(batch, seq, d)
```

```python
import functools
import math

import jax
import jax.numpy as jnp
from jax import lax
from jax.experimental import pallas as pl
from jax.experimental.pallas import tpu as pltpu

F32 = jnp.float32
BF16 = jnp.bfloat16

D_MODEL = 1024
HEAD_DIM = 64
BRANCH_WIDTH = 256
N_BRANCHES = 4
N_HEADS = BRANCH_WIDTH // HEAD_DIM
DILATED_PATTERNS = ((128, 1), (512, 4), (2048, 16))
ROPE_THETA = 500000.0
ROPE_DIMS = HEAD_DIM // 4
RWKV_GN_EPS = 64e-5
S5_GROUP_CH = 16
S5_GROUPS = BRANCH_WIDTH // S5_GROUP_CH
S5_STATE = 64
GQA_KV_HEADS = 2
AXIAL_THETA = 10000.0
GRID_W = 64
N_EXPERTS = 8
TOP_K = 2
NORM_EPS = 1e-6
NEG_INF = -1e30

IN_A = 3 * BRANCH_WIDTH
IN_B = 3 * BRANCH_WIDTH
IN_C = BRANCH_WIDTH
IN_DQ = BRANCH_WIDTH
IN_DKV = GQA_KV_HEADS * HEAD_DIM
IN_GATES = N_BRANCHES * D_MODEL
OFF_B = IN_A
OFF_C = OFF_B + IN_B
OFF_DQ = OFF_C + IN_C
OFF_DKV = OFF_DQ + IN_DQ
OFF_GATES = OFF_DKV + 2 * IN_DKV
IN_TOTAL = OFF_GATES + IN_GATES

VMEM_LIMIT_BYTES = 56 * 1024 * 1024


def _params(*semantics):
    return pltpu.CompilerParams(dimension_semantics=semantics, vmem_limit_bytes=VMEM_LIMIT_BYTES)


def _bdot(a, b):
    return jnp.dot(a.astype(BF16), b.astype(BF16), preferred_element_type=F32)


def _bdot_nt(a, b):
    return lax.dot_general(a.astype(BF16), b.astype(BF16), (((1,), (1,)), ((), ())),
                           preferred_element_type=F32)


def _rms(x, g):
    return x * lax.rsqrt(jnp.mean(x * x, axis=-1, keepdims=True) + NORM_EPS) * g


def _rms_in_proj_kernel(x_ref, g_ref, w_ref, xn_ref, xnb_ref, z_ref):
    @pl.when(pl.program_id(1) == 0)
    def _():
        y = _rms(x_ref[...], g_ref[...])
        xn_ref[...] = y
        xnb_ref[...] = y.astype(BF16)

    z_ref[...] = jnp.dot(xnb_ref[...], w_ref[...], preferred_element_type=F32)


def rms_in_proj(x, g, w_bf16, tm=1024, tn=768):
    n, d = x.shape
    nout = w_bf16.shape[1]
    tm = min(tm, n)
    return pl.pallas_call(
        _rms_in_proj_kernel,
        grid=(n // tm, nout // tn),
        in_specs=[pl.BlockSpec((tm, d), lambda i, j: (i, 0)),
                  pl.BlockSpec((1, d), lambda i, j: (0, 0)),
                  pl.BlockSpec((d, tn), lambda i, j: (0, j))],
        out_specs=[pl.BlockSpec((tm, d), lambda i, j: (i, 0)),
                   pl.BlockSpec((tm, d), lambda i, j: (i, 0)),
                   pl.BlockSpec((tm, tn), lambda i, j: (i, j))],
        out_shape=[jax.ShapeDtypeStruct((n, d), F32), jax.ShapeDtypeStruct((n, d), BF16),
                   jax.ShapeDtypeStruct((n, nout), F32)],
        compiler_params=_params("parallel", "arbitrary"),
        name="rms_in_proj",
    )(x, g.reshape(1, d), w_bf16)


def _rotary_tables(pos_angles, n_heads):
    s, n = pos_angles.shape
    pad = HEAD_DIM - 2 * n
    cos = jnp.concatenate([jnp.cos(pos_angles), jnp.cos(pos_angles), jnp.ones((s, pad), F32)], axis=-1)
    zeros_n = jnp.zeros((s, n), F32)
    zeros_p = jnp.zeros((s, pad), F32)
    sin_lo = jnp.concatenate([-jnp.sin(pos_angles), zeros_n, zeros_p], axis=-1)
    sin_hi = jnp.concatenate([zeros_n, jnp.sin(pos_angles), zeros_p], axis=-1)
    return tuple(jnp.tile(t, (1, n_heads)) for t in (cos, sin_lo, sin_hi))


def _rotate(x, cos, sin_lo, sin_hi, n):
    width = x.shape[-1]
    from_above = pltpu.roll(x, width - n, 1)
    from_below = pltpu.roll(x, n, 1)
    return x * cos + from_above * sin_lo + from_below * sin_hi


def _head_sum(x, n_heads):
    lane = lax.broadcasted_iota(jnp.int32, x.shape, 1)
    out = jnp.zeros_like(x)
    for h in range(n_heads):
        in_head = (lane >= h * HEAD_DIM) & (lane < (h + 1) * HEAD_DIM)
        s = jnp.sum(jnp.where(in_head, x, 0.0), axis=-1, keepdims=True)
        out = jnp.where(in_head, s, out)
    return out


def _head_rms(x, g, n_heads):
    ms = _head_sum(x * x, n_heads) * (1.0 / HEAD_DIM)
    return x * lax.rsqrt(ms + NORM_EPS) * g


Q_SCALE = HEAD_DIM ** -0.5 * math.log2(math.e)


def _qkv_prep_kernel(za_ref, zq_ref, zkv_ref, rc_ref, rl_ref, rh_ref, ac_ref, al_ref, ah_ref,
                     qn_ref, kn_ref, qa_ref, ka_ref, va_ref, qd_ref, kd_ref, vd_ref):
    w = BRANCH_WIDTH
    n_rope = ROPE_DIMS // 2
    n_ax = HEAD_DIM // 2
    za = za_ref[...]
    rc, rl, rh = rc_ref[...], rl_ref[...], rh_ref[...]
    qa = _rotate(za[:, :w], rc, rl, rh, n_rope) * Q_SCALE
    ka = _rotate(za[:, w:2 * w], rc, rl, rh, n_rope)
    va = za[:, 2 * w:]
    ac, al, ah = ac_ref[...], al_ref[...], ah_ref[...]
    qd = _rotate(_head_rms(zq_ref[...], qn_ref[...], N_HEADS), ac, al, ah, n_ax) * Q_SCALE
    zkv = zkv_ref[...]
    kw = GQA_KV_HEADS * HEAD_DIM
    kd = _rotate(_head_rms(zkv[:, :kw], kn_ref[...], GQA_KV_HEADS), ac[:, :kw], al[:, :kw], ah[:, :kw], n_ax)
    vd = zkv[:, kw:]
    for h in range(N_HEADS):
        sl = slice(h * HEAD_DIM, (h + 1) * HEAD_DIM)
        qa_ref[0, h] = qa[:, sl].astype(BF16)
        ka_ref[0, h] = ka[:, sl].astype(BF16)
        va_ref[0, h] = va[:, sl].astype(BF16)
        qd_ref[0, h] = qd[:, sl].astype(BF16)
    for h in range(GQA_KV_HEADS):
        sl = slice(h * HEAD_DIM, (h + 1) * HEAD_DIM)
        kd_ref[0, h] = kd[:, sl].astype(BF16)
        vd_ref[0, h] = vd[:, sl].astype(BF16)


def qkv_prep(z, batch, seq, rope_tabs, axial_tabs, q_norm, k_norm, tm=512):
    tm = min(tm, seq)
    nt = seq // tm
    w = BRANCH_WIDTH
    row = lambda b, i: b * nt + i
    tab_spec = pl.BlockSpec((tm, w), lambda b, i: (i, 0))
    head_out = lambda nh: pl.BlockSpec((1, nh, tm, HEAD_DIM), lambda b, i: (b, 0, i, 0))
    head_shape = lambda nh: jax.ShapeDtypeStruct((batch, nh, seq, HEAD_DIM), BF16)
    return pl.pallas_call(
        _qkv_prep_kernel,
        grid=(batch, nt),
        in_specs=[pl.BlockSpec((tm, IN_A), lambda b, i: (row(b, i), 0)),
                  pl.BlockSpec((tm, w), lambda b, i: (row(b, i), OFF_DQ // w)),
                  pl.BlockSpec((tm, w), lambda b, i: (row(b, i), OFF_DKV // w)),
                  tab_spec, tab_spec, tab_spec, tab_spec, tab_spec, tab_spec,
                  pl.BlockSpec((1, w), lambda b, i: (0, 0)),
                  pl.BlockSpec((1, GQA_KV_HEADS * HEAD_DIM), lambda b, i: (0, 0))],
        out_specs=[head_out(N_HEADS), head_out(N_HEADS), head_out(N_HEADS),
                   head_out(N_HEADS), head_out(GQA_KV_HEADS), head_out(GQA_KV_HEADS)],
        out_shape=[head_shape(N_HEADS), head_shape(N_HEADS), head_shape(N_HEADS),
                   head_shape(N_HEADS), head_shape(GQA_KV_HEADS), head_shape(GQA_KV_HEADS)],
        compiler_params=_params("parallel", "parallel"),
        name="qkv_prep",
    )(z, z, z, *rope_tabs, *axial_tabs,
      jnp.tile(q_norm.reshape(1, HEAD_DIM), (1, N_HEADS)),
      jnp.tile(k_norm.reshape(1, HEAD_DIM), (1, GQA_KV_HEADS)))


A_TQ = 1024
A_SUB = 256
A_RADIUS = 64
DILATED_GROUPS = ((1, ((128, 1), (512, 4))), (16, ((128, 1),)))
assert sorted(w * g for g, ps in DILATED_GROUPS for w, _ in ps) == sorted(w for w, _ in DILATED_PATTERNS)
assert all(w // (2 * d) == A_RADIUS for w, d in DILATED_PATTERNS)


def _window_geometry(patterns, sub):
    out = []
    for window, dil in patterns:
        halo = -(-(window // 2) // 128) * 128
        out.append((dil, -halo, sub + 2 * halo))
    return out


def _window_bias(patterns, sub):
    biases = []
    for dil, first, width in _window_geometry(patterns, sub):
        qi = jnp.arange(sub, dtype=jnp.int32)[:, None]
        kj = jnp.arange(width, dtype=jnp.int32)[None, :] + first
        delta = kj - qi
        ok = (jnp.abs(delta) <= A_RADIUS * dil) & ((delta & (dil - 1)) == 0)
        biases.append(jnp.where(ok, 0.0, NEG_INF).astype(F32))
    return biases


def _window_attn_kernel(q_ref, kp_ref, kc_ref, kn_ref, vp_ref, vc_ref, vn_ref, *rest, seq, tq, sub, windows):
    bias_refs = rest[:len(windows)]
    o_ref, k3_ref, v3_ref = rest[len(windows):]
    i = pl.program_id(2)
    k3_ref[0:tq] = kp_ref[0, 0]
    k3_ref[tq:2 * tq] = kc_ref[0, 0]
    k3_ref[2 * tq:3 * tq] = kn_ref[0, 0]
    v3_ref[0:tq] = vp_ref[0, 0]
    v3_ref[tq:2 * tq] = vc_ref[0, 0]
    v3_ref[2 * tq:3 * tq] = vn_ref[0, 0]
    for u in range(tq // sub):
        q = q_ref[0, 0, u * sub:(u + 1) * sub, :]
        scores = []
        for (dil, first, width), b_ref in zip(windows, bias_refs):
            start = tq + u * sub + first
            s = _bdot_nt(q, k3_ref[start:start + width, :]) + b_ref[...]
            kpos = (i - 1) * tq + start + lax.broadcasted_iota(jnp.int32, (1, width), 1)
            s = jnp.where((kpos >= 0) & (kpos < seq), s, NEG_INF)
            scores.append((s, start, width))
        m = functools.reduce(jnp.maximum, [jnp.max(s, axis=-1, keepdims=True) for s, _, _ in scores])
        l = jnp.zeros_like(m)
        acc = jnp.zeros((sub, HEAD_DIM), F32)
        for s, start, width in scores:
            p = jnp.exp2(s - m)
            l = l + jnp.sum(p, axis=-1, keepdims=True)
            acc = acc + _bdot(p, v3_ref[start:start + width, :])
        log_den = jnp.broadcast_to(m + jnp.log2(l), (sub, HEAD_DIM))
        o_ref[0, 0, u * sub:(u + 1) * sub, :] = jnp.concatenate([acc / l, log_den], axis=1)


def window_attention(q, k, v, patterns):
    batch, nh, seq, hd = q.shape
    tq = min(A_TQ, seq)
    sub = min(A_SUB, tq)
    windows = _window_geometry(patterns, sub)
    assert all(-first <= tq for _, first, _ in windows), "the key halo must fit in one neighbouring tile"
    nt = seq // tq
    cur = pl.BlockSpec((1, 1, tq, hd), lambda b, h, i: (b, h, i, 0))
    prev = pl.BlockSpec((1, 1, tq, hd), lambda b, h, i: (b, h, jnp.maximum(i - 1, 0), 0))
    nxt = pl.BlockSpec((1, 1, tq, hd), lambda b, h, i: (b, h, jnp.minimum(i + 1, nt - 1), 0))
    biases = _window_bias(patterns, sub)
    bias_specs = [pl.BlockSpec(b.shape, lambda b_, h, i: (0, 0)) for b in biases]
    return pl.pallas_call(
        functools.partial(_window_attn_kernel, seq=seq, tq=tq, sub=sub, windows=windows),
        grid=(batch, nh, nt),
        in_specs=[cur, prev, cur, nxt, prev, cur, nxt] + bias_specs,
        out_specs=pl.BlockSpec((1, 1, tq, 2 * hd), lambda b, h, i: (b, h, i, 0)),
        out_shape=jax.ShapeDtypeStruct((batch, nh, seq, 2 * hd), F32),
        scratch_shapes=[pltpu.VMEM((3 * tq, hd), BF16), pltpu.VMEM((3 * tq, hd), BF16)],
        compiler_params=_params("parallel", "parallel", "parallel"),
        name="window_attention_x%d" % len(patterns),
    )(q, k, k, k, v, v, v, *biases)


def _to_residues(a, stride):
    b, h, s, w = a.shape
    return a.reshape(b, h, s // stride, stride, w).transpose(0, 3, 1, 2, 4).reshape(b * stride, h, s // stride, w)


def _from_residues(a, stride):
    bs, h, l, w = a.shape
    return a.reshape(bs // stride, stride, h, l, w).transpose(0, 2, 3, 1, 4).reshape(bs // stride, h, l * stride, w)


def dilated_attention(qa, ka, va):
    parts = []
    for stride, patterns in DILATED_GROUPS:
        if stride == 1:
            parts.append(window_attention(qa, ka, va, patterns))
        else:
            part = window_attention(*(_to_residues(a, stride) for a in (qa, ka, va)), patterns)
            parts.append(_from_residues(part, stride))
    return parts


GQA_SUB = 128
GQA_LOOKAHEAD = 4


def _gqa_kernel(q_ref, k_ref, v_ref, o_ref, *scratch, rep, tq):
    j = pl.program_id(3)
    n_sub = tq // GQA_SUB
    blocks = [(r, u) for r in range(rep) for u in range(n_sub)]
    m_refs, l_refs, acc_refs = (scratch[i * len(blocks):(i + 1) * len(blocks)] for i in range(3))

    @pl.when(j == 0)
    def _():
        for m_ref, l_ref, acc_ref in zip(m_refs, l_refs, acc_refs):
            m_ref[...] = jnp.full(m_ref.shape, NEG_INF, F32)
            l_ref[...] = jnp.zeros(l_ref.shape, F32)
            acc_ref[...] = jnp.zeros(acc_ref.shape, F32)

    k = k_ref[0, 0]
    v = v_ref[0, 0]
    def score(block):
        r, u = block
        return _bdot_nt(k, q_ref[0, r, u * GQA_SUB:(u + 1) * GQA_SUB, :])

    scores = [score(b) for b in blocks[:GQA_LOOKAHEAD]]
    for i, (m_ref, l_ref, acc_ref) in enumerate(zip(m_refs, l_refs, acc_refs)):
        s = scores[i]
        if i + GQA_LOOKAHEAD < len(blocks):
            scores.append(score(blocks[i + GQA_LOOKAHEAD]))
        m_prev = m_ref[...]
        m_new = jnp.maximum(m_prev, jnp.max(s, axis=0, keepdims=True))
        alpha = jnp.exp2(m_prev - m_new)
        p = jnp.exp2(s - m_new)
        l_ref[...] = alpha * l_ref[...] + jnp.sum(p, axis=0, keepdims=True)
        pv = lax.dot_general(v, p.astype(BF16), (((0,), (0,)), ((), ())), preferred_element_type=F32)
        acc_ref[...] = alpha * acc_ref[...] + pv
        m_ref[...] = m_new

    @pl.when(j == pl.num_programs(3) - 1)
    def _():
        for (r, u), l_ref, acc_ref in zip(blocks, l_refs, acc_refs):
            o_ref[0, r, :, u * GQA_SUB:(u + 1) * GQA_SUB] = acc_ref[...] / l_ref[...]


def gqa_attention(qd, kd, vd, tq=1024, tk=2048):
    batch, nh, seq, hd = qd.shape
    ng = kd.shape[1]
    rep = nh // ng
    tq = min(tq, seq)
    tk = min(tk, seq)
    n_blocks = rep * (tq // GQA_SUB)
    return pl.pallas_call(
        functools.partial(_gqa_kernel, rep=rep, tq=tq),
        grid=(batch, ng, seq // tq, seq // tk),
        in_specs=[pl.BlockSpec((1, rep, tq, hd), lambda b, g, i, j: (b, g, i, 0)),
                  pl.BlockSpec((1, 1, tk, hd), lambda b, g, i, j: (b, g, j, 0)),
                  pl.BlockSpec((1, 1, tk, hd), lambda b, g, i, j: (b, g, j, 0))],
        out_specs=pl.BlockSpec((1, rep, hd, tq), lambda b, g, i, j: (b, g, 0, i)),
        out_shape=jax.ShapeDtypeStruct((batch, nh, hd, seq), F32),
        scratch_shapes=([pltpu.VMEM((1, GQA_SUB), F32)] * (2 * n_blocks)
                        + [pltpu.VMEM((hd, GQA_SUB), F32)] * n_blocks),
        compiler_params=_params("parallel", "parallel", "parallel", "arbitrary"),
        name="gqa_attention",
    )(qd, kd, vd)


def _matmul_kernel(a_ref, w_ref, o_ref):
    o_ref[...] = jnp.dot(a_ref[...], w_ref[...], preferred_element_type=F32).astype(o_ref.dtype)


def matmul_bf16(a, w, tm, tn, out_dtype=F32):
    n, k = a.shape
    m = w.shape[1]
    tm = min(tm, n)
    return pl.pallas_call(
        _matmul_kernel,
        grid=(n // tm, m // tn),
        in_specs=[pl.BlockSpec((tm, k), lambda i, j: (i, 0)),
                  pl.BlockSpec((k, tn), lambda i, j: (0, j))],
        out_specs=pl.BlockSpec((tm, tn), lambda i, j: (i, j)),
        out_shape=jax.ShapeDtypeStruct((n, m), out_dtype),
        compiler_params=_params("parallel", "arbitrary"),
        name="matmul_bf16",
    )(a, w)


def _sigmoid(x):
    return 1.0 / (1.0 + jnp.exp(-x))


def _softplus(x):
    return jnp.maximum(x, 0.0) + jnp.log(1.0 + jnp.exp(-jnp.abs(x)))


def _shift_rows(x, edge_row, down):
    rows = x.shape[0]
    ridx = lax.broadcasted_iota(jnp.int32, x.shape, 0)
    if down:
        return jnp.where(ridx == 0, edge_row, pltpu.roll(x, 1, 0))
    return jnp.where(ridx == rows - 1, edge_row, pltpu.roll(x, rows - 1, 0))


WKV_FIELDS = 6


def _rwkv_prep_kernel(xn_ref, xp_ref, xq_ref, zb_ref, zp_ref, zq_ref, mux_ref, murkv_ref, lw1_ref, lw2_ref,
                      w0a0_ref, g1_ref, g2_ref, kk_ref, ka_ref, rk_ref,
                      fw_ref, bw_ref, bonus_ref, gate_ref, *, tiles_per_seq):
    w = BRANCH_WIDTH
    i = pl.program_id(0)
    first = (i % tiles_per_seq) == 0
    last = (i % tiles_per_seq) == tiles_per_seq - 1
    xn = xn_ref[...]
    x_shift = (_shift_rows(xn, jnp.where(first, 0.0, xp_ref[7:8, :]), True),
               _shift_rows(xn, jnp.where(last, 0.0, xq_ref[0:1, :]), False))
    zb = zb_ref[...]
    z_prev = _shift_rows(zb, jnp.where(first, 0.0, zp_ref[7:8, :]), True)
    z_next = _shift_rows(zb, jnp.where(last, 0.0, zq_ref[0:1, :]), False)
    mu = murkv_ref[...]
    rkv = zb + mu[0:1] * (z_prev - zb) + mu[1:2] * (z_next - zb)
    r, k, v = rkv[:, :w], rkv[:, w:2 * w], rkv[:, 2 * w:]
    kap = k * kk_ref[...]
    kap = kap * lax.rsqrt(_head_sum(kap * kap, N_HEADS) + 1e-12)
    gate_ref[...] = _bdot(_sigmoid(_bdot(xn, g1_ref[...])), g2_ref[...])
    bonus = jnp.zeros_like(v)
    lora_lane = lax.broadcasted_iota(jnp.int32, (xn.shape[0], lw1_ref.shape[-1]), 1)
    for d, out_ref in enumerate((fw_ref, bw_ref)):
        xd = xn + mux_ref[d:d + 1, :] * (x_shift[d] - xn)
        h = _bdot(xd, lw1_ref[d])
        h = jnp.where(lora_lane < lw1_ref.shape[-1] // 2, jnp.tanh(h), h)
        h = _bdot(h, lw2_ref[d]) + w0a0_ref[d:d + 1, :]
        w_log = -_softplus(-h[:, :w]) - 0.5
        log_decay = -jnp.exp(w_log)
        iclr = _sigmoid(h[:, w:])
        k_d = k * (1.0 + (iclr - 1.0) * ka_ref[...])
        bonus = bonus + _head_sum(r * k_d * rk_ref[...], N_HEADS) * v
        for j, field in enumerate((r, log_decay, k_d, v, kap, iclr * kap)):
            out_ref[:, j * w:(j + 1) * w] = field
    bonus_ref[...] = bonus


def rwkv_prep(xn, z, seq, mu_x, mu_rkv, lw1, lw2, w0a0, g1, g2, k_k, k_a, r_k, tm=512):
    n, d = xn.shape
    w = BRANCH_WIDTH
    tm = min(tm, seq)
    halo = 8
    prev_halo = lambda i: (jnp.maximum(i * (tm // halo) - 1, 0), 0)
    next_halo = lambda i: (jnp.minimum((i + 1) * (tm // halo), n // halo - 1), 0)
    full = lambda a: pl.BlockSpec(a.shape, lambda i: (0,) * a.ndim)
    params = (mu_x, mu_rkv, lw1, lw2, w0a0, g1, g2, k_k.reshape(1, w), k_a.reshape(1, w), r_k.reshape(1, w))
    rows = lambda width: pl.BlockSpec((tm, width), lambda i: (i, 0))
    return pl.pallas_call(
        functools.partial(_rwkv_prep_kernel, tiles_per_seq=seq // tm),
        grid=(n // tm,),
        in_specs=[rows(d), pl.BlockSpec((halo, d), prev_halo), pl.BlockSpec((halo, d), next_halo),
                  pl.BlockSpec((tm, IN_B), lambda i: (i, OFF_B // IN_B)),
                  pl.BlockSpec((halo, IN_B), lambda i: (prev_halo(i)[0], OFF_B // IN_B)),
                  pl.BlockSpec((halo, IN_B), lambda i: (next_halo(i)[0], OFF_B // IN_B))]
                 + [full(p) for p in params],
        out_specs=[rows(WKV_FIELDS * w), rows(WKV_FIELDS * w), rows(w), rows(w)],
        out_shape=[jax.ShapeDtypeStruct((n, WKV_FIELDS * w), F32), jax.ShapeDtypeStruct((n, WKV_FIELDS * w), F32),
                   jax.ShapeDtypeStruct((n, w), F32), jax.ShapeDtypeStruct((n, w), F32)],
        compiler_params=_params("parallel"),
        name="rwkv_prep",
    )(xn, xn, xn, z, z, z, *params)


WKV_CHUNK = 64


def _wkv_constants(reverse):
    c, nh = WKV_CHUNK, N_HEADS
    t = jnp.arange(c)
    before = (t[None, :] > t[:, None]) if reverse else (t[None, :] < t[:, None])
    incl = before | (t[None, :] == t[:, None])
    per_head = lambda m: jnp.kron(jnp.eye(nh, dtype=F32), m.astype(F32))
    return incl.astype(F32), per_head(before), per_head(incl)


def _cumulative_log_decay(tri, logws):
    w = logws[0].shape[1]
    cat = jnp.concatenate(logws, axis=1)
    hi = cat.astype(BF16)
    lo = (cat - hi.astype(F32)).astype(BF16)
    tri = tri.astype(BF16)
    g = jnp.dot(tri, hi, preferred_element_type=F32) + jnp.dot(tri, lo, preferred_element_type=F32)
    return [g[:, i * w:(i + 1) * w] for i in range(len(logws))]


def _wkv_chunks(chains, head_rows, s_mask):
    c, w, nh = WKV_CHUNK, BRANCH_WIDTH, N_HEADS
    hc = nh * c
    every = range(len(chains))

    def per_head(a):
        return jnp.concatenate([jnp.where(head_rows[h:h + 1, :] > 0.0, a, 0.0) for h in range(nh)], axis=0)

    def wide(m):
        return m[0:c] + m[c:2 * c] + m[2 * c:3 * c] + m[3 * c:4 * c]

    gram_lhs, gram_rhs, gam, vs, k_ts, b_ts = [], [], [], [], [], []
    for x, g, s_bd, _, _, _ in chains:
        r, logw, k, v, kap, b = (x[:, j * w:(j + 1) * w] for j in range(WKV_FIELDS))
        gam.append(jnp.exp(g))
        g_inv = jnp.exp(-g)
        kap_t, r_t = kap * jnp.exp(g - logw), r * gam[-1]
        k_ts.append(k * g_inv)
        b_ts.append(b * g_inv)
        vs.append(v)
        gram_lhs.append(jnp.concatenate([per_head(kap_t), per_head(r_t)], axis=0))
        gram_rhs.append(jnp.concatenate([per_head(k_ts[-1]), per_head(b_ts[-1]), s_bd], axis=0))
    gram = [_bdot_nt(gram_lhs[i], gram_rhs[i]) for i in every]
    kk = [gram[i][:hc, :hc] * chains[i][3] for i in every]
    n = [gram[i][:hc, hc:2 * hc] * chains[i][3] for i in every]
    rk = [gram[i][hc:, :hc] * chains[i][4] for i in every]
    rb = [gram[i][hc:, hc:2 * hc] * chains[i][4] for i in every]
    from_kap = [wide(gram[i][:hc, 2 * hc:]) for i in every]
    from_r = [wide(gram[i][hc:, 2 * hc:]) for i in every]
    eye = jnp.where(lax.broadcasted_iota(jnp.int32, (hc, hc), 0) == lax.broadcasted_iota(jnp.int32, (hc, hc), 1),
                    1.0, 0.0)
    inv = [eye - n[i] for i in every]
    power = [_bdot(n[i], n[i]) for i in every]
    levels = c.bit_length() - 2
    for level in range(levels):
        inv = [inv[i] + _bdot(inv[i], power[i]) for i in every]
        if level + 1 < levels:
            power = [_bdot(power[i], power[i]) for i in every]
    v_heads = [per_head(v) for v in vs]
    kk_v = [_bdot(wide(kk[i]), v_heads[i]) for i in every]
    u = [_bdot(wide(inv[i]), per_head(from_kap[i] + kk_v[i])) for i in every]
    y = [from_r[i] + _bdot(jnp.concatenate([wide(rk[i]), -wide(rb[i])], axis=1),
                           jnp.concatenate([v_heads[i], per_head(u[i])], axis=0)) for i in every]
    update = [lax.dot_general(jnp.concatenate([vs[i], u[i]], axis=0).astype(BF16),
                              jnp.concatenate([k_ts[i], -b_ts[i]], axis=0).astype(BF16),
                              (((0,), (0,)), ((), ())), preferred_element_type=F32) for i in every]
    s_new = [(chains[i][2] + update[i] * s_mask) * gam[i][chains[i][5]:chains[i][5] + 1, :] for i in every]
    return y, s_new


def _wkv_chunked_kernel(fw_ref, bw_ref, trif_ref, msf_ref, mif_ref, trib_ref, msb_ref, mib_ref, hr_ref, sm_ref,
                        yf_ref, yb_ref, state_ref, *, batch, rows):
    @pl.when(pl.program_id(0) == 0)
    def _():
        state_ref[...] = jnp.zeros(state_ref.shape, F32)

    c = WKV_CHUNK
    n_chunks = rows // c
    head_rows = hr_ref[...]
    s_mask = sm_ref[...]
    tris = (trif_ref[...], trib_ref[...])
    masks = ((msf_ref[...], mif_ref[...]), (msb_ref[...], mib_ref[...]))
    refs, y_refs = (fw_ref, bw_ref), (yf_ref, yb_ref)
    w = BRANCH_WIDTH

    def body(i, carry):
        bases = (pl.multiple_of(i * c, c), pl.multiple_of((n_chunks - 1 - i) * c, c))
        chains, where = [], []
        for d in range(2):
            xs = [refs[d][bi, pl.ds(bases[d], c), :] for bi in range(batch)]
            gs = _cumulative_log_decay(tris[d], [x[:, w:2 * w] for x in xs])
            for bi in range(batch):
                chains.append((xs[bi], gs[bi], state_ref[2 * bi + d], *masks[d], 0 if d == 1 else c - 1))
                where.append((bi, d))
        ys, states = _wkv_chunks(chains, head_rows, s_mask)
        for (bi, d), y, s_new in zip(where, ys, states):
            y_refs[d][bi, pl.ds(bases[d], c), :] = y
            state_ref[2 * bi + d] = s_new
        return carry

    lax.fori_loop(0, n_chunks, body, 0)


def wkv_chunked(fw, bw, batch, seq, rows=256):
    w = BRANCH_WIDTH
    rows = min(rows, seq)
    nb = seq // rows
    head_of_lane = jnp.arange(w) // HEAD_DIM
    head_rows = (jnp.arange(8)[:, None] == head_of_lane[None, :]).astype(F32)
    s_mask = (head_of_lane[:, None] == head_of_lane[None, :]).astype(F32)
    consts = [*_wkv_constants(False), *_wkv_constants(True), head_rows, s_mask]
    in_f = pl.BlockSpec((batch, rows, WKV_FIELDS * w), lambda c: (0, c, 0))
    in_b = pl.BlockSpec((batch, rows, WKV_FIELDS * w), lambda c: (0, nb - 1 - c, 0))
    out_shape = jax.ShapeDtypeStruct((batch, seq, w), F32)
    yf, yb = pl.pallas_call(
        functools.partial(_wkv_chunked_kernel, batch=batch, rows=rows),
        grid=(nb,),
        in_specs=[in_f, in_b] + [pl.BlockSpec(a.shape, lambda c: (0, 0)) for a in consts],
        out_specs=[pl.BlockSpec((batch, rows, w), lambda c: (0, c, 0)),
                   pl.BlockSpec((batch, rows, w), lambda c: (0, nb - 1 - c, 0))],
        out_shape=[out_shape, out_shape],
        scratch_shapes=[pltpu.VMEM((2 * batch, w, w), F32)],
        compiler_params=_params("arbitrary"),
        name="wkv_chunked",
    )(fw.reshape(batch, seq, -1), bw.reshape(batch, seq, -1), *consts)
    return yf.reshape(batch * seq, w), yb.reshape(batch * seq, w)


S5_T = 64


def _s5_operators(a_re, a_im, log_dt, b_re, b_im, c_re, c_im, n_chunks, reverse):
    g, p, c, t = S5_GROUPS, S5_STATE, S5_GROUP_CH, S5_T
    dt = jnp.exp(log_dt)[:, None]
    mag = jnp.exp(a_re * dt)
    bar_re, bar_im = mag * jnp.cos(a_im * dt), mag * jnp.sin(a_im * dt)
    den = a_re * a_re + a_im * a_im
    f_re = ((bar_re - 1.0) * a_re + bar_im * a_im) / den
    f_im = (bar_im * a_re - (bar_re - 1.0) * a_im) / den
    bb_re = f_re[..., None] * b_re - f_im[..., None] * b_im
    bb_im = f_re[..., None] * b_im + f_im[..., None] * b_re

    def powers(j):
        j = j.astype(F32)[:, None, None]
        m = jnp.exp(j * (a_re * dt)[None])
        return m * jnp.cos(j * (a_im * dt)[None]), m * jnp.sin(j * (a_im * dt)[None])

    pw_re, pw_im = powers(jnp.arange(t + 1))
    z_re = pw_re[..., None] * bb_re[None] - pw_im[..., None] * bb_im[None]
    z_im = pw_re[..., None] * bb_im[None] + pw_im[..., None] * bb_re[None]
    kern = (jnp.einsum('gcp,jgpd->jgcd', c_re, z_re) - jnp.einsum('gcp,jgpd->jgcd', c_im, z_im))
    ti = jnp.arange(t)
    lag = (ti[None, :] - ti[:, None]) if reverse else (ti[:, None] - ti[None, :])
    toep = jnp.where((lag >= 0)[:, :, None, None, None], kern[jnp.maximum(lag, 0)], 0.0)
    toep = toep.transpose(2, 0, 3, 1, 4).reshape(g, t * c, t * c)
    steps_in = (t - ti) if reverse else (ti + 1)
    qr, qi = pw_re[steps_in], pw_im[steps_in]
    out_re = c_re[None] * qr[:, :, None, :] - c_im[None] * qi[:, :, None, :]
    out_im = c_re[None] * qi[:, :, None, :] + c_im[None] * qr[:, :, None, :]
    state_out = jnp.concatenate([out_re, -out_im], axis=-1).transpose(1, 0, 2, 3).reshape(g, t * c, 2 * p)
    steps_left = ti if reverse else (t - 1 - ti)
    in_state = jnp.concatenate([z_re[steps_left], z_im[steps_left]], axis=2)
    in_state = in_state.transpose(1, 2, 0, 3).reshape(g, 2 * p, t * c)
    levels = max(1, (n_chunks - 1).bit_length())
    lr, li = powers(t * (2 ** jnp.arange(levels)))
    chunk_pow = jnp.stack([lr, li], axis=1)[..., None]
    return toep.astype(BF16), state_out.astype(BF16), in_state.astype(BF16), chunk_pow


def _s5_chunk_states(fin, pow_ref, n_chunks, reverse):
    p = S5_STATE
    fr, fi = fin[:p], fin[p:]
    lanes = fr.shape[1]
    pos = lax.broadcasted_iota(jnp.int32, fr.shape, 1) % n_chunks
    shift_of = lambda step: (lanes - step) if reverse else step
    reachable = lambda step: (pos < n_chunks - step) if reverse else (pos >= step)
    for level in range(pow_ref.shape[0]):
        step = 2 ** level
        pr, pi = pow_ref[level, 0], pow_ref[level, 1]
        er = pltpu.roll(fr, shift_of(step), 1)
        ei = pltpu.roll(fi, shift_of(step), 1)
        ok = reachable(step)
        fr, fi = (fr + jnp.where(ok, pr * er - pi * ei, 0.0), fi + jnp.where(ok, pr * ei + pi * er, 0.0))
    ok = reachable(1)
    fr = jnp.where(ok, pltpu.roll(fr, shift_of(1), 1), 0.0)
    fi = jnp.where(ok, pltpu.roll(fi, shift_of(1), 1), 0.0)
    return jnp.concatenate([fr, fi], axis=0)


def _s5_conv_kernel(u_ref, mf_ref, sf_ref, ef_ref, pf_ref, mb_ref, sb_ref, eb_ref, pb_ref, y_ref, *, n_chunks):
    u = u_ref[0]
    y = None
    for m_ref, s_ref, e_ref, p_ref, reverse in ((mf_ref, sf_ref, ef_ref, pf_ref, False),
                                                (mb_ref, sb_ref, eb_ref, pb_ref, True)):
        within = jnp.dot(m_ref[0], u, preferred_element_type=F32)
        fin = jnp.dot(e_ref[0], u, preferred_element_type=F32)
        x_in = _s5_chunk_states(fin, p_ref.at[:, :, 0], n_chunks, reverse)
        part = within + _bdot(s_ref[0], x_in)
        y = part if y is None else y + part
    y_ref[0] = y


def s5_bidirectional_conv(z, batch, seq, ops_fwd, ops_bwd):
    g, c, t = S5_GROUPS, S5_GROUP_CH, S5_T
    nc = seq // t
    cols = batch * nc
    u = z[:, OFF_C:OFF_C + BRANCH_WIDTH].reshape(batch, nc, t, g, c)
    u = u.transpose(3, 2, 4, 0, 1).reshape(g, t * c, cols).astype(BF16)
    ops = (*ops_fwd, *ops_bwd)
    spec = lambda a: pl.BlockSpec((1,) + a.shape[1:], lambda i: (i,) + (0,) * (a.ndim - 1))
    pow_spec = lambda a: pl.BlockSpec(a.shape[:2] + (1,) + a.shape[3:], lambda i: (0, 0, i, 0, 0))
    y = pl.pallas_call(
        functools.partial(_s5_conv_kernel, n_chunks=nc),
        grid=(g,),
        in_specs=[spec(u)] + [pow_spec(a) if a.ndim == 5 else spec(a) for a in ops],
        out_specs=pl.BlockSpec((1, t * c, cols), lambda i: (i, 0, 0)),
        out_shape=jax.ShapeDtypeStruct((g, t * c, cols), F32),
        compiler_params=_params("parallel"),
        name="s5_conv",
    )(u, *ops)
    return y.reshape(g, t, c, batch, nc).transpose(3, 4, 1, 0, 2).reshape(batch * seq, g * c)


def _gelu_tanh(y):
    return 0.5 * y * (1.0 + jnp.tanh(math.sqrt(2.0 / math.pi) * (y + 0.044715 * (y * y * y))))


def _merge_kernel(x_ref, *refs, n_parts):
    ya_refs = refs[:n_parts]
    (yd_ref, wf_ref, wb_ref, bonus_ref, rg_ref, s5_ref, u_ref, zg_ref,
     gb_ref, wbr_ref, wout_ref, lnw_ref, lnb_ref, s5d_ref, gluw_ref, glub_ref, o_ref) = refs[n_parts:]
    w = BRANCH_WIDTH
    heads = []
    for h in range(N_HEADS):
        parts = [ref[0, h] for ref in ya_refs]
        log_den = [p[:, HEAD_DIM:HEAD_DIM + 1] for p in parts]
        top = functools.reduce(jnp.maximum, log_den)
        share = [jnp.exp2(ld - top) for ld in log_den]
        heads.append(sum(s * p[:, :HEAD_DIM] for s, p in zip(share, parts)) / sum(share))
    ya = jnp.concatenate(heads, axis=1)
    ys = wf_ref[...] + wb_ref[...]
    cen = ys - _head_sum(ys, N_HEADS) * (1.0 / HEAD_DIM)
    var = _head_sum(cen * cen, N_HEADS) * (1.0 / HEAD_DIM)
    yb = (cen * lax.rsqrt(var + RWKV_GN_EPS) * lnw_ref[...] + lnb_ref[...] + bonus_ref[...]) * rg_ref[...]
    yc = s5_ref[...] + s5d_ref[...] * u_ref[...]
    h = _bdot(_gelu_tanh(yc), gluw_ref[...]) + glub_ref[...]
    yc = h[:, :w] * _sigmoid(h[:, w:])
    proj_d = sum(lax.dot_general(yd_ref[0, h].astype(BF16), wbr_ref[3, h * HEAD_DIM:(h + 1) * HEAD_DIM, :],
                                 (((0,), (0,)), ((), ())), preferred_element_type=F32) for h in range(N_HEADS))
    merged = jnp.zeros(o_ref.shape, F32)
    for i, proj in enumerate((_bdot(ya, wbr_ref[0]), _bdot(yb, wbr_ref[1]), _bdot(yc, wbr_ref[2]), proj_d)):
        gate = _sigmoid(zg_ref[:, i * D_MODEL:(i + 1) * D_MODEL] + gb_ref[i:i + 1, :])
        merged = merged + gate * proj
    o_ref[...] = x_ref[...] + _bdot(merged, wout_ref[...])


def merge_branches(x, ya_parts, yd, wkv_f, wkv_b, bonus, rgate, s5_y, z, zg, gate_b, w_branch, w_out,
                   ln_w, ln_b, s5_d, glu_w, glu_b, batch, seq, tm=512):
    w = BRANCH_WIDTH
    d = D_MODEL
    tm = min(tm, seq)
    nt = seq // tm
    rows = lambda width, col=0: pl.BlockSpec((tm, width), lambda b, i: (b * nt + i, col))
    heads = pl.BlockSpec((1, N_HEADS, tm, 2 * HEAD_DIM), lambda b, i: (b, 0, i, 0))
    heads_t = pl.BlockSpec((1, N_HEADS, HEAD_DIM, tm), lambda b, i: (b, 0, 0, i))
    full = lambda a: pl.BlockSpec(a.shape, lambda b, i: (0,) * a.ndim)
    params = (gate_b, w_branch, w_out, ln_w.reshape(1, w), ln_b.reshape(1, w), s5_d.reshape(1, w),
              glu_w, glu_b.reshape(1, 2 * w))
    return pl.pallas_call(
        functools.partial(_merge_kernel, n_parts=len(ya_parts)),
        grid=(batch, nt),
        in_specs=[rows(d)] + [heads] * len(ya_parts)
                 + [heads_t, rows(w), rows(w), rows(w), rows(w), rows(w),
                    rows(w, OFF_C // w), rows(N_BRANCHES * d)] + [full(p) for p in params],
        out_specs=rows(d),
        out_shape=jax.ShapeDtypeStruct(x.shape, F32),
        compiler_params=_params("parallel", "parallel"),
        name="merge_branches",
    )(x, *ya_parts, yd, wkv_f, wkv_b, bonus, rgate, s5_y, z, zg, *params)


def _silu(x):
    return x * _sigmoid(x)


def _swiglu_tile(x, wg_ref, wu_ref, wd_ref):
    h = (_silu(jnp.dot(x, wg_ref[...], preferred_element_type=F32))
         * jnp.dot(x, wu_ref[...], preferred_element_type=F32))
    return _bdot(h, wd_ref[...])


def _dense_ffn_kernel(x_ref, g_ref, wg_ref, wu_ref, wd_ref, o_ref, xn_ref):
    @pl.when(pl.program_id(1) == 0)
    def _():
        x = x_ref[...]
        xn_ref[...] = _rms(x, g_ref[...]).astype(BF16)
        o_ref[...] = x

    o_ref[...] += _swiglu_tile(xn_ref[...], wg_ref, wu_ref, wd_ref)


def dense_ffn(x, g, w_gate, w_up, w_down, tm=1024, tf=1408):
    n, d = x.shape
    ff = w_gate.shape[1]
    tm = min(tm, n)
    return pl.pallas_call(
        _dense_ffn_kernel,
        grid=(n // tm, ff // tf),
        in_specs=[pl.BlockSpec((tm, d), lambda i, f: (i, 0)),
                  pl.BlockSpec((1, d), lambda i, f: (0, 0)),
                  pl.BlockSpec((d, tf), lambda i, f: (0, f)),
                  pl.BlockSpec((d, tf), lambda i, f: (0, f)),
                  pl.BlockSpec((tf, d), lambda i, f: (f, 0))],
        out_specs=pl.BlockSpec((tm, d), lambda i, f: (i, 0)),
        out_shape=jax.ShapeDtypeStruct((n, d), F32),
        scratch_shapes=[pltpu.VMEM((tm, d), BF16)],
        compiler_params=_params("parallel", "arbitrary"),
        name="dense_ffn",
    )(x, g.reshape(1, d), w_gate, w_up, w_down)


def _router_kernel(x_ref, g_ref, rt_ref, xnb_ref, sel_ref, wt_ref):
    xn = _rms(x_ref[...], g_ref[...])
    xnb_ref[...] = xn.astype(BF16)
    logits = lax.dot_general(rt_ref[...], xn, (((1,), (1,)), ((), ())),
                             precision=lax.Precision.HIGHEST, preferred_element_type=F32)
    e = lax.broadcasted_iota(jnp.int32, logits.shape, 0)
    m1 = jnp.max(logits, axis=0, keepdims=True)
    i1 = jnp.min(jnp.where(logits == m1, e, N_EXPERTS), axis=0, keepdims=True)
    rest = jnp.where(e == i1, NEG_INF, logits)
    m2 = jnp.max(rest, axis=0, keepdims=True)
    i2 = jnp.min(jnp.where(rest == m2, e, N_EXPERTS), axis=0, keepdims=True)
    ratio = jnp.exp(m2 - m1)
    w1 = 1.0 / (1.0 + ratio)
    w2 = ratio / (1.0 + ratio)
    sel_ref[...] = jnp.where((e == i1) | (e == i2), 1.0, 0.0)
    wt_ref[...] = jnp.where(e == i1, w1, jnp.where(e == i2, w2, 0.0))


def moe_route(x, g, router, tm=1024):
    n, d = x.shape
    tm = min(tm, n)
    ne = router.shape[1]
    return pl.pallas_call(
        _router_kernel,
        grid=(n // tm,),
        in_specs=[pl.BlockSpec((tm, d), lambda i: (i, 0)),
                  pl.BlockSpec((1, d), lambda i: (0, 0)),
                  pl.BlockSpec((ne, d), lambda i: (0, 0))],
        out_specs=[pl.BlockSpec((tm, d), lambda i: (i, 0)),
                   pl.BlockSpec((ne, tm), lambda i: (0, i)),
                   pl.BlockSpec((ne, tm), lambda i: (0, i))],
        out_shape=[jax.ShapeDtypeStruct((n, d), BF16), jax.ShapeDtypeStruct((ne, n), F32),
                   jax.ShapeDtypeStruct((ne, n), F32)],
        compiler_params=_params("parallel"),
        name="moe_router",
    )(x, g.reshape(1, d), router.T)


MOE_ROWS = 32
MOE_STATIC_BLOCKS = (8, 9, 10)


def _moe_kernel(x_ref, xnb_ref, sel_ref, wt_ref, wg_ref, wu_ref, wd_ref, o_ref,
                rank_ref, xg_ref, acc_ref, nblk_ref):
    e = pl.program_id(1)
    f = pl.program_id(2)
    nf = pl.num_programs(2)
    tm = x_ref.shape[0]

    @pl.when((e == 0) & (f == 0))
    def _():
        o_ref[...] = x_ref[...]
        before = (lax.broadcasted_iota(jnp.int32, (tm, tm), 0) < lax.broadcasted_iota(jnp.int32, (tm, tm), 1))
        rank_ref[...] = jnp.dot(sel_ref[...].astype(BF16), jnp.where(before, 1.0, 0.0).astype(BF16),
                                preferred_element_type=F32)

    sel_e = sel_ref[pl.ds(e, 1), :]
    rank_e = rank_ref[pl.ds(e, 1), :]
    wt_e = wt_ref[pl.ds(e, 1), :]

    @pl.when(f == 0)
    def _():
        count = jnp.sum(sel_e).astype(jnp.int32)
        nblk_ref[0] = (count + MOE_ROWS - 1) // MOE_ROWS

    nblk = nblk_ref[0]

    def process(rows):
        n_rows = rows.stop - rows.start if isinstance(rows, slice) else rows.size
        first = rows.start

        def one_hot():
            slot = (first + lax.broadcasted_iota(jnp.int32, (n_rows, tm), 0)).astype(F32)
            return jnp.where((rank_e == slot) & (sel_e > 0.0), 1.0, 0.0)

        @pl.when(f == 0)
        def _():
            xg_ref[rows, :] = jnp.dot(one_hot().astype(BF16), xnb_ref[...],
                                      preferred_element_type=F32).astype(BF16)
            acc_ref[rows, :] = jnp.zeros((n_rows, acc_ref.shape[1]), F32)

        xg = xg_ref[rows, :]
        acc_ref[rows, :] += _swiglu_tile(xg, wg_ref.at[0], wu_ref.at[0], wd_ref.at[0])

        @pl.when(f == nf - 1)
        def _():
            hot = one_hot()
            row_w = jnp.sum(hot * wt_e, axis=1, keepdims=True)
            yw = (acc_ref[rows, :] * row_w).astype(BF16)
            o_ref[...] += lax.dot_general(hot.astype(BF16), yw, (((0,), (0,)), ((), ())),
                                          preferred_element_type=F32)

    for n_static in MOE_STATIC_BLOCKS:
        lo = 0 if n_static == MOE_STATIC_BLOCKS[0] else n_static
        hi = n_static if n_static != MOE_STATIC_BLOCKS[-1] else tm // MOE_ROWS
        pl.when((nblk >= lo) & (nblk <= hi))(functools.partial(process, slice(0, n_static * MOE_ROWS)))

    def tail(b, carry):
        process(pl.ds(pl.multiple_of(b * MOE_ROWS, MOE_ROWS), MOE_ROWS))
        return carry

    lax.fori_loop(MOE_STATIC_BLOCKS[-1], nblk, tail, 0)


def moe_ffn(x, xnb, sel, wt, w_gate, w_up, w_down, tm=1024, tf=896):
    n, d = x.shape
    ne, _, ff = w_gate.shape
    tm = min(tm, n)
    return pl.pallas_call(
        _moe_kernel,
        grid=(n // tm, ne, ff // tf),
        in_specs=[pl.BlockSpec((tm, d), lambda i, e, f: (i, 0)),
                  pl.BlockSpec((tm, d), lambda i, e, f: (i, 0)),
                  pl.BlockSpec((ne, tm), lambda i, e, f: (0, i)),
                  pl.BlockSpec((ne, tm), lambda i, e, f: (0, i)),
                  pl.BlockSpec((1, d, tf), lambda i, e, f: (e, 0, f)),
                  pl.BlockSpec((1, d, tf), lambda i, e, f: (e, 0, f)),
                  pl.BlockSpec((1, tf, d), lambda i, e, f: (e, f, 0))],
        out_specs=pl.BlockSpec((tm, d), lambda i, e, f: (i, 0)),
        out_shape=jax.ShapeDtypeStruct((n, d), F32),
        scratch_shapes=[pltpu.VMEM((ne, tm), F32), pltpu.VMEM((tm, d), BF16), pltpu.VMEM((tm, d), F32),
                        pltpu.SMEM((1,), jnp.int32)],
        compiler_params=_params("parallel", "arbitrary", "arbitrary"),
        name="moe_ffn",
    )(x, xnb, sel, wt, w_gate, w_up, w_down)


def _rms_kernel(x_ref, g_ref, o_ref):
    o_ref[...] = _rms(x_ref[...], g_ref[...])


def rms_norm(x, g, tm=1024):
    n, d = x.shape
    tm = min(tm, n)
    return pl.pallas_call(
        _rms_kernel,
        grid=(n // tm,),
        in_specs=[pl.BlockSpec((tm, d), lambda i: (i, 0)), pl.BlockSpec((1, d), lambda i: (0, 0))],
        out_specs=pl.BlockSpec((tm, d), lambda i: (i, 0)),
        out_shape=jax.ShapeDtypeStruct((n, d), F32),
        compiler_params=_params("parallel"),
        name="final_rms_norm",
    )(x, g.reshape(1, d))


def _rope_angles(pos, n_freq, theta):
    inv_freq = theta ** (-jnp.arange(n_freq, dtype=F32) / n_freq)
    return pos.astype(F32)[:, None] * inv_freq[None, :]


def kernel(x, norm_mix_g, w_in, gate_b, w_branch, w_out, rwkv_mu_rkv, rwkv_mu_x, rwkv_w0, rwkv_w1, rwkv_w2,
           rwkv_a0, rwkv_a1, rwkv_a2, rwkv_g1, rwkv_g2, rwkv_k_k, rwkv_k_a, rwkv_r_k, rwkv_ln_w, rwkv_ln_b,
           s5_a_re, s5_a_im, s5_log_dt, s5_b_re, s5_b_im, s5_c_re, s5_c_im, s5_d, s5_glu_w, s5_glu_b,
           gqa_q_norm, gqa_k_norm, norm_ffn_g, dense_w_gate, dense_w_up, dense_w_down,
           moe_router, moe_w_gate, moe_w_up, moe_w_down, final_norm_g):
    batch, seq, d = x.shape
    depth = w_in.shape[0]
    n = batch * seq
    t = jnp.arange(seq, dtype=jnp.int32)
    rope_tabs = _rotary_tables(_rope_angles(t, ROPE_DIMS // 2, ROPE_THETA), N_HEADS)
    ang_axial = jnp.concatenate([_rope_angles(t // GRID_W, HEAD_DIM // 4, AXIAL_THETA),
                                 _rope_angles(t % GRID_W, HEAD_DIM // 4, AXIAL_THETA)], axis=-1)
    axial_tabs = _rotary_tables(ang_axial, N_HEADS)
    x = x.reshape(n, d)
    for l in range(depth):
        w_small = w_in[l, :, :OFF_GATES].astype(BF16)
        w_gates = w_in[l, :, OFF_GATES:].astype(BF16)
        xn, xnb, z = rms_in_proj(x, norm_mix_g[l], w_small)
        zg = matmul_bf16(xnb, w_gates, tm=1024, tn=1024, out_dtype=BF16)
        qa, ka, va, qd, kd, vd = qkv_prep(z, batch, seq, rope_tabs, axial_tabs, gqa_q_norm[l], gqa_k_norm[l])
        ya = dilated_attention(qa, ka, va)
        yd = gqa_attention(qd, kd, vd)
        lw1 = jnp.concatenate([rwkv_w1[l], rwkv_a1[l]], axis=-1).astype(BF16)
        zeros = jnp.zeros_like(rwkv_w2[l])
        lw2 = jnp.concatenate([jnp.concatenate([rwkv_w2[l], zeros], axis=-1),
                               jnp.concatenate([zeros, rwkv_a2[l]], axis=-1)], axis=1).astype(BF16)
        w0a0 = jnp.concatenate([rwkv_w0[l], rwkv_a0[l]], axis=-1)
        fw, bw, bonus, rgate = rwkv_prep(xn, z, seq, rwkv_mu_x[l], rwkv_mu_rkv[l].reshape(2, IN_B), lw1, lw2, w0a0,
                                         rwkv_g1[l].astype(BF16), rwkv_g2[l].astype(BF16),
                                         rwkv_k_k[l], rwkv_k_a[l], rwkv_r_k[l])
        wkv_f, wkv_b = wkv_chunked(fw, bw, batch, seq)
        s5_ops = [_s5_operators(s5_a_re[l, dr], s5_a_im[l, dr], s5_log_dt[l, dr], s5_b_re[l], s5_b_im[l],
                                s5_c_re[l, dr], s5_c_im[l, dr], seq // S5_T, reverse=(dr == 1)) for dr in range(2)]
        s5_y = s5_bidirectional_conv(z, batch, seq, *s5_ops)
        x = merge_branches(x, ya, yd, wkv_f, wkv_b, bonus, rgate, s5_y, z, zg, gate_b[l],
                           w_branch[l].astype(BF16), w_out[l].astype(BF16), rwkv_ln_w[l], rwkv_ln_b[l],
                           s5_d[l], s5_glu_w[l].astype(BF16), s5_glu_b[l], batch, seq)
        i = l // 2
        if l % 2 == 0:
            x = dense_ffn(x, norm_ffn_g[l], dense_w_gate[i].astype(BF16), dense_w_up[i].astype(BF16),
                          dense_w_down[i].astype(BF16))
        else:
            xnb_f, sel, wt = moe_route(x, norm_ffn_g[l], moe_router[i])
            x = moe_ffn(x, xnb_f, sel, wt, moe_w_gate[i].astype(BF16), moe_w_up[i].astype(BF16),
                        moe_w_down[i].astype(BF16))
    return rms_norm(x, final_norm_g).reshape(batch, seq, d)
```

```python
import functools
import math

import jax
import jax.numpy as jnp
from jax import lax
from jax.experimental import pallas as pl
from jax.experimental.pallas import tpu as pltpu

F32 = jnp.float32
BF16 = jnp.bfloat16

D_MODEL = 1024
HEAD_DIM = 64
BRANCH_WIDTH = 256
N_BRANCHES = 4
N_HEADS = BRANCH_WIDTH // HEAD_DIM
DILATED_PATTERNS = ((128, 1), (512, 4), (2048, 16))
ROPE_THETA = 500000.0
ROPE_DIMS = HEAD_DIM // 4
RWKV_GN_EPS = 64e-5
S5_GROUP_CH = 16
S5_GROUPS = BRANCH_WIDTH // S5_GROUP_CH
S5_STATE = 64
GQA_KV_HEADS = 2
AXIAL_THETA = 10000.0
GRID_W = 64
N_EXPERTS = 8
TOP_K = 2
NORM_EPS = 1e-6
NEG_INF = -1e30

IN_A = 3 * BRANCH_WIDTH
IN_B = 3 * BRANCH_WIDTH
IN_C = BRANCH_WIDTH
IN_DQ = BRANCH_WIDTH
IN_DKV = GQA_KV_HEADS * HEAD_DIM
IN_GATES = N_BRANCHES * D_MODEL
OFF_B = IN_A
OFF_C = OFF_B + IN_B
OFF_DQ = OFF_C + IN_C
OFF_DKV = OFF_DQ + IN_DQ
OFF_GATES = OFF_DKV + 2 * IN_DKV
IN_TOTAL = OFF_GATES + IN_GATES

VMEM_LIMIT_BYTES = 56 * 1024 * 1024


def _params(*semantics):
    return pltpu.CompilerParams(dimension_semantics=semantics, vmem_limit_bytes=VMEM_LIMIT_BYTES)


def _bdot(a, b):
    return jnp.dot(a.astype(BF16), b.astype(BF16), preferred_element_type=F32)


def _bdot_nt(a, b):
    return lax.dot_general(a.astype(BF16), b.astype(BF16), (((1,), (1,)), ((), ())),
                           preferred_element_type=F32)


def _rms(x, g):
    return x * lax.rsqrt(jnp.mean(x * x, axis=-1, keepdims=True) + NORM_EPS) * g


def _rms_in_proj_kernel(x_ref, g_ref, w_ref, xn_ref, xnb_ref, z_ref):
    @pl.when(pl.program_id(1) == 0)
    def _():
        y = _rms(x_ref[...], g_ref[...])
        xn_ref[...] = y
        xnb_ref[...] = y.astype(BF16)

    z_ref[...] = jnp.dot(xnb_ref[...], w_ref[...], preferred_element_type=F32)


def rms_in_proj(x, g, w_bf16, tm=1024, tn=768):
    n, d = x.shape
    nout = w_bf16.shape[1]
    tm = min(tm, n)
    return pl.pallas_call(
        _rms_in_proj_kernel,
        grid=(n // tm, nout // tn),
        in_specs=[pl.BlockSpec((tm, d), lambda i, j: (i, 0)),
                  pl.BlockSpec((1, d), lambda i, j: (0, 0)),
                  pl.BlockSpec((d, tn), lambda i, j: (0, j))],
        out_specs=[pl.BlockSpec((tm, d), lambda i, j: (i, 0)),
                   pl.BlockSpec((tm, d), lambda i, j: (i, 0)),
                   pl.BlockSpec((tm, tn), lambda i, j: (i, j))],
        out_shape=[jax.ShapeDtypeStruct((n, d), F32), jax.ShapeDtypeStruct((n, d), BF16),
                   jax.ShapeDtypeStruct((n, nout), F32)],
        compiler_params=_params("parallel", "arbitrary"),
        name="rms_in_proj",
    )(x, g.reshape(1, d), w_bf16)


def _rotary_tables(pos_angles, n_heads):
    s, n = pos_angles.shape
    pad = HEAD_DIM - 2 * n
    cos = jnp.concatenate([jnp.cos(pos_angles), jnp.cos(pos_angles), jnp.ones((s, pad), F32)], axis=-1)
    zeros_n = jnp.zeros((s, n), F32)
    zeros_p = jnp.zeros((s, pad), F32)
    sin_lo = jnp.concatenate([-jnp.sin(pos_angles), zeros_n, zeros_p], axis=-1)
    sin_hi = jnp.concatenate([zeros_n, jnp.sin(pos_angles), zeros_p], axis=-1)
    return tuple(jnp.tile(t, (1, n_heads)) for t in (cos, sin_lo, sin_hi))


def _rotate(x, cos, sin_lo, sin_hi, n):
    width = x.shape[-1]
    from_above = pltpu.roll(x, width - n, 1)
    from_below = pltpu.roll(x, n, 1)
    return x * cos + from_above * sin_lo + from_below * sin_hi


def _head_sum(x, n_heads):
    lane = lax.broadcasted_iota(jnp.int32, x.shape, 1)
    out = jnp.zeros_like(x)
    for h in range(n_heads):
        in_head = (lane >= h * HEAD_DIM) & (lane < (h + 1) * HEAD_DIM)
        s = jnp.sum(jnp.where(in_head, x, 0.0), axis=-1, keepdims=True)
        out = jnp.where(in_head, s, out)
    return out


def _head_rms(x, g, n_heads):
    ms = _head_sum(x * x, n_heads) * (1.0 / HEAD_DIM)
    return x * lax.rsqrt(ms + NORM_EPS) * g


Q_SCALE = HEAD_DIM ** -0.5 * math.log2(math.e)


def _qkv_prep_kernel(za_ref, zq_ref, zkv_ref, rc_ref, rl_ref, rh_ref, ac_ref, al_ref, ah_ref,
                     qn_ref, kn_ref, qa_ref, ka_ref, va_ref, qd_ref, kd_ref, vd_ref):
    w = BRANCH_WIDTH
    n_rope = ROPE_DIMS // 2
    n_ax = HEAD_DIM // 2
    za = za_ref[...]
    rc, rl, rh = rc_ref[...], rl_ref[...], rh_ref[...]
    qa = _rotate(za[:, :w], rc, rl, rh, n_rope) * Q_SCALE
    ka = _rotate(za[:, w:2 * w], rc, rl, rh, n_rope)
    va = za[:, 2 * w:]
    ac, al, ah = ac_ref[...], al_ref[...], ah_ref[...]
    qd = _rotate(_head_rms(zq_ref[...], qn_ref[...], N_HEADS), ac, al, ah, n_ax) * Q_SCALE
    zkv = zkv_ref[...]
    kw = GQA_KV_HEADS * HEAD_DIM
    kd = _rotate(_head_rms(zkv[:, :kw], kn_ref[...], GQA_KV_HEADS), ac[:, :kw], al[:, :kw], ah[:, :kw], n_ax)
    vd = zkv[:, kw:]
    for h in range(N_HEADS):
        sl = slice(h * HEAD_DIM, (h + 1) * HEAD_DIM)
        qa_ref[0, h] = qa[:, sl].astype(BF16)
        ka_ref[0, h] = ka[:, sl].astype(BF16)
        va_ref[0, h] = va[:, sl].astype(BF16)
        qd_ref[0, h] = qd[:, sl].astype(BF16)
    for h in range(GQA_KV_HEADS):
        sl = slice(h * HEAD_DIM, (h + 1) * HEAD_DIM)
        kd_ref[0, h] = kd[:, sl].astype(BF16)
        vd_ref[0, h] = vd[:, sl].astype(BF16)


def qkv_prep(z, batch, seq, rope_tabs, axial_tabs, q_norm, k_norm, tm=512):
    tm = min(tm, seq)
    nt = seq // tm
    w = BRANCH_WIDTH
    row = lambda b, i: b * nt + i
    tab_spec = pl.BlockSpec((tm, w), lambda b, i: (i, 0))
    head_out = lambda nh: pl.BlockSpec((1, nh, tm, HEAD_DIM), lambda b, i: (b, 0, i, 0))
    head_shape = lambda nh: jax.ShapeDtypeStruct((batch, nh, seq, HEAD_DIM), BF16)
    return pl.pallas_call(
        _qkv_prep_kernel,
        grid=(batch, nt),
        in_specs=[pl.BlockSpec((tm, IN_A), lambda b, i: (row(b, i), 0)),
                  pl.BlockSpec((tm, w), lambda b, i: (row(b, i), OFF_DQ // w)),
                  pl.BlockSpec((tm, w), lambda b, i: (row(b, i), OFF_DKV // w)),
                  tab_spec, tab_spec, tab_spec, tab_spec, tab_spec, tab_spec,
                  pl.BlockSpec((1, w), lambda b, i: (0, 0)),
                  pl.BlockSpec((1, GQA_KV_HEADS * HEAD_DIM), lambda b, i: (0, 0))],
        out_specs=[head_out(N_HEADS), head_out(N_HEADS), head_out(N_HEADS),
                   head_out(N_HEADS), head_out(GQA_KV_HEADS), head_out(GQA_KV_HEADS)],
        out_shape=[head_shape(N_HEADS), head_shape(N_HEADS), head_shape(N_HEADS),
                   head_shape(N_HEADS), head_shape(GQA_KV_HEADS), head_shape(GQA_KV_HEADS)],
        compiler_params=_params("parallel", "parallel"),
        name="qkv_prep",
    )(z, z, z, *rope_tabs, *axial_tabs,
      jnp.tile(q_norm.reshape(1, HEAD_DIM), (1, N_HEADS)),
      jnp.tile(k_norm.reshape(1, HEAD_DIM), (1, GQA_KV_HEADS)))


A_TQ = 1024
A_SUB = 256
A_RADIUS = 64
DILATED_GROUPS = ((1, ((128, 1), (512, 4))), (16, ((128, 1),)))
assert sorted(w * g for g, ps in DILATED_GROUPS for w, _ in ps) == sorted(w for w, _ in DILATED_PATTERNS)
assert all(w // (2 * d) == A_RADIUS for w, d in DILATED_PATTERNS)


def _window_geometry(patterns, sub):
    out = []
    for window, dil in patterns:
        halo = -(-(window // 2) // 128) * 128
        out.append((dil, -halo, sub + 2 * halo))
    return out


def _window_bias(patterns, sub):
    biases = []
    for dil, first, width in _window_geometry(patterns, sub):
        qi = jnp.arange(sub, dtype=jnp.int32)[:, None]
        kj = jnp.arange(width, dtype=jnp.int32)[None, :] + first
        delta = kj - qi
        ok = (jnp.abs(delta) <= A_RADIUS * dil) & ((delta & (dil - 1)) == 0)
        biases.append(jnp.where(ok, 0.0, NEG_INF).astype(F32))
    return biases


def _window_attn_kernel(q_ref, kp_ref, kc_ref, kn_ref, vp_ref, vc_ref, vn_ref, *rest, seq, tq, sub, windows):
    bias_refs = rest[:len(windows)]
    o_ref, k3_ref, v3_ref = rest[len(windows):]
    i = pl.program_id(2)
    k3_ref[0:tq] = kp_ref[0, 0]
    k3_ref[tq:2 * tq] = kc_ref[0, 0]
    k3_ref[2 * tq:3 * tq] = kn_ref[0, 0]
    v3_ref[0:tq] = vp_ref[0, 0]
    v3_ref[tq:2 * tq] = vc_ref[0, 0]
    v3_ref[2 * tq:3 * tq] = vn_ref[0, 0]
    for u in range(tq // sub):
        q = q_ref[0, 0, u * sub:(u + 1) * sub, :]
        scores = []
        for (dil, first, width), b_ref in zip(windows, bias_refs):
            start = tq + u * sub + first
            s = _bdot_nt(q, k3_ref[start:start + width, :]) + b_ref[...]
            kpos = (i - 1) * tq + start + lax.broadcasted_iota(jnp.int32, (1, width), 1)
            s = jnp.where((kpos >= 0) & (kpos < seq), s, NEG_INF)
            scores.append((s, start, width))
        m = functools.reduce(jnp.maximum, [jnp.max(s, axis=-1, keepdims=True) for s, _, _ in scores])
        l = jnp.zeros_like(m)
        acc = jnp.zeros((sub, HEAD_DIM), F32)
        for s, start, width in scores:
            p = jnp.exp2(s - m)
            l = l + jnp.sum(p, axis=-1, keepdims=True)
            acc = acc + _bdot(p, v3_ref[start:start + width, :])
        log_den = jnp.broadcast_to(m + jnp.log2(l), (sub, HEAD_DIM))
        o_ref[0, 0, u * sub:(u + 1) * sub, :] = jnp.concatenate([acc / l, log_den], axis=1)


def window_attention(q, k, v, patterns):
    batch, nh, seq, hd = q.shape
    tq = min(A_TQ, seq)
    sub = min(A_SUB, tq)
    windows = _window_geometry(patterns, sub)
    assert all(-first <= tq for _, first, _ in windows), "the key halo must fit in one neighbouring tile"
    nt = seq // tq
    cur = pl.BlockSpec((1, 1, tq, hd), lambda b, h, i: (b, h, i, 0))
    prev = pl.BlockSpec((1, 1, tq, hd), lambda b, h, i: (b, h, jnp.maximum(i - 1, 0), 0))
    nxt = pl.BlockSpec((1, 1, tq, hd), lambda b, h, i: (b, h, jnp.minimum(i + 1, nt - 1), 0))
    biases = _window_bias(patterns, sub)
    bias_specs = [pl.BlockSpec(b.shape, lambda b_, h, i: (0, 0)) for b in biases]
    return pl.pallas_call(
        functools.partial(_window_attn_kernel, seq=seq, tq=tq, sub=sub, windows=windows),
        grid=(batch, nh, nt),
        in_specs=[cur, prev, cur, nxt, prev, cur, nxt] + bias_specs,
        out_specs=pl.BlockSpec((1, 1, tq, 2 * hd), lambda b, h, i: (b, h, i, 0)),
        out_shape=jax.ShapeDtypeStruct((batch, nh, seq, 2 * hd), F32),
        scratch_shapes=[pltpu.VMEM((3 * tq, hd), BF16), pltpu.VMEM((3 * tq, hd), BF16)],
        compiler_params=_params("parallel", "parallel", "parallel"),
        name="window_attention_x%d" % len(patterns),
    )(q, k, k, k, v, v, v, *biases)


def _to_residues(a, stride):
    b, h, s, w = a.shape
    return a.reshape(b, h, s // stride, stride, w).transpose(0, 3, 1, 2, 4).reshape(b * stride, h, s // stride, w)


def _from_residues(a, stride):
    bs, h, l, w = a.shape
    return a.reshape(bs // stride, stride, h, l, w).transpose(0, 2, 3, 1, 4).reshape(bs // stride, h, l * stride, w)


def dilated_attention(qa, ka, va):
    parts = []
    for stride, patterns in DILATED_GROUPS:
        if stride == 1:
            parts.append(window_attention(qa, ka, va, patterns))
        else:
            part = window_attention(*(_to_residues(a, stride) for a in (qa, ka, va)), patterns)
            parts.append(_from_residues(part, stride))
    return parts


GQA_SUB = 128
GQA_LOOKAHEAD = 4


def _gqa_kernel(q_ref, k_ref, v_ref, o_ref, *scratch, rep, tq):
    j = pl.program_id(3)
    n_sub = tq // GQA_SUB
    blocks = [(r, u) for r in range(rep) for u in range(n_sub)]
    m_refs, l_refs, acc_refs = (scratch[i * len(blocks):(i + 1) * len(blocks)] for i in range(3))

    @pl.when(j == 0)
    def _():
        for m_ref, l_ref, acc_ref in zip(m_refs, l_refs, acc_refs):
            m_ref[...] = jnp.full(m_ref.shape, NEG_INF, F32)
            l_ref[...] = jnp.zeros(l_ref.shape, F32)
            acc_ref[...] = jnp.zeros(acc_ref.shape, F32)

    k = k_ref[0, 0]
    v = v_ref[0, 0]
    def score(block):
        r, u = block
        return _bdot_nt(k, q_ref[0, r, u * GQA_SUB:(u + 1) * GQA_SUB, :])

    scores = [score(b) for b in blocks[:GQA_LOOKAHEAD]]
    for i, (m_ref, l_ref, acc_ref) in enumerate(zip(m_refs, l_refs, acc_refs)):
        s = scores[i]
        if i + GQA_LOOKAHEAD < len(blocks):
            scores.append(score(blocks[i + GQA_LOOKAHEAD]))
        m_prev = m_ref[...]
        m_new = jnp.maximum(m_prev, jnp.max(s, axis=0, keepdims=True))
        alpha = jnp.exp2(m_prev - m_new)
        p = jnp.exp2(s - m_new)
        l_ref[...] = alpha * l_ref[...] + jnp.sum(p, axis=0, keepdims=True)
        pv = lax.dot_general(v, p.astype(BF16), (((0,), (0,)), ((), ())), preferred_element_type=F32)
        acc_ref[...] = alpha * acc_ref[...] + pv
        m_ref[...] = m_new

    @pl.when(j == pl.num_programs(3) - 1)
    def _():
        for (r, u), l_ref, acc_ref in zip(blocks, l_refs, acc_refs):
            o_ref[0, r, :, u * GQA_SUB:(u + 1) * GQA_SUB] = acc_ref[...] / l_ref[...]


def gqa_attention(qd, kd, vd, tq=1024, tk=2048):
    batch, nh, seq, hd = qd.shape
    ng = kd.shape[1]
    rep = nh // ng
    tq = min(tq, seq)
    tk = min(tk, seq)
    n_blocks = rep * (tq // GQA_SUB)
    return pl.pallas_call(
        functools.partial(_gqa_kernel, rep=rep, tq=tq),
        grid=(batch, ng, seq // tq, seq // tk),
        in_specs=[pl.BlockSpec((1, rep, tq, hd), lambda b, g, i, j: (b, g, i, 0)),
                  pl.BlockSpec((1, 1, tk, hd), lambda b, g, i, j: (b, g, j, 0)),
                  pl.BlockSpec((1, 1, tk, hd), lambda b, g, i, j: (b, g, j, 0))],
        out_specs=pl.BlockSpec((1, rep, hd, tq), lambda b, g, i, j: (b, g, 0, i)),
        out_shape=jax.ShapeDtypeStruct((batch, nh, hd, seq), F32),
        scratch_shapes=([pltpu.VMEM((1, GQA_SUB), F32)] * (2 * n_blocks)
                        + [pltpu.VMEM((hd, GQA_SUB), F32)] * n_blocks),
        compiler_params=_params("parallel", "parallel", "parallel", "arbitrary"),
        name="gqa_attention",
    )(qd, kd, vd)


def _matmul_kernel(a_ref, w_ref, o_ref):
    o_ref[...] = jnp.dot(a_ref[...], w_ref[...], preferred_element_type=F32).astype(o_ref.dtype)


def matmul_bf16(a, w, tm, tn, out_dtype=F32):
    n, k = a.shape
    m = w.shape[1]
    tm = min(tm, n)
    return pl.pallas_call(
        _matmul_kernel,
        grid=(n // tm, m // tn),
        in_specs=[pl.BlockSpec((tm, k), lambda i, j: (i, 0)),
                  pl.BlockSpec((k, tn), lambda i, j: (0, j))],
        out_specs=pl.BlockSpec((tm, tn), lambda i, j: (i, j)),
        out_shape=jax.ShapeDtypeStruct((n, m), out_dtype),
        compiler_params=_params("parallel", "arbitrary"),
        name="matmul_bf16",
    )(a, w)


def _sigmoid(x):
    return 1.0 / (1.0 + jnp.exp(-x))


def _softplus(x):
    return jnp.maximum(x, 0.0) + jnp.log(1.0 + jnp.exp(-jnp.abs(x)))


def _shift_rows(x, edge_row, down):
    rows = x.shape[0]
    ridx = lax.broadcasted_iota(jnp.int32, x.shape, 0)
    if down:
        return jnp.where(ridx == 0, edge_row, pltpu.roll(x, 1, 0))
    return jnp.where(ridx == rows - 1, edge_row, pltpu.roll(x, rows - 1, 0))


WKV_FIELDS = 6


def _rwkv_prep_kernel(xn_ref, xp_ref, xq_ref, zb_ref, zp_ref, zq_ref, mux_ref, murkv_ref, lw1_ref, lw2_ref,
                      w0a0_ref, g1_ref, g2_ref, kk_ref, ka_ref, rk_ref,
                      fw_ref, bw_ref, bonus_ref, gate_ref, *, tiles_per_seq):
    w = BRANCH_WIDTH
    i = pl.program_id(0)
    first = (i % tiles_per_seq) == 0
    last = (i % tiles_per_seq) == tiles_per_seq - 1
    xn = xn_ref[...]
    x_shift = (_shift_rows(xn, jnp.where(first, 0.0, xp_ref[7:8, :]), True),
               _shift_rows(xn, jnp.where(last, 0.0, xq_ref[0:1, :]), False))
    zb = zb_ref[...]
    z_prev = _shift_rows(zb, jnp.where(first, 0.0, zp_ref[7:8, :]), True)
    z_next = _shift_rows(zb, jnp.where(last, 0.0, zq_ref[0:1, :]), False)
    mu = murkv_ref[...]
    rkv = zb + mu[0:1] * (z_prev - zb) + mu[1:2] * (z_next - zb)
    r, k, v = rkv[:, :w], rkv[:, w:2 * w], rkv[:, 2 * w:]
    kap = k * kk_ref[...]
    kap = kap * lax.rsqrt(_head_sum(kap * kap, N_HEADS) + 1e-12)
    gate_ref[...] = _bdot(_sigmoid(_bdot(xn, g1_ref[...])), g2_ref[...])
    bonus = jnp.zeros_like(v)
    lora_lane = lax.broadcasted_iota(jnp.int32, (xn.shape[0], lw1_ref.shape[-1]), 1)
    for d, out_ref in enumerate((fw_ref, bw_ref)):
        xd = xn + mux_ref[d:d + 1, :] * (x_shift[d] - xn)
        h = _bdot(xd, lw1_ref[d])
        h = jnp.where(lora_lane < lw1_ref.shape[-1] // 2, jnp.tanh(h), h)
        h = _bdot(h, lw2_ref[d]) + w0a0_ref[d:d + 1, :]
        w_log = -_softplus(-h[:, :w]) - 0.5
        log_decay = -jnp.exp(w_log)
        iclr = _sigmoid(h[:, w:])
        k_d = k * (1.0 + (iclr - 1.0) * ka_ref[...])
        bonus = bonus + _head_sum(r * k_d * rk_ref[...], N_HEADS) * v
        for j, field in enumerate((r, log_decay, k_d, v, kap, iclr * kap)):
            out_ref[:, j * w:(j + 1) * w] = field
    bonus_ref[...] = bonus


def rwkv_prep(xn, z, seq, mu_x, mu_rkv, lw1, lw2, w0a0, g1, g2, k_k, k_a, r_k, tm=512):
    n, d = xn.shape
    w = BRANCH_WIDTH
    tm = min(tm, seq)
    halo = 8
    prev_halo = lambda i: (jnp.maximum(i * (tm // halo) - 1, 0), 0)
    next_halo = lambda i: (jnp.minimum((i + 1) * (tm // halo), n // halo - 1), 0)
    full = lambda a: pl.BlockSpec(a.shape, lambda i: (0,) * a.ndim)
    params = (mu_x, mu_rkv, lw1, lw2, w0a0, g1, g2, k_k.reshape(1, w), k_a.reshape(1, w), r_k.reshape(1, w))
    rows = lambda width: pl.BlockSpec((tm, width), lambda i: (i, 0))
    return pl.pallas_call(
        functools.partial(_rwkv_prep_kernel, tiles_per_seq=seq // tm),
        grid=(n // tm,),
        in_specs=[rows(d), pl.BlockSpec((halo, d), prev_halo), pl.BlockSpec((halo, d), next_halo),
                  pl.BlockSpec((tm, IN_B), lambda i: (i, OFF_B // IN_B)),
                  pl.BlockSpec((halo, IN_B), lambda i: (prev_halo(i)[0], OFF_B // IN_B)),
                  pl.BlockSpec((halo, IN_B), lambda i: (next_halo(i)[0], OFF_B // IN_B))]
                 + [full(p) for p in params],
        out_specs=[rows(WKV_FIELDS * w), rows(WKV_FIELDS * w), rows(w), rows(w)],
        out_shape=[jax.ShapeDtypeStruct((n, WKV_FIELDS * w), F32), jax.ShapeDtypeStruct((n, WKV_FIELDS * w), F32),
                   jax.ShapeDtypeStruct((n, w), F32), jax.ShapeDtypeStruct((n, w), F32)],
        compiler_params=_params("parallel"),
        name="rwkv_prep",
    )(xn, xn, xn, z, z, z, *params)


WKV_CHUNK = 64


def _wkv_constants(reverse):
    c, nh = WKV_CHUNK, N_HEADS
    t = jnp.arange(c)
    before = (t[None, :] > t[:, None]) if reverse else (t[None, :] < t[:, None])
    incl = before | (t[None, :] == t[:, None])
    per_head = lambda m: jnp.kron(jnp.eye(nh, dtype=F32), m.astype(F32))
    return incl.astype(F32), per_head(before), per_head(incl)


def _cumulative_log_decay(tri, logws):
    w = logws[0].shape[1]
    cat = jnp.concatenate(logws, axis=1)
    hi = cat.astype(BF16)
    lo = (cat - hi.astype(F32)).astype(BF16)
    tri = tri.astype(BF16)
    g = jnp.dot(tri, hi, preferred_element_type=F32) + jnp.dot(tri, lo, preferred_element_type=F32)
    return [g[:, i * w:(i + 1) * w] for i in range(len(logws))]


def _wkv_chunks(chains, head_rows, s_mask):
    c, w, nh = WKV_CHUNK, BRANCH_WIDTH, N_HEADS
    hc = nh * c
    every = range(len(chains))

    def per_head(a):
        return jnp.concatenate([jnp.where(head_rows[h:h + 1, :] > 0.0, a, 0.0) for h in range(nh)], axis=0)

    def wide(m):
        return m[0:c] + m[c:2 * c] + m[2 * c:3 * c] + m[3 * c:4 * c]

    gram_lhs, gram_rhs, gam, vs, k_ts, b_ts = [], [], [], [], [], []
    for x, g, s_bd, _, _, _ in chains:
        r, logw, k, v, kap, b = (x[:, j * w:(j + 1) * w] for j in range(WKV_FIELDS))
        gam.append(jnp.exp(g))
        g_inv = jnp.exp(-g)
        kap_t, r_t = kap * jnp.exp(g - logw), r * gam[-1]
        k_ts.append(k * g_inv)
        b_ts.append(b * g_inv)
        vs.append(v)
        gram_lhs.append(jnp.concatenate([per_head(kap_t), per_head(r_t)], axis=0))
        gram_rhs.append(jnp.concatenate([per_head(k_ts[-1]), per_head(b_ts[-1]), s_bd], axis=0))
    gram = [_bdot_nt(gram_lhs[i], gram_rhs[i]) for i in every]
    kk = [gram[i][:hc, :hc] * chains[i][3] for i in every]
    n = [gram[i][:hc, hc:2 * hc] * chains[i][3] for i in every]
    rk = [gram[i][hc:, :hc] * chains[i][4] for i in every]
    rb = [gram[i][hc:, hc:2 * hc] * chains[i][4] for i in every]
    from_kap = [wide(gram[i][:hc, 2 * hc:]) for i in every]
    from_r = [wide(gram[i][hc:, 2 * hc:]) for i in every]
    eye = jnp.where(lax.broadcasted_iota(jnp.int32, (hc, hc), 0) == lax.broadcasted_iota(jnp.int32, (hc, hc), 1),
                    1.0, 0.0)
    inv = [eye - n[i] for i in every]
    power = [_bdot(n[i], n[i]) for i in every]
    levels = c.bit_length() - 2
    for level in range(levels):
        inv = [inv[i] + _bdot(inv[i], power[i]) for i in every]
        if level + 1 < levels:
            power = [_bdot(power[i], power[i]) for i in every]
    v_heads = [per_head(v) for v in vs]
    kk_v = [_bdot(wide(kk[i]), v_heads[i]) for i in every]
    u = [_bdot(wide(inv[i]), per_head(from_kap[i] + kk_v[i])) for i in every]
    y = [from_r[i] + _bdot(jnp.concatenate([wide(rk[i]), -wide(rb[i])], axis=1),
                           jnp.concatenate([v_heads[i], per_head(u[i])], axis=0)) for i in every]
    update = [lax.dot_general(jnp.concatenate([vs[i], u[i]], axis=0).astype(BF16),
                              jnp.concatenate([k_ts[i], -b_ts[i]], axis=0).astype(BF16),
                              (((0,), (0,)), ((), ())), preferred_element_type=F32) for i in every]
    s_new = [(chains[i][2] + update[i] * s_mask) * gam[i][chains[i][5]:chains[i][5] + 1, :] for i in every]
    return y, s_new


def _wkv_chunked_kernel(fw_ref, bw_ref, trif_ref, msf_ref, mif_ref, trib_ref, msb_ref, mib_ref, hr_ref, sm_ref,
                        yf_ref, yb_ref, state_ref, *, batch, rows):
    @pl.when(pl.program_id(0) == 0)
    def _():
        state_ref[...] = jnp.zeros(state_ref.shape, F32)

    c = WKV_CHUNK
    n_chunks = rows // c
    head_rows = hr_ref[...]
    s_mask = sm_ref[...]
    tris = (trif_ref[...], trib_ref[...])
    masks = ((msf_ref[...], mif_ref[...]), (msb_ref[...], mib_ref[...]))
    refs, y_refs = (fw_ref, bw_ref), (yf_ref, yb_ref)
    w = BRANCH_WIDTH

    def body(i, carry):
        bases = (pl.multiple_of(i * c, c), pl.multiple_of((n_chunks - 1 - i) * c, c))
        chains, where = [], []
        for d in range(2):
            xs = [refs[d][bi, pl.ds(bases[d], c), :] for bi in range(batch)]
            gs = _cumulative_log_decay(tris[d], [x[:, w:2 * w] for x in xs])
            for bi in range(batch):
                chains.append((xs[bi], gs[bi], state_ref[2 * bi + d], *masks[d], 0 if d == 1 else c - 1))
                where.append((bi, d))
        ys, states = _wkv_chunks(chains, head_rows, s_mask)
        for (bi, d), y, s_new in zip(where, ys, states):
            y_refs[d][bi, pl.ds(bases[d], c), :] = y
            state_ref[2 * bi + d] = s_new
        return carry

    lax.fori_loop(0, n_chunks, body, 0)


def wkv_chunked(fw, bw, batch, seq, rows=256):
    w = BRANCH_WIDTH
    rows = min(rows, seq)
    nb = seq // rows
    head_of_lane = jnp.arange(w) // HEAD_DIM
    head_rows = (jnp.arange(8)[:, None] == head_of_lane[None, :]).astype(F32)
    s_mask = (head_of_lane[:, None] == head_of_lane[None, :]).astype(F32)
    consts = [*_wkv_constants(False), *_wkv_constants(True), head_rows, s_mask]
    in_f = pl.BlockSpec((batch, rows, WKV_FIELDS * w), lambda c: (0, c, 0))
    in_b = pl.BlockSpec((batch, rows, WKV_FIELDS * w), lambda c: (0, nb - 1 - c, 0))
    out_shape = jax.ShapeDtypeStruct((batch, seq, w), F32)
    yf, yb = pl.pallas_call(
        functools.partial(_wkv_chunked_kernel, batch=batch, rows=rows),
        grid=(nb,),
        in_specs=[in_f, in_b] + [pl.BlockSpec(a.shape, lambda c: (0, 0)) for a in consts],
        out_specs=[pl.BlockSpec((batch, rows, w), lambda c: (0, c, 0)),
                   pl.BlockSpec((batch, rows, w), lambda c: (0, nb - 1 - c, 0))],
        out_shape=[out_shape, out_shape],
        scratch_shapes=[pltpu.VMEM((2 * batch, w, w), F32)],
        compiler_params=_params("arbitrary"),
        name="wkv_chunked",
    )(fw.reshape(batch, seq, -1), bw.reshape(batch, seq, -1), *consts)
    return yf.reshape(batch * seq, w), yb.reshape(batch * seq, w)


S5_SEGMENTS = 8
S5_WIDTH = S5_GROUPS * S5_STATE


def _s5_discretize(a_re, a_im, log_dt, b_re, b_im, c_re, c_im):
    g, p, c = S5_GROUPS, S5_STATE, S5_GROUP_CH
    dt = jnp.exp(log_dt)[:, None]
    mag = jnp.exp(a_re * dt)
    bar_re, bar_im = mag * jnp.cos(a_im * dt), mag * jnp.sin(a_im * dt)
    den = a_re * a_re + a_im * a_im
    f_re = ((bar_re - 1.0) * a_re + bar_im * a_im) / den
    f_im = (bar_im * a_re - (bar_re - 1.0) * a_im) / den
    bb_re = f_re[..., None] * b_re - f_im[..., None] * b_im
    bb_im = f_re[..., None] * b_im + f_im[..., None] * b_re
    eye_g = jnp.eye(g, dtype=F32)
    w_in = jnp.concatenate(
        [jnp.einsum('gpc,gh->gchp', bb, eye_g).reshape(g * c, g * p) for bb in (bb_re, bb_im)], axis=1)
    w_out = jnp.concatenate(
        [jnp.einsum('gcp,gh->gphc', cc, eye_g).reshape(g * p, g * c) for cc in (c_re, -c_im)], axis=0)
    return bar_re.reshape(1, g * p), bar_im.reshape(1, g * p), w_in.astype(BF16), w_out.astype(BF16), dt, a_re, a_im


def _s5_powers(a_re, a_im, dt, count, reverse):
    j = jnp.arange(1, count + 1, dtype=F32)
    if reverse:
        j = j[::-1]
    e = j[:, None, None] * (a_re * dt)[None]
    th = j[:, None, None] * (a_im * dt)[None]
    mag = jnp.exp(e)
    return (mag * jnp.cos(th)).reshape(count, -1), (mag * jnp.sin(th)).reshape(count, -1)


def _s5_scan_kernel(u_ref, win_ref, lre_ref, lim_ref, pre_ref, pim_ref, wout_ref, y_ref,
                    xre_ref, xim_ref, cre_ref, cim_ref, *, reverse, chunk):
    seg = chunk // S5_SEGMENTS
    nw = S5_WIDTH

    @pl.when(pl.program_id(1) == 0)
    def _():
        cre_ref[...] = jnp.zeros(cre_ref.shape, F32)
        cim_ref[...] = jnp.zeros(cim_ref.shape, F32)

    bu = jnp.dot(u_ref[...].astype(BF16), win_ref[...], preferred_element_type=F32)
    xre_ref[...] = bu[:, :nw]
    xim_ref[...] = bu[:, nw:]
    lre = lre_ref[...]
    lim = lim_ref[...]
    group = lambda s: pl.ds(pl.multiple_of(s * S5_SEGMENTS, S5_SEGMENTS), S5_SEGMENTS)

    def local_step(s, carry):
        xr, xi = carry
        rows = group((seg - 1 - s) if reverse else s)
        nr = lre * xr - lim * xi + xre_ref[rows, :]
        ni = lre * xi + lim * xr + xim_ref[rows, :]
        xre_ref[rows, :] = nr
        xim_ref[rows, :] = ni
        return nr, ni

    zero = jnp.zeros((S5_SEGMENTS, nw), F32)
    fin_re, fin_im = lax.fori_loop(0, seg, local_step, (zero, zero))

    full_seg = 0 if reverse else seg - 1
    pl_re, pl_im = pre_ref[full_seg:full_seg + 1, :], pim_ref[full_seg:full_seg + 1, :]
    ir, ii = cre_ref[...], cim_ref[...]
    init_re, init_im = [None] * S5_SEGMENTS, [None] * S5_SEGMENTS
    for j in (range(S5_SEGMENTS - 1, -1, -1) if reverse else range(S5_SEGMENTS)):
        init_re[j], init_im[j] = ir, ii
        fr, fi = fin_re[j:j + 1, :], fin_im[j:j + 1, :]
        ir, ii = fr + (pl_re * ir - pl_im * ii), fi + (pl_re * ii + pl_im * ir)
    cre_ref[...] = ir
    cim_ref[...] = ii
    init_re = jnp.concatenate(init_re, axis=0)
    init_im = jnp.concatenate(init_im, axis=0)

    def correct(s, carry):
        rows = group(s)
        pr, pi = pre_ref[pl.ds(s, 1), :], pim_ref[pl.ds(s, 1), :]
        xre_ref[rows, :] = xre_ref[rows, :] + (pr * init_re - pi * init_im)
        xim_ref[rows, :] = xim_ref[rows, :] + (pr * init_im + pi * init_re)
        return carry

    lax.fori_loop(0, seg, correct, 0)
    y_ref[...] = (jnp.dot(xre_ref[...].astype(BF16), wout_ref[:nw, :], preferred_element_type=F32)
                  + jnp.dot(xim_ref[...].astype(BF16), wout_ref[nw:, :], preferred_element_type=F32))


def _segment_interleave(a, batch, seq, chunk, inverse=False):
    w = a.shape[-1]
    seg = chunk // S5_SEGMENTS
    shape = (batch, seq // chunk, seg, S5_SEGMENTS, w) if inverse else (batch, seq // chunk, S5_SEGMENTS, seg, w)
    return jnp.swapaxes(a.reshape(shape), 2, 3).reshape(batch * seq, w)


def s5_scan(u_interleaved, batch, seq, disc, reverse, chunk):
    lam_re, lam_im, w_in, w_out, dt, a_re, a_im = disc
    w = BRANCH_WIDTH
    nc = seq // chunk
    seg = chunk // S5_SEGMENTS
    pw_re, pw_im = _s5_powers(a_re, a_im, dt, seg, reverse)
    order = (lambda c: nc - 1 - c) if reverse else (lambda c: c)
    full = lambda a: pl.BlockSpec(a.shape, lambda b, c: (0, 0))
    return pl.pallas_call(
        functools.partial(_s5_scan_kernel, reverse=reverse, chunk=chunk),
        grid=(batch, nc),
        in_specs=[pl.BlockSpec((chunk, w), lambda b, c: (b * nc + order(c), 0)),
                  full(w_in), full(lam_re), full(lam_im), full(pw_re), full(pw_im), full(w_out)],
        out_specs=pl.BlockSpec((chunk, w), lambda b, c: (b * nc + order(c), 0)),
        out_shape=jax.ShapeDtypeStruct((batch * seq, w), F32),
        scratch_shapes=[pltpu.VMEM((chunk, S5_WIDTH), F32), pltpu.VMEM((chunk, S5_WIDTH), F32),
                        pltpu.VMEM((1, S5_WIDTH), F32), pltpu.VMEM((1, S5_WIDTH), F32)],
        compiler_params=_params("parallel", "arbitrary"),
        name="s5_scan_bwd" if reverse else "s5_scan_fwd",
    )(u_interleaved, w_in, lam_re, lam_im, pw_re, pw_im, w_out)


def s5_bidirectional(z, batch, seq, discs, chunk=1024):
    chunk = min(chunk, seq)
    u = _segment_interleave(z[:, OFF_C:OFF_C + BRANCH_WIDTH], batch, seq, chunk)
    return [_segment_interleave(s5_scan(u, batch, seq, disc, reverse=(d == 1), chunk=chunk),
                                batch, seq, chunk, inverse=True) for d, disc in enumerate(discs)]


S5_T = 64


def _s5_operators(a_re, a_im, log_dt, b_re, b_im, c_re, c_im, n_chunks, reverse):
    g, p, c, t = S5_GROUPS, S5_STATE, S5_GROUP_CH, S5_T
    dt = jnp.exp(log_dt)[:, None]
    mag = jnp.exp(a_re * dt)
    bar_re, bar_im = mag * jnp.cos(a_im * dt), mag * jnp.sin(a_im * dt)
    den = a_re * a_re + a_im * a_im
    f_re = ((bar_re - 1.0) * a_re + bar_im * a_im) / den
    f_im = (bar_im * a_re - (bar_re - 1.0) * a_im) / den
    bb_re = f_re[..., None] * b_re - f_im[..., None] * b_im
    bb_im = f_re[..., None] * b_im + f_im[..., None] * b_re

    def powers(j):
        j = j.astype(F32)[:, None, None]
        m = jnp.exp(j * (a_re * dt)[None])
        return m * jnp.cos(j * (a_im * dt)[None]), m * jnp.sin(j * (a_im * dt)[None])

    pw_re, pw_im = powers(jnp.arange(t + 1))
    z_re = pw_re[..., None] * bb_re[None] - pw_im[..., None] * bb_im[None]
    z_im = pw_re[..., None] * bb_im[None] + pw_im[..., None] * bb_re[None]
    kern = (jnp.einsum('gcp,jgpd->gcdj', c_re, z_re[:t]) - jnp.einsum('gcp,jgpd->gcdj', c_im, z_im[:t]))
    ramp = jnp.concatenate([kern[..., ::-1], jnp.zeros(kern.shape[:-1] + (t - 1,), F32)], axis=-1)
    if reverse:
        ramp = ramp[..., ::-1]
    tiled = jnp.tile(ramp, (1, 1, 1, t + 1))[..., t - 1:t - 1 + t * (2 * t - 2)]
    toep = tiled.reshape(g, c, c, t, 2 * t - 2)[..., :t]
    toep = toep.transpose(0, 1, 3, 2, 4).reshape(g, c * t, c * t)
    ti = jnp.arange(t)
    steps_in = (t - ti) if reverse else (ti + 1)
    qr, qi = pw_re[steps_in], pw_im[steps_in]
    out_re = c_re[None] * qr[:, :, None, :] - c_im[None] * qi[:, :, None, :]
    out_im = c_re[None] * qi[:, :, None, :] + c_im[None] * qr[:, :, None, :]
    state_out = jnp.concatenate([out_re, -out_im], axis=-1).transpose(1, 2, 0, 3).reshape(g, c * t, 2 * p)
    steps_left = ti if reverse else (t - 1 - ti)
    in_state = jnp.concatenate([z_re[steps_left], z_im[steps_left]], axis=2)
    in_state = in_state.transpose(1, 2, 3, 0).reshape(g, 2 * p, c * t)
    levels = max(1, (n_chunks - 1).bit_length())
    lr, li = powers(t * (2 ** jnp.arange(levels)))
    chunk_pow = jnp.stack([lr, li], axis=1)[..., None]
    return toep.astype(BF16), state_out.astype(BF16), in_state.astype(BF16), chunk_pow


def _s5_chunk_states(fin, pow_ref, n_chunks, reverse):
    p = S5_STATE
    fr, fi = fin[:p], fin[p:]
    lanes = fr.shape[1]
    pos = lax.broadcasted_iota(jnp.int32, fr.shape, 1) % n_chunks
    shift_of = lambda step: (lanes - step) if reverse else step
    reachable = lambda step: (pos < n_chunks - step) if reverse else (pos >= step)
    for level in range(pow_ref.shape[0]):
        step = 2 ** level
        pr, pi = pow_ref[level, 0], pow_ref[level, 1]
        er = pltpu.roll(fr, shift_of(step), 1)
        ei = pltpu.roll(fi, shift_of(step), 1)
        ok = reachable(step)
        fr, fi = (fr + jnp.where(ok, pr * er - pi * ei, 0.0), fi + jnp.where(ok, pr * ei + pi * er, 0.0))
    ok = reachable(1)
    fr = jnp.where(ok, pltpu.roll(fr, shift_of(1), 1), 0.0)
    fi = jnp.where(ok, pltpu.roll(fi, shift_of(1), 1), 0.0)
    return jnp.concatenate([fr, fi], axis=0)


def _s5_conv_kernel(u_ref, mf_ref, sf_ref, ef_ref, pf_ref, mb_ref, sb_ref, eb_ref, pb_ref, y_ref, *, n_chunks):
    u = u_ref[...]
    y = None
    for m_ref, s_ref, e_ref, p_ref, reverse in ((mf_ref, sf_ref, ef_ref, pf_ref, False),
                                                (mb_ref, sb_ref, eb_ref, pb_ref, True)):
        within = jnp.dot(m_ref[0], u, preferred_element_type=F32)
        fin = jnp.dot(e_ref[0], u, preferred_element_type=F32)
        x_in = _s5_chunk_states(fin, p_ref.at[:, :, 0], n_chunks, reverse)
        part = within + _bdot(s_ref[0], x_in)
        y = part if y is None else y + part
    y_ref[...] = y


def s5_bidirectional_conv(z, batch, seq, ops_fwd, ops_bwd):
    g, c, t, w = S5_GROUPS, S5_GROUP_CH, S5_T, BRANCH_WIDTH
    nc = seq // t
    cols = batch * nc
    u = z[:, OFF_C:OFF_C + w].reshape(cols, t, w).transpose(2, 1, 0).reshape(w * t, cols).astype(BF16)
    ops = (*ops_fwd, *ops_bwd)
    spec = lambda a: pl.BlockSpec((1,) + a.shape[1:], lambda i: (i,) + (0,) * (a.ndim - 1))
    pow_spec = lambda a: pl.BlockSpec(a.shape[:2] + (1,) + a.shape[3:], lambda i: (0, 0, i, 0, 0))
    y = pl.pallas_call(
        functools.partial(_s5_conv_kernel, n_chunks=nc),
        grid=(g,),
        in_specs=[pl.BlockSpec((c * t, cols), lambda i: (i, 0))]
                 + [pow_spec(a) if a.ndim == 5 else spec(a) for a in ops],
        out_specs=pl.BlockSpec((c * t, cols), lambda i: (i, 0)),
        out_shape=jax.ShapeDtypeStruct((w * t, cols), F32),
        compiler_params=_params("parallel"),
        name="s5_conv",
    )(u, *ops)
    return y.reshape(w, t, cols).transpose(2, 1, 0).reshape(batch * seq, w)


def _gelu_tanh(y):
    return 0.5 * y * (1.0 + jnp.tanh(math.sqrt(2.0 / math.pi) * (y + 0.044715 * (y * y * y))))


def _merge_kernel(x_ref, *refs, n_parts):
    ya_refs = refs[:n_parts]
    (yd_ref, wf_ref, wb_ref, bonus_ref, rg_ref, s5_ref, u_ref, zg_ref,
     gb_ref, wbr_ref, wout_ref, lnw_ref, lnb_ref, s5d_ref, gluw_ref, glub_ref, o_ref) = refs[n_parts:]
    w = BRANCH_WIDTH
    heads = []
    for h in range(N_HEADS):
        parts = [ref[0, h] for ref in ya_refs]
        log_den = [p[:, HEAD_DIM:HEAD_DIM + 1] for p in parts]
        top = functools.reduce(jnp.maximum, log_den)
        share = [jnp.exp2(ld - top) for ld in log_den]
        heads.append(sum(s * p[:, :HEAD_DIM] for s, p in zip(share, parts)) / sum(share))
    ya = jnp.concatenate(heads, axis=1)
    ys = wf_ref[...] + wb_ref[...]
    cen = ys - _head_sum(ys, N_HEADS) * (1.0 / HEAD_DIM)
    var = _head_sum(cen * cen, N_HEADS) * (1.0 / HEAD_DIM)
    yb = (cen * lax.rsqrt(var + RWKV_GN_EPS) * lnw_ref[...] + lnb_ref[...] + bonus_ref[...]) * rg_ref[...]
    yc = s5_ref[...] + s5d_ref[...] * u_ref[...]
    h = _bdot(_gelu_tanh(yc), gluw_ref[...]) + glub_ref[...]
    yc = h[:, :w] * _sigmoid(h[:, w:])
    proj_d = sum(lax.dot_general(yd_ref[0, h].astype(BF16), wbr_ref[3, h * HEAD_DIM:(h + 1) * HEAD_DIM, :],
                                 (((0,), (0,)), ((), ())), preferred_element_type=F32) for h in range(N_HEADS))
    merged = jnp.zeros(o_ref.shape, F32)
    for i, proj in enumerate((_bdot(ya, wbr_ref[0]), _bdot(yb, wbr_ref[1]), _bdot(yc, wbr_ref[2]), proj_d)):
        gate = _sigmoid(zg_ref[:, i * D_MODEL:(i + 1) * D_MODEL] + gb_ref[i:i + 1, :])
        merged = merged + gate * proj
    o_ref[...] = x_ref[...] + _bdot(merged, wout_ref[...])


def merge_branches(x, ya_parts, yd, wkv_f, wkv_b, bonus, rgate, s5_y, z, zg, gate_b, w_branch, w_out,
                   ln_w, ln_b, s5_d, glu_w, glu_b, batch, seq, tm=512):
    w = BRANCH_WIDTH
    d = D_MODEL
    tm = min(tm, seq)
    nt = seq // tm
    rows = lambda width, col=0: pl.BlockSpec((tm, width), lambda b, i: (b * nt + i, col))
    heads = pl.BlockSpec((1, N_HEADS, tm, 2 * HEAD_DIM), lambda b, i: (b, 0, i, 0))
    heads_t = pl.BlockSpec((1, N_HEADS, HEAD_DIM, tm), lambda b, i: (b, 0, 0, i))
    full = lambda a: pl.BlockSpec(a.shape, lambda b, i: (0,) * a.ndim)
    params = (gate_b, w_branch, w_out, ln_w.reshape(1, w), ln_b.reshape(1, w), s5_d.reshape(1, w),
              glu_w, glu_b.reshape(1, 2 * w))
    return pl.pallas_call(
        functools.partial(_merge_kernel, n_parts=len(ya_parts)),
        grid=(batch, nt),
        in_specs=[rows(d)] + [heads] * len(ya_parts)
                 + [heads_t, rows(w), rows(w), rows(w), rows(w), rows(w),
                    rows(w, OFF_C // w), rows(N_BRANCHES * d)] + [full(p) for p in params],
        out_specs=rows(d),
        out_shape=jax.ShapeDtypeStruct(x.shape, F32),
        compiler_params=_params("parallel", "parallel"),
        name="merge_branches",
    )(x, *ya_parts, yd, wkv_f, wkv_b, bonus, rgate, s5_y, z, zg, *params)


def _silu(x):
    return x * _sigmoid(x)


def _swiglu_tile(x, wg_ref, wu_ref, wd_ref):
    h = (_silu(jnp.dot(x, wg_ref[...], preferred_element_type=F32))
         * jnp.dot(x, wu_ref[...], preferred_element_type=F32))
    return _bdot(h, wd_ref[...])


def _dense_ffn_kernel(x_ref, g_ref, wg_ref, wu_ref, wd_ref, o_ref, xn_ref):
    @pl.when(pl.program_id(1) == 0)
    def _():
        x = x_ref[...]
        xn_ref[...] = _rms(x, g_ref[...]).astype(BF16)
        o_ref[...] = x

    o_ref[...] += _swiglu_tile(xn_ref[...], wg_ref, wu_ref, wd_ref)


def dense_ffn(x, g, w_gate, w_up, w_down, tm=1024, tf=1408):
    n, d = x.shape
    ff = w_gate.shape[1]
    tm = min(tm, n)
    return pl.pallas_call(
        _dense_ffn_kernel,
        grid=(n // tm, ff // tf),
        in_specs=[pl.BlockSpec((tm, d), lambda i, f: (i, 0)),
                  pl.BlockSpec((1, d), lambda i, f: (0, 0)),
                  pl.BlockSpec((d, tf), lambda i, f: (0, f)),
                  pl.BlockSpec((d, tf), lambda i, f: (0, f)),
                  pl.BlockSpec((tf, d), lambda i, f: (f, 0))],
        out_specs=pl.BlockSpec((tm, d), lambda i, f: (i, 0)),
        out_shape=jax.ShapeDtypeStruct((n, d), F32),
        scratch_shapes=[pltpu.VMEM((tm, d), BF16)],
        compiler_params=_params("parallel", "arbitrary"),
        name="dense_ffn",
    )(x, g.reshape(1, d), w_gate, w_up, w_down)


def _router_kernel(x_ref, g_ref, rt_ref, xnb_ref, sel_ref, wt_ref):
    xn = _rms(x_ref[...], g_ref[...])
    xnb_ref[...] = xn.astype(BF16)
    logits = lax.dot_general(rt_ref[...], xn, (((1,), (1,)), ((), ())),
                             precision=lax.Precision.HIGHEST, preferred_element_type=F32)
    e = lax.broadcasted_iota(jnp.int32, logits.shape, 0)
    m1 = jnp.max(logits, axis=0, keepdims=True)
    i1 = jnp.min(jnp.where(logits == m1, e, N_EXPERTS), axis=0, keepdims=True)
    rest = jnp.where(e == i1, NEG_INF, logits)
    m2 = jnp.max(rest, axis=0, keepdims=True)
    i2 = jnp.min(jnp.where(rest == m2, e, N_EXPERTS), axis=0, keepdims=True)
    ratio = jnp.exp(m2 - m1)
    w1 = 1.0 / (1.0 + ratio)
    w2 = ratio / (1.0 + ratio)
    sel_ref[...] = jnp.where((e == i1) | (e == i2), 1.0, 0.0)
    wt_ref[...] = jnp.where(e == i1, w1, jnp.where(e == i2, w2, 0.0))


def moe_route(x, g, router, tm=1024):
    n, d = x.shape
    tm = min(tm, n)
    ne = router.shape[1]
    return pl.pallas_call(
        _router_kernel,
        grid=(n // tm,),
        in_specs=[pl.BlockSpec((tm, d), lambda i: (i, 0)),
                  pl.BlockSpec((1, d), lambda i: (0, 0)),
                  pl.BlockSpec((ne, d), lambda i: (0, 0))],
        out_specs=[pl.BlockSpec((tm, d), lambda i: (i, 0)),
                   pl.BlockSpec((ne, tm), lambda i: (0, i)),
                   pl.BlockSpec((ne, tm), lambda i: (0, i))],
        out_shape=[jax.ShapeDtypeStruct((n, d), BF16), jax.ShapeDtypeStruct((ne, n), F32),
                   jax.ShapeDtypeStruct((ne, n), F32)],
        compiler_params=_params("parallel"),
        name="moe_router",
    )(x, g.reshape(1, d), router.T)


MOE_ROWS = 32
MOE_STATIC_BLOCKS = (8, 9, 10)


def _moe_kernel(x_ref, xnb_ref, sel_ref, wt_ref, wg_ref, wu_ref, wd_ref, o_ref,
                rank_ref, xg_ref, acc_ref, nblk_ref):
    e = pl.program_id(1)
    f = pl.program_id(2)
    nf = pl.num_programs(2)
    tm = x_ref.shape[0]

    @pl.when((e == 0) & (f == 0))
    def _():
        o_ref[...] = x_ref[...]
        before = (lax.broadcasted_iota(jnp.int32, (tm, tm), 0) < lax.broadcasted_iota(jnp.int32, (tm, tm), 1))
        rank_ref[...] = jnp.dot(sel_ref[...].astype(BF16), jnp.where(before, 1.0, 0.0).astype(BF16),
                                preferred_element_type=F32)

    sel_e = sel_ref[pl.ds(e, 1), :]
    rank_e = rank_ref[pl.ds(e, 1), :]
    wt_e = wt_ref[pl.ds(e, 1), :]

    @pl.when(f == 0)
    def _():
        count = jnp.sum(sel_e).astype(jnp.int32)
        nblk_ref[0] = (count + MOE_ROWS - 1) // MOE_ROWS

    nblk = nblk_ref[0]

    def process(rows):
        n_rows = rows.stop - rows.start if isinstance(rows, slice) else rows.size
        first = rows.start

        def one_hot():
            slot = (first + lax.broadcasted_iota(jnp.int32, (n_rows, tm), 0)).astype(F32)
            return jnp.where((rank_e == slot) & (sel_e > 0.0), 1.0, 0.0)

        @pl.when(f == 0)
        def _():
            xg_ref[rows, :] = jnp.dot(one_hot().astype(BF16), xnb_ref[...],
                                      preferred_element_type=F32).astype(BF16)
            acc_ref[rows, :] = jnp.zeros((n_rows, acc_ref.shape[1]), F32)

        xg = xg_ref[rows, :]
        acc_ref[rows, :] += _swiglu_tile(xg, wg_ref.at[0], wu_ref.at[0], wd_ref.at[0])

        @pl.when(f == nf - 1)
        def _():
            hot = one_hot()
            row_w = jnp.sum(hot * wt_e, axis=1, keepdims=True)
            yw = (acc_ref[rows, :] * row_w).astype(BF16)
            o_ref[...] += lax.dot_general(hot.astype(BF16), yw, (((0,), (0,)), ((), ())),
                                          preferred_element_type=F32)

    for n_static in MOE_STATIC_BLOCKS:
        lo = 0 if n_static == MOE_STATIC_BLOCKS[0] else n_static
        hi = n_static if n_static != MOE_STATIC_BLOCKS[-1] else tm // MOE_ROWS
        pl.when((nblk >= lo) & (nblk <= hi))(functools.partial(process, slice(0, n_static * MOE_ROWS)))

    def tail(b, carry):
        process(pl.ds(pl.multiple_of(b * MOE_ROWS, MOE_ROWS), MOE_ROWS))
        return carry

    lax.fori_loop(MOE_STATIC_BLOCKS[-1], nblk, tail, 0)


def moe_ffn(x, xnb, sel, wt, w_gate, w_up, w_down, tm=1024, tf=896):
    n, d = x.shape
    ne, _, ff = w_gate.shape
    tm = min(tm, n)
    return pl.pallas_call(
        _moe_kernel,
        grid=(n // tm, ne, ff // tf),
        in_specs=[pl.BlockSpec((tm, d), lambda i, e, f: (i, 0)),
                  pl.BlockSpec((tm, d), lambda i, e, f: (i, 0)),
                  pl.BlockSpec((ne, tm), lambda i, e, f: (0, i)),
                  pl.BlockSpec((ne, tm), lambda i, e, f: (0, i)),
                  pl.BlockSpec((1, d, tf), lambda i, e, f: (e, 0, f)),
                  pl.BlockSpec((1, d, tf), lambda i, e, f: (e, 0, f)),
                  pl.BlockSpec((1, tf, d), lambda i, e, f: (e, f, 0))],
        out_specs=pl.BlockSpec((tm, d), lambda i, e, f: (i, 0)),
        out_shape=jax.ShapeDtypeStruct((n, d), F32),
        scratch_shapes=[pltpu.VMEM((ne, tm), F32), pltpu.VMEM((tm, d), BF16), pltpu.VMEM((tm, d), F32),
                        pltpu.SMEM((1,), jnp.int32)],
        compiler_params=_params("parallel", "arbitrary", "arbitrary"),
        name="moe_ffn",
    )(x, xnb, sel, wt, w_gate, w_up, w_down)


def _rms_kernel(x_ref, g_ref, o_ref):
    o_ref[...] = _rms(x_ref[...], g_ref[...])


def rms_norm(x, g, tm=1024):
    n, d = x.shape
    tm = min(tm, n)
    return pl.pallas_call(
        _rms_kernel,
        grid=(n // tm,),
        in_specs=[pl.BlockSpec((tm, d), lambda i: (i, 0)), pl.BlockSpec((1, d), lambda i: (0, 0))],
        out_specs=pl.BlockSpec((tm, d), lambda i: (i, 0)),
        out_shape=jax.ShapeDtypeStruct((n, d), F32),
        compiler_params=_params("parallel"),
        name="final_rms_norm",
    )(x, g.reshape(1, d))


def _rope_angles(pos, n_freq, theta):
    inv_freq = theta ** (-jnp.arange(n_freq, dtype=F32) / n_freq)
    return pos.astype(F32)[:, None] * inv_freq[None, :]


def kernel(x, norm_mix_g, w_in, gate_b, w_branch, w_out, rwkv_mu_rkv, rwkv_mu_x, rwkv_w0, rwkv_w1, rwkv_w2,
           rwkv_a0, rwkv_a1, rwkv_a2, rwkv_g1, rwkv_g2, rwkv_k_k, rwkv_k_a, rwkv_r_k, rwkv_ln_w, rwkv_ln_b,
           s5_a_re, s5_a_im, s5_log_dt, s5_b_re, s5_b_im, s5_c_re, s5_c_im, s5_d, s5_glu_w, s5_glu_b,
           gqa_q_norm, gqa_k_norm, norm_ffn_g, dense_w_gate, dense_w_up, dense_w_down,
           moe_router, moe_w_gate, moe_w_up, moe_w_down, final_norm_g):
    batch, seq, d = x.shape
    depth = w_in.shape[0]
    n = batch * seq
    t = jnp.arange(seq, dtype=jnp.int32)
    rope_tabs = _rotary_tables(_rope_angles(t, ROPE_DIMS // 2, ROPE_THETA), N_HEADS)
    ang_axial = jnp.concatenate([_rope_angles(t // GRID_W, HEAD_DIM // 4, AXIAL_THETA),
                                 _rope_angles(t % GRID_W, HEAD_DIM // 4, AXIAL_THETA)], axis=-1)
    axial_tabs = _rotary_tables(ang_axial, N_HEADS)
    x = x.reshape(n, d)
    for l in range(depth):
        w_small = w_in[l, :, :OFF_GATES].astype(BF16)
        w_gates = w_in[l, :, OFF_GATES:].astype(BF16)
        xn, xnb, z = rms_in_proj(x, norm_mix_g[l], w_small)
        zg = matmul_bf16(xnb, w_gates, tm=1024, tn=1024, out_dtype=BF16)
        qa, ka, va, qd, kd, vd = qkv_prep(z, batch, seq, rope_tabs, axial_tabs, gqa_q_norm[l], gqa_k_norm[l])
        ya = dilated_attention(qa, ka, va)
        yd = gqa_attention(qd, kd, vd)
        lw1 = jnp.concatenate([rwkv_w1[l], rwkv_a1[l]], axis=-1).astype(BF16)
        zeros = jnp.zeros_like(rwkv_w2[l])
        lw2 = jnp.concatenate([jnp.concatenate([rwkv_w2[l], zeros], axis=-1),
                               jnp.concatenate([zeros, rwkv_a2[l]], axis=-1)], axis=1).astype(BF16)
        w0a0 = jnp.concatenate([rwkv_w0[l], rwkv_a0[l]], axis=-1)
        fw, bw, bonus, rgate = rwkv_prep(xn, z, seq, rwkv_mu_x[l], rwkv_mu_rkv[l].reshape(2, IN_B), lw1, lw2, w0a0,
                                         rwkv_g1[l].astype(BF16), rwkv_g2[l].astype(BF16),
                                         rwkv_k_k[l], rwkv_k_a[l], rwkv_r_k[l])
        wkv_f, wkv_b = wkv_chunked(fw, bw, batch, seq)
        s5_ops = [_s5_operators(s5_a_re[l, dr], s5_a_im[l, dr], s5_log_dt[l, dr], s5_b_re[l], s5_b_im[l],
                                s5_c_re[l, dr], s5_c_im[l, dr], seq // S5_T, reverse=(dr == 1)) for dr in range(2)]
        s5_y = s5_bidirectional_conv(z, batch, seq, *s5_ops)
        x = merge_branches(x, ya, yd, wkv_f, wkv_b, bonus, rgate, s5_y, z, zg, gate_b[l],
                           w_branch[l].astype(BF16), w_out[l].astype(BF16), rwkv_ln_w[l], rwkv_ln_b[l],
                           s5_d[l], s5_glu_w[l].astype(BF16), s5_glu_b[l], batch, seq)
        i = l // 2
        if l % 2 == 0:
            x = dense_ffn(x, norm_ffn_g[l], dense_w_gate[i].astype(BF16), dense_w_up[i].astype(BF16),
                          dense_w_down[i].astype(BF16))
        else:
            xnb_f, sel, wt = moe_route(x, norm_ffn_g[l], moe_router[i])
            x = moe_ffn(x, xnb_f, sel, wt, moe_w_gate[i].astype(BF16), moe_w_up[i].astype(BF16),
                        moe_w_down[i].astype(BF16))
    return rms_norm(x, final_norm_g).reshape(batch, seq, d)
```

```python
import functools
import math

import jax
import jax.numpy as jnp
from jax import lax
from jax.experimental import pallas as pl
from jax.experimental.pallas import tpu as pltpu

F32 = jnp.float32
BF16 = jnp.bfloat16

D_MODEL = 1024
HEAD_DIM = 64
BRANCH_WIDTH = 256
N_BRANCHES = 4
N_HEADS = BRANCH_WIDTH // HEAD_DIM
DILATED_PATTERNS = ((128, 1), (512, 4), (2048, 16))
ROPE_THETA = 500000.0
ROPE_DIMS = HEAD_DIM // 4
RWKV_GN_EPS = 64e-5
S5_GROUP_CH = 16
S5_GROUPS = BRANCH_WIDTH // S5_GROUP_CH
S5_STATE = 64
GQA_KV_HEADS = 2
AXIAL_THETA = 10000.0
GRID_W = 64
N_EXPERTS = 8
TOP_K = 2
NORM_EPS = 1e-6
NEG_INF = -1e30

IN_A = 3 * BRANCH_WIDTH
IN_B = 3 * BRANCH_WIDTH
IN_C = BRANCH_WIDTH
IN_DQ = BRANCH_WIDTH
IN_DKV = GQA_KV_HEADS * HEAD_DIM
IN_GATES = N_BRANCHES * D_MODEL
OFF_B = IN_A
OFF_C = OFF_B + IN_B
OFF_DQ = OFF_C + IN_C
OFF_DKV = OFF_DQ + IN_DQ
OFF_GATES = OFF_DKV + 2 * IN_DKV
IN_TOTAL = OFF_GATES + IN_GATES

VMEM_LIMIT_BYTES = 56 * 1024 * 1024


def _params(*semantics):
    return pltpu.CompilerParams(dimension_semantics=semantics, vmem_limit_bytes=VMEM_LIMIT_BYTES)


def _bdot(a, b):
    return jnp.dot(a.astype(BF16), b.astype(BF16), preferred_element_type=F32)


def _bdot_nt(a, b):
    return lax.dot_general(a.astype(BF16), b.astype(BF16), (((1,), (1,)), ((), ())),
                           preferred_element_type=F32)


def _rms(x, g):
    return x * lax.rsqrt(jnp.mean(x * x, axis=-1, keepdims=True) + NORM_EPS) * g


def _rms_in_proj_kernel(x_ref, g_ref, w_ref, xn_ref, xnb_ref, z_ref):
    @pl.when(pl.program_id(1) == 0)
    def _():
        y = _rms(x_ref[...], g_ref[...])
        xn_ref[...] = y
        xnb_ref[...] = y.astype(BF16)

    z_ref[...] = jnp.dot(xnb_ref[...], w_ref[...], preferred_element_type=F32)


def rms_in_proj(x, g, w_bf16, tm=1024, tn=768):
    n, d = x.shape
    nout = w_bf16.shape[1]
    tm = min(tm, n)
    return pl.pallas_call(
        _rms_in_proj_kernel,
        grid=(n // tm, nout // tn),
        in_specs=[pl.BlockSpec((tm, d), lambda i, j: (i, 0)),
                  pl.BlockSpec((1, d), lambda i, j: (0, 0)),
                  pl.BlockSpec((d, tn), lambda i, j: (0, j))],
        out_specs=[pl.BlockSpec((tm, d), lambda i, j: (i, 0)),
                   pl.BlockSpec((tm, d), lambda i, j: (i, 0)),
                   pl.BlockSpec((tm, tn), lambda i, j: (i, j))],
        out_shape=[jax.ShapeDtypeStruct((n, d), F32), jax.ShapeDtypeStruct((n, d), BF16),
                   jax.ShapeDtypeStruct((n, nout), F32)],
        compiler_params=_params("parallel", "arbitrary"),
        name="rms_in_proj",
    )(x, g.reshape(1, d), w_bf16)


def _rotary_tables(pos_angles, n_heads):
    s, n = pos_angles.shape
    pad = HEAD_DIM - 2 * n
    cos = jnp.concatenate([jnp.cos(pos_angles), jnp.cos(pos_angles), jnp.ones((s, pad), F32)], axis=-1)
    zeros_n = jnp.zeros((s, n), F32)
    zeros_p = jnp.zeros((s, pad), F32)
    sin_lo = jnp.concatenate([-jnp.sin(pos_angles), zeros_n, zeros_p], axis=-1)
    sin_hi = jnp.concatenate([zeros_n, jnp.sin(pos_angles), zeros_p], axis=-1)
    return tuple(jnp.tile(t, (1, n_heads)) for t in (cos, sin_lo, sin_hi))


def _rotate(x, cos, sin_lo, sin_hi, n):
    width = x.shape[-1]
    from_above = pltpu.roll(x, width - n, 1)
    from_below = pltpu.roll(x, n, 1)
    return x * cos + from_above * sin_lo + from_below * sin_hi


def _head_sum(x, n_heads):
    lane = lax.broadcasted_iota(jnp.int32, x.shape, 1)
    out = jnp.zeros_like(x)
    for h in range(n_heads):
        in_head = (lane >= h * HEAD_DIM) & (lane < (h + 1) * HEAD_DIM)
        s = jnp.sum(jnp.where(in_head, x, 0.0), axis=-1, keepdims=True)
        out = jnp.where(in_head, s, out)
    return out


def _head_rms(x, g, n_heads):
    ms = _head_sum(x * x, n_heads) * (1.0 / HEAD_DIM)
    return x * lax.rsqrt(ms + NORM_EPS) * g


Q_SCALE = HEAD_DIM ** -0.5 * math.log2(math.e)


def _qkv_prep_kernel(za_ref, zq_ref, zkv_ref, rc_ref, rl_ref, rh_ref, ac_ref, al_ref, ah_ref,
                     qn_ref, kn_ref, qa_ref, ka_ref, va_ref, qd_ref, kd_ref, vd_ref):
    w = BRANCH_WIDTH
    n_rope = ROPE_DIMS // 2
    n_ax = HEAD_DIM // 2
    za = za_ref[...]
    rc, rl, rh = rc_ref[...], rl_ref[...], rh_ref[...]
    qa = _rotate(za[:, :w], rc, rl, rh, n_rope) * Q_SCALE
    ka = _rotate(za[:, w:2 * w], rc, rl, rh, n_rope)
    va = za[:, 2 * w:]
    ac, al, ah = ac_ref[...], al_ref[...], ah_ref[...]
    qd = _rotate(_head_rms(zq_ref[...], qn_ref[...], N_HEADS), ac, al, ah, n_ax) * Q_SCALE
    zkv = zkv_ref[...]
    kw = GQA_KV_HEADS * HEAD_DIM
    kd = _rotate(_head_rms(zkv[:, :kw], kn_ref[...], GQA_KV_HEADS), ac[:, :kw], al[:, :kw], ah[:, :kw], n_ax)
    vd = zkv[:, kw:]
    for h in range(N_HEADS):
        sl = slice(h * HEAD_DIM, (h + 1) * HEAD_DIM)
        qa_ref[0, h] = qa[:, sl].astype(BF16)
        ka_ref[0, h] = ka[:, sl].astype(BF16)
        va_ref[0, h] = va[:, sl].astype(BF16)
        qd_ref[0, h] = qd[:, sl].astype(BF16)
    for h in range(GQA_KV_HEADS):
        sl = slice(h * HEAD_DIM, (h + 1) * HEAD_DIM)
        kd_ref[0, h] = kd[:, sl].astype(BF16)
        vd_ref[0, h] = vd[:, sl].astype(BF16)


def qkv_prep(z, batch, seq, rope_tabs, axial_tabs, q_norm, k_norm, tm=512):
    tm = min(tm, seq)
    nt = seq // tm
    w = BRANCH_WIDTH
    row = lambda b, i: b * nt + i
    tab_spec = pl.BlockSpec((tm, w), lambda b, i: (i, 0))
    head_out = lambda nh: pl.BlockSpec((1, nh, tm, HEAD_DIM), lambda b, i: (b, 0, i, 0))
    head_shape = lambda nh: jax.ShapeDtypeStruct((batch, nh, seq, HEAD_DIM), BF16)
    return pl.pallas_call(
        _qkv_prep_kernel,
        grid=(batch, nt),
        in_specs=[pl.BlockSpec((tm, IN_A), lambda b, i: (row(b, i), 0)),
                  pl.BlockSpec((tm, w), lambda b, i: (row(b, i), OFF_DQ // w)),
                  pl.BlockSpec((tm, w), lambda b, i: (row(b, i), OFF_DKV // w)),
                  tab_spec, tab_spec, tab_spec, tab_spec, tab_spec, tab_spec,
                  pl.BlockSpec((1, w), lambda b, i: (0, 0)),
                  pl.BlockSpec((1, GQA_KV_HEADS * HEAD_DIM), lambda b, i: (0, 0))],
        out_specs=[head_out(N_HEADS), head_out(N_HEADS), head_out(N_HEADS),
                   head_out(N_HEADS), head_out(GQA_KV_HEADS), head_out(GQA_KV_HEADS)],
        out_shape=[head_shape(N_HEADS), head_shape(N_HEADS), head_shape(N_HEADS),
                   head_shape(N_HEADS), head_shape(GQA_KV_HEADS), head_shape(GQA_KV_HEADS)],
        compiler_params=_params("parallel", "parallel"),
        name="qkv_prep",
    )(z, z, z, *rope_tabs, *axial_tabs,
      jnp.tile(q_norm.reshape(1, HEAD_DIM), (1, N_HEADS)),
      jnp.tile(k_norm.reshape(1, HEAD_DIM), (1, GQA_KV_HEADS)))


A_TQ = 1024
A_SUB = 256
A_RADIUS = 64
DILATED_GROUPS = ((1, ((128, 1), (512, 4))), (16, ((128, 1),)))
assert sorted(w * g for g, ps in DILATED_GROUPS for w, _ in ps) == sorted(w for w, _ in DILATED_PATTERNS)
assert all(w // (2 * d) == A_RADIUS for w, d in DILATED_PATTERNS)


def _window_geometry(patterns, sub):
    out = []
    for window, dil in patterns:
        halo = -(-(window // 2) // 128) * 128
        out.append((dil, -halo, sub + 2 * halo))
    return out


def _window_bias(patterns, sub):
    biases = []
    for dil, first, width in _window_geometry(patterns, sub):
        qi = jnp.arange(sub, dtype=jnp.int32)[:, None]
        kj = jnp.arange(width, dtype=jnp.int32)[None, :] + first
        delta = kj - qi
        ok = (jnp.abs(delta) <= A_RADIUS * dil) & ((delta & (dil - 1)) == 0)
        biases.append(jnp.where(ok, 0.0, NEG_INF).astype(F32))
    return biases


def _window_attn_kernel(q_ref, kp_ref, kc_ref, kn_ref, vp_ref, vc_ref, vn_ref, *rest, seq, tq, sub, windows):
    bias_refs = rest[:len(windows)]
    o_ref, k3_ref, v3_ref = rest[len(windows):]
    i = pl.program_id(2)
    k3_ref[0:tq] = kp_ref[0, 0]
    k3_ref[tq:2 * tq] = kc_ref[0, 0]
    k3_ref[2 * tq:3 * tq] = kn_ref[0, 0]
    v3_ref[0:tq] = vp_ref[0, 0]
    v3_ref[tq:2 * tq] = vc_ref[0, 0]
    v3_ref[2 * tq:3 * tq] = vn_ref[0, 0]
    for u in range(tq // sub):
        q = q_ref[0, 0, u * sub:(u + 1) * sub, :]
        scores = []
        for (dil, first, width), b_ref in zip(windows, bias_refs):
            start = tq + u * sub + first
            s = _bdot_nt(q, k3_ref[start:start + width, :]) + b_ref[...]
            kpos = (i - 1) * tq + start + lax.broadcasted_iota(jnp.int32, (1, width), 1)
            s = jnp.where((kpos >= 0) & (kpos < seq), s, NEG_INF)
            scores.append((s, start, width))
        m = functools.reduce(jnp.maximum, [jnp.max(s, axis=-1, keepdims=True) for s, _, _ in scores])
        l = jnp.zeros_like(m)
        acc = jnp.zeros((sub, HEAD_DIM), F32)
        for s, start, width in scores:
            p = jnp.exp2(s - m)
            l = l + jnp.sum(p, axis=-1, keepdims=True)
            acc = acc + _bdot(p, v3_ref[start:start + width, :])
        log_den = jnp.broadcast_to(m + jnp.log2(l), (sub, HEAD_DIM))
        o_ref[0, 0, u * sub:(u + 1) * sub, :] = jnp.concatenate([acc / l, log_den], axis=1)


def window_attention(q, k, v, patterns):
    batch, nh, seq, hd = q.shape
    tq = min(A_TQ, seq)
    sub = min(A_SUB, tq)
    windows = _window_geometry(patterns, sub)
    assert all(-first <= tq for _, first, _ in windows), "the key halo must fit in one neighbouring tile"
    nt = seq // tq
    cur = pl.BlockSpec((1, 1, tq, hd), lambda b, h, i: (b, h, i, 0))
    prev = pl.BlockSpec((1, 1, tq, hd), lambda b, h, i: (b, h, jnp.maximum(i - 1, 0), 0))
    nxt = pl.BlockSpec((1, 1, tq, hd), lambda b, h, i: (b, h, jnp.minimum(i + 1, nt - 1), 0))
    biases = _window_bias(patterns, sub)
    bias_specs = [pl.BlockSpec(b.shape, lambda b_, h, i: (0, 0)) for b in biases]
    return pl.pallas_call(
        functools.partial(_window_attn_kernel, seq=seq, tq=tq, sub=sub, windows=windows),
        grid=(batch, nh, nt),
        in_specs=[cur, prev, cur, nxt, prev, cur, nxt] + bias_specs,
        out_specs=pl.BlockSpec((1, 1, tq, 2 * hd), lambda b, h, i: (b, h, i, 0)),
        out_shape=jax.ShapeDtypeStruct((batch, nh, seq, 2 * hd), F32),
        scratch_shapes=[pltpu.VMEM((3 * tq, hd), BF16), pltpu.VMEM((3 * tq, hd), BF16)],
        compiler_params=_params("parallel", "parallel", "parallel"),
        name="window_attention_x%d" % len(patterns),
    )(q, k, k, k, v, v, v, *biases)


def _to_residues(a, stride):
    b, h, s, w = a.shape
    return a.reshape(b, h, s // stride, stride, w).transpose(0, 3, 1, 2, 4).reshape(b * stride, h, s // stride, w)


def _from_residues(a, stride):
    bs, h, l, w = a.shape
    return a.reshape(bs // stride, stride, h, l, w).transpose(0, 2, 3, 1, 4).reshape(bs // stride, h, l * stride, w)


def dilated_attention(qa, ka, va):
    parts = []
    for stride, patterns in DILATED_GROUPS:
        if stride == 1:
            parts.append(window_attention(qa, ka, va, patterns))
        else:
            part = window_attention(*(_to_residues(a, stride) for a in (qa, ka, va)), patterns)
            parts.append(_from_residues(part, stride))
    return parts


GQA_SUB = 128
GQA_LOOKAHEAD = 4


def _gqa_kernel(q_ref, k_ref, v_ref, o_ref, *scratch, rep, tq):
    j = pl.program_id(3)
    n_sub = tq // GQA_SUB
    blocks = [(r, u) for r in range(rep) for u in range(n_sub)]
    m_refs, l_refs, acc_refs = (scratch[i * len(blocks):(i + 1) * len(blocks)] for i in range(3))

    @pl.when(j == 0)
    def _():
        for m_ref, l_ref, acc_ref in zip(m_refs, l_refs, acc_refs):
            m_ref[...] = jnp.full(m_ref.shape, NEG_INF, F32)
            l_ref[...] = jnp.zeros(l_ref.shape, F32)
            acc_ref[...] = jnp.zeros(acc_ref.shape, F32)

    k = k_ref[0, 0]
    v = v_ref[0, 0]
    def score(block):
        r, u = block
        return _bdot_nt(k, q_ref[0, r, u * GQA_SUB:(u + 1) * GQA_SUB, :])

    scores = [score(b) for b in blocks[:GQA_LOOKAHEAD]]
    for i, (m_ref, l_ref, acc_ref) in enumerate(zip(m_refs, l_refs, acc_refs)):
        s = scores[i]
        if i + GQA_LOOKAHEAD < len(blocks):
            scores.append(score(blocks[i + GQA_LOOKAHEAD]))
        m_prev = m_ref[...]
        m_new = jnp.maximum(m_prev, jnp.max(s, axis=0, keepdims=True))
        alpha = jnp.exp2(m_prev - m_new)
        p = jnp.exp2(s - m_new)
        l_ref[...] = alpha * l_ref[...] + jnp.sum(p, axis=0, keepdims=True)
        pv = lax.dot_general(v, p.astype(BF16), (((0,), (0,)), ((), ())), preferred_element_type=F32)
        acc_ref[...] = alpha * acc_ref[...] + pv
        m_ref[...] = m_new

    @pl.when(j == pl.num_programs(3) - 1)
    def _():
        for (r, u), l_ref, acc_ref in zip(blocks, l_refs, acc_refs):
            o_ref[0, r, :, u * GQA_SUB:(u + 1) * GQA_SUB] = acc_ref[...] / l_ref[...]


def gqa_attention(qd, kd, vd, tq=1024, tk=2048):
    batch, nh, seq, hd = qd.shape
    ng = kd.shape[1]
    rep = nh // ng
    tq = min(tq, seq)
    tk = min(tk, seq)
    n_blocks = rep * (tq // GQA_SUB)
    return pl.pallas_call(
        functools.partial(_gqa_kernel, rep=rep, tq=tq),
        grid=(batch, ng, seq // tq, seq // tk),
        in_specs=[pl.BlockSpec((1, rep, tq, hd), lambda b, g, i, j: (b, g, i, 0)),
                  pl.BlockSpec((1, 1, tk, hd), lambda b, g, i, j: (b, g, j, 0)),
                  pl.BlockSpec((1, 1, tk, hd), lambda b, g, i, j: (b, g, j, 0))],
        out_specs=pl.BlockSpec((1, rep, hd, tq), lambda b, g, i, j: (b, g, 0, i)),
        out_shape=jax.ShapeDtypeStruct((batch, nh, hd, seq), F32),
        scratch_shapes=([pltpu.VMEM((1, GQA_SUB), F32)] * (2 * n_blocks)
                        + [pltpu.VMEM((hd, GQA_SUB), F32)] * n_blocks),
        compiler_params=_params("parallel", "parallel", "parallel", "arbitrary"),
        name="gqa_attention",
    )(qd, kd, vd)


def _matmul_kernel(a_ref, w_ref, o_ref):
    o_ref[...] = jnp.dot(a_ref[...], w_ref[...], preferred_element_type=F32).astype(o_ref.dtype)


def matmul_bf16(a, w, tm, tn, out_dtype=F32):
    n, k = a.shape
    m = w.shape[1]
    tm = min(tm, n)
    return pl.pallas_call(
        _matmul_kernel,
        grid=(n // tm, m // tn),
        in_specs=[pl.BlockSpec((tm, k), lambda i, j: (i, 0)),
                  pl.BlockSpec((k, tn), lambda i, j: (0, j))],
        out_specs=pl.BlockSpec((tm, tn), lambda i, j: (i, j)),
        out_shape=jax.ShapeDtypeStruct((n, m), out_dtype),
        compiler_params=_params("parallel", "arbitrary"),
        name="matmul_bf16",
    )(a, w)


def _sigmoid(x):
    return 1.0 / (1.0 + jnp.exp(-x))


def _softplus(x):
    return jnp.maximum(x, 0.0) + jnp.log(1.0 + jnp.exp(-jnp.abs(x)))


def _shift_rows(x, edge_row, down):
    rows = x.shape[0]
    ridx = lax.broadcasted_iota(jnp.int32, x.shape, 0)
    if down:
        return jnp.where(ridx == 0, edge_row, pltpu.roll(x, 1, 0))
    return jnp.where(ridx == rows - 1, edge_row, pltpu.roll(x, rows - 1, 0))


WKV_FIELDS = 6


def _rwkv_prep_kernel(xn_ref, xp_ref, xq_ref, zb_ref, zp_ref, zq_ref, mux_ref, murkv_ref, lw1_ref, lw2_ref,
                      w0a0_ref, g1_ref, g2_ref, kk_ref, ka_ref, rk_ref,
                      fw_ref, bw_ref, bonus_ref, gate_ref, *, tiles_per_seq):
    w = BRANCH_WIDTH
    i = pl.program_id(0)
    first = (i % tiles_per_seq) == 0
    last = (i % tiles_per_seq) == tiles_per_seq - 1
    xn = xn_ref[...]
    x_shift = (_shift_rows(xn, jnp.where(first, 0.0, xp_ref[7:8, :]), True),
               _shift_rows(xn, jnp.where(last, 0.0, xq_ref[0:1, :]), False))
    zb = zb_ref[...]
    z_prev = _shift_rows(zb, jnp.where(first, 0.0, zp_ref[7:8, :]), True)
    z_next = _shift_rows(zb, jnp.where(last, 0.0, zq_ref[0:1, :]), False)
    mu = murkv_ref[...]
    rkv = zb + mu[0:1] * (z_prev - zb) + mu[1:2] * (z_next - zb)
    r, k, v = rkv[:, :w], rkv[:, w:2 * w], rkv[:, 2 * w:]
    kap = k * kk_ref[...]
    kap = kap * lax.rsqrt(_head_sum(kap * kap, N_HEADS) + 1e-12)
    gate_ref[...] = _bdot(_sigmoid(_bdot(xn, g1_ref[...])), g2_ref[...])
    bonus = jnp.zeros_like(v)
    lora_lane = lax.broadcasted_iota(jnp.int32, (xn.shape[0], lw1_ref.shape[-1]), 1)
    for d, out_ref in enumerate((fw_ref, bw_ref)):
        xd = xn + mux_ref[d:d + 1, :] * (x_shift[d] - xn)
        h = _bdot(xd, lw1_ref[d])
        h = jnp.where(lora_lane < lw1_ref.shape[-1] // 2, jnp.tanh(h), h)
        h = _bdot(h, lw2_ref[d]) + w0a0_ref[d:d + 1, :]
        w_log = -_softplus(-h[:, :w]) - 0.5
        log_decay = -jnp.exp(w_log)
        iclr = _sigmoid(h[:, w:])
        k_d = k * (1.0 + (iclr - 1.0) * ka_ref[...])
        bonus = bonus + _head_sum(r * k_d * rk_ref[...], N_HEADS) * v
        for j, field in enumerate((r, log_decay, k_d, v, kap, iclr * kap)):
            out_ref[:, j * w:(j + 1) * w] = field
    bonus_ref[...] = bonus


def rwkv_prep(xn, z, seq, mu_x, mu_rkv, lw1, lw2, w0a0, g1, g2, k_k, k_a, r_k, tm=512):
    n, d = xn.shape
    w = BRANCH_WIDTH
    tm = min(tm, seq)
    halo = 8
    prev_halo = lambda i: (jnp.maximum(i * (tm // halo) - 1, 0), 0)
    next_halo = lambda i: (jnp.minimum((i + 1) * (tm // halo), n // halo - 1), 0)
    full = lambda a: pl.BlockSpec(a.shape, lambda i: (0,) * a.ndim)
    params = (mu_x, mu_rkv, lw1, lw2, w0a0, g1, g2, k_k.reshape(1, w), k_a.reshape(1, w), r_k.reshape(1, w))
    rows = lambda width: pl.BlockSpec((tm, width), lambda i: (i, 0))
    return pl.pallas_call(
        functools.partial(_rwkv_prep_kernel, tiles_per_seq=seq // tm),
        grid=(n // tm,),
        in_specs=[rows(d), pl.BlockSpec((halo, d), prev_halo), pl.BlockSpec((halo, d), next_halo),
                  pl.BlockSpec((tm, IN_B), lambda i: (i, OFF_B // IN_B)),
                  pl.BlockSpec((halo, IN_B), lambda i: (prev_halo(i)[0], OFF_B // IN_B)),
                  pl.BlockSpec((halo, IN_B), lambda i: (next_halo(i)[0], OFF_B // IN_B))]
                 + [full(p) for p in params],
        out_specs=[rows(WKV_FIELDS * w), rows(WKV_FIELDS * w), rows(w), rows(w)],
        out_shape=[jax.ShapeDtypeStruct((n, WKV_FIELDS * w), F32), jax.ShapeDtypeStruct((n, WKV_FIELDS * w), F32),
                   jax.ShapeDtypeStruct((n, w), F32), jax.ShapeDtypeStruct((n, w), F32)],
        compiler_params=_params("parallel"),
        name="rwkv_prep",
    )(xn, xn, xn, z, z, z, *params)


WKV_CHUNK = 64


def _wkv_constants(reverse):
    c, nh = WKV_CHUNK, N_HEADS
    t = jnp.arange(c)
    before = (t[None, :] > t[:, None]) if reverse else (t[None, :] < t[:, None])
    incl = before | (t[None, :] == t[:, None])
    per_head = lambda m: jnp.kron(jnp.eye(nh, dtype=F32), m.astype(F32))
    return incl.astype(F32), per_head(before), per_head(incl)


def _cumulative_log_decay(tri, logws):
    w = logws[0].shape[1]
    cat = jnp.concatenate(logws, axis=1)
    hi = cat.astype(BF16)
    lo = (cat - hi.astype(F32)).astype(BF16)
    tri = tri.astype(BF16)
    g = jnp.dot(tri, hi, preferred_element_type=F32) + jnp.dot(tri, lo, preferred_element_type=F32)
    return [g[:, i * w:(i + 1) * w] for i in range(len(logws))]


def _wkv_chunks(chains, head_rows, s_mask):
    c, w, nh = WKV_CHUNK, BRANCH_WIDTH, N_HEADS
    hc = nh * c
    every = range(len(chains))

    def per_head(a):
        return jnp.concatenate([jnp.where(head_rows[h:h + 1, :] > 0.0, a, 0.0) for h in range(nh)], axis=0)

    def wide(m):
        return m[0:c] + m[c:2 * c] + m[2 * c:3 * c] + m[3 * c:4 * c]

    gram_lhs, gram_rhs, gam, vs, k_ts, b_ts = [], [], [], [], [], []
    for x, g, s_bd, _, _, _ in chains:
        r, logw, k, v, kap, b = (x[:, j * w:(j + 1) * w] for j in range(WKV_FIELDS))
        gam.append(jnp.exp(g))
        g_inv = jnp.exp(-g)
        kap_t, r_t = kap * jnp.exp(g - logw), r * gam[-1]
        k_ts.append(k * g_inv)
        b_ts.append(b * g_inv)
        vs.append(v)
        gram_lhs.append(jnp.concatenate([per_head(kap_t), per_head(r_t)], axis=0))
        gram_rhs.append(jnp.concatenate([per_head(k_ts[-1]), per_head(b_ts[-1]), s_bd], axis=0))
    gram = [_bdot_nt(gram_lhs[i], gram_rhs[i]) for i in every]
    kk = [gram[i][:hc, :hc] * chains[i][3] for i in every]
    n = [gram[i][:hc, hc:2 * hc] * chains[i][3] for i in every]
    rk = [gram[i][hc:, :hc] * chains[i][4] for i in every]
    rb = [gram[i][hc:, hc:2 * hc] * chains[i][4] for i in every]
    from_kap = [wide(gram[i][:hc, 2 * hc:]) for i in every]
    from_r = [wide(gram[i][hc:, 2 * hc:]) for i in every]
    eye = jnp.where(lax.broadcasted_iota(jnp.int32, (hc, hc), 0) == lax.broadcasted_iota(jnp.int32, (hc, hc), 1),
                    1.0, 0.0)
    inv = [eye - n[i] for i in every]
    power = [_bdot(n[i], n[i]) for i in every]
    levels = c.bit_length() - 2
    for level in range(levels):
        inv = [inv[i] + _bdot(inv[i], power[i]) for i in every]
        if level + 1 < levels:
            power = [_bdot(power[i], power[i]) for i in every]
    v_heads = [per_head(v) for v in vs]
    kk_v = [_bdot(wide(kk[i]), v_heads[i]) for i in every]
    u = [_bdot(wide(inv[i]), per_head(from_kap[i] + kk_v[i])) for i in every]
    y = [from_r[i] + _bdot(jnp.concatenate([wide(rk[i]), -wide(rb[i])], axis=1),
                           jnp.concatenate([v_heads[i], per_head(u[i])], axis=0)) for i in every]
    update = [lax.dot_general(jnp.concatenate([vs[i], u[i]], axis=0).astype(BF16),
                              jnp.concatenate([k_ts[i], -b_ts[i]], axis=0).astype(BF16),
                              (((0,), (0,)), ((), ())), preferred_element_type=F32) for i in every]
    s_new = [(chains[i][2] + update[i] * s_mask) * gam[i][chains[i][5]:chains[i][5] + 1, :] for i in every]
    return y, s_new


def _wkv_chunked_kernel(fw_ref, bw_ref, trif_ref, msf_ref, mif_ref, trib_ref, msb_ref, mib_ref, hr_ref, sm_ref,
                        yf_ref, yb_ref, state_ref, *, batch, rows):
    @pl.when(pl.program_id(0) == 0)
    def _():
        state_ref[...] = jnp.zeros(state_ref.shape, F32)

    c = WKV_CHUNK
    n_chunks = rows // c
    head_rows = hr_ref[...]
    s_mask = sm_ref[...]
    tris = (trif_ref[...], trib_ref[...])
    masks = ((msf_ref[...], mif_ref[...]), (msb_ref[...], mib_ref[...]))
    refs, y_refs = (fw_ref, bw_ref), (yf_ref, yb_ref)
    w = BRANCH_WIDTH

    def body(i, carry):
        bases = (pl.multiple_of(i * c, c), pl.multiple_of((n_chunks - 1 - i) * c, c))
        chains, where = [], []
        for d in range(2):
            xs = [refs[d][bi, pl.ds(bases[d], c), :] for bi in range(batch)]
            gs = _cumulative_log_decay(tris[d], [x[:, w:2 * w] for x in xs])
            for bi in range(batch):
                chains.append((xs[bi], gs[bi], state_ref[2 * bi + d], *masks[d], 0 if d == 1 else c - 1))
                where.append((bi, d))
        ys, states = _wkv_chunks(chains, head_rows, s_mask)
        for (bi, d), y, s_new in zip(where, ys, states):
            y_refs[d][bi, pl.ds(bases[d], c), :] = y
            state_ref[2 * bi + d] = s_new
        return carry

    lax.fori_loop(0, n_chunks, body, 0)


def wkv_chunked(fw, bw, batch, seq, rows=256):
    w = BRANCH_WIDTH
    rows = min(rows, seq)
    nb = seq // rows
    head_of_lane = jnp.arange(w) // HEAD_DIM
    head_rows = (jnp.arange(8)[:, None] == head_of_lane[None, :]).astype(F32)
    s_mask = (head_of_lane[:, None] == head_of_lane[None, :]).astype(F32)
    consts = [*_wkv_constants(False), *_wkv_constants(True), head_rows, s_mask]
    in_f = pl.BlockSpec((batch, rows, WKV_FIELDS * w), lambda c: (0, c, 0))
    in_b = pl.BlockSpec((batch, rows, WKV_FIELDS * w), lambda c: (0, nb - 1 - c, 0))
    out_shape = jax.ShapeDtypeStruct((batch, seq, w), F32)
    yf, yb = pl.pallas_call(
        functools.partial(_wkv_chunked_kernel, batch=batch, rows=rows),
        grid=(nb,),
        in_specs=[in_f, in_b] + [pl.BlockSpec(a.shape, lambda c: (0, 0)) for a in consts],
        out_specs=[pl.BlockSpec((batch, rows, w), lambda c: (0, c, 0)),
                   pl.BlockSpec((batch, rows, w), lambda c: (0, nb - 1 - c, 0))],
        out_shape=[out_shape, out_shape],
        scratch_shapes=[pltpu.VMEM((2 * batch, w, w), F32)],
        compiler_params=_params("arbitrary"),
        name="wkv_chunked",
    )(fw.reshape(batch, seq, -1), bw.reshape(batch, seq, -1), *consts)
    return yf.reshape(batch * seq, w), yb.reshape(batch * seq, w)


S5_T = 64


def _s5_operators(a_re, a_im, log_dt, b_re, b_im, c_re, c_im, n_chunks, reverse):
    g, p, c, t = S5_GROUPS, S5_STATE, S5_GROUP_CH, S5_T
    dt = jnp.exp(log_dt)[:, None]
    mag = jnp.exp(a_re * dt)
    bar_re, bar_im = mag * jnp.cos(a_im * dt), mag * jnp.sin(a_im * dt)
    den = a_re * a_re + a_im * a_im
    f_re = ((bar_re - 1.0) * a_re + bar_im * a_im) / den
    f_im = (bar_im * a_re - (bar_re - 1.0) * a_im) / den
    bb_re = f_re[..., None] * b_re - f_im[..., None] * b_im
    bb_im = f_re[..., None] * b_im + f_im[..., None] * b_re

    def powers(j):
        j = j.astype(F32)[:, None, None]
        m = jnp.exp(j * (a_re * dt)[None])
        return m * jnp.cos(j * (a_im * dt)[None]), m * jnp.sin(j * (a_im * dt)[None])

    pw_re, pw_im = powers(jnp.arange(t + 1))
    z_re = pw_re[..., None] * bb_re[None] - pw_im[..., None] * bb_im[None]
    z_im = pw_re[..., None] * bb_im[None] + pw_im[..., None] * bb_re[None]
    kern = (jnp.einsum('gcp,jgpd->gcdj', c_re, z_re[:t]) - jnp.einsum('gcp,jgpd->gcdj', c_im, z_im[:t]))
    pad = jnp.zeros(kern.shape[:-1] + (t,), F32)
    ramp = (jnp.concatenate([kern, pad], axis=-1) if reverse
            else jnp.concatenate([kern[..., :1], pad, kern[..., :0:-1]], axis=-1))
    ramp = jnp.where((jnp.arange(c) % 2 == 1)[None, None, :, None], jnp.roll(ramp, t, axis=-1), ramp)
    toep = ramp.reshape(g, c * c, 2 * t)
    ti = jnp.arange(t)
    steps_in = (t - ti) if reverse else (ti + 1)
    qr, qi = pw_re[steps_in], pw_im[steps_in]
    out_re = c_re[None] * qr[:, :, None, :] - c_im[None] * qi[:, :, None, :]
    out_im = c_re[None] * qi[:, :, None, :] + c_im[None] * qr[:, :, None, :]
    state_out = jnp.concatenate([out_re, -out_im], axis=-1).transpose(1, 2, 0, 3).reshape(g, c * t, 2 * p)
    steps_left = ti if reverse else (t - 1 - ti)
    in_state = jnp.concatenate([z_re[steps_left], z_im[steps_left]], axis=2)
    in_state = in_state.transpose(1, 2, 3, 0).reshape(g, 2 * p, c * t)
    levels = max(1, (n_chunks - 1).bit_length())
    lr, li = powers(t * (2 ** jnp.arange(levels)))
    chunk_pow = jnp.stack([lr, li], axis=1)[..., None]
    return toep, state_out.astype(BF16), in_state.astype(BF16), chunk_pow


def _s5_chunk_states(fin, pow_ref, n_chunks, reverse):
    p = S5_STATE
    fr, fi = fin[:p], fin[p:]
    lanes = fr.shape[1]
    pos = lax.broadcasted_iota(jnp.int32, fr.shape, 1) % n_chunks
    shift_of = lambda step: (lanes - step) if reverse else step
    reachable = lambda step: (pos < n_chunks - step) if reverse else (pos >= step)
    for level in range(pow_ref.shape[0]):
        step = 2 ** level
        pr, pi = pow_ref[level, 0], pow_ref[level, 1]
        er = pltpu.roll(fr, shift_of(step), 1)
        ei = pltpu.roll(fi, shift_of(step), 1)
        ok = reachable(step)
        fr, fi = (fr + jnp.where(ok, pr * er - pi * ei, 0.0), fi + jnp.where(ok, pr * ei + pi * er, 0.0))
    ok = reachable(1)
    fr = jnp.where(ok, pltpu.roll(fr, shift_of(1), 1), 0.0)
    fi = jnp.where(ok, pltpu.roll(fi, shift_of(1), 1), 0.0)
    return jnp.concatenate([fr, fi], axis=0)


def _s5_toeplitz(ramp_ref, m_ref):
    c, t = S5_GROUP_CH, S5_T
    low = lax.broadcasted_iota(jnp.int32, (t, 2 * t), 1) < t

    def fill(ci, carry):
        for pair in range(c // 2):
            row = ci * c + 2 * pair
            even = jnp.broadcast_to(ramp_ref[0, pl.ds(row, 1), :], (t, 2 * t))
            odd = jnp.broadcast_to(ramp_ref[0, pl.ds(row + 1, 1), :], (t, 2 * t))
            tile = jnp.where(low, pltpu.roll(even, 0, 1, stride=1, stride_axis=0),
                             pltpu.roll(odd, 0, 1, stride=1, stride_axis=0))
            m_ref[pl.ds(pl.multiple_of(ci * t, t), t), pair * 2 * t:(pair + 1) * 2 * t] = tile.astype(BF16)
        return carry

    lax.fori_loop(0, c, fill, 0)


def _s5_conv_kernel(u_ref, rf_ref, sf_ref, ef_ref, pf_ref, rb_ref, sb_ref, eb_ref, pb_ref, y_ref,
                    mf_ref, mb_ref, *, n_chunks):
    u = u_ref[...]
    y = None
    for r_ref, m_ref, s_ref, e_ref, p_ref, reverse in ((rf_ref, mf_ref, sf_ref, ef_ref, pf_ref, False),
                                                       (rb_ref, mb_ref, sb_ref, eb_ref, pb_ref, True)):
        _s5_toeplitz(r_ref, m_ref)
        within = jnp.dot(m_ref[...], u, preferred_element_type=F32)
        fin = jnp.dot(e_ref[0], u, preferred_element_type=F32)
        x_in = _s5_chunk_states(fin, p_ref.at[:, :, 0], n_chunks, reverse)
        part = within + _bdot(s_ref[0], x_in)
        y = part if y is None else y + part
    y_ref[...] = y


def s5_bidirectional_conv(z, batch, seq, ops_fwd, ops_bwd):
    g, c, t, w = S5_GROUPS, S5_GROUP_CH, S5_T, BRANCH_WIDTH
    nc = seq // t
    cols = batch * nc
    u = z[:, OFF_C:OFF_C + w].reshape(cols, t, w).transpose(2, 1, 0).reshape(w * t, cols).astype(BF16)
    ops = (*ops_fwd, *ops_bwd)
    spec = lambda a: pl.BlockSpec((1,) + a.shape[1:], lambda i: (i,) + (0,) * (a.ndim - 1))
    pow_spec = lambda a: pl.BlockSpec(a.shape[:2] + (1,) + a.shape[3:], lambda i: (0, 0, i, 0, 0))
    y = pl.pallas_call(
        functools.partial(_s5_conv_kernel, n_chunks=nc),
        grid=(g,),
        in_specs=[pl.BlockSpec((c * t, cols), lambda i: (i, 0))]
                 + [pow_spec(a) if a.ndim == 5 else spec(a) for a in ops],
        out_specs=pl.BlockSpec((c * t, cols), lambda i: (i, 0)),
        out_shape=jax.ShapeDtypeStruct((w * t, cols), F32),
        scratch_shapes=[pltpu.VMEM((c * t, c * t), BF16), pltpu.VMEM((c * t, c * t), BF16)],
        compiler_params=_params("parallel"),
        name="s5_conv",
    )(u, *ops)
    return y.reshape(w, t, cols).transpose(2, 1, 0).reshape(batch * seq, w)


def _gelu_tanh(y):
    return 0.5 * y * (1.0 + jnp.tanh(math.sqrt(2.0 / math.pi) * (y + 0.044715 * (y * y * y))))


def _merge_kernel(x_ref, *refs, n_parts):
    ya_refs = refs[:n_parts]
    (yd_ref, wf_ref, wb_ref, bonus_ref, rg_ref, s5_ref, u_ref, zg_ref,
     gb_ref, wbr_ref, wout_ref, lnw_ref, lnb_ref, s5d_ref, gluw_ref, glub_ref, o_ref) = refs[n_parts:]
    w = BRANCH_WIDTH
    heads = []
    for h in range(N_HEADS):
        parts = [ref[0, h] for ref in ya_refs]
        log_den = [p[:, HEAD_DIM:HEAD_DIM + 1] for p in parts]
        top = functools.reduce(jnp.maximum, log_den)
        share = [jnp.exp2(ld - top) for ld in log_den]
        heads.append(sum(s * p[:, :HEAD_DIM] for s, p in zip(share, parts)) / sum(share))
    ya = jnp.concatenate(heads, axis=1)
    ys = wf_ref[...] + wb_ref[...]
    cen = ys - _head_sum(ys, N_HEADS) * (1.0 / HEAD_DIM)
    var = _head_sum(cen * cen, N_HEADS) * (1.0 / HEAD_DIM)
    yb = (cen * lax.rsqrt(var + RWKV_GN_EPS) * lnw_ref[...] + lnb_ref[...] + bonus_ref[...]) * rg_ref[...]
    yc = s5_ref[...] + s5d_ref[...] * u_ref[...]
    h = _bdot(_gelu_tanh(yc), gluw_ref[...]) + glub_ref[...]
    yc = h[:, :w] * _sigmoid(h[:, w:])
    proj_d = sum(lax.dot_general(yd_ref[0, h].astype(BF16), wbr_ref[3, h * HEAD_DIM:(h + 1) * HEAD_DIM, :],
                                 (((0,), (0,)), ((), ())), preferred_element_type=F32) for h in range(N_HEADS))
    merged = jnp.zeros(o_ref.shape, F32)
    for i, proj in enumerate((_bdot(ya, wbr_ref[0]), _bdot(yb, wbr_ref[1]), _bdot(yc, wbr_ref[2]), proj_d)):
        gate = _sigmoid(zg_ref[:, i * D_MODEL:(i + 1) * D_MODEL] + gb_ref[i:i + 1, :])
        merged = merged + gate * proj
    o_ref[...] = x_ref[...] + _bdot(merged, wout_ref[...])


def merge_branches(x, ya_parts, yd, wkv_f, wkv_b, bonus, rgate, s5_y, z, zg, gate_b, w_branch, w_out,
                   ln_w, ln_b, s5_d, glu_w, glu_b, batch, seq, tm=512):
    w = BRANCH_WIDTH
    d = D_MODEL
    tm = min(tm, seq)
    nt = seq // tm
    rows = lambda width, col=0: pl.BlockSpec((tm, width), lambda b, i: (b * nt + i, col))
    heads = pl.BlockSpec((1, N_HEADS, tm, 2 * HEAD_DIM), lambda b, i: (b, 0, i, 0))
    heads_t = pl.BlockSpec((1, N_HEADS, HEAD_DIM, tm), lambda b, i: (b, 0, 0, i))
    full = lambda a: pl.BlockSpec(a.shape, lambda b, i: (0,) * a.ndim)
    params = (gate_b, w_branch, w_out, ln_w.reshape(1, w), ln_b.reshape(1, w), s5_d.reshape(1, w),
              glu_w, glu_b.reshape(1, 2 * w))
    return pl.pallas_call(
        functools.partial(_merge_kernel, n_parts=len(ya_parts)),
        grid=(batch, nt),
        in_specs=[rows(d)] + [heads] * len(ya_parts)
                 + [heads_t, rows(w), rows(w), rows(w), rows(w), rows(w),
                    rows(w, OFF_C // w), rows(N_BRANCHES * d)] + [full(p) for p in params],
        out_specs=rows(d),
        out_shape=jax.ShapeDtypeStruct(x.shape, F32),
        compiler_params=_params("parallel", "parallel"),
        name="merge_branches",
    )(x, *ya_parts, yd, wkv_f, wkv_b, bonus, rgate, s5_y, z, zg, *params)


def _silu(x):
    return x * _sigmoid(x)


def _swiglu_tile(x, wg_ref, wu_ref, wd_ref):
    h = (_silu(jnp.dot(x, wg_ref[...], preferred_element_type=F32))
         * jnp.dot(x, wu_ref[...], preferred_element_type=F32))
    return _bdot(h, wd_ref[...])


def _dense_ffn_kernel(x_ref, g_ref, wg_ref, wu_ref, wd_ref, o_ref, xn_ref):
    @pl.when(pl.program_id(1) == 0)
    def _():
        x = x_ref[...]
        xn_ref[...] = _rms(x, g_ref[...]).astype(BF16)
        o_ref[...] = x

    o_ref[...] += _swiglu_tile(xn_ref[...], wg_ref, wu_ref, wd_ref)


def dense_ffn(x, g, w_gate, w_up, w_down, tm=1024, tf=1408):
    n, d = x.shape
    ff = w_gate.shape[1]
    tm = min(tm, n)
    return pl.pallas_call(
        _dense_ffn_kernel,
        grid=(n // tm, ff // tf),
        in_specs=[pl.BlockSpec((tm, d), lambda i, f: (i, 0)),
                  pl.BlockSpec((1, d), lambda i, f: (0, 0)),
                  pl.BlockSpec((d, tf), lambda i, f: (0, f)),
                  pl.BlockSpec((d, tf), lambda i, f: (0, f)),
                  pl.BlockSpec((tf, d), lambda i, f: (f, 0))],
        out_specs=pl.BlockSpec((tm, d), lambda i, f: (i, 0)),
        out_shape=jax.ShapeDtypeStruct((n, d), F32),
        scratch_shapes=[pltpu.VMEM((tm, d), BF16)],
        compiler_params=_params("parallel", "arbitrary"),
        name="dense_ffn",
    )(x, g.reshape(1, d), w_gate, w_up, w_down)


def _router_kernel(x_ref, g_ref, rt_ref, xnb_ref, sel_ref, wt_ref):
    xn = _rms(x_ref[...], g_ref[...])
    xnb_ref[...] = xn.astype(BF16)
    logits = lax.dot_general(rt_ref[...], xn, (((1,), (1,)), ((), ())),
                             precision=lax.Precision.HIGHEST, preferred_element_type=F32)
    e = lax.broadcasted_iota(jnp.int32, logits.shape, 0)
    m1 = jnp.max(logits, axis=0, keepdims=True)
    i1 = jnp.min(jnp.where(logits == m1, e, N_EXPERTS), axis=0, keepdims=True)
    rest = jnp.where(e == i1, NEG_INF, logits)
    m2 = jnp.max(rest, axis=0, keepdims=True)
    i2 = jnp.min(jnp.where(rest == m2, e, N_EXPERTS), axis=0, keepdims=True)
    ratio = jnp.exp(m2 - m1)
    w1 = 1.0 / (1.0 + ratio)
    w2 = ratio / (1.0 + ratio)
    sel_ref[...] = jnp.where((e == i1) | (e == i2), 1.0, 0.0)
    wt_ref[...] = jnp.where(e == i1, w1, jnp.where(e == i2, w2, 0.0))


def moe_route(x, g, router, tm=1024):
    n, d = x.shape
    tm = min(tm, n)
    ne = router.shape[1]
    return pl.pallas_call(
        _router_kernel,
        grid=(n // tm,),
        in_specs=[pl.BlockSpec((tm, d), lambda i: (i, 0)),
                  pl.BlockSpec((1, d), lambda i: (0, 0)),
                  pl.BlockSpec((ne, d), lambda i: (0, 0))],
        out_specs=[pl.BlockSpec((tm, d), lambda i: (i, 0)),
                   pl.BlockSpec((ne, tm), lambda i: (0, i)),
                   pl.BlockSpec((ne, tm), lambda i: (0, i))],
        out_shape=[jax.ShapeDtypeStruct((n, d), BF16), jax.ShapeDtypeStruct((ne, n), F32),
                   jax.ShapeDtypeStruct((ne, n), F32)],
        compiler_params=_params("parallel"),
        name="moe_router",
    )(x, g.reshape(1, d), router.T)


MOE_ROWS = 32
MOE_STATIC_BLOCKS = (8, 9, 10)


def _moe_kernel(x_ref, xnb_ref, sel_ref, wt_ref, wg_ref, wu_ref, wd_ref, o_ref,
                rank_ref, xg_ref, acc_ref, nblk_ref):
    e = pl.program_id(1)
    f = pl.program_id(2)
    nf = pl.num_programs(2)
    tm = x_ref.shape[0]

    @pl.when((e == 0) & (f == 0))
    def _():
        o_ref[...] = x_ref[...]
        before = (lax.broadcasted_iota(jnp.int32, (tm, tm), 0) < lax.broadcasted_iota(jnp.int32, (tm, tm), 1))
        rank_ref[...] = jnp.dot(sel_ref[...].astype(BF16), jnp.where(before, 1.0, 0.0).astype(BF16),
                                preferred_element_type=F32)

    sel_e = sel_ref[pl.ds(e, 1), :]
    rank_e = rank_ref[pl.ds(e, 1), :]
    wt_e = wt_ref[pl.ds(e, 1), :]

    @pl.when(f == 0)
    def _():
        count = jnp.sum(sel_e).astype(jnp.int32)
        nblk_ref[0] = (count + MOE_ROWS - 1) // MOE_ROWS

    nblk = nblk_ref[0]

    def process(rows):
        n_rows = rows.stop - rows.start if isinstance(rows, slice) else rows.size
        first = rows.start

        def one_hot():
            slot = (first + lax.broadcasted_iota(jnp.int32, (n_rows, tm), 0)).astype(F32)
            return jnp.where((rank_e == slot) & (sel_e > 0.0), 1.0, 0.0)

        @pl.when(f == 0)
        def _():
            xg_ref[rows, :] = jnp.dot(one_hot().astype(BF16), xnb_ref[...],
                                      preferred_element_type=F32).astype(BF16)
            acc_ref[rows, :] = jnp.zeros((n_rows, acc_ref.shape[1]), F32)

        xg = xg_ref[rows, :]
        acc_ref[rows, :] += _swiglu_tile(xg, wg_ref.at[0], wu_ref.at[0], wd_ref.at[0])

        @pl.when(f == nf - 1)
        def _():
            hot = one_hot()
            row_w = jnp.sum(hot * wt_e, axis=1, keepdims=True)
            yw = (acc_ref[rows, :] * row_w).astype(BF16)
            o_ref[...] += lax.dot_general(hot.astype(BF16), yw, (((0,), (0,)), ((), ())),
                                          preferred_element_type=F32)

    for n_static in MOE_STATIC_BLOCKS:
        lo = 0 if n_static == MOE_STATIC_BLOCKS[0] else n_static
        hi = n_static if n_static != MOE_STATIC_BLOCKS[-1] else tm // MOE_ROWS
        pl.when((nblk >= lo) & (nblk <= hi))(functools.partial(process, slice(0, n_static * MOE_ROWS)))

    def tail(b, carry):
        process(pl.ds(pl.multiple_of(b * MOE_ROWS, MOE_ROWS), MOE_ROWS))
        return carry

    lax.fori_loop(MOE_STATIC_BLOCKS[-1], nblk, tail, 0)


def moe_ffn(x, xnb, sel, wt, w_gate, w_up, w_down, tm=1024, tf=896):
    n, d = x.shape
    ne, _, ff = w_gate.shape
    tm = min(tm, n)
    return pl.pallas_call(
        _moe_kernel,
        grid=(n // tm, ne, ff // tf),
        in_specs=[pl.BlockSpec((tm, d), lambda i, e, f: (i, 0)),
                  pl.BlockSpec((tm, d), lambda i, e, f: (i, 0)),
                  pl.BlockSpec((ne, tm), lambda i, e, f: (0, i)),
                  pl.BlockSpec((ne, tm), lambda i, e, f: (0, i)),
                  pl.BlockSpec((1, d, tf), lambda i, e, f: (e, 0, f)),
                  pl.BlockSpec((1, d, tf), lambda i, e, f: (e, 0, f)),
                  pl.BlockSpec((1, tf, d), lambda i, e, f: (e, f, 0))],
        out_specs=pl.BlockSpec((tm, d), lambda i, e, f: (i, 0)),
        out_shape=jax.ShapeDtypeStruct((n, d), F32),
        scratch_shapes=[pltpu.VMEM((ne, tm), F32), pltpu.VMEM((tm, d), BF16), pltpu.VMEM((tm, d), F32),
                        pltpu.SMEM((1,), jnp.int32)],
        compiler_params=_params("parallel", "arbitrary", "arbitrary"),
        name="moe_ffn",
    )(x, xnb, sel, wt, w_gate, w_up, w_down)


def _rms_kernel(x_ref, g_ref, o_ref):
    o_ref[...] = _rms(x_ref[...], g_ref[...])


def rms_norm(x, g, tm=1024):
    n, d = x.shape
    tm = min(tm, n)
    return pl.pallas_call(
        _rms_kernel,
        grid=(n // tm,),
        in_specs=[pl.BlockSpec((tm, d), lambda i: (i, 0)), pl.BlockSpec((1, d), lambda i: (0, 0))],
        out_specs=pl.BlockSpec((tm, d), lambda i: (i, 0)),
        out_shape=jax.ShapeDtypeStruct((n, d), F32),
        compiler_params=_params("parallel"),
        name="final_rms_norm",
    )(x, g.reshape(1, d))


def _rope_angles(pos, n_freq, theta):
    inv_freq = theta ** (-jnp.arange(n_freq, dtype=F32) / n_freq)
    return pos.astype(F32)[:, None] * inv_freq[None, :]


def kernel(x, norm_mix_g, w_in, gate_b, w_branch, w_out, rwkv_mu_rkv, rwkv_mu_x, rwkv_w0, rwkv_w1, rwkv_w2,
           rwkv_a0, rwkv_a1, rwkv_a2, rwkv_g1, rwkv_g2, rwkv_k_k, rwkv_k_a, rwkv_r_k, rwkv_ln_w, rwkv_ln_b,
           s5_a_re, s5_a_im, s5_log_dt, s5_b_re, s5_b_im, s5_c_re, s5_c_im, s5_d, s5_glu_w, s5_glu_b,
           gqa_q_norm, gqa_k_norm, norm_ffn_g, dense_w_gate, dense_w_up, dense_w_down,
           moe_router, moe_w_gate, moe_w_up, moe_w_down, final_norm_g):
    batch, seq, d = x.shape
    depth = w_in.shape[0]
    n = batch * seq
    t = jnp.arange(seq, dtype=jnp.int32)
    rope_tabs = _rotary_tables(_rope_angles(t, ROPE_DIMS // 2, ROPE_THETA), N_HEADS)
    ang_axial = jnp.concatenate([_rope_angles(t // GRID_W, HEAD_DIM // 4, AXIAL_THETA),
                                 _rope_angles(t % GRID_W, HEAD_DIM // 4, AXIAL_THETA)], axis=-1)
    axial_tabs = _rotary_tables(ang_axial, N_HEADS)
    x = x.reshape(n, d)
    for l in range(depth):
        w_small = w_in[l, :, :OFF_GATES].astype(BF16)
        w_gates = w_in[l, :, OFF_GATES:].astype(BF16)
        xn, xnb, z = rms_in_proj(x, norm_mix_g[l], w_small)
        zg = matmul_bf16(xnb, w_gates, tm=1024, tn=1024, out_dtype=BF16)
        qa, ka, va, qd, kd, vd = qkv_prep(z, batch, seq, rope_tabs, axial_tabs, gqa_q_norm[l], gqa_k_norm[l])
        ya = dilated_attention(qa, ka, va)
        yd = gqa_attention(qd, kd, vd)
        lw1 = jnp.concatenate([rwkv_w1[l], rwkv_a1[l]], axis=-1).astype(BF16)
        zeros = jnp.zeros_like(rwkv_w2[l])
        lw2 = jnp.concatenate([jnp.concatenate([rwkv_w2[l], zeros], axis=-1),
                               jnp.concatenate([zeros, rwkv_a2[l]], axis=-1)], axis=1).astype(BF16)
        w0a0 = jnp.concatenate([rwkv_w0[l], rwkv_a0[l]], axis=-1)
        fw, bw, bonus, rgate = rwkv_prep(xn, z, seq, rwkv_mu_x[l], rwkv_mu_rkv[l].reshape(2, IN_B), lw1, lw2, w0a0,
                                         rwkv_g1[l].astype(BF16), rwkv_g2[l].astype(BF16),
                                         rwkv_k_k[l], rwkv_k_a[l], rwkv_r_k[l])
        wkv_f, wkv_b = wkv_chunked(fw, bw, batch, seq)
        s5_ops = [_s5_operators(s5_a_re[l, dr], s5_a_im[l, dr], s5_log_dt[l, dr], s5_b_re[l], s5_b_im[l],
                                s5_c_re[l, dr], s5_c_im[l, dr], seq // S5_T, reverse=(dr == 1)) for dr in range(2)]
        s5_y = s5_bidirectional_conv(z, batch, seq, *s5_ops)
        x = merge_branches(x, ya, yd, wkv_f, wkv_b, bonus, rgate, s5_y, z, zg, gate_b[l],
                           w_branch[l].astype(BF16), w_out[l].astype(BF16), rwkv_ln_w[l], rwkv_ln_b[l],
                           s5_d[l], s5_glu_w[l].astype(BF16), s5_glu_b[l], batch, seq)
        i = l // 2
        if l % 2 == 0:
            x = dense_ffn(x, norm_ffn_g[l], dense_w_gate[i].astype(BF16), dense_w_up[i].astype(BF16),
                          dense_w_down[i].astype(BF16))
        else:
            xnb_f, sel, wt = moe_route(x, norm_ffn_g[l], moe_router[i])
            x = moe_ffn(x, xnb_f, sel, wt, moe_w_gate[i].astype(BF16), moe_w_up[i].astype(BF16),
                        moe_w_down[i].astype(BF16))
    return rms_norm(x, final_norm_g).reshape(batch, seq, d)
```

```python
import functools
import math

import jax
import jax.numpy as jnp
from jax import lax
from jax.experimental import pallas as pl
from jax.experimental.pallas import tpu as pltpu

F32 = jnp.float32
BF16 = jnp.bfloat16

D_MODEL = 1024
HEAD_DIM = 64
BRANCH_WIDTH = 256
N_BRANCHES = 4
N_HEADS = BRANCH_WIDTH // HEAD_DIM
DILATED_PATTERNS = ((128, 1), (512, 4), (2048, 16))
ROPE_THETA = 500000.0
ROPE_DIMS = HEAD_DIM // 4
RWKV_GN_EPS = 64e-5
S5_GROUP_CH = 16
S5_GROUPS = BRANCH_WIDTH // S5_GROUP_CH
S5_STATE = 64
GQA_KV_HEADS = 2
AXIAL_THETA = 10000.0
GRID_W = 64
N_EXPERTS = 8
TOP_K = 2
NORM_EPS = 1e-6
NEG_INF = -1e30

IN_A = 3 * BRANCH_WIDTH
IN_B = 3 * BRANCH_WIDTH
IN_C = BRANCH_WIDTH
IN_DQ = BRANCH_WIDTH
IN_DKV = GQA_KV_HEADS * HEAD_DIM
IN_GATES = N_BRANCHES * D_MODEL
OFF_B = IN_A
OFF_C = OFF_B + IN_B
OFF_DQ = OFF_C + IN_C
OFF_DKV = OFF_DQ + IN_DQ
OFF_GATES = OFF_DKV + 2 * IN_DKV
IN_TOTAL = OFF_GATES + IN_GATES

VMEM_LIMIT_BYTES = 56 * 1024 * 1024


def _params(*semantics):
    return pltpu.CompilerParams(dimension_semantics=semantics, vmem_limit_bytes=VMEM_LIMIT_BYTES)


def _bdot(a, b):
    return jnp.dot(a.astype(BF16), b.astype(BF16), preferred_element_type=F32)


def _bdot_nt(a, b):
    return lax.dot_general(a.astype(BF16), b.astype(BF16), (((1,), (1,)), ((), ())),
                           preferred_element_type=F32)


def _rms(x, g):
    return x * lax.rsqrt(jnp.mean(x * x, axis=-1, keepdims=True) + NORM_EPS) * g


def _rms_in_proj_kernel(x_ref, g_ref, w_ref, xn_ref, xnb_ref, z_ref):
    @pl.when(pl.program_id(1) == 0)
    def _():
        y = _rms(x_ref[...], g_ref[...])
        xn_ref[...] = y
        xnb_ref[...] = y.astype(BF16)

    z_ref[...] = jnp.dot(xnb_ref[...], w_ref[...], preferred_element_type=F32)


def rms_in_proj(x, g, w_bf16, tm=1024, tn=768):
    n, d = x.shape
    nout = w_bf16.shape[1]
    tm = min(tm, n)
    return pl.pallas_call(
        _rms_in_proj_kernel,
        grid=(n // tm, nout // tn),
        in_specs=[pl.BlockSpec((tm, d), lambda i, j: (i, 0)),
                  pl.BlockSpec((1, d), lambda i, j: (0, 0)),
                  pl.BlockSpec((d, tn), lambda i, j: (0, j))],
        out_specs=[pl.BlockSpec((tm, d), lambda i, j: (i, 0)),
                   pl.BlockSpec((tm, d), lambda i, j: (i, 0)),
                   pl.BlockSpec((tm, tn), lambda i, j: (i, j))],
        out_shape=[jax.ShapeDtypeStruct((n, d), F32), jax.ShapeDtypeStruct((n, d), BF16),
                   jax.ShapeDtypeStruct((n, nout), F32)],
        compiler_params=_params("parallel", "arbitrary"),
        name="rms_in_proj",
    )(x, g.reshape(1, d), w_bf16)


def _rotary_tables(pos_angles, n_heads):
    s, n = pos_angles.shape
    pad = HEAD_DIM - 2 * n
    cos = jnp.concatenate([jnp.cos(pos_angles), jnp.cos(pos_angles), jnp.ones((s, pad), F32)], axis=-1)
    zeros_n = jnp.zeros((s, n), F32)
    zeros_p = jnp.zeros((s, pad), F32)
    sin_lo = jnp.concatenate([-jnp.sin(pos_angles), zeros_n, zeros_p], axis=-1)
    sin_hi = jnp.concatenate([zeros_n, jnp.sin(pos_angles), zeros_p], axis=-1)
    return tuple(jnp.tile(t, (1, n_heads)) for t in (cos, sin_lo, sin_hi))


def _rotate(x, cos, sin_lo, sin_hi, n):
    width = x.shape[-1]
    from_above = pltpu.roll(x, width - n, 1)
    from_below = pltpu.roll(x, n, 1)
    return x * cos + from_above * sin_lo + from_below * sin_hi


def _head_sum(x, n_heads):
    lane = lax.broadcasted_iota(jnp.int32, x.shape, 1)
    out = jnp.zeros_like(x)
    for h in range(n_heads):
        in_head = (lane >= h * HEAD_DIM) & (lane < (h + 1) * HEAD_DIM)
        s = jnp.sum(jnp.where(in_head, x, 0.0), axis=-1, keepdims=True)
        out = jnp.where(in_head, s, out)
    return out


def _head_rms(x, g, n_heads):
    ms = _head_sum(x * x, n_heads) * (1.0 / HEAD_DIM)
    return x * lax.rsqrt(ms + NORM_EPS) * g


Q_SCALE = HEAD_DIM ** -0.5 * math.log2(math.e)


def _qkv_prep_kernel(za_ref, zq_ref, zkv_ref, rc_ref, rl_ref, rh_ref, ac_ref, al_ref, ah_ref,
                     qn_ref, kn_ref, qa_ref, ka_ref, va_ref, qd_ref, kd_ref, vd_ref):
    w = BRANCH_WIDTH
    n_rope = ROPE_DIMS // 2
    n_ax = HEAD_DIM // 2
    za = za_ref[...]
    rc, rl, rh = rc_ref[...], rl_ref[...], rh_ref[...]
    qa = _rotate(za[:, :w], rc, rl, rh, n_rope) * Q_SCALE
    ka = _rotate(za[:, w:2 * w], rc, rl, rh, n_rope)
    va = za[:, 2 * w:]
    ac, al, ah = ac_ref[...], al_ref[...], ah_ref[...]
    qd = _rotate(_head_rms(zq_ref[...], qn_ref[...], N_HEADS), ac, al, ah, n_ax) * Q_SCALE
    zkv = zkv_ref[...]
    kw = GQA_KV_HEADS * HEAD_DIM
    kd = _rotate(_head_rms(zkv[:, :kw], kn_ref[...], GQA_KV_HEADS), ac[:, :kw], al[:, :kw], ah[:, :kw], n_ax)
    vd = zkv[:, kw:]
    for h in range(N_HEADS):
        sl = slice(h * HEAD_DIM, (h + 1) * HEAD_DIM)
        qa_ref[0, h] = qa[:, sl].astype(BF16)
        ka_ref[0, h] = ka[:, sl].astype(BF16)
        va_ref[0, h] = va[:, sl].astype(BF16)
        qd_ref[0, h] = qd[:, sl].astype(BF16)
    for h in range(GQA_KV_HEADS):
        sl = slice(h * HEAD_DIM, (h + 1) * HEAD_DIM)
        kd_ref[0, h] = kd[:, sl].astype(BF16)
        vd_ref[0, h] = vd[:, sl].astype(BF16)


def qkv_prep(z, batch, seq, rope_tabs, axial_tabs, q_norm, k_norm, tm=512):
    tm = min(tm, seq)
    nt = seq // tm
    w = BRANCH_WIDTH
    row = lambda b, i: b * nt + i
    tab_spec = pl.BlockSpec((tm, w), lambda b, i: (i, 0))
    head_out = lambda nh: pl.BlockSpec((1, nh, tm, HEAD_DIM), lambda b, i: (b, 0, i, 0))
    head_shape = lambda nh: jax.ShapeDtypeStruct((batch, nh, seq, HEAD_DIM), BF16)
    return pl.pallas_call(
        _qkv_prep_kernel,
        grid=(batch, nt),
        in_specs=[pl.BlockSpec((tm, IN_A), lambda b, i: (row(b, i), 0)),
                  pl.BlockSpec((tm, w), lambda b, i: (row(b, i), OFF_DQ // w)),
                  pl.BlockSpec((tm, w), lambda b, i: (row(b, i), OFF_DKV // w)),
                  tab_spec, tab_spec, tab_spec, tab_spec, tab_spec, tab_spec,
                  pl.BlockSpec((1, w), lambda b, i: (0, 0)),
                  pl.BlockSpec((1, GQA_KV_HEADS * HEAD_DIM), lambda b, i: (0, 0))],
        out_specs=[head_out(N_HEADS), head_out(N_HEADS), head_out(N_HEADS),
                   head_out(N_HEADS), head_out(GQA_KV_HEADS), head_out(GQA_KV_HEADS)],
        out_shape=[head_shape(N_HEADS), head_shape(N_HEADS), head_shape(N_HEADS),
                   head_shape(N_HEADS), head_shape(GQA_KV_HEADS), head_shape(GQA_KV_HEADS)],
        compiler_params=_params("parallel", "parallel"),
        name="qkv_prep",
    )(z, z, z, *rope_tabs, *axial_tabs,
      jnp.tile(q_norm.reshape(1, HEAD_DIM), (1, N_HEADS)),
      jnp.tile(k_norm.reshape(1, HEAD_DIM), (1, GQA_KV_HEADS)))


A_TQ = 1024
A_SUB = 256
A_RADIUS = 64
DILATED_GROUPS = ((1, ((128, 1), (512, 4))), (16, ((128, 1),)))
assert sorted(w * g for g, ps in DILATED_GROUPS for w, _ in ps) == sorted(w for w, _ in DILATED_PATTERNS)
assert all(w // (2 * d) == A_RADIUS for w, d in DILATED_PATTERNS)


def _window_geometry(patterns, sub):
    out = []
    for window, dil in patterns:
        halo = -(-(window // 2) // 128) * 128
        out.append((dil, -halo, sub + 2 * halo))
    return out


def _window_bias(patterns, sub):
    biases = []
    for dil, first, width in _window_geometry(patterns, sub):
        qi = jnp.arange(sub, dtype=jnp.int32)[:, None]
        kj = jnp.arange(width, dtype=jnp.int32)[None, :] + first
        delta = kj - qi
        ok = (jnp.abs(delta) <= A_RADIUS * dil) & ((delta & (dil - 1)) == 0)
        biases.append(jnp.where(ok, 0.0, NEG_INF).astype(F32))
    return biases


def _window_attn_kernel(q_ref, kp_ref, kc_ref, kn_ref, vp_ref, vc_ref, vn_ref, *rest, seq, tq, sub, windows):
    bias_refs = rest[:len(windows)]
    o_ref, k3_ref, v3_ref = rest[len(windows):]
    i = pl.program_id(2)
    k3_ref[0:tq] = kp_ref[0, 0]
    k3_ref[tq:2 * tq] = kc_ref[0, 0]
    k3_ref[2 * tq:3 * tq] = kn_ref[0, 0]
    v3_ref[0:tq] = vp_ref[0, 0]
    v3_ref[tq:2 * tq] = vc_ref[0, 0]
    v3_ref[2 * tq:3 * tq] = vn_ref[0, 0]
    for u in range(tq // sub):
        q = q_ref[0, 0, u * sub:(u + 1) * sub, :]
        scores = []
        for (dil, first, width), b_ref in zip(windows, bias_refs):
            start = tq + u * sub + first
            s = _bdot_nt(q, k3_ref[start:start + width, :]) + b_ref[...]
            kpos = (i - 1) * tq + start + lax.broadcasted_iota(jnp.int32, (1, width), 1)
            s = jnp.where((kpos >= 0) & (kpos < seq), s, NEG_INF)
            scores.append((s, start, width))
        m = functools.reduce(jnp.maximum, [jnp.max(s, axis=-1, keepdims=True) for s, _, _ in scores])
        l = jnp.zeros_like(m)
        acc = jnp.zeros((sub, HEAD_DIM), F32)
        for s, start, width in scores:
            p = jnp.exp2(s - m)
            l = l + jnp.sum(p, axis=-1, keepdims=True)
            acc = acc + _bdot(p, v3_ref[start:start + width, :])
        log_den = jnp.broadcast_to(m + jnp.log2(l), (sub, HEAD_DIM))
        o_ref[0, 0, u * sub:(u + 1) * sub, :] = jnp.concatenate([acc / l, log_den], axis=1)


def window_attention(q, k, v, patterns):
    batch, nh, seq, hd = q.shape
    tq = min(A_TQ, seq)
    sub = min(A_SUB, tq)
    windows = _window_geometry(patterns, sub)
    assert all(-first <= tq for _, first, _ in windows), "the key halo must fit in one neighbouring tile"
    nt = seq // tq
    cur = pl.BlockSpec((1, 1, tq, hd), lambda b, h, i: (b, h, i, 0))
    prev = pl.BlockSpec((1, 1, tq, hd), lambda b, h, i: (b, h, jnp.maximum(i - 1, 0), 0))
    nxt = pl.BlockSpec((1, 1, tq, hd), lambda b, h, i: (b, h, jnp.minimum(i + 1, nt - 1), 0))
    biases = _window_bias(patterns, sub)
    bias_specs = [pl.BlockSpec(b.shape, lambda b_, h, i: (0, 0)) for b in biases]
    return pl.pallas_call(
        functools.partial(_window_attn_kernel, seq=seq, tq=tq, sub=sub, windows=windows),
        grid=(batch, nh, nt),
        in_specs=[cur, prev, cur, nxt, prev, cur, nxt] + bias_specs,
        out_specs=pl.BlockSpec((1, 1, tq, 2 * hd), lambda b, h, i: (b, h, i, 0)),
        out_shape=jax.ShapeDtypeStruct((batch, nh, seq, 2 * hd), F32),
        scratch_shapes=[pltpu.VMEM((3 * tq, hd), BF16), pltpu.VMEM((3 * tq, hd), BF16)],
        compiler_params=_params("parallel", "parallel", "parallel"),
        name="window_attention_x%d" % len(patterns),
    )(q, k, k, k, v, v, v, *biases)


def _to_residues(a, stride):
    b, h, s, w = a.shape
    return a.reshape(b, h, s // stride, stride, w).transpose(0, 3, 1, 2, 4).reshape(b * stride, h, s // stride, w)


def _from_residues(a, stride):
    bs, h, l, w = a.shape
    return a.reshape(bs // stride, stride, h, l, w).transpose(0, 2, 3, 1, 4).reshape(bs // stride, h, l * stride, w)


def dilated_attention(qa, ka, va):
    parts = []
    for stride, patterns in DILATED_GROUPS:
        if stride == 1:
            parts.append(window_attention(qa, ka, va, patterns))
        else:
            part = window_attention(*(_to_residues(a, stride) for a in (qa, ka, va)), patterns)
            parts.append(_from_residues(part, stride))
    return parts


GQA_SUB = 128
GQA_LOOKAHEAD = 4


def _gqa_kernel(q_ref, k_ref, v_ref, o_ref, *scratch, rep, tq):
    j = pl.program_id(3)
    n_sub = tq // GQA_SUB
    blocks = [(r, u) for r in range(rep) for u in range(n_sub)]
    m_refs, l_refs, acc_refs = (scratch[i * len(blocks):(i + 1) * len(blocks)] for i in range(3))

    @pl.when(j == 0)
    def _():
        for m_ref, l_ref, acc_ref in zip(m_refs, l_refs, acc_refs):
            m_ref[...] = jnp.full(m_ref.shape, NEG_INF, F32)
            l_ref[...] = jnp.zeros(l_ref.shape, F32)
            acc_ref[...] = jnp.zeros(acc_ref.shape, F32)

    k = k_ref[0, 0]
    v = v_ref[0, 0]
    def score(block):
        r, u = block
        return _bdot_nt(k, q_ref[0, r, u * GQA_SUB:(u + 1) * GQA_SUB, :])

    scores = [score(b) for b in blocks[:GQA_LOOKAHEAD]]
    for i, (m_ref, l_ref, acc_ref) in enumerate(zip(m_refs, l_refs, acc_refs)):
        s = scores[i]
        if i + GQA_LOOKAHEAD < len(blocks):
            scores.append(score(blocks[i + GQA_LOOKAHEAD]))
        m_prev = m_ref[...]
        m_new = jnp.maximum(m_prev, jnp.max(s, axis=0, keepdims=True))
        alpha = jnp.exp2(m_prev - m_new)
        p = jnp.exp2(s - m_new)
        l_ref[...] = alpha * l_ref[...] + jnp.sum(p, axis=0, keepdims=True)
        pv = lax.dot_general(v, p.astype(BF16), (((0,), (0,)), ((), ())), preferred_element_type=F32)
        acc_ref[...] = alpha * acc_ref[...] + pv
        m_ref[...] = m_new

    @pl.when(j == pl.num_programs(3) - 1)
    def _():
        for (r, u), l_ref, acc_ref in zip(blocks, l_refs, acc_refs):
            o_ref[0, r, :, u * GQA_SUB:(u + 1) * GQA_SUB] = acc_ref[...] / l_ref[...]


def gqa_attention(qd, kd, vd, tq=1024, tk=2048):
    batch, nh, seq, hd = qd.shape
    ng = kd.shape[1]
    rep = nh // ng
    tq = min(tq, seq)
    tk = min(tk, seq)
    n_blocks = rep * (tq // GQA_SUB)
    return pl.pallas_call(
        functools.partial(_gqa_kernel, rep=rep, tq=tq),
        grid=(batch, ng, seq // tq, seq // tk),
        in_specs=[pl.BlockSpec((1, rep, tq, hd), lambda b, g, i, j: (b, g, i, 0)),
                  pl.BlockSpec((1, 1, tk, hd), lambda b, g, i, j: (b, g, j, 0)),
                  pl.BlockSpec((1, 1, tk, hd), lambda b, g, i, j: (b, g, j, 0))],
        out_specs=pl.BlockSpec((1, rep, hd, tq), lambda b, g, i, j: (b, g, 0, i)),
        out_shape=jax.ShapeDtypeStruct((batch, nh, hd, seq), F32),
        scratch_shapes=([pltpu.VMEM((1, GQA_SUB), F32)] * (2 * n_blocks)
                        + [pltpu.VMEM((hd, GQA_SUB), F32)] * n_blocks),
        compiler_params=_params("parallel", "parallel", "parallel", "arbitrary"),
        name="gqa_attention",
    )(qd, kd, vd)


def _matmul_kernel(a_ref, w_ref, o_ref):
    o_ref[...] = jnp.dot(a_ref[...], w_ref[...], preferred_element_type=F32).astype(o_ref.dtype)


def matmul_bf16(a, w, tm, tn, out_dtype=F32):
    n, k = a.shape
    m = w.shape[1]
    tm = min(tm, n)
    return pl.pallas_call(
        _matmul_kernel,
        grid=(n // tm, m // tn),
        in_specs=[pl.BlockSpec((tm, k), lambda i, j: (i, 0)),
                  pl.BlockSpec((k, tn), lambda i, j: (0, j))],
        out_specs=pl.BlockSpec((tm, tn), lambda i, j: (i, j)),
        out_shape=jax.ShapeDtypeStruct((n, m), out_dtype),
        compiler_params=_params("parallel", "arbitrary"),
        name="matmul_bf16",
    )(a, w)


def _sigmoid(x):
    return 1.0 / (1.0 + jnp.exp(-x))


def _softplus(x):
    return jnp.maximum(x, 0.0) + jnp.log(1.0 + jnp.exp(-jnp.abs(x)))


def _shift_rows(x, edge_row, down):
    rows = x.shape[0]
    ridx = lax.broadcasted_iota(jnp.int32, x.shape, 0)
    if down:
        return jnp.where(ridx == 0, edge_row, pltpu.roll(x, 1, 0))
    return jnp.where(ridx == rows - 1, edge_row, pltpu.roll(x, rows - 1, 0))


WKV_FIELDS = 6


def _rwkv_prep_kernel(xn_ref, xp_ref, xq_ref, zb_ref, zp_ref, zq_ref, mux_ref, murkv_ref, lw1_ref, lw2_ref,
                      w0a0_ref, g1_ref, g2_ref, kk_ref, ka_ref, rk_ref,
                      fw_ref, bw_ref, bonus_ref, gate_ref, *, tiles_per_seq):
    w = BRANCH_WIDTH
    i = pl.program_id(0)
    first = (i % tiles_per_seq) == 0
    last = (i % tiles_per_seq) == tiles_per_seq - 1
    xn = xn_ref[...]
    x_shift = (_shift_rows(xn, jnp.where(first, 0.0, xp_ref[7:8, :]), True),
               _shift_rows(xn, jnp.where(last, 0.0, xq_ref[0:1, :]), False))
    zb = zb_ref[...]
    z_prev = _shift_rows(zb, jnp.where(first, 0.0, zp_ref[7:8, :]), True)
    z_next = _shift_rows(zb, jnp.where(last, 0.0, zq_ref[0:1, :]), False)
    mu = murkv_ref[...]
    rkv = zb + mu[0:1] * (z_prev - zb) + mu[1:2] * (z_next - zb)
    r, k, v = rkv[:, :w], rkv[:, w:2 * w], rkv[:, 2 * w:]
    kap = k * kk_ref[...]
    kap = kap * lax.rsqrt(_head_sum(kap * kap, N_HEADS) + 1e-12)
    gate_ref[...] = _bdot(_sigmoid(_bdot(xn, g1_ref[...])), g2_ref[...])
    bonus = jnp.zeros_like(v)
    lora_lane = lax.broadcasted_iota(jnp.int32, (xn.shape[0], lw1_ref.shape[-1]), 1)
    for d, out_ref in enumerate((fw_ref, bw_ref)):
        xd = xn + mux_ref[d:d + 1, :] * (x_shift[d] - xn)
        h = _bdot(xd, lw1_ref[d])
        h = jnp.where(lora_lane < lw1_ref.shape[-1] // 2, jnp.tanh(h), h)
        h = _bdot(h, lw2_ref[d]) + w0a0_ref[d:d + 1, :]
        w_log = -_softplus(-h[:, :w]) - 0.5
        log_decay = -jnp.exp(w_log)
        iclr = _sigmoid(h[:, w:])
        k_d = k * (1.0 + (iclr - 1.0) * ka_ref[...])
        bonus = bonus + _head_sum(r * k_d * rk_ref[...], N_HEADS) * v
        for j, field in enumerate((r, log_decay, k_d, v, kap, iclr * kap)):
            out_ref[:, j * w:(j + 1) * w] = field
    bonus_ref[...] = bonus


def rwkv_prep(xn, z, seq, mu_x, mu_rkv, lw1, lw2, w0a0, g1, g2, k_k, k_a, r_k, tm=512):
    n, d = xn.shape
    w = BRANCH_WIDTH
    tm = min(tm, seq)
    halo = 8
    prev_halo = lambda i: (jnp.maximum(i * (tm // halo) - 1, 0), 0)
    next_halo = lambda i: (jnp.minimum((i + 1) * (tm // halo), n // halo - 1), 0)
    full = lambda a: pl.BlockSpec(a.shape, lambda i: (0,) * a.ndim)
    params = (mu_x, mu_rkv, lw1, lw2, w0a0, g1, g2, k_k.reshape(1, w), k_a.reshape(1, w), r_k.reshape(1, w))
    rows = lambda width: pl.BlockSpec((tm, width), lambda i: (i, 0))
    return pl.pallas_call(
        functools.partial(_rwkv_prep_kernel, tiles_per_seq=seq // tm),
        grid=(n // tm,),
        in_specs=[rows(d), pl.BlockSpec((halo, d), prev_halo), pl.BlockSpec((halo, d), next_halo),
                  pl.BlockSpec((tm, IN_B), lambda i: (i, OFF_B // IN_B)),
                  pl.BlockSpec((halo, IN_B), lambda i: (prev_halo(i)[0], OFF_B // IN_B)),
                  pl.BlockSpec((halo, IN_B), lambda i: (next_halo(i)[0], OFF_B // IN_B))]
                 + [full(p) for p in params],
        out_specs=[rows(WKV_FIELDS * w), rows(WKV_FIELDS * w), rows(w), rows(w)],
        out_shape=[jax.ShapeDtypeStruct((n, WKV_FIELDS * w), F32), jax.ShapeDtypeStruct((n, WKV_FIELDS * w), F32),
                   jax.ShapeDtypeStruct((n, w), F32), jax.ShapeDtypeStruct((n, w), F32)],
        compiler_params=_params("parallel"),
        name="rwkv_prep",
    )(xn, xn, xn, z, z, z, *params)


WKV_CHUNK = 64


def _wkv_constants(reverse):
    c, nh = WKV_CHUNK, N_HEADS
    t = jnp.arange(c)
    before = (t[None, :] > t[:, None]) if reverse else (t[None, :] < t[:, None])
    incl = before | (t[None, :] == t[:, None])
    per_head = lambda m: jnp.kron(jnp.eye(nh, dtype=F32), m.astype(F32))
    return incl.astype(F32), per_head(before), per_head(incl)


def _cumulative_log_decay(tri, logws):
    w = logws[0].shape[1]
    cat = jnp.concatenate(logws, axis=1)
    hi = cat.astype(BF16)
    lo = (cat - hi.astype(F32)).astype(BF16)
    tri = tri.astype(BF16)
    g = jnp.dot(tri, hi, preferred_element_type=F32) + jnp.dot(tri, lo, preferred_element_type=F32)
    return [g[:, i * w:(i + 1) * w] for i in range(len(logws))]


def _wkv_chunks(chains, head_rows, s_mask):
    c, w, nh = WKV_CHUNK, BRANCH_WIDTH, N_HEADS
    hc = nh * c
    every = range(len(chains))

    def per_head(a):
        return jnp.concatenate([jnp.where(head_rows[h:h + 1, :] > 0.0, a, 0.0) for h in range(nh)], axis=0)

    def wide(m):
        return m[0:c] + m[c:2 * c] + m[2 * c:3 * c] + m[3 * c:4 * c]

    gram_lhs, gram_rhs, gam, vs, k_ts, b_ts = [], [], [], [], [], []
    for x, g, s_bd, _, _, _ in chains:
        r, logw, k, v, kap, b = (x[:, j * w:(j + 1) * w] for j in range(WKV_FIELDS))
        gam.append(jnp.exp(g))
        g_inv = jnp.exp(-g)
        kap_t, r_t = kap * jnp.exp(g - logw), r * gam[-1]
        k_ts.append(k * g_inv)
        b_ts.append(b * g_inv)
        vs.append(v)
        gram_lhs.append(jnp.concatenate([per_head(kap_t), per_head(r_t)], axis=0))
        gram_rhs.append(jnp.concatenate([per_head(k_ts[-1]), per_head(b_ts[-1]), s_bd], axis=0))
    gram = [_bdot_nt(gram_lhs[i], gram_rhs[i]) for i in every]
    kk = [gram[i][:hc, :hc] * chains[i][3] for i in every]
    n = [gram[i][:hc, hc:2 * hc] * chains[i][3] for i in every]
    rk = [gram[i][hc:, :hc] * chains[i][4] for i in every]
    rb = [gram[i][hc:, hc:2 * hc] * chains[i][4] for i in every]
    from_kap = [wide(gram[i][:hc, 2 * hc:]) for i in every]
    from_r = [wide(gram[i][hc:, 2 * hc:]) for i in every]
    eye = jnp.where(lax.broadcasted_iota(jnp.int32, (hc, hc), 0) == lax.broadcasted_iota(jnp.int32, (hc, hc), 1),
                    1.0, 0.0)
    inv = [eye - n[i] for i in every]
    power = [_bdot(n[i], n[i]) for i in every]
    levels = c.bit_length() - 2
    for level in range(levels):
        inv = [inv[i] + _bdot(inv[i], power[i]) for i in every]
        if level + 1 < levels:
            power = [_bdot(power[i], power[i]) for i in every]
    v_heads = [per_head(v) for v in vs]
    kk_v = [_bdot(wide(kk[i]), v_heads[i]) for i in every]
    u = [_bdot(wide(inv[i]), per_head(from_kap[i] + kk_v[i])) for i in every]
    y = [from_r[i] + _bdot(jnp.concatenate([wide(rk[i]), -wide(rb[i])], axis=1),
                           jnp.concatenate([v_heads[i], per_head(u[i])], axis=0)) for i in every]
    update = [lax.dot_general(jnp.concatenate([vs[i], u[i]], axis=0).astype(BF16),
                              jnp.concatenate([k_ts[i], -b_ts[i]], axis=0).astype(BF16),
                              (((0,), (0,)), ((), ())), preferred_element_type=F32) for i in every]
    s_new = [(chains[i][2] + update[i] * s_mask) * gam[i][chains[i][5]:chains[i][5] + 1, :] for i in every]
    return y, s_new


def _wkv_chunked_kernel(fw_ref, bw_ref, trif_ref, msf_ref, mif_ref, trib_ref, msb_ref, mib_ref, hr_ref, sm_ref,
                        yf_ref, yb_ref, state_ref, *, batch, rows):
    @pl.when(pl.program_id(0) == 0)
    def _():
        state_ref[...] = jnp.zeros(state_ref.shape, F32)

    c = WKV_CHUNK
    n_chunks = rows // c
    head_rows = hr_ref[...]
    s_mask = sm_ref[...]
    tris = (trif_ref[...], trib_ref[...])
    masks = ((msf_ref[...], mif_ref[...]), (msb_ref[...], mib_ref[...]))
    refs, y_refs = (fw_ref, bw_ref), (yf_ref, yb_ref)
    w = BRANCH_WIDTH

    def body(i, carry):
        bases = (pl.multiple_of(i * c, c), pl.multiple_of((n_chunks - 1 - i) * c, c))
        chains, where = [], []
        for d in range(2):
            xs = [refs[d][bi, pl.ds(bases[d], c), :] for bi in range(batch)]
            gs = _cumulative_log_decay(tris[d], [x[:, w:2 * w] for x in xs])
            for bi in range(batch):
                chains.append((xs[bi], gs[bi], state_ref[2 * bi + d], *masks[d], 0 if d == 1 else c - 1))
                where.append((bi, d))
        ys, states = _wkv_chunks(chains, head_rows, s_mask)
        for (bi, d), y, s_new in zip(where, ys, states):
            y_refs[d][bi, pl.ds(bases[d], c), :] = y
            state_ref[2 * bi + d] = s_new
        return carry

    lax.fori_loop(0, n_chunks, body, 0)


def wkv_chunked(fw, bw, batch, seq, rows=256):
    w = BRANCH_WIDTH
    rows = min(rows, seq)
    nb = seq // rows
    head_of_lane = jnp.arange(w) // HEAD_DIM
    head_rows = (jnp.arange(8)[:, None] == head_of_lane[None, :]).astype(F32)
    s_mask = (head_of_lane[:, None] == head_of_lane[None, :]).astype(F32)
    consts = [*_wkv_constants(False), *_wkv_constants(True), head_rows, s_mask]
    in_f = pl.BlockSpec((batch, rows, WKV_FIELDS * w), lambda c: (0, c, 0))
    in_b = pl.BlockSpec((batch, rows, WKV_FIELDS * w), lambda c: (0, nb - 1 - c, 0))
    out_shape = jax.ShapeDtypeStruct((batch, seq, w), F32)
    yf, yb = pl.pallas_call(
        functools.partial(_wkv_chunked_kernel, batch=batch, rows=rows),
        grid=(nb,),
        in_specs=[in_f, in_b] + [pl.BlockSpec(a.shape, lambda c: (0, 0)) for a in consts],
        out_specs=[pl.BlockSpec((batch, rows, w), lambda c: (0, c, 0)),
                   pl.BlockSpec((batch, rows, w), lambda c: (0, nb - 1 - c, 0))],
        out_shape=[out_shape, out_shape],
        scratch_shapes=[pltpu.VMEM((2 * batch, w, w), F32)],
        compiler_params=_params("arbitrary"),
        name="wkv_chunked",
    )(fw.reshape(batch, seq, -1), bw.reshape(batch, seq, -1), *consts)
    return yf.reshape(batch * seq, w), yb.reshape(batch * seq, w)


S5_T = 64


def _s5_operators(a_re, a_im, log_dt, b_re, b_im, c_re, c_im, n_chunks, reverse):
    g, p, c, t = S5_GROUPS, S5_STATE, S5_GROUP_CH, S5_T
    dt = jnp.exp(log_dt)[:, None]
    mag = jnp.exp(a_re * dt)
    bar_re, bar_im = mag * jnp.cos(a_im * dt), mag * jnp.sin(a_im * dt)
    den = a_re * a_re + a_im * a_im
    f_re = ((bar_re - 1.0) * a_re + bar_im * a_im) / den
    f_im = (bar_im * a_re - (bar_re - 1.0) * a_im) / den
    bb_re = f_re[..., None] * b_re - f_im[..., None] * b_im
    bb_im = f_re[..., None] * b_im + f_im[..., None] * b_re

    def powers(j):
        j = j.astype(F32)[:, None, None]
        m = jnp.exp(j * (a_re * dt)[None])
        return m * jnp.cos(j * (a_im * dt)[None]), m * jnp.sin(j * (a_im * dt)[None])

    pw_re, pw_im = powers(jnp.arange(t + 1))
    z_re = pw_re[..., None] * bb_re[None] - pw_im[..., None] * bb_im[None]
    z_im = pw_re[..., None] * bb_im[None] + pw_im[..., None] * bb_re[None]
    kern = (jnp.einsum('gcp,jgpd->gcdj', c_re, z_re[:t]) - jnp.einsum('gcp,jgpd->gcdj', c_im, z_im[:t]))
    pad = jnp.zeros(kern.shape[:-1] + (t,), F32)
    ramp = (jnp.concatenate([kern, pad], axis=-1) if reverse
            else jnp.concatenate([kern[..., :1], pad, kern[..., :0:-1]], axis=-1))
    ramp = jnp.where((jnp.arange(c) % 2 == 1)[None, None, :, None], jnp.roll(ramp, t, axis=-1), ramp)
    toep = ramp.reshape(g, c * c, 2 * t)
    ti = jnp.arange(t)
    steps_in = (t - ti) if reverse else (ti + 1)
    qr, qi = pw_re[steps_in], pw_im[steps_in]
    out_re = c_re[None] * qr[:, :, None, :] - c_im[None] * qi[:, :, None, :]
    out_im = c_re[None] * qi[:, :, None, :] + c_im[None] * qr[:, :, None, :]
    state_out = jnp.concatenate([out_re, -out_im], axis=-1).transpose(1, 2, 0, 3).reshape(g, c * t, 2 * p)
    steps_left = ti if reverse else (t - 1 - ti)
    in_state = jnp.concatenate([z_re[steps_left], z_im[steps_left]], axis=2)
    in_state = in_state.transpose(1, 2, 3, 0).reshape(g, 2 * p, c * t)
    levels = max(1, (n_chunks - 1).bit_length())
    lr, li = powers(t * (2 ** jnp.arange(levels)))
    chunk_pow = jnp.stack([lr, li], axis=1)[..., None]
    return toep, state_out.astype(BF16), in_state.astype(BF16), chunk_pow


def _s5_chunk_states(fin, pow_ref, n_chunks, reverse):
    p = S5_STATE
    fr, fi = fin[:p], fin[p:]
    lanes = fr.shape[1]
    pos = lax.broadcasted_iota(jnp.int32, fr.shape, 1) % n_chunks
    shift_of = lambda step: (lanes - step) if reverse else step
    reachable = lambda step: (pos < n_chunks - step) if reverse else (pos >= step)
    for level in range(pow_ref.shape[0]):
        step = 2 ** level
        pr, pi = pow_ref[level, 0], pow_ref[level, 1]
        er = pltpu.roll(fr, shift_of(step), 1)
        ei = pltpu.roll(fi, shift_of(step), 1)
        ok = reachable(step)
        fr, fi = (fr + jnp.where(ok, pr * er - pi * ei, 0.0), fi + jnp.where(ok, pr * ei + pi * er, 0.0))
    ok = reachable(1)
    fr = jnp.where(ok, pltpu.roll(fr, shift_of(1), 1), 0.0)
    fi = jnp.where(ok, pltpu.roll(fi, shift_of(1), 1), 0.0)
    return jnp.concatenate([fr, fi], axis=0)


def _s5_toeplitz(ramp_ref, m_ref):
    c, t = S5_GROUP_CH, S5_T
    low = lax.broadcasted_iota(jnp.int32, (t, 2 * t), 1) < t

    def fill(ci, carry):
        for pair in range(c // 2):
            row = ci * c + 2 * pair
            even = jnp.broadcast_to(ramp_ref[0, pl.ds(row, 1), :], (t, 2 * t))
            odd = jnp.broadcast_to(ramp_ref[0, pl.ds(row + 1, 1), :], (t, 2 * t))
            tile = jnp.where(low, pltpu.roll(even, 0, 1, stride=1, stride_axis=0),
                             pltpu.roll(odd, 0, 1, stride=1, stride_axis=0))
            m_ref[pl.ds(pl.multiple_of(ci * t, t), t), pair * 2 * t:(pair + 1) * 2 * t] = tile.astype(BF16)
        return carry

    lax.fori_loop(0, c, fill, 0)


def _s5_conv_kernel(u_ref, rf_ref, sf_ref, ef_ref, pf_ref, rb_ref, sb_ref, eb_ref, pb_ref, y_ref,
                    mf_ref, mb_ref, *, n_chunks):
    u = u_ref[...]
    y = None
    for r_ref, m_ref, s_ref, e_ref, p_ref, reverse in ((rf_ref, mf_ref, sf_ref, ef_ref, pf_ref, False),
                                                       (rb_ref, mb_ref, sb_ref, eb_ref, pb_ref, True)):
        _s5_toeplitz(r_ref, m_ref)
        within = jnp.dot(m_ref[...], u, preferred_element_type=F32)
        fin = jnp.dot(e_ref[0], u, preferred_element_type=F32)
        x_in = _s5_chunk_states(fin, p_ref.at[:, :, 0], n_chunks, reverse)
        part = within + _bdot(s_ref[0], x_in)
        y = part if y is None else y + part
    y_ref[...] = y


def s5_bidirectional_conv(z, batch, seq, ops_fwd, ops_bwd):
    g, c, t, w = S5_GROUPS, S5_GROUP_CH, S5_T, BRANCH_WIDTH
    nc = seq // t
    cols = batch * nc
    u = z[:, OFF_C:OFF_C + w].reshape(cols, t, w).transpose(2, 1, 0).reshape(w * t, cols).astype(BF16)
    ops = (*ops_fwd, *ops_bwd)
    spec = lambda a: pl.BlockSpec((1,) + a.shape[1:], lambda i: (i,) + (0,) * (a.ndim - 1))
    pow_spec = lambda a: pl.BlockSpec(a.shape[:2] + (1,) + a.shape[3:], lambda i: (0, 0, i, 0, 0))
    y = pl.pallas_call(
        functools.partial(_s5_conv_kernel, n_chunks=nc),
        grid=(g,),
        in_specs=[pl.BlockSpec((c * t, cols), lambda i: (i, 0))]
                 + [pow_spec(a) if a.ndim == 5 else spec(a) for a in ops],
        out_specs=pl.BlockSpec((c * t, cols), lambda i: (i, 0)),
        out_shape=jax.ShapeDtypeStruct((w * t, cols), F32),
        scratch_shapes=[pltpu.VMEM((c * t, c * t), BF16), pltpu.VMEM((c * t, c * t), BF16)],
        compiler_params=_params("parallel"),
        name="s5_conv",
    )(u, *ops)
    return y.reshape(w, t, cols).transpose(2, 1, 0).reshape(batch * seq, w)


def _gelu_tanh(y):
    return 0.5 * y * (1.0 + jnp.tanh(math.sqrt(2.0 / math.pi) * (y + 0.044715 * (y * y * y))))


def _merge_kernel(x_ref, *refs, n_parts):
    ya_refs = refs[:n_parts]
    (yd_ref, wf_ref, wb_ref, bonus_ref, rg_ref, s5_ref, u_ref, zg_ref,
     gb_ref, wbr_ref, wout_ref, lnw_ref, lnb_ref, s5d_ref, gluw_ref, glub_ref, o_ref) = refs[n_parts:]
    w = BRANCH_WIDTH
    heads = []
    for h in range(N_HEADS):
        parts = [ref[0, h] for ref in ya_refs]
        log_den = [p[:, HEAD_DIM:HEAD_DIM + 1] for p in parts]
        top = functools.reduce(jnp.maximum, log_den)
        share = [jnp.exp2(ld - top) for ld in log_den]
        heads.append(sum(s * p[:, :HEAD_DIM] for s, p in zip(share, parts)) / sum(share))
    ya = jnp.concatenate(heads, axis=1)
    ys = wf_ref[...] + wb_ref[...]
    cen = ys - _head_sum(ys, N_HEADS) * (1.0 / HEAD_DIM)
    var = _head_sum(cen * cen, N_HEADS) * (1.0 / HEAD_DIM)
    yb = (cen * lax.rsqrt(var + RWKV_GN_EPS) * lnw_ref[...] + lnb_ref[...] + bonus_ref[...]) * rg_ref[...]
    yc = s5_ref[...] + s5d_ref[...] * u_ref[...]
    h = _bdot(_gelu_tanh(yc), gluw_ref[...]) + glub_ref[...]
    yc = h[:, :w] * _sigmoid(h[:, w:])
    proj_d = sum(lax.dot_general(yd_ref[0, h].astype(BF16), wbr_ref[3, h * HEAD_DIM:(h + 1) * HEAD_DIM, :],
                                 (((0,), (0,)), ((), ())), preferred_element_type=F32) for h in range(N_HEADS))
    merged = jnp.zeros(o_ref.shape, F32)
    for i, proj in enumerate((_bdot(ya, wbr_ref[0]), _bdot(yb, wbr_ref[1]), _bdot(yc, wbr_ref[2]), proj_d)):
        gate = _sigmoid(zg_ref[:, i * D_MODEL:(i + 1) * D_MODEL] + gb_ref[i:i + 1, :])
        merged = merged + gate * proj
    o_ref[...] = x_ref[...] + _bdot(merged, wout_ref[...])


def merge_branches(x, ya_parts, yd, wkv_f, wkv_b, bonus, rgate, s5_y, z, zg, gate_b, w_branch, w_out,
                   ln_w, ln_b, s5_d, glu_w, glu_b, batch, seq, tm=512):
    w = BRANCH_WIDTH
    d = D_MODEL
    tm = min(tm, seq)
    nt = seq // tm
    rows = lambda width, col=0: pl.BlockSpec((tm, width), lambda b, i: (b * nt + i, col))
    heads = pl.BlockSpec((1, N_HEADS, tm, 2 * HEAD_DIM), lambda b, i: (b, 0, i, 0))
    heads_t = pl.BlockSpec((1, N_HEADS, HEAD_DIM, tm), lambda b, i: (b, 0, 0, i))
    full = lambda a: pl.BlockSpec(a.shape, lambda b, i: (0,) * a.ndim)
    params = (gate_b, w_branch, w_out, ln_w.reshape(1, w), ln_b.reshape(1, w), s5_d.reshape(1, w),
              glu_w, glu_b.reshape(1, 2 * w))
    return pl.pallas_call(
        functools.partial(_merge_kernel, n_parts=len(ya_parts)),
        grid=(batch, nt),
        in_specs=[rows(d)] + [heads] * len(ya_parts)
                 + [heads_t, rows(w), rows(w), rows(w), rows(w), rows(w),
                    rows(w, OFF_C // w), rows(N_BRANCHES * d)] + [full(p) for p in params],
        out_specs=rows(d),
        out_shape=jax.ShapeDtypeStruct(x.shape, F32),
        compiler_params=_params("parallel", "parallel"),
        name="merge_branches",
    )(x, *ya_parts, yd, wkv_f, wkv_b, bonus, rgate, s5_y, z, zg, *params)


def _silu(x):
    return x * _sigmoid(x)


def _swiglu_tile(x, wg_ref, wu_ref, wd_ref):
    h = (_silu(jnp.dot(x, wg_ref[...], preferred_element_type=F32))
         * jnp.dot(x, wu_ref[...], preferred_element_type=F32))
    return _bdot(h, wd_ref[...])


def _dense_ffn_kernel(x_ref, g_ref, wg_ref, wu_ref, wd_ref, o_ref, xn_ref):
    @pl.when(pl.program_id(1) == 0)
    def _():
        x = x_ref[...]
        xn_ref[...] = _rms(x, g_ref[...]).astype(BF16)
        o_ref[...] = x

    o_ref[...] += _swiglu_tile(xn_ref[...], wg_ref, wu_ref, wd_ref)


def dense_ffn(x, g, w_gate, w_up, w_down, tm=1024, tf=1408):
    n, d = x.shape
    ff = w_gate.shape[1]
    tm = min(tm, n)
    return pl.pallas_call(
        _dense_ffn_kernel,
        grid=(n // tm, ff // tf),
        in_specs=[pl.BlockSpec((tm, d), lambda i, f: (i, 0)),
                  pl.BlockSpec((1, d), lambda i, f: (0, 0)),
                  pl.BlockSpec((d, tf), lambda i, f: (0, f)),
                  pl.BlockSpec((d, tf), lambda i, f: (0, f)),
                  pl.BlockSpec((tf, d), lambda i, f: (f, 0))],
        out_specs=pl.BlockSpec((tm, d), lambda i, f: (i, 0)),
        out_shape=jax.ShapeDtypeStruct((n, d), F32),
        scratch_shapes=[pltpu.VMEM((tm, d), BF16)],
        compiler_params=_params("parallel", "arbitrary"),
        name="dense_ffn",
    )(x, g.reshape(1, d), w_gate, w_up, w_down)


def _router_kernel(x_ref, g_ref, rt_ref, xnb_ref, sel_ref, wt_ref):
    xn = _rms(x_ref[...], g_ref[...])
    xnb_ref[...] = xn.astype(BF16)
    logits = lax.dot_general(rt_ref[...], xn, (((1,), (1,)), ((), ())),
                             precision=lax.Precision.HIGHEST, preferred_element_type=F32)
    e = lax.broadcasted_iota(jnp.int32, logits.shape, 0)
    m1 = jnp.max(logits, axis=0, keepdims=True)
    i1 = jnp.min(jnp.where(logits == m1, e, N_EXPERTS), axis=0, keepdims=True)
    rest = jnp.where(e == i1, NEG_INF, logits)
    m2 = jnp.max(rest, axis=0, keepdims=True)
    i2 = jnp.min(jnp.where(rest == m2, e, N_EXPERTS), axis=0, keepdims=True)
    ratio = jnp.exp(m2 - m1)
    w1 = 1.0 / (1.0 + ratio)
    w2 = ratio / (1.0 + ratio)
    sel_ref[...] = jnp.where((e == i1) | (e == i2), 1.0, 0.0)
    wt_ref[...] = jnp.where(e == i1, w1, jnp.where(e == i2, w2, 0.0))


def moe_route(x, g, router, tm=1024):
    n, d = x.shape
    tm = min(tm, n)
    ne = router.shape[1]
    return pl.pallas_call(
        _router_kernel,
        grid=(n // tm,),
        in_specs=[pl.BlockSpec((tm, d), lambda i: (i, 0)),
                  pl.BlockSpec((1, d), lambda i: (0, 0)),
                  pl.BlockSpec((ne, d), lambda i: (0, 0))],
        out_specs=[pl.BlockSpec((tm, d), lambda i: (i, 0)),
                   pl.BlockSpec((ne, tm), lambda i: (0, i)),
                   pl.BlockSpec((ne, tm), lambda i: (0, i))],
        out_shape=[jax.ShapeDtypeStruct((n, d), BF16), jax.ShapeDtypeStruct((ne, n), F32),
                   jax.ShapeDtypeStruct((ne, n), F32)],
        compiler_params=_params("parallel"),
        name="moe_router",
    )(x, g.reshape(1, d), router.T)


MOE_ROWS = 32
MOE_STATIC_BLOCKS = (8, 9, 10)


def _moe_kernel(x_ref, xnb_ref, sel_ref, wt_ref, wg_ref, wu_ref, wd_ref, o_ref,
                rank_ref, xg_ref, acc_ref, nblk_ref):
    e = pl.program_id(1)
    f = pl.program_id(2)
    nf = pl.num_programs(2)
    tm = x_ref.shape[0]

    @pl.when((e == 0) & (f == 0))
    def _():
        o_ref[...] = x_ref[...]
        before = (lax.broadcasted_iota(jnp.int32, (tm, tm), 0) < lax.broadcasted_iota(jnp.int32, (tm, tm), 1))
        rank_ref[...] = jnp.dot(sel_ref[...].astype(BF16), jnp.where(before, 1.0, 0.0).astype(BF16),
                                preferred_element_type=F32)

    sel_e = sel_ref[pl.ds(e, 1), :]
    rank_e = rank_ref[pl.ds(e, 1), :]
    wt_e = wt_ref[pl.ds(e, 1), :]

    @pl.when(f == 0)
    def _():
        count = jnp.sum(sel_e).astype(jnp.int32)
        nblk_ref[0] = (count + MOE_ROWS - 1) // MOE_ROWS

    nblk = nblk_ref[0]

    def process(rows):
        n_rows = rows.stop - rows.start if isinstance(rows, slice) else rows.size
        first = rows.start

        def one_hot():
            slot = (first + lax.broadcasted_iota(jnp.int32, (n_rows, tm), 0)).astype(F32)
            return jnp.where((rank_e == slot) & (sel_e > 0.0), 1.0, 0.0)

        @pl.when(f == 0)
        def _():
            xg_ref[rows, :] = jnp.dot(one_hot().astype(BF16), xnb_ref[...],
                                      preferred_element_type=F32).astype(BF16)
            acc_ref[rows, :] = jnp.zeros((n_rows, acc_ref.shape[1]), F32)

        xg = xg_ref[rows, :]
        acc_ref[rows, :] += _swiglu_tile(xg, wg_ref.at[0], wu_ref.at[0], wd_ref.at[0])

        @pl.when(f == nf - 1)
        def _():
            hot = one_hot()
            row_w = jnp.sum(hot * wt_e, axis=1, keepdims=True)
            yw = (acc_ref[rows, :] * row_w).astype(BF16)
            o_ref[...] += lax.dot_general(hot.astype(BF16), yw, (((0,), (0,)), ((), ())),
                                          preferred_element_type=F32)

    for n_static in MOE_STATIC_BLOCKS:
        lo = 0 if n_static == MOE_STATIC_BLOCKS[0] else n_static
        hi = n_static if n_static != MOE_STATIC_BLOCKS[-1] else tm // MOE_ROWS
        pl.when((nblk >= lo) & (nblk <= hi))(functools.partial(process, slice(0, n_static * MOE_ROWS)))

    def tail(b, carry):
        process(pl.ds(pl.multiple_of(b * MOE_ROWS, MOE_ROWS), MOE_ROWS))
        return carry

    lax.fori_loop(MOE_STATIC_BLOCKS[-1], nblk, tail, 0)


def moe_ffn(x, xnb, sel, wt, w_gate, w_up, w_down, tm=1024, tf=1792):
    n, d = x.shape
    ne, _, ff = w_gate.shape
    tm = min(tm, n)
    return pl.pallas_call(
        _moe_kernel,
        grid=(n // tm, ne, ff // tf),
        in_specs=[pl.BlockSpec((tm, d), lambda i, e, f: (i, 0)),
                  pl.BlockSpec((tm, d), lambda i, e, f: (i, 0)),
                  pl.BlockSpec((ne, tm), lambda i, e, f: (0, i)),
                  pl.BlockSpec((ne, tm), lambda i, e, f: (0, i)),
                  pl.BlockSpec((1, d, tf), lambda i, e, f: (e, 0, f)),
                  pl.BlockSpec((1, d, tf), lambda i, e, f: (e, 0, f)),
                  pl.BlockSpec((1, tf, d), lambda i, e, f: (e, f, 0))],
        out_specs=pl.BlockSpec((tm, d), lambda i, e, f: (i, 0)),
        out_shape=jax.ShapeDtypeStruct((n, d), F32),
        scratch_shapes=[pltpu.VMEM((ne, tm), F32), pltpu.VMEM((tm, d), BF16), pltpu.VMEM((tm, d), F32),
                        pltpu.SMEM((1,), jnp.int32)],
        compiler_params=_params("parallel", "arbitrary", "arbitrary"),
        name="moe_ffn",
    )(x, xnb, sel, wt, w_gate, w_up, w_down)


def _rms_kernel(x_ref, g_ref, o_ref):
    o_ref[...] = _rms(x_ref[...], g_ref[...])


def rms_norm(x, g, tm=1024):
    n, d = x.shape
    tm = min(tm, n)
    return pl.pallas_call(
        _rms_kernel,
        grid=(n // tm,),
        in_specs=[pl.BlockSpec((tm, d), lambda i: (i, 0)), pl.BlockSpec((1, d), lambda i: (0, 0))],
        out_specs=pl.BlockSpec((tm, d), lambda i: (i, 0)),
        out_shape=jax.ShapeDtypeStruct((n, d), F32),
        compiler_params=_params("parallel"),
        name="final_rms_norm",
    )(x, g.reshape(1, d))


def _rope_angles(pos, n_freq, theta):
    inv_freq = theta ** (-jnp.arange(n_freq, dtype=F32) / n_freq)
    return pos.astype(F32)[:, None] * inv_freq[None, :]


def kernel(x, norm_mix_g, w_in, gate_b, w_branch, w_out, rwkv_mu_rkv, rwkv_mu_x, rwkv_w0, rwkv_w1, rwkv_w2,
           rwkv_a0, rwkv_a1, rwkv_a2, rwkv_g1, rwkv_g2, rwkv_k_k, rwkv_k_a, rwkv_r_k, rwkv_ln_w, rwkv_ln_b,
           s5_a_re, s5_a_im, s5_log_dt, s5_b_re, s5_b_im, s5_c_re, s5_c_im, s5_d, s5_glu_w, s5_glu_b,
           gqa_q_norm, gqa_k_norm, norm_ffn_g, dense_w_gate, dense_w_up, dense_w_down,
           moe_router, moe_w_gate, moe_w_up, moe_w_down, final_norm_g):
    batch, seq, d = x.shape
    depth = w_in.shape[0]
    n = batch * seq
    t = jnp.arange(seq, dtype=jnp.int32)
    rope_tabs = _rotary_tables(_rope_angles(t, ROPE_DIMS // 2, ROPE_THETA), N_HEADS)
    ang_axial = jnp.concatenate([_rope_angles(t // GRID_W, HEAD_DIM // 4, AXIAL_THETA),
                                 _rope_angles(t % GRID_W, HEAD_DIM // 4, AXIAL_THETA)], axis=-1)
    axial_tabs = _rotary_tables(ang_axial, N_HEADS)
    x = x.reshape(n, d)
    for l in range(depth):
        w_small = w_in[l, :, :OFF_GATES].astype(BF16)
        w_gates = w_in[l, :, OFF_GATES:].astype(BF16)
        xn, xnb, z = rms_in_proj(x, norm_mix_g[l], w_small)
        zg = matmul_bf16(xnb, w_gates, tm=1024, tn=1024, out_dtype=BF16)
        qa, ka, va, qd, kd, vd = qkv_prep(z, batch, seq, rope_tabs, axial_tabs, gqa_q_norm[l], gqa_k_norm[l])
        ya = dilated_attention(qa, ka, va)
        yd = gqa_attention(qd, kd, vd)
        lw1 = jnp.concatenate([rwkv_w1[l], rwkv_a1[l]], axis=-1).astype(BF16)
        zeros = jnp.zeros_like(rwkv_w2[l])
        lw2 = jnp.concatenate([jnp.concatenate([rwkv_w2[l], zeros], axis=-1),
                               jnp.concatenate([zeros, rwkv_a2[l]], axis=-1)], axis=1).astype(BF16)
        w0a0 = jnp.concatenate([rwkv_w0[l], rwkv_a0[l]], axis=-1)
        fw, bw, bonus, rgate = rwkv_prep(xn, z, seq, rwkv_mu_x[l], rwkv_mu_rkv[l].reshape(2, IN_B), lw1, lw2, w0a0,
                                         rwkv_g1[l].astype(BF16), rwkv_g2[l].astype(BF16),
                                         rwkv_k_k[l], rwkv_k_a[l], rwkv_r_k[l])
        wkv_f, wkv_b = wkv_chunked(fw, bw, batch, seq)
        s5_ops = [_s5_operators(s5_a_re[l, dr], s5_a_im[l, dr], s5_log_dt[l, dr], s5_b_re[l], s5_b_im[l],
                                s5_c_re[l, dr], s5_c_im[l, dr], seq // S5_T, reverse=(dr == 1)) for dr in range(2)]
        s5_y = s5_bidirectional_conv(z, batch, seq, *s5_ops)
        x = merge_branches(x, ya, yd, wkv_f, wkv_b, bonus, rgate, s5_y, z, zg, gate_b[l],
                           w_branch[l].astype(BF16), w_out[l].astype(BF16), rwkv_ln_w[l], rwkv_ln_b[l],
                           s5_d[l], s5_glu_w[l].astype(BF16), s5_glu_b[l], batch, seq)
        i = l // 2
        if l % 2 == 0:
            x = dense_ffn(x, norm_ffn_g[l], dense_w_gate[i].astype(BF16), dense_w_up[i].astype(BF16),
                          dense_w_down[i].astype(BF16))
        else:
            xnb_f, sel, wt = moe_route(x, norm_ffn_g[l], moe_router[i])
            x = moe_ffn(x, xnb_f, sel, wt, moe_w_gate[i].astype(BF16), moe_w_up[i].astype(BF16),
                        moe_w_down[i].astype(BF16))
    return rms_norm(x, final_norm_g).reshape(batch, seq, d)
```

```python
import functools
import math

import jax
import jax.numpy as jnp
from jax import lax
from jax.experimental import pallas as pl
from jax.experimental.pallas import tpu as pltpu

F32 = jnp.float32
BF16 = jnp.bfloat16

D_MODEL = 1024
HEAD_DIM = 64
BRANCH_WIDTH = 256
N_BRANCHES = 4
N_HEADS = BRANCH_WIDTH // HEAD_DIM
DILATED_PATTERNS = ((128, 1), (512, 4), (2048, 16))
ROPE_THETA = 500000.0
ROPE_DIMS = HEAD_DIM // 4
RWKV_GN_EPS = 64e-5
S5_GROUP_CH = 16
S5_GROUPS = BRANCH_WIDTH // S5_GROUP_CH
S5_STATE = 64
GQA_KV_HEADS = 2
AXIAL_THETA = 10000.0
GRID_W = 64
N_EXPERTS = 8
TOP_K = 2
NORM_EPS = 1e-6
NEG_INF = -1e30

IN_A = 3 * BRANCH_WIDTH
IN_B = 3 * BRANCH_WIDTH
IN_C = BRANCH_WIDTH
IN_DQ = BRANCH_WIDTH
IN_DKV = GQA_KV_HEADS * HEAD_DIM
IN_GATES = N_BRANCHES * D_MODEL
OFF_B = IN_A
OFF_C = OFF_B + IN_B
OFF_DQ = OFF_C + IN_C
OFF_DKV = OFF_DQ + IN_DQ
OFF_GATES = OFF_DKV + 2 * IN_DKV
IN_TOTAL = OFF_GATES + IN_GATES

VMEM_LIMIT_BYTES = 56 * 1024 * 1024


def _params(*semantics):
    return pltpu.CompilerParams(dimension_semantics=semantics, vmem_limit_bytes=VMEM_LIMIT_BYTES)


def _bdot(a, b):
    return jnp.dot(a.astype(BF16), b.astype(BF16), preferred_element_type=F32)


def _bdot_nt(a, b):
    return lax.dot_general(a.astype(BF16), b.astype(BF16), (((1,), (1,)), ((), ())),
                           preferred_element_type=F32)


def _rms(x, g):
    return x * lax.rsqrt(jnp.mean(x * x, axis=-1, keepdims=True) + NORM_EPS) * g


def _rms_in_proj_kernel(x_ref, g_ref, w_ref, xn_ref, xnb_ref, z_ref):
    @pl.when(pl.program_id(1) == 0)
    def _():
        y = _rms(x_ref[...], g_ref[...])
        xn_ref[...] = y
        xnb_ref[...] = y.astype(BF16)

    z_ref[...] = jnp.dot(xnb_ref[...], w_ref[...], preferred_element_type=F32)


def rms_in_proj(x, g, w_bf16, tm=1024, tn=768):
    n, d = x.shape
    nout = w_bf16.shape[1]
    tm = min(tm, n)
    return pl.pallas_call(
        _rms_in_proj_kernel,
        grid=(n // tm, nout // tn),
        in_specs=[pl.BlockSpec((tm, d), lambda i, j: (i, 0)),
                  pl.BlockSpec((1, d), lambda i, j: (0, 0)),
                  pl.BlockSpec((d, tn), lambda i, j: (0, j))],
        out_specs=[pl.BlockSpec((tm, d), lambda i, j: (i, 0)),
                   pl.BlockSpec((tm, d), lambda i, j: (i, 0)),
                   pl.BlockSpec((tm, tn), lambda i, j: (i, j))],
        out_shape=[jax.ShapeDtypeStruct((n, d), F32), jax.ShapeDtypeStruct((n, d), BF16),
                   jax.ShapeDtypeStruct((n, nout), F32)],
        compiler_params=_params("parallel", "arbitrary"),
        name="rms_in_proj",
    )(x, g.reshape(1, d), w_bf16)


def _rotary_tables(pos_angles, n_heads):
    s, n = pos_angles.shape
    pad = HEAD_DIM - 2 * n
    cos = jnp.concatenate([jnp.cos(pos_angles), jnp.cos(pos_angles), jnp.ones((s, pad), F32)], axis=-1)
    zeros_n = jnp.zeros((s, n), F32)
    zeros_p = jnp.zeros((s, pad), F32)
    sin_lo = jnp.concatenate([-jnp.sin(pos_angles), zeros_n, zeros_p], axis=-1)
    sin_hi = jnp.concatenate([zeros_n, jnp.sin(pos_angles), zeros_p], axis=-1)
    return tuple(jnp.tile(t, (1, n_heads)) for t in (cos, sin_lo, sin_hi))


def _rotate(x, cos, sin_lo, sin_hi, n):
    width = x.shape[-1]
    from_above = pltpu.roll(x, width - n, 1)
    from_below = pltpu.roll(x, n, 1)
    return x * cos + from_above * sin_lo + from_below * sin_hi


def _head_sum(x, n_heads):
    lane = lax.broadcasted_iota(jnp.int32, x.shape, 1)
    out = jnp.zeros_like(x)
    for h in range(n_heads):
        in_head = (lane >= h * HEAD_DIM) & (lane < (h + 1) * HEAD_DIM)
        s = jnp.sum(jnp.where(in_head, x, 0.0), axis=-1, keepdims=True)
        out = jnp.where(in_head, s, out)
    return out


def _head_rms(x, g, n_heads):
    ms = _head_sum(x * x, n_heads) * (1.0 / HEAD_DIM)
    return x * lax.rsqrt(ms + NORM_EPS) * g


Q_SCALE = HEAD_DIM ** -0.5 * math.log2(math.e)


def _qkv_prep_kernel(za_ref, zq_ref, zkv_ref, rc_ref, rl_ref, rh_ref, ac_ref, al_ref, ah_ref,
                     qn_ref, kn_ref, qa_ref, ka_ref, va_ref, qd_ref, kd_ref, vd_ref):
    w = BRANCH_WIDTH
    n_rope = ROPE_DIMS // 2
    n_ax = HEAD_DIM // 2
    za = za_ref[...]
    rc, rl, rh = rc_ref[...], rl_ref[...], rh_ref[...]
    qa = _rotate(za[:, :w], rc, rl, rh, n_rope) * Q_SCALE
    ka = _rotate(za[:, w:2 * w], rc, rl, rh, n_rope)
    va = za[:, 2 * w:]
    ac, al, ah = ac_ref[...], al_ref[...], ah_ref[...]
    qd = _rotate(_head_rms(zq_ref[...], qn_ref[...], N_HEADS), ac, al, ah, n_ax) * Q_SCALE
    zkv = zkv_ref[...]
    kw = GQA_KV_HEADS * HEAD_DIM
    kd = _rotate(_head_rms(zkv[:, :kw], kn_ref[...], GQA_KV_HEADS), ac[:, :kw], al[:, :kw], ah[:, :kw], n_ax)
    vd = zkv[:, kw:]
    for h in range(N_HEADS):
        sl = slice(h * HEAD_DIM, (h + 1) * HEAD_DIM)
        qa_ref[0, h] = qa[:, sl].astype(BF16)
        ka_ref[0, h] = ka[:, sl].astype(BF16)
        va_ref[0, h] = va[:, sl].astype(BF16)
        qd_ref[0, h] = qd[:, sl].astype(BF16)
    for h in range(GQA_KV_HEADS):
        sl = slice(h * HEAD_DIM, (h + 1) * HEAD_DIM)
        kd_ref[0, h] = kd[:, sl].astype(BF16)
        vd_ref[0, h] = vd[:, sl].astype(BF16)


def qkv_prep(z, batch, seq, rope_tabs, axial_tabs, q_norm, k_norm, tm=512):
    tm = min(tm, seq)
    nt = seq // tm
    w = BRANCH_WIDTH
    row = lambda b, i: b * nt + i
    tab_spec = pl.BlockSpec((tm, w), lambda b, i: (i, 0))
    head_out = lambda nh: pl.BlockSpec((1, nh, tm, HEAD_DIM), lambda b, i: (b, 0, i, 0))
    head_shape = lambda nh: jax.ShapeDtypeStruct((batch, nh, seq, HEAD_DIM), BF16)
    return pl.pallas_call(
        _qkv_prep_kernel,
        grid=(batch, nt),
        in_specs=[pl.BlockSpec((tm, IN_A), lambda b, i: (row(b, i), 0)),
                  pl.BlockSpec((tm, w), lambda b, i: (row(b, i), OFF_DQ // w)),
                  pl.BlockSpec((tm, w), lambda b, i: (row(b, i), OFF_DKV // w)),
                  tab_spec, tab_spec, tab_spec, tab_spec, tab_spec, tab_spec,
                  pl.BlockSpec((1, w), lambda b, i: (0, 0)),
                  pl.BlockSpec((1, GQA_KV_HEADS * HEAD_DIM), lambda b, i: (0, 0))],
        out_specs=[head_out(N_HEADS), head_out(N_HEADS), head_out(N_HEADS),
                   head_out(N_HEADS), head_out(GQA_KV_HEADS), head_out(GQA_KV_HEADS)],
        out_shape=[head_shape(N_HEADS), head_shape(N_HEADS), head_shape(N_HEADS),
                   head_shape(N_HEADS), head_shape(GQA_KV_HEADS), head_shape(GQA_KV_HEADS)],
        compiler_params=_params("parallel", "parallel"),
        name="qkv_prep",
    )(z, z, z, *rope_tabs, *axial_tabs,
      jnp.tile(q_norm.reshape(1, HEAD_DIM), (1, N_HEADS)),
      jnp.tile(k_norm.reshape(1, HEAD_DIM), (1, GQA_KV_HEADS)))


A_TQ = 1024
A_SUB = 256
A_RADIUS = 64
DILATED_GROUPS = ((1, ((128, 1), (512, 4))), (16, ((128, 1),)))
assert sorted(w * g for g, ps in DILATED_GROUPS for w, _ in ps) == sorted(w for w, _ in DILATED_PATTERNS)
assert all(w // (2 * d) == A_RADIUS for w, d in DILATED_PATTERNS)


def _window_geometry(patterns, sub):
    out = []
    for window, dil in patterns:
        halo = -(-(window // 2) // 128) * 128
        out.append((dil, -halo, sub + 2 * halo))
    return out


def _window_bias(patterns, sub):
    biases = []
    for dil, first, width in _window_geometry(patterns, sub):
        qi = jnp.arange(sub, dtype=jnp.int32)[:, None]
        kj = jnp.arange(width, dtype=jnp.int32)[None, :] + first
        delta = kj - qi
        ok = (jnp.abs(delta) <= A_RADIUS * dil) & ((delta & (dil - 1)) == 0)
        biases.append(jnp.where(ok, 0.0, NEG_INF).astype(F32))
    return biases


def _window_attn_kernel(q_ref, kp_ref, kc_ref, kn_ref, vp_ref, vc_ref, vn_ref, *rest, seq, tq, sub, windows):
    bias_refs = rest[:len(windows)]
    o_ref, k3_ref, v3_ref = rest[len(windows):]
    i = pl.program_id(2)
    k3_ref[0:tq] = kp_ref[0, 0]
    k3_ref[tq:2 * tq] = kc_ref[0, 0]
    k3_ref[2 * tq:3 * tq] = kn_ref[0, 0]
    v3_ref[0:tq] = vp_ref[0, 0]
    v3_ref[tq:2 * tq] = vc_ref[0, 0]
    v3_ref[2 * tq:3 * tq] = vn_ref[0, 0]
    for u in range(tq // sub):
        q = q_ref[0, 0, u * sub:(u + 1) * sub, :]
        scores = []
        for (dil, first, width), b_ref in zip(windows, bias_refs):
            start = tq + u * sub + first
            s = _bdot_nt(q, k3_ref[start:start + width, :]) + b_ref[...]
            kpos = (i - 1) * tq + start + lax.broadcasted_iota(jnp.int32, (1, width), 1)
            s = jnp.where((kpos >= 0) & (kpos < seq), s, NEG_INF)
            scores.append((s, start, width))
        m = functools.reduce(jnp.maximum, [jnp.max(s, axis=-1, keepdims=True) for s, _, _ in scores])
        l = jnp.zeros_like(m)
        acc = jnp.zeros((sub, HEAD_DIM), F32)
        for s, start, width in scores:
            p = jnp.exp2(s - m)
            l = l + jnp.sum(p, axis=-1, keepdims=True)
            acc = acc + _bdot(p, v3_ref[start:start + width, :])
        log_den = jnp.broadcast_to(m + jnp.log2(l), (sub, HEAD_DIM))
        o_ref[0, 0, u * sub:(u + 1) * sub, :] = jnp.concatenate([acc / l, log_den], axis=1)


def window_attention(q, k, v, patterns):
    batch, nh, seq, hd = q.shape
    tq = min(A_TQ, seq)
    sub = min(A_SUB, tq)
    windows = _window_geometry(patterns, sub)
    assert all(-first <= tq for _, first, _ in windows), "the key halo must fit in one neighbouring tile"
    nt = seq // tq
    cur = pl.BlockSpec((1, 1, tq, hd), lambda b, h, i: (b, h, i, 0))
    prev = pl.BlockSpec((1, 1, tq, hd), lambda b, h, i: (b, h, jnp.maximum(i - 1, 0), 0))
    nxt = pl.BlockSpec((1, 1, tq, hd), lambda b, h, i: (b, h, jnp.minimum(i + 1, nt - 1), 0))
    biases = _window_bias(patterns, sub)
    bias_specs = [pl.BlockSpec(b.shape, lambda b_, h, i: (0, 0)) for b in biases]
    return pl.pallas_call(
        functools.partial(_window_attn_kernel, seq=seq, tq=tq, sub=sub, windows=windows),
        grid=(batch, nh, nt),
        in_specs=[cur, prev, cur, nxt, prev, cur, nxt] + bias_specs,
        out_specs=pl.BlockSpec((1, 1, tq, 2 * hd), lambda b, h, i: (b, h, i, 0)),
        out_shape=jax.ShapeDtypeStruct((batch, nh, seq, 2 * hd), F32),
        scratch_shapes=[pltpu.VMEM((3 * tq, hd), BF16), pltpu.VMEM((3 * tq, hd), BF16)],
        compiler_params=_params("parallel", "parallel", "parallel"),
        name="window_attention_x%d" % len(patterns),
    )(q, k, k, k, v, v, v, *biases)


def _to_residues(a, stride):
    b, h, s, w = a.shape
    return a.reshape(b, h, s // stride, stride, w).transpose(0, 3, 1, 2, 4).reshape(b * stride, h, s // stride, w)


def _from_residues(a, stride):
    bs, h, l, w = a.shape
    return a.reshape(bs // stride, stride, h, l, w).transpose(0, 2, 3, 1, 4).reshape(bs // stride, h, l * stride, w)


def dilated_attention(qa, ka, va):
    parts = []
    for stride, patterns in DILATED_GROUPS:
        if stride == 1:
            parts.append(window_attention(qa, ka, va, patterns))
        else:
            part = window_attention(*(_to_residues(a, stride) for a in (qa, ka, va)), patterns)
            parts.append(_from_residues(part, stride))
    return parts


GQA_SUB = 128
GQA_LOOKAHEAD = 4


def _gqa_kernel(q_ref, k_ref, v_ref, o_ref, *scratch, rep, tq):
    j = pl.program_id(3)
    n_sub = tq // GQA_SUB
    blocks = [(r, u) for r in range(rep) for u in range(n_sub)]
    m_refs, l_refs, acc_refs = (scratch[i * len(blocks):(i + 1) * len(blocks)] for i in range(3))

    @pl.when(j == 0)
    def _():
        for m_ref, l_ref, acc_ref in zip(m_refs, l_refs, acc_refs):
            m_ref[...] = jnp.full(m_ref.shape, NEG_INF, F32)
            l_ref[...] = jnp.zeros(l_ref.shape, F32)
            acc_ref[...] = jnp.zeros(acc_ref.shape, F32)

    k = k_ref[0, 0]
    v = v_ref[0, 0]
    def score(block):
        r, u = block
        return _bdot_nt(k, q_ref[0, r, u * GQA_SUB:(u + 1) * GQA_SUB, :])

    scores = [score(b) for b in blocks[:GQA_LOOKAHEAD]]
    for i, (m_ref, l_ref, acc_ref) in enumerate(zip(m_refs, l_refs, acc_refs)):
        s = scores[i]
        if i + GQA_LOOKAHEAD < len(blocks):
            scores.append(score(blocks[i + GQA_LOOKAHEAD]))
        m_prev = m_ref[...]
        m_new = jnp.maximum(m_prev, jnp.max(s, axis=0, keepdims=True))
        alpha = jnp.exp2(m_prev - m_new)
        p = jnp.exp2(s - m_new)
        l_ref[...] = alpha * l_ref[...] + jnp.sum(p, axis=0, keepdims=True)
        pv = lax.dot_general(v, p.astype(BF16), (((0,), (0,)), ((), ())), preferred_element_type=F32)
        acc_ref[...] = alpha * acc_ref[...] + pv
        m_ref[...] = m_new

    @pl.when(j == pl.num_programs(3) - 1)
    def _():
        for (r, u), l_ref, acc_ref in zip(blocks, l_refs, acc_refs):
            o_ref[0, r, :, u * GQA_SUB:(u + 1) * GQA_SUB] = acc_ref[...] / l_ref[...]


def gqa_attention(qd, kd, vd, tq=1024, tk=4096):
    batch, nh, seq, hd = qd.shape
    ng = kd.shape[1]
    rep = nh // ng
    tq = min(tq, seq)
    tk = min(tk, seq)
    n_blocks = rep * (tq // GQA_SUB)
    return pl.pallas_call(
        functools.partial(_gqa_kernel, rep=rep, tq=tq),
        grid=(batch, ng, seq // tq, seq // tk),
        in_specs=[pl.BlockSpec((1, rep, tq, hd), lambda b, g, i, j: (b, g, i, 0)),
                  pl.BlockSpec((1, 1, tk, hd), lambda b, g, i, j: (b, g, j, 0)),
                  pl.BlockSpec((1, 1, tk, hd), lambda b, g, i, j: (b, g, j, 0))],
        out_specs=pl.BlockSpec((1, rep, hd, tq), lambda b, g, i, j: (b, g, 0, i)),
        out_shape=jax.ShapeDtypeStruct((batch, nh, hd, seq), F32),
        scratch_shapes=([pltpu.VMEM((1, GQA_SUB), F32)] * (2 * n_blocks)
                        + [pltpu.VMEM((hd, GQA_SUB), F32)] * n_blocks),
        compiler_params=_params("parallel", "parallel", "parallel", "arbitrary"),
        name="gqa_attention",
    )(qd, kd, vd)


def _matmul_kernel(a_ref, w_ref, o_ref):
    o_ref[...] = jnp.dot(a_ref[...], w_ref[...], preferred_element_type=F32).astype(o_ref.dtype)


def matmul_bf16(a, w, tm, tn, out_dtype=F32):
    n, k = a.shape
    m = w.shape[1]
    tm = min(tm, n)
    return pl.pallas_call(
        _matmul_kernel,
        grid=(n // tm, m // tn),
        in_specs=[pl.BlockSpec((tm, k), lambda i, j: (i, 0)),
                  pl.BlockSpec((k, tn), lambda i, j: (0, j))],
        out_specs=pl.BlockSpec((tm, tn), lambda i, j: (i, j)),
        out_shape=jax.ShapeDtypeStruct((n, m), out_dtype),
        compiler_params=_params("parallel", "arbitrary"),
        name="matmul_bf16",
    )(a, w)


def _sigmoid(x):
    return 1.0 / (1.0 + jnp.exp(-x))


def _softplus(x):
    return jnp.maximum(x, 0.0) + jnp.log(1.0 + jnp.exp(-jnp.abs(x)))


def _shift_rows(x, edge_row, down):
    rows = x.shape[0]
    ridx = lax.broadcasted_iota(jnp.int32, x.shape, 0)
    if down:
        return jnp.where(ridx == 0, edge_row, pltpu.roll(x, 1, 0))
    return jnp.where(ridx == rows - 1, edge_row, pltpu.roll(x, rows - 1, 0))


WKV_FIELDS = 6


def _rwkv_prep_kernel(xn_ref, xp_ref, xq_ref, zb_ref, zp_ref, zq_ref, mux_ref, murkv_ref, lw1_ref, lw2_ref,
                      w0a0_ref, g1_ref, g2_ref, kk_ref, ka_ref, rk_ref,
                      fw_ref, bw_ref, bonus_ref, gate_ref, *, tiles_per_seq):
    w = BRANCH_WIDTH
    i = pl.program_id(0)
    first = (i % tiles_per_seq) == 0
    last = (i % tiles_per_seq) == tiles_per_seq - 1
    xn = xn_ref[...]
    x_shift = (_shift_rows(xn, jnp.where(first, 0.0, xp_ref[7:8, :]), True),
               _shift_rows(xn, jnp.where(last, 0.0, xq_ref[0:1, :]), False))
    zb = zb_ref[...]
    z_prev = _shift_rows(zb, jnp.where(first, 0.0, zp_ref[7:8, :]), True)
    z_next = _shift_rows(zb, jnp.where(last, 0.0, zq_ref[0:1, :]), False)
    mu = murkv_ref[...]
    rkv = zb + mu[0:1] * (z_prev - zb) + mu[1:2] * (z_next - zb)
    r, k, v = rkv[:, :w], rkv[:, w:2 * w], rkv[:, 2 * w:]
    kap = k * kk_ref[...]
    kap = kap * lax.rsqrt(_head_sum(kap * kap, N_HEADS) + 1e-12)
    gate_ref[...] = _bdot(_sigmoid(_bdot(xn, g1_ref[...])), g2_ref[...])
    bonus = jnp.zeros_like(v)
    lora_lane = lax.broadcasted_iota(jnp.int32, (xn.shape[0], lw1_ref.shape[-1]), 1)
    for d, out_ref in enumerate((fw_ref, bw_ref)):
        xd = xn + mux_ref[d:d + 1, :] * (x_shift[d] - xn)
        h = _bdot(xd, lw1_ref[d])
        h = jnp.where(lora_lane < lw1_ref.shape[-1] // 2, jnp.tanh(h), h)
        h = _bdot(h, lw2_ref[d]) + w0a0_ref[d:d + 1, :]
        w_log = -_softplus(-h[:, :w]) - 0.5
        log_decay = -jnp.exp(w_log)
        iclr = _sigmoid(h[:, w:])
        k_d = k * (1.0 + (iclr - 1.0) * ka_ref[...])
        bonus = bonus + _head_sum(r * k_d * rk_ref[...], N_HEADS) * v
        for j, field in enumerate((r, log_decay, k_d, v, kap, iclr * kap)):
            out_ref[:, j * w:(j + 1) * w] = field
    bonus_ref[...] = bonus


def rwkv_prep(xn, z, seq, mu_x, mu_rkv, lw1, lw2, w0a0, g1, g2, k_k, k_a, r_k, tm=512):
    n, d = xn.shape
    w = BRANCH_WIDTH
    tm = min(tm, seq)
    halo = 8
    prev_halo = lambda i: (jnp.maximum(i * (tm // halo) - 1, 0), 0)
    next_halo = lambda i: (jnp.minimum((i + 1) * (tm // halo), n // halo - 1), 0)
    full = lambda a: pl.BlockSpec(a.shape, lambda i: (0,) * a.ndim)
    params = (mu_x, mu_rkv, lw1, lw2, w0a0, g1, g2, k_k.reshape(1, w), k_a.reshape(1, w), r_k.reshape(1, w))
    rows = lambda width: pl.BlockSpec((tm, width), lambda i: (i, 0))
    return pl.pallas_call(
        functools.partial(_rwkv_prep_kernel, tiles_per_seq=seq // tm),
        grid=(n // tm,),
        in_specs=[rows(d), pl.BlockSpec((halo, d), prev_halo), pl.BlockSpec((halo, d), next_halo),
                  pl.BlockSpec((tm, IN_B), lambda i: (i, OFF_B // IN_B)),
                  pl.BlockSpec((halo, IN_B), lambda i: (prev_halo(i)[0], OFF_B // IN_B)),
                  pl.BlockSpec((halo, IN_B), lambda i: (next_halo(i)[0], OFF_B // IN_B))]
                 + [full(p) for p in params],
        out_specs=[rows(WKV_FIELDS * w), rows(WKV_FIELDS * w), rows(w), rows(w)],
        out_shape=[jax.ShapeDtypeStruct((n, WKV_FIELDS * w), F32), jax.ShapeDtypeStruct((n, WKV_FIELDS * w), F32),
                   jax.ShapeDtypeStruct((n, w), F32), jax.ShapeDtypeStruct((n, w), F32)],
        compiler_params=_params("parallel"),
        name="rwkv_prep",
    )(xn, xn, xn, z, z, z, *params)


WKV_CHUNK = 64


def _wkv_constants(reverse):
    c, nh = WKV_CHUNK, N_HEADS
    t = jnp.arange(c)
    before = (t[None, :] > t[:, None]) if reverse else (t[None, :] < t[:, None])
    incl = before | (t[None, :] == t[:, None])
    per_head = lambda m: jnp.kron(jnp.eye(nh, dtype=F32), m.astype(F32))
    return incl.astype(F32), per_head(before), per_head(incl)


def _cumulative_log_decay(tri, logws):
    w = logws[0].shape[1]
    cat = jnp.concatenate(logws, axis=1)
    hi = cat.astype(BF16)
    lo = (cat - hi.astype(F32)).astype(BF16)
    tri = tri.astype(BF16)
    g = jnp.dot(tri, hi, preferred_element_type=F32) + jnp.dot(tri, lo, preferred_element_type=F32)
    return [g[:, i * w:(i + 1) * w] for i in range(len(logws))]


def _wkv_chunks(chains, head_rows, s_mask):
    c, w, nh = WKV_CHUNK, BRANCH_WIDTH, N_HEADS
    hc = nh * c
    every = range(len(chains))

    def per_head(a):
        return jnp.concatenate([jnp.where(head_rows[h:h + 1, :] > 0.0, a, 0.0) for h in range(nh)], axis=0)

    def wide(m):
        return m[0:c] + m[c:2 * c] + m[2 * c:3 * c] + m[3 * c:4 * c]

    gram_lhs, gram_rhs, gam, vs, k_ts, b_ts = [], [], [], [], [], []
    for x, g, s_bd, _, _, _ in chains:
        r, logw, k, v, kap, b = (x[:, j * w:(j + 1) * w] for j in range(WKV_FIELDS))
        gam.append(jnp.exp(g))
        g_inv = jnp.exp(-g)
        kap_t, r_t = kap * jnp.exp(g - logw), r * gam[-1]
        k_ts.append(k * g_inv)
        b_ts.append(b * g_inv)
        vs.append(v)
        gram_lhs.append(jnp.concatenate([per_head(kap_t), per_head(r_t)], axis=0))
        gram_rhs.append(jnp.concatenate([per_head(k_ts[-1]), per_head(b_ts[-1]), s_bd], axis=0))
    gram = [_bdot_nt(gram_lhs[i], gram_rhs[i]) for i in every]
    kk = [gram[i][:hc, :hc] * chains[i][3] for i in every]
    n = [gram[i][:hc, hc:2 * hc] * chains[i][3] for i in every]
    rk = [gram[i][hc:, :hc] * chains[i][4] for i in every]
    rb = [gram[i][hc:, hc:2 * hc] * chains[i][4] for i in every]
    from_kap = [wide(gram[i][:hc, 2 * hc:]) for i in every]
    from_r = [wide(gram[i][hc:, 2 * hc:]) for i in every]
    eye = jnp.where(lax.broadcasted_iota(jnp.int32, (hc, hc), 0) == lax.broadcasted_iota(jnp.int32, (hc, hc), 1),
                    1.0, 0.0)
    inv = [eye - n[i] for i in every]
    power = [_bdot(n[i], n[i]) for i in every]
    levels = c.bit_length() - 2
    for level in range(levels):
        inv = [inv[i] + _bdot(inv[i], power[i]) for i in every]
        if level + 1 < levels:
            power = [_bdot(power[i], power[i]) for i in every]
    v_heads = [per_head(v) for v in vs]
    kk_v = [_bdot(wide(kk[i]), v_heads[i]) for i in every]
    u = [_bdot(wide(inv[i]), per_head(from_kap[i] + kk_v[i])) for i in every]
    y = [from_r[i] + _bdot(jnp.concatenate([wide(rk[i]), -wide(rb[i])], axis=1),
                           jnp.concatenate([v_heads[i], per_head(u[i])], axis=0)) for i in every]
    update = [lax.dot_general(jnp.concatenate([vs[i], u[i]], axis=0).astype(BF16),
                              jnp.concatenate([k_ts[i], -b_ts[i]], axis=0).astype(BF16),
                              (((0,), (0,)), ((), ())), preferred_element_type=F32) for i in every]
    s_new = [(chains[i][2] + update[i] * s_mask) * gam[i][chains[i][5]:chains[i][5] + 1, :] for i in every]
    return y, s_new


def _wkv_chunked_kernel(fw_ref, bw_ref, trif_ref, msf_ref, mif_ref, trib_ref, msb_ref, mib_ref, hr_ref, sm_ref,
                        yf_ref, yb_ref, state_ref, *, batch, rows):
    @pl.when(pl.program_id(0) == 0)
    def _():
        state_ref[...] = jnp.zeros(state_ref.shape, F32)

    c = WKV_CHUNK
    n_chunks = rows // c
    head_rows = hr_ref[...]
    s_mask = sm_ref[...]
    tris = (trif_ref[...], trib_ref[...])
    masks = ((msf_ref[...], mif_ref[...]), (msb_ref[...], mib_ref[...]))
    refs, y_refs = (fw_ref, bw_ref), (yf_ref, yb_ref)
    w = BRANCH_WIDTH

    def body(i, carry):
        bases = (pl.multiple_of(i * c, c), pl.multiple_of((n_chunks - 1 - i) * c, c))
        chains, where = [], []
        for d in range(2):
            xs = [refs[d][bi, pl.ds(bases[d], c), :] for bi in range(batch)]
            gs = _cumulative_log_decay(tris[d], [x[:, w:2 * w] for x in xs])
            for bi in range(batch):
                chains.append((xs[bi], gs[bi], state_ref[2 * bi + d], *masks[d], 0 if d == 1 else c - 1))
                where.append((bi, d))
        ys, states = _wkv_chunks(chains, head_rows, s_mask)
        for (bi, d), y, s_new in zip(where, ys, states):
            y_refs[d][bi, pl.ds(bases[d], c), :] = y
            state_ref[2 * bi + d] = s_new
        return carry

    lax.fori_loop(0, n_chunks, body, 0)


def wkv_chunked(fw, bw, batch, seq, rows=256):
    w = BRANCH_WIDTH
    rows = min(rows, seq)
    nb = seq // rows
    head_of_lane = jnp.arange(w) // HEAD_DIM
    head_rows = (jnp.arange(8)[:, None] == head_of_lane[None, :]).astype(F32)
    s_mask = (head_of_lane[:, None] == head_of_lane[None, :]).astype(F32)
    consts = [*_wkv_constants(False), *_wkv_constants(True), head_rows, s_mask]
    in_f = pl.BlockSpec((batch, rows, WKV_FIELDS * w), lambda c: (0, c, 0))
    in_b = pl.BlockSpec((batch, rows, WKV_FIELDS * w), lambda c: (0, nb - 1 - c, 0))
    out_shape = jax.ShapeDtypeStruct((batch, seq, w), F32)
    yf, yb = pl.pallas_call(
        functools.partial(_wkv_chunked_kernel, batch=batch, rows=rows),
        grid=(nb,),
        in_specs=[in_f, in_b] + [pl.BlockSpec(a.shape, lambda c: (0, 0)) for a in consts],
        out_specs=[pl.BlockSpec((batch, rows, w), lambda c: (0, c, 0)),
                   pl.BlockSpec((batch, rows, w), lambda c: (0, nb - 1 - c, 0))],
        out_shape=[out_shape, out_shape],
        scratch_shapes=[pltpu.VMEM((2 * batch, w, w), F32)],
        compiler_params=_params("arbitrary"),
        name="wkv_chunked",
    )(fw.reshape(batch, seq, -1), bw.reshape(batch, seq, -1), *consts)
    return yf.reshape(batch * seq, w), yb.reshape(batch * seq, w)


S5_T = 64


def _s5_operators(a_re, a_im, log_dt, b_re, b_im, c_re, c_im, n_chunks, reverse):
    g, p, c, t = S5_GROUPS, S5_STATE, S5_GROUP_CH, S5_T
    dt = jnp.exp(log_dt)[:, None]
    mag = jnp.exp(a_re * dt)
    bar_re, bar_im = mag * jnp.cos(a_im * dt), mag * jnp.sin(a_im * dt)
    den = a_re * a_re + a_im * a_im
    f_re = ((bar_re - 1.0) * a_re + bar_im * a_im) / den
    f_im = (bar_im * a_re - (bar_re - 1.0) * a_im) / den
    bb_re = f_re[..., None] * b_re - f_im[..., None] * b_im
    bb_im = f_re[..., None] * b_im + f_im[..., None] * b_re

    def powers(j):
        j = j.astype(F32)[:, None, None]
        m = jnp.exp(j * (a_re * dt)[None])
        return m * jnp.cos(j * (a_im * dt)[None]), m * jnp.sin(j * (a_im * dt)[None])

    pw_re, pw_im = powers(jnp.arange(t + 1))
    z_re = pw_re[..., None] * bb_re[None] - pw_im[..., None] * bb_im[None]
    z_im = pw_re[..., None] * bb_im[None] + pw_im[..., None] * bb_re[None]
    kern = (jnp.einsum('gcp,jgpd->gcdj', c_re, z_re[:t]) - jnp.einsum('gcp,jgpd->gcdj', c_im, z_im[:t]))
    pad = jnp.zeros(kern.shape[:-1] + (t,), F32)
    ramp = (jnp.concatenate([kern, pad], axis=-1) if reverse
            else jnp.concatenate([kern[..., :1], pad, kern[..., :0:-1]], axis=-1))
    ramp = jnp.where((jnp.arange(c) % 2 == 1)[None, None, :, None], jnp.roll(ramp, t, axis=-1), ramp)
    toep = ramp.reshape(g, c * c, 2 * t)
    ti = jnp.arange(t)
    steps_in = (t - ti) if reverse else (ti + 1)
    qr, qi = pw_re[steps_in], pw_im[steps_in]
    out_re = c_re[None] * qr[:, :, None, :] - c_im[None] * qi[:, :, None, :]
    out_im = c_re[None] * qi[:, :, None, :] + c_im[None] * qr[:, :, None, :]
    state_out = jnp.concatenate([out_re, -out_im], axis=-1).transpose(1, 2, 0, 3).reshape(g, c * t, 2 * p)
    steps_left = ti if reverse else (t - 1 - ti)
    in_state = jnp.concatenate([z_re[steps_left], z_im[steps_left]], axis=2)
    in_state = in_state.transpose(1, 2, 3, 0).reshape(g, 2 * p, c * t)
    levels = max(1, (n_chunks - 1).bit_length())
    lr, li = powers(t * (2 ** jnp.arange(levels)))
    chunk_pow = jnp.stack([lr, li], axis=1)[..., None]
    return toep, state_out.astype(BF16), in_state.astype(BF16), chunk_pow


def _s5_chunk_states(fin, pow_ref, n_chunks, reverse):
    p = S5_STATE
    fr, fi = fin[:p], fin[p:]
    lanes = fr.shape[1]
    pos = lax.broadcasted_iota(jnp.int32, fr.shape, 1) % n_chunks
    shift_of = lambda step: (lanes - step) if reverse else step
    reachable = lambda step: (pos < n_chunks - step) if reverse else (pos >= step)
    for level in range(pow_ref.shape[0]):
        step = 2 ** level
        pr, pi = pow_ref[level, 0], pow_ref[level, 1]
        er = pltpu.roll(fr, shift_of(step), 1)
        ei = pltpu.roll(fi, shift_of(step), 1)
        ok = reachable(step)
        fr, fi = (fr + jnp.where(ok, pr * er - pi * ei, 0.0), fi + jnp.where(ok, pr * ei + pi * er, 0.0))
    ok = reachable(1)
    fr = jnp.where(ok, pltpu.roll(fr, shift_of(1), 1), 0.0)
    fi = jnp.where(ok, pltpu.roll(fi, shift_of(1), 1), 0.0)
    return jnp.concatenate([fr, fi], axis=0)


def _s5_toeplitz(ramp_ref, m_ref):
    c, t = S5_GROUP_CH, S5_T
    low = lax.broadcasted_iota(jnp.int32, (t, 2 * t), 1) < t

    def fill(ci, carry):
        for pair in range(c // 2):
            row = ci * c + 2 * pair
            even = jnp.broadcast_to(ramp_ref[0, pl.ds(row, 1), :], (t, 2 * t))
            odd = jnp.broadcast_to(ramp_ref[0, pl.ds(row + 1, 1), :], (t, 2 * t))
            tile = jnp.where(low, pltpu.roll(even, 0, 1, stride=1, stride_axis=0),
                             pltpu.roll(odd, 0, 1, stride=1, stride_axis=0))
            m_ref[pl.ds(pl.multiple_of(ci * t, t), t), pair * 2 * t:(pair + 1) * 2 * t] = tile.astype(BF16)
        return carry

    lax.fori_loop(0, c, fill, 0)


def _s5_conv_kernel(u_ref, rf_ref, sf_ref, ef_ref, pf_ref, rb_ref, sb_ref, eb_ref, pb_ref, y_ref,
                    mf_ref, mb_ref, *, n_chunks):
    u = u_ref[...]
    y = None
    for r_ref, m_ref, s_ref, e_ref, p_ref, reverse in ((rf_ref, mf_ref, sf_ref, ef_ref, pf_ref, False),
                                                       (rb_ref, mb_ref, sb_ref, eb_ref, pb_ref, True)):
        _s5_toeplitz(r_ref, m_ref)
        within = jnp.dot(m_ref[...], u, preferred_element_type=F32)
        fin = jnp.dot(e_ref[0], u, preferred_element_type=F32)
        x_in = _s5_chunk_states(fin, p_ref.at[:, :, 0], n_chunks, reverse)
        part = within + _bdot(s_ref[0], x_in)
        y = part if y is None else y + part
    y_ref[...] = y


def s5_bidirectional_conv(z, batch, seq, ops_fwd, ops_bwd):
    g, c, t, w = S5_GROUPS, S5_GROUP_CH, S5_T, BRANCH_WIDTH
    nc = seq // t
    cols = batch * nc
    u = z[:, OFF_C:OFF_C + w].reshape(cols, t, w).transpose(2, 1, 0).reshape(w * t, cols).astype(BF16)
    ops = (*ops_fwd, *ops_bwd)
    spec = lambda a: pl.BlockSpec((1,) + a.shape[1:], lambda i: (i,) + (0,) * (a.ndim - 1))
    pow_spec = lambda a: pl.BlockSpec(a.shape[:2] + (1,) + a.shape[3:], lambda i: (0, 0, i, 0, 0))
    y = pl.pallas_call(
        functools.partial(_s5_conv_kernel, n_chunks=nc),
        grid=(g,),
        in_specs=[pl.BlockSpec((c * t, cols), lambda i: (i, 0))]
                 + [pow_spec(a) if a.ndim == 5 else spec(a) for a in ops],
        out_specs=pl.BlockSpec((c * t, cols), lambda i: (i, 0)),
        out_shape=jax.ShapeDtypeStruct((w * t, cols), F32),
        scratch_shapes=[pltpu.VMEM((c * t, c * t), BF16), pltpu.VMEM((c * t, c * t), BF16)],
        compiler_params=_params("parallel"),
        name="s5_conv",
    )(u, *ops)
    return y.reshape(w, t, cols).transpose(2, 1, 0).reshape(batch * seq, w)


def _gelu_tanh(y):
    return 0.5 * y * (1.0 + jnp.tanh(math.sqrt(2.0 / math.pi) * (y + 0.044715 * (y * y * y))))


def _merge_kernel(x_ref, *refs, n_parts):
    ya_refs = refs[:n_parts]
    (yd_ref, wf_ref, wb_ref, bonus_ref, rg_ref, s5_ref, u_ref, zg_ref,
     gb_ref, wbr_ref, wout_ref, lnw_ref, lnb_ref, s5d_ref, gluw_ref, glub_ref, o_ref) = refs[n_parts:]
    w = BRANCH_WIDTH
    heads = []
    for h in range(N_HEADS):
        parts = [ref[0, h] for ref in ya_refs]
        log_den = [p[:, HEAD_DIM:HEAD_DIM + 1] for p in parts]
        top = functools.reduce(jnp.maximum, log_den)
        share = [jnp.exp2(ld - top) for ld in log_den]
        heads.append(sum(s * p[:, :HEAD_DIM] for s, p in zip(share, parts)) / sum(share))
    ya = jnp.concatenate(heads, axis=1)
    ys = wf_ref[...] + wb_ref[...]
    cen = ys - _head_sum(ys, N_HEADS) * (1.0 / HEAD_DIM)
    var = _head_sum(cen * cen, N_HEADS) * (1.0 / HEAD_DIM)
    yb = (cen * lax.rsqrt(var + RWKV_GN_EPS) * lnw_ref[...] + lnb_ref[...] + bonus_ref[...]) * rg_ref[...]
    yc = s5_ref[...] + s5d_ref[...] * u_ref[...]
    h = _bdot(_gelu_tanh(yc), gluw_ref[...]) + glub_ref[...]
    yc = h[:, :w] * _sigmoid(h[:, w:])
    proj_d = sum(lax.dot_general(yd_ref[0, h].astype(BF16), wbr_ref[3, h * HEAD_DIM:(h + 1) * HEAD_DIM, :],
                                 (((0,), (0,)), ((), ())), preferred_element_type=F32) for h in range(N_HEADS))
    merged = jnp.zeros(o_ref.shape, F32)
    for i, proj in enumerate((_bdot(ya, wbr_ref[0]), _bdot(yb, wbr_ref[1]), _bdot(yc, wbr_ref[2]), proj_d)):
        gate = _sigmoid(zg_ref[:, i * D_MODEL:(i + 1) * D_MODEL] + gb_ref[i:i + 1, :])
        merged = merged + gate * proj
    o_ref[...] = x_ref[...] + _bdot(merged, wout_ref[...])


def merge_branches(x, ya_parts, yd, wkv_f, wkv_b, bonus, rgate, s5_y, z, zg, gate_b, w_branch, w_out,
                   ln_w, ln_b, s5_d, glu_w, glu_b, batch, seq, tm=512):
    w = BRANCH_WIDTH
    d = D_MODEL
    tm = min(tm, seq)
    nt = seq // tm
    rows = lambda width, col=0: pl.BlockSpec((tm, width), lambda b, i: (b * nt + i, col))
    heads = pl.BlockSpec((1, N_HEADS, tm, 2 * HEAD_DIM), lambda b, i: (b, 0, i, 0))
    heads_t = pl.BlockSpec((1, N_HEADS, HEAD_DIM, tm), lambda b, i: (b, 0, 0, i))
    full = lambda a: pl.BlockSpec(a.shape, lambda b, i: (0,) * a.ndim)
    params = (gate_b, w_branch, w_out, ln_w.reshape(1, w), ln_b.reshape(1, w), s5_d.reshape(1, w),
              glu_w, glu_b.reshape(1, 2 * w))
    return pl.pallas_call(
        functools.partial(_merge_kernel, n_parts=len(ya_parts)),
        grid=(batch, nt),
        in_specs=[rows(d)] + [heads] * len(ya_parts)
                 + [heads_t, rows(w), rows(w), rows(w), rows(w), rows(w),
                    rows(w, OFF_C // w), rows(N_BRANCHES * d)] + [full(p) for p in params],
        out_specs=rows(d),
        out_shape=jax.ShapeDtypeStruct(x.shape, F32),
        compiler_params=_params("parallel", "parallel"),
        name="merge_branches",
    )(x, *ya_parts, yd, wkv_f, wkv_b, bonus, rgate, s5_y, z, zg, *params)


def _silu(x):
    return x * _sigmoid(x)


def _swiglu_tile(x, wg_ref, wu_ref, wd_ref):
    h = (_silu(jnp.dot(x, wg_ref[...], preferred_element_type=F32))
         * jnp.dot(x, wu_ref[...], preferred_element_type=F32))
    return _bdot(h, wd_ref[...])


def _dense_ffn_kernel(x_ref, g_ref, wg_ref, wu_ref, wd_ref, o_ref, xn_ref):
    @pl.when(pl.program_id(1) == 0)
    def _():
        x = x_ref[...]
        xn_ref[...] = _rms(x, g_ref[...]).astype(BF16)
        o_ref[...] = x

    o_ref[...] += _swiglu_tile(xn_ref[...], wg_ref, wu_ref, wd_ref)


def dense_ffn(x, g, w_gate, w_up, w_down, tm=1024, tf=1408):
    n, d = x.shape
    ff = w_gate.shape[1]
    tm = min(tm, n)
    return pl.pallas_call(
        _dense_ffn_kernel,
        grid=(n // tm, ff // tf),
        in_specs=[pl.BlockSpec((tm, d), lambda i, f: (i, 0)),
                  pl.BlockSpec((1, d), lambda i, f: (0, 0)),
                  pl.BlockSpec((d, tf), lambda i, f: (0, f)),
                  pl.BlockSpec((d, tf), lambda i, f: (0, f)),
                  pl.BlockSpec((tf, d), lambda i, f: (f, 0))],
        out_specs=pl.BlockSpec((tm, d), lambda i, f: (i, 0)),
        out_shape=jax.ShapeDtypeStruct((n, d), F32),
        scratch_shapes=[pltpu.VMEM((tm, d), BF16)],
        compiler_params=_params("parallel", "arbitrary"),
        name="dense_ffn",
    )(x, g.reshape(1, d), w_gate, w_up, w_down)


def _router_kernel(x_ref, g_ref, rt_ref, xnb_ref, sel_ref, wt_ref):
    xn = _rms(x_ref[...], g_ref[...])
    xnb_ref[...] = xn.astype(BF16)
    logits = lax.dot_general(rt_ref[...], xn, (((1,), (1,)), ((), ())),
                             precision=lax.Precision.HIGHEST, preferred_element_type=F32)
    e = lax.broadcasted_iota(jnp.int32, logits.shape, 0)
    m1 = jnp.max(logits, axis=0, keepdims=True)
    i1 = jnp.min(jnp.where(logits == m1, e, N_EXPERTS), axis=0, keepdims=True)
    rest = jnp.where(e == i1, NEG_INF, logits)
    m2 = jnp.max(rest, axis=0, keepdims=True)
    i2 = jnp.min(jnp.where(rest == m2, e, N_EXPERTS), axis=0, keepdims=True)
    ratio = jnp.exp(m2 - m1)
    w1 = 1.0 / (1.0 + ratio)
    w2 = ratio / (1.0 + ratio)
    sel_ref[...] = jnp.where((e == i1) | (e == i2), 1.0, 0.0)
    wt_ref[...] = jnp.where(e == i1, w1, jnp.where(e == i2, w2, 0.0))


def moe_route(x, g, router, tm=1024):
    n, d = x.shape
    tm = min(tm, n)
    ne = router.shape[1]
    return pl.pallas_call(
        _router_kernel,
        grid=(n // tm,),
        in_specs=[pl.BlockSpec((tm, d), lambda i: (i, 0)),
                  pl.BlockSpec((1, d), lambda i: (0, 0)),
                  pl.BlockSpec((ne, d), lambda i: (0, 0))],
        out_specs=[pl.BlockSpec((tm, d), lambda i: (i, 0)),
                   pl.BlockSpec((ne, tm), lambda i: (0, i)),
                   pl.BlockSpec((ne, tm), lambda i: (0, i))],
        out_shape=[jax.ShapeDtypeStruct((n, d), BF16), jax.ShapeDtypeStruct((ne, n), F32),
                   jax.ShapeDtypeStruct((ne, n), F32)],
        compiler_params=_params("parallel"),
        name="moe_router",
    )(x, g.reshape(1, d), router.T)


MOE_ROWS = 32
MOE_STATIC_BLOCKS = (8, 9, 10)


def _moe_kernel(x_ref, xnb_ref, sel_ref, wt_ref, wg_ref, wu_ref, wd_ref, o_ref,
                rank_ref, xg_ref, acc_ref, nblk_ref):
    e = pl.program_id(1)
    f = pl.program_id(2)
    nf = pl.num_programs(2)
    tm = x_ref.shape[0]

    @pl.when((e == 0) & (f == 0))
    def _():
        o_ref[...] = x_ref[...]
        before = (lax.broadcasted_iota(jnp.int32, (tm, tm), 0) < lax.broadcasted_iota(jnp.int32, (tm, tm), 1))
        rank_ref[...] = jnp.dot(sel_ref[...].astype(BF16), jnp.where(before, 1.0, 0.0).astype(BF16),
                                preferred_element_type=F32)

    sel_e = sel_ref[pl.ds(e, 1), :]
    rank_e = rank_ref[pl.ds(e, 1), :]
    wt_e = wt_ref[pl.ds(e, 1), :]

    @pl.when(f == 0)
    def _():
        count = jnp.sum(sel_e).astype(jnp.int32)
        nblk_ref[0] = (count + MOE_ROWS - 1) // MOE_ROWS

    nblk = nblk_ref[0]

    def process(rows):
        n_rows = rows.stop - rows.start if isinstance(rows, slice) else rows.size
        first = rows.start

        def one_hot():
            slot = (first + lax.broadcasted_iota(jnp.int32, (n_rows, tm), 0)).astype(F32)
            return jnp.where((rank_e == slot) & (sel_e > 0.0), 1.0, 0.0)

        @pl.when(f == 0)
        def _():
            xg_ref[rows, :] = jnp.dot(one_hot().astype(BF16), xnb_ref[...],
                                      preferred_element_type=F32).astype(BF16)
            acc_ref[rows, :] = jnp.zeros((n_rows, acc_ref.shape[1]), F32)

        xg = xg_ref[rows, :]
        acc_ref[rows, :] += _swiglu_tile(xg, wg_ref.at[0], wu_ref.at[0], wd_ref.at[0])

        @pl.when(f == nf - 1)
        def _():
            hot = one_hot()
            row_w = jnp.sum(hot * wt_e, axis=1, keepdims=True)
            yw = (acc_ref[rows, :] * row_w).astype(BF16)
            o_ref[...] += lax.dot_general(hot.astype(BF16), yw, (((0,), (0,)), ((), ())),
                                          preferred_element_type=F32)

    for n_static in MOE_STATIC_BLOCKS:
        lo = 0 if n_static == MOE_STATIC_BLOCKS[0] else n_static
        hi = n_static if n_static != MOE_STATIC_BLOCKS[-1] else tm // MOE_ROWS
        pl.when((nblk >= lo) & (nblk <= hi))(functools.partial(process, slice(0, n_static * MOE_ROWS)))

    def tail(b, carry):
        process(pl.ds(pl.multiple_of(b * MOE_ROWS, MOE_ROWS), MOE_ROWS))
        return carry

    lax.fori_loop(MOE_STATIC_BLOCKS[-1], nblk, tail, 0)


def moe_ffn(x, xnb, sel, wt, w_gate, w_up, w_down, tm=1024, tf=1792):
    n, d = x.shape
    ne, _, ff = w_gate.shape
    tm = min(tm, n)
    return pl.pallas_call(
        _moe_kernel,
        grid=(n // tm, ne, ff // tf),
        in_specs=[pl.BlockSpec((tm, d), lambda i, e, f: (i, 0)),
                  pl.BlockSpec((tm, d), lambda i, e, f: (i, 0)),
                  pl.BlockSpec((ne, tm), lambda i, e, f: (0, i)),
                  pl.BlockSpec((ne, tm), lambda i, e, f: (0, i)),
                  pl.BlockSpec((1, d, tf), lambda i, e, f: (e, 0, f)),
                  pl.BlockSpec((1, d, tf), lambda i, e, f: (e, 0, f)),
                  pl.BlockSpec((1, tf, d), lambda i, e, f: (e, f, 0))],
        out_specs=pl.BlockSpec((tm, d), lambda i, e, f: (i, 0)),
        out_shape=jax.ShapeDtypeStruct((n, d), F32),
        scratch_shapes=[pltpu.VMEM((ne, tm), F32), pltpu.VMEM((tm, d), BF16), pltpu.VMEM((tm, d), F32),
                        pltpu.SMEM((1,), jnp.int32)],
        compiler_params=_params("parallel", "arbitrary", "arbitrary"),
        name="moe_ffn",
    )(x, xnb, sel, wt, w_gate, w_up, w_down)


def _rms_kernel(x_ref, g_ref, o_ref):
    o_ref[...] = _rms(x_ref[...], g_ref[...])


def rms_norm(x, g, tm=1024):
    n, d = x.shape
    tm = min(tm, n)
    return pl.pallas_call(
        _rms_kernel,
        grid=(n // tm,),
        in_specs=[pl.BlockSpec((tm, d), lambda i: (i, 0)), pl.BlockSpec((1, d), lambda i: (0, 0))],
        out_specs=pl.BlockSpec((tm, d), lambda i: (i, 0)),
        out_shape=jax.ShapeDtypeStruct((n, d), F32),
        compiler_params=_params("parallel"),
        name="final_rms_norm",
    )(x, g.reshape(1, d))


def _rope_angles(pos, n_freq, theta):
    inv_freq = theta ** (-jnp.arange(n_freq, dtype=F32) / n_freq)
    return pos.astype(F32)[:, None] * inv_freq[None, :]


def kernel(x, norm_mix_g, w_in, gate_b, w_branch, w_out, rwkv_mu_rkv, rwkv_mu_x, rwkv_w0, rwkv_w1, rwkv_w2,
           rwkv_a0, rwkv_a1, rwkv_a2, rwkv_g1, rwkv_g2, rwkv_k_k, rwkv_k_a, rwkv_r_k, rwkv_ln_w, rwkv_ln_b,
           s5_a_re, s5_a_im, s5_log_dt, s5_b_re, s5_b_im, s5_c_re, s5_c_im, s5_d, s5_glu_w, s5_glu_b,
           gqa_q_norm, gqa_k_norm, norm_ffn_g, dense_w_gate, dense_w_up, dense_w_down,
           moe_router, moe_w_gate, moe_w_up, moe_w_down, final_norm_g):
    batch, seq, d = x.shape
    depth = w_in.shape[0]
    n = batch * seq
    t = jnp.arange(seq, dtype=jnp.int32)
    rope_tabs = _rotary_tables(_rope_angles(t, ROPE_DIMS // 2, ROPE_THETA), N_HEADS)
    ang_axial = jnp.concatenate([_rope_angles(t // GRID_W, HEAD_DIM // 4, AXIAL_THETA),
                                 _rope_angles(t % GRID_W, HEAD_DIM // 4, AXIAL_THETA)], axis=-1)
    axial_tabs = _rotary_tables(ang_axial, N_HEADS)
    x = x.reshape(n, d)
    for l in range(depth):
        w_small = w_in[l, :, :OFF_GATES].astype(BF16)
        w_gates = w_in[l, :, OFF_GATES:].astype(BF16)
        xn, xnb, z = rms_in_proj(x, norm_mix_g[l], w_small)
        zg = matmul_bf16(xnb, w_gates, tm=1024, tn=1024, out_dtype=BF16)
        qa, ka, va, qd, kd, vd = qkv_prep(z, batch, seq, rope_tabs, axial_tabs, gqa_q_norm[l], gqa_k_norm[l])
        ya = dilated_attention(qa, ka, va)
        yd = gqa_attention(qd, kd, vd)
        lw1 = jnp.concatenate([rwkv_w1[l], rwkv_a1[l]], axis=-1).astype(BF16)
        zeros = jnp.zeros_like(rwkv_w2[l])
        lw2 = jnp.concatenate([jnp.concatenate([rwkv_w2[l], zeros], axis=-1),
                               jnp.concatenate([zeros, rwkv_a2[l]], axis=-1)], axis=1).astype(BF16)
        w0a0 = jnp.concatenate([rwkv_w0[l], rwkv_a0[l]], axis=-1)
        fw, bw, bonus, rgate = rwkv_prep(xn, z, seq, rwkv_mu_x[l], rwkv_mu_rkv[l].reshape(2, IN_B), lw1, lw2, w0a0,
                                         rwkv_g1[l].astype(BF16), rwkv_g2[l].astype(BF16),
                                         rwkv_k_k[l], rwkv_k_a[l], rwkv_r_k[l])
        wkv_f, wkv_b = wkv_chunked(fw, bw, batch, seq)
        s5_ops = [_s5_operators(s5_a_re[l, dr], s5_a_im[l, dr], s5_log_dt[l, dr], s5_b_re[l], s5_b_im[l],
                                s5_c_re[l, dr], s5_c_im[l, dr], seq // S5_T, reverse=(dr == 1)) for dr in range(2)]
        s5_y = s5_bidirectional_conv(z, batch, seq, *s5_ops)
        x = merge_branches(x, ya, yd, wkv_f, wkv_b, bonus, rgate, s5_y, z, zg, gate_b[l],
                           w_branch[l].astype(BF16), w_out[l].astype(BF16), rwkv_ln_w[l], rwkv_ln_b[l],
                           s5_d[l], s5_glu_w[l].astype(BF16), s5_glu_b[l], batch, seq)
        i = l // 2
        if l % 2 == 0:
            x = dense_ffn(x, norm_ffn_g[l], dense_w_gate[i].astype(BF16), dense_w_up[i].astype(BF16),
                          dense_w_down[i].astype(BF16))
        else:
            xnb_f, sel, wt = moe_route(x, norm_ffn_g[l], moe_router[i])
            x = moe_ffn(x, xnb_f, sel, wt, moe_w_gate[i].astype(BF16), moe_w_up[i].astype(BF16),
                        moe_w_down[i].astype(BF16))
    return rms_norm(x, final_norm_g).reshape(batch, seq, d)
```

```python
import functools
import math

import jax
import jax.numpy as jnp
from jax import lax
from jax.experimental import pallas as pl
from jax.experimental.pallas import tpu as pltpu

F32 = jnp.float32
BF16 = jnp.bfloat16

D_MODEL = 1024
HEAD_DIM = 64
BRANCH_WIDTH = 256
N_BRANCHES = 4
N_HEADS = BRANCH_WIDTH // HEAD_DIM
DILATED_PATTERNS = ((128, 1), (512, 4), (2048, 16))
ROPE_THETA = 500000.0
ROPE_DIMS = HEAD_DIM // 4
RWKV_GN_EPS = 64e-5
S5_GROUP_CH = 16
S5_GROUPS = BRANCH_WIDTH // S5_GROUP_CH
S5_STATE = 64
GQA_KV_HEADS = 2
AXIAL_THETA = 10000.0
GRID_W = 64
N_EXPERTS = 8
TOP_K = 2
NORM_EPS = 1e-6
NEG_INF = -1e30

IN_A = 3 * BRANCH_WIDTH
IN_B = 3 * BRANCH_WIDTH
IN_C = BRANCH_WIDTH
IN_DQ = BRANCH_WIDTH
IN_DKV = GQA_KV_HEADS * HEAD_DIM
IN_GATES = N_BRANCHES * D_MODEL
OFF_B = IN_A
OFF_C = OFF_B + IN_B
OFF_DQ = OFF_C + IN_C
OFF_DKV = OFF_DQ + IN_DQ
OFF_GATES = OFF_DKV + 2 * IN_DKV
IN_TOTAL = OFF_GATES + IN_GATES

VMEM_LIMIT_BYTES = 56 * 1024 * 1024


def _params(*semantics):
    return pltpu.CompilerParams(dimension_semantics=semantics, vmem_limit_bytes=VMEM_LIMIT_BYTES)


def _bdot(a, b):
    return jnp.dot(a.astype(BF16), b.astype(BF16), preferred_element_type=F32)


def _bdot_nt(a, b):
    return lax.dot_general(a.astype(BF16), b.astype(BF16), (((1,), (1,)), ((), ())),
                           preferred_element_type=F32)


def _rms(x, g):
    return x * lax.rsqrt(jnp.mean(x * x, axis=-1, keepdims=True) + NORM_EPS) * g


def _rms_in_proj_kernel(x_ref, g_ref, w_ref, xn_ref, xnb_ref, z_ref):
    @pl.when(pl.program_id(1) == 0)
    def _():
        y = _rms(x_ref[...], g_ref[...])
        xn_ref[...] = y
        xnb_ref[...] = y.astype(BF16)

    z_ref[...] = jnp.dot(xnb_ref[...], w_ref[...], preferred_element_type=F32)


def rms_in_proj(x, g, w_bf16, tm=1024, tn=768):
    n, d = x.shape
    nout = w_bf16.shape[1]
    tm = min(tm, n)
    return pl.pallas_call(
        _rms_in_proj_kernel,
        grid=(n // tm, nout // tn),
        in_specs=[pl.BlockSpec((tm, d), lambda i, j: (i, 0)),
                  pl.BlockSpec((1, d), lambda i, j: (0, 0)),
                  pl.BlockSpec((d, tn), lambda i, j: (0, j))],
        out_specs=[pl.BlockSpec((tm, d), lambda i, j: (i, 0)),
                   pl.BlockSpec((tm, d), lambda i, j: (i, 0)),
                   pl.BlockSpec((tm, tn), lambda i, j: (i, j))],
        out_shape=[jax.ShapeDtypeStruct((n, d), F32), jax.ShapeDtypeStruct((n, d), BF16),
                   jax.ShapeDtypeStruct((n, nout), F32)],
        compiler_params=_params("parallel", "arbitrary"),
        name="rms_in_proj",
    )(x, g.reshape(1, d), w_bf16)


def _rotary_tables(pos_angles, n_heads):
    s, n = pos_angles.shape
    pad = HEAD_DIM - 2 * n
    cos = jnp.concatenate([jnp.cos(pos_angles), jnp.cos(pos_angles), jnp.ones((s, pad), F32)], axis=-1)
    zeros_n = jnp.zeros((s, n), F32)
    zeros_p = jnp.zeros((s, pad), F32)
    sin_lo = jnp.concatenate([-jnp.sin(pos_angles), zeros_n, zeros_p], axis=-1)
    sin_hi = jnp.concatenate([zeros_n, jnp.sin(pos_angles), zeros_p], axis=-1)
    return tuple(jnp.tile(t, (1, n_heads)) for t in (cos, sin_lo, sin_hi))


def _rotate(x, cos, sin_lo, sin_hi, n):
    width = x.shape[-1]
    from_above = pltpu.roll(x, width - n, 1)
    from_below = pltpu.roll(x, n, 1)
    return x * cos + from_above * sin_lo + from_below * sin_hi


def _head_sum(x, n_heads):
    lane = lax.broadcasted_iota(jnp.int32, x.shape, 1)
    out = jnp.zeros_like(x)
    for h in range(n_heads):
        in_head = (lane >= h * HEAD_DIM) & (lane < (h + 1) * HEAD_DIM)
        s = jnp.sum(jnp.where(in_head, x, 0.0), axis=-1, keepdims=True)
        out = jnp.where(in_head, s, out)
    return out


def _head_rms(x, g, n_heads):
    ms = _head_sum(x * x, n_heads) * (1.0 / HEAD_DIM)
    return x * lax.rsqrt(ms + NORM_EPS) * g


Q_SCALE = HEAD_DIM ** -0.5 * math.log2(math.e)


def _qkv_prep_kernel(za_ref, zq_ref, zkv_ref, rc_ref, rl_ref, rh_ref, ac_ref, al_ref, ah_ref,
                     qn_ref, kn_ref, qa_ref, ka_ref, va_ref, qd_ref, kd_ref, vd_ref):
    w = BRANCH_WIDTH
    n_rope = ROPE_DIMS // 2
    n_ax = HEAD_DIM // 2
    za = za_ref[...]
    rc, rl, rh = rc_ref[...], rl_ref[...], rh_ref[...]
    qa = _rotate(za[:, :w], rc, rl, rh, n_rope) * Q_SCALE
    ka = _rotate(za[:, w:2 * w], rc, rl, rh, n_rope)
    va = za[:, 2 * w:]
    ac, al, ah = ac_ref[...], al_ref[...], ah_ref[...]
    qd = _rotate(_head_rms(zq_ref[...], qn_ref[...], N_HEADS), ac, al, ah, n_ax) * Q_SCALE
    zkv = zkv_ref[...]
    kw = GQA_KV_HEADS * HEAD_DIM
    kd = _rotate(_head_rms(zkv[:, :kw], kn_ref[...], GQA_KV_HEADS), ac[:, :kw], al[:, :kw], ah[:, :kw], n_ax)
    vd = zkv[:, kw:]
    for h in range(N_HEADS):
        sl = slice(h * HEAD_DIM, (h + 1) * HEAD_DIM)
        qa_ref[0, h] = qa[:, sl].astype(BF16)
        ka_ref[0, h] = ka[:, sl].astype(BF16)
        va_ref[0, h] = va[:, sl].astype(BF16)
        qd_ref[0, h] = qd[:, sl].astype(BF16)
    for h in range(GQA_KV_HEADS):
        sl = slice(h * HEAD_DIM, (h + 1) * HEAD_DIM)
        kd_ref[0, h] = kd[:, sl].astype(BF16)
        vd_ref[0, h] = vd[:, sl].astype(BF16)


def qkv_prep(z, batch, seq, rope_tabs, axial_tabs, q_norm, k_norm, tm=512):
    tm = min(tm, seq)
    nt = seq // tm
    w = BRANCH_WIDTH
    row = lambda b, i: b * nt + i
    tab_spec = pl.BlockSpec((tm, w), lambda b, i: (i, 0))
    head_out = lambda nh: pl.BlockSpec((1, nh, tm, HEAD_DIM), lambda b, i: (b, 0, i, 0))
    head_shape = lambda nh: jax.ShapeDtypeStruct((batch, nh, seq, HEAD_DIM), BF16)
    return pl.pallas_call(
        _qkv_prep_kernel,
        grid=(batch, nt),
        in_specs=[pl.BlockSpec((tm, IN_A), lambda b, i: (row(b, i), 0)),
                  pl.BlockSpec((tm, w), lambda b, i: (row(b, i), OFF_DQ // w)),
                  pl.BlockSpec((tm, w), lambda b, i: (row(b, i), OFF_DKV // w)),
                  tab_spec, tab_spec, tab_spec, tab_spec, tab_spec, tab_spec,
                  pl.BlockSpec((1, w), lambda b, i: (0, 0)),
                  pl.BlockSpec((1, GQA_KV_HEADS * HEAD_DIM), lambda b, i: (0, 0))],
        out_specs=[head_out(N_HEADS), head_out(N_HEADS), head_out(N_HEADS),
                   head_out(N_HEADS), head_out(GQA_KV_HEADS), head_out(GQA_KV_HEADS)],
        out_shape=[head_shape(N_HEADS), head_shape(N_HEADS), head_shape(N_HEADS),
                   head_shape(N_HEADS), head_shape(GQA_KV_HEADS), head_shape(GQA_KV_HEADS)],
        compiler_params=_params("parallel", "parallel"),
        name="qkv_prep",
    )(z, z, z, *rope_tabs, *axial_tabs,
      jnp.tile(q_norm.reshape(1, HEAD_DIM), (1, N_HEADS)),
      jnp.tile(k_norm.reshape(1, HEAD_DIM), (1, GQA_KV_HEADS)))


A_TQ = 1024
A_SUB = 256
A_RADIUS = 64
DILATED_GROUPS = ((1, ((128, 1), (512, 4))), (16, ((128, 1),)))
assert sorted(w * g for g, ps in DILATED_GROUPS for w, _ in ps) == sorted(w for w, _ in DILATED_PATTERNS)
assert all(w // (2 * d) == A_RADIUS for w, d in DILATED_PATTERNS)


def _window_geometry(patterns, sub):
    out = []
    for window, dil in patterns:
        halo = -(-(window // 2) // 128) * 128
        out.append((dil, -halo, sub + 2 * halo))
    return out


def _window_bias(patterns, sub):
    biases = []
    for dil, first, width in _window_geometry(patterns, sub):
        qi = jnp.arange(sub, dtype=jnp.int32)[:, None]
        kj = jnp.arange(width, dtype=jnp.int32)[None, :] + first
        delta = kj - qi
        ok = (jnp.abs(delta) <= A_RADIUS * dil) & ((delta & (dil - 1)) == 0)
        biases.append(jnp.where(ok, 0.0, NEG_INF).astype(F32))
    return biases


def _window_attn_kernel(q_ref, kp_ref, kc_ref, kn_ref, vp_ref, vc_ref, vn_ref, *rest, seq, tq, sub, windows):
    bias_refs = rest[:len(windows)]
    o_ref, k3_ref, v3_ref = rest[len(windows):]
    i = pl.program_id(2)
    k3_ref[0:tq] = kp_ref[0, 0]
    k3_ref[tq:2 * tq] = kc_ref[0, 0]
    k3_ref[2 * tq:3 * tq] = kn_ref[0, 0]
    v3_ref[0:tq] = vp_ref[0, 0]
    v3_ref[tq:2 * tq] = vc_ref[0, 0]
    v3_ref[2 * tq:3 * tq] = vn_ref[0, 0]
    for u in range(tq // sub):
        q = q_ref[0, 0, u * sub:(u + 1) * sub, :]
        scores = []
        for (dil, first, width), b_ref in zip(windows, bias_refs):
            start = tq + u * sub + first
            s = _bdot_nt(q, k3_ref[start:start + width, :]) + b_ref[...]
            kpos = (i - 1) * tq + start + lax.broadcasted_iota(jnp.int32, (1, width), 1)
            s = jnp.where((kpos >= 0) & (kpos < seq), s, NEG_INF)
            scores.append((s, start, width))
        m = functools.reduce(jnp.maximum, [jnp.max(s, axis=-1, keepdims=True) for s, _, _ in scores])
        l = jnp.zeros_like(m)
        acc = jnp.zeros((sub, HEAD_DIM), F32)
        for s, start, width in scores:
            p = jnp.exp2(s - m)
            l = l + jnp.sum(p, axis=-1, keepdims=True)
            acc = acc + _bdot(p, v3_ref[start:start + width, :])
        log_den = jnp.broadcast_to(m + jnp.log2(l), (sub, HEAD_DIM))
        o_ref[0, 0, u * sub:(u + 1) * sub, :] = jnp.concatenate([acc / l, log_den], axis=1)


def window_attention(q, k, v, patterns):
    batch, nh, seq, hd = q.shape
    tq = min(A_TQ, seq)
    sub = min(A_SUB, tq)
    windows = _window_geometry(patterns, sub)
    assert all(-first <= tq for _, first, _ in windows), "the key halo must fit in one neighbouring tile"
    nt = seq // tq
    cur = pl.BlockSpec((1, 1, tq, hd), lambda b, h, i: (b, h, i, 0))
    prev = pl.BlockSpec((1, 1, tq, hd), lambda b, h, i: (b, h, jnp.maximum(i - 1, 0), 0))
    nxt = pl.BlockSpec((1, 1, tq, hd), lambda b, h, i: (b, h, jnp.minimum(i + 1, nt - 1), 0))
    biases = _window_bias(patterns, sub)
    bias_specs = [pl.BlockSpec(b.shape, lambda b_, h, i: (0, 0)) for b in biases]
    return pl.pallas_call(
        functools.partial(_window_attn_kernel, seq=seq, tq=tq, sub=sub, windows=windows),
        grid=(batch, nh, nt),
        in_specs=[cur, prev, cur, nxt, prev, cur, nxt] + bias_specs,
        out_specs=pl.BlockSpec((1, 1, tq, 2 * hd), lambda b, h, i: (b, h, i, 0)),
        out_shape=jax.ShapeDtypeStruct((batch, nh, seq, 2 * hd), F32),
        scratch_shapes=[pltpu.VMEM((3 * tq, hd), BF16), pltpu.VMEM((3 * tq, hd), BF16)],
        compiler_params=_params("parallel", "parallel", "parallel"),
        name="window_attention_x%d" % len(patterns),
    )(q, k, k, k, v, v, v, *biases)


def _to_residues(a, stride):
    b, h, s, w = a.shape
    return a.reshape(b, h, s // stride, stride, w).transpose(0, 3, 1, 2, 4).reshape(b * stride, h, s // stride, w)


def _from_residues(a, stride):
    bs, h, l, w = a.shape
    return a.reshape(bs // stride, stride, h, l, w).transpose(0, 2, 3, 1, 4).reshape(bs // stride, h, l * stride, w)


def dilated_attention(qa, ka, va):
    parts = []
    for stride, patterns in DILATED_GROUPS:
        if stride == 1:
            parts.append(window_attention(qa, ka, va, patterns))
        else:
            part = window_attention(*(_to_residues(a, stride) for a in (qa, ka, va)), patterns)
            parts.append(_from_residues(part, stride))
    return parts


GQA_SUB = 128
GQA_LOOKAHEAD = 4


def _gqa_kernel(q_ref, k_ref, v_ref, o_ref, *scratch, rep, tq):
    j = pl.program_id(3)
    n_sub = tq // GQA_SUB
    blocks = [(r, u) for r in range(rep) for u in range(n_sub)]
    m_refs, l_refs, acc_refs = (scratch[i * len(blocks):(i + 1) * len(blocks)] for i in range(3))

    @pl.when(j == 0)
    def _():
        for m_ref, l_ref, acc_ref in zip(m_refs, l_refs, acc_refs):
            m_ref[...] = jnp.full(m_ref.shape, NEG_INF, F32)
            l_ref[...] = jnp.zeros(l_ref.shape, F32)
            acc_ref[...] = jnp.zeros(acc_ref.shape, F32)

    k = k_ref[0, 0]
    v = v_ref[0, 0]
    def score(block):
        r, u = block
        return _bdot_nt(k, q_ref[0, r, u * GQA_SUB:(u + 1) * GQA_SUB, :])

    scores = [score(b) for b in blocks[:GQA_LOOKAHEAD]]
    for i, (m_ref, l_ref, acc_ref) in enumerate(zip(m_refs, l_refs, acc_refs)):
        s = scores[i]
        if i + GQA_LOOKAHEAD < len(blocks):
            scores.append(score(blocks[i + GQA_LOOKAHEAD]))
        m_prev = m_ref[...]
        m_new = jnp.maximum(m_prev, jnp.max(s, axis=0, keepdims=True))
        alpha = jnp.exp2(m_prev - m_new)
        p = jnp.exp2(s - m_new)
        l_ref[...] = alpha * l_ref[...] + jnp.sum(p, axis=0, keepdims=True)
        pv = lax.dot_general(v, p.astype(BF16), (((0,), (0,)), ((), ())), preferred_element_type=F32)
        acc_ref[...] = alpha * acc_ref[...] + pv
        m_ref[...] = m_new

    @pl.when(j == pl.num_programs(3) - 1)
    def _():
        for (r, u), l_ref, acc_ref in zip(blocks, l_refs, acc_refs):
            o_ref[0, r, :, u * GQA_SUB:(u + 1) * GQA_SUB] = acc_ref[...] / l_ref[...]


def gqa_attention(qd, kd, vd, tq=1024, tk=8192):
    batch, nh, seq, hd = qd.shape
    ng = kd.shape[1]
    rep = nh // ng
    tq = min(tq, seq)
    tk = min(tk, seq)
    n_blocks = rep * (tq // GQA_SUB)
    return pl.pallas_call(
        functools.partial(_gqa_kernel, rep=rep, tq=tq),
        grid=(batch, ng, seq // tq, seq // tk),
        in_specs=[pl.BlockSpec((1, rep, tq, hd), lambda b, g, i, j: (b, g, i, 0)),
                  pl.BlockSpec((1, 1, tk, hd), lambda b, g, i, j: (b, g, j, 0)),
                  pl.BlockSpec((1, 1, tk, hd), lambda b, g, i, j: (b, g, j, 0))],
        out_specs=pl.BlockSpec((1, rep, hd, tq), lambda b, g, i, j: (b, g, 0, i)),
        out_shape=jax.ShapeDtypeStruct((batch, nh, hd, seq), F32),
        scratch_shapes=([pltpu.VMEM((1, GQA_SUB), F32)] * (2 * n_blocks)
                        + [pltpu.VMEM((hd, GQA_SUB), F32)] * n_blocks),
        compiler_params=_params("parallel", "parallel", "parallel", "arbitrary"),
        name="gqa_attention",
    )(qd, kd, vd)


def _matmul_kernel(a_ref, w_ref, o_ref):
    o_ref[...] = jnp.dot(a_ref[...], w_ref[...], preferred_element_type=F32).astype(o_ref.dtype)


def matmul_bf16(a, w, tm, tn, out_dtype=F32):
    n, k = a.shape
    m = w.shape[1]
    tm = min(tm, n)
    return pl.pallas_call(
        _matmul_kernel,
        grid=(n // tm, m // tn),
        in_specs=[pl.BlockSpec((tm, k), lambda i, j: (i, 0)),
                  pl.BlockSpec((k, tn), lambda i, j: (0, j))],
        out_specs=pl.BlockSpec((tm, tn), lambda i, j: (i, j)),
        out_shape=jax.ShapeDtypeStruct((n, m), out_dtype),
        compiler_params=_params("parallel", "arbitrary"),
        name="matmul_bf16",
    )(a, w)


def _sigmoid(x):
    return 1.0 / (1.0 + jnp.exp(-x))


def _softplus(x):
    return jnp.maximum(x, 0.0) + jnp.log(1.0 + jnp.exp(-jnp.abs(x)))


def _shift_rows(x, edge_row, down):
    rows = x.shape[0]
    ridx = lax.broadcasted_iota(jnp.int32, x.shape, 0)
    if down:
        return jnp.where(ridx == 0, edge_row, pltpu.roll(x, 1, 0))
    return jnp.where(ridx == rows - 1, edge_row, pltpu.roll(x, rows - 1, 0))


WKV_FIELDS = 6


def _rwkv_prep_kernel(xn_ref, xp_ref, xq_ref, zb_ref, zp_ref, zq_ref, mux_ref, murkv_ref, lw1_ref, lw2_ref,
                      w0a0_ref, g1_ref, g2_ref, kk_ref, ka_ref, rk_ref,
                      fw_ref, bw_ref, bonus_ref, gate_ref, *, tiles_per_seq):
    w = BRANCH_WIDTH
    i = pl.program_id(0)
    first = (i % tiles_per_seq) == 0
    last = (i % tiles_per_seq) == tiles_per_seq - 1
    xn = xn_ref[...]
    x_shift = (_shift_rows(xn, jnp.where(first, 0.0, xp_ref[7:8, :]), True),
               _shift_rows(xn, jnp.where(last, 0.0, xq_ref[0:1, :]), False))
    zb = zb_ref[...]
    z_prev = _shift_rows(zb, jnp.where(first, 0.0, zp_ref[7:8, :]), True)
    z_next = _shift_rows(zb, jnp.where(last, 0.0, zq_ref[0:1, :]), False)
    mu = murkv_ref[...]
    rkv = zb + mu[0:1] * (z_prev - zb) + mu[1:2] * (z_next - zb)
    r, k, v = rkv[:, :w], rkv[:, w:2 * w], rkv[:, 2 * w:]
    kap = k * kk_ref[...]
    kap = kap * lax.rsqrt(_head_sum(kap * kap, N_HEADS) + 1e-12)
    gate_ref[...] = _bdot(_sigmoid(_bdot(xn, g1_ref[...])), g2_ref[...])
    bonus = jnp.zeros_like(v)
    lora_lane = lax.broadcasted_iota(jnp.int32, (xn.shape[0], lw1_ref.shape[-1]), 1)
    for d, out_ref in enumerate((fw_ref, bw_ref)):
        xd = xn + mux_ref[d:d + 1, :] * (x_shift[d] - xn)
        h = _bdot(xd, lw1_ref[d])
        h = jnp.where(lora_lane < lw1_ref.shape[-1] // 2, jnp.tanh(h), h)
        h = _bdot(h, lw2_ref[d]) + w0a0_ref[d:d + 1, :]
        w_log = -_softplus(-h[:, :w]) - 0.5
        log_decay = -jnp.exp(w_log)
        iclr = _sigmoid(h[:, w:])
        k_d = k * (1.0 + (iclr - 1.0) * ka_ref[...])
        bonus = bonus + _head_sum(r * k_d * rk_ref[...], N_HEADS) * v
        for j, field in enumerate((r, log_decay, k_d, v, kap, iclr * kap)):
            out_ref[:, j * w:(j + 1) * w] = field
    bonus_ref[...] = bonus


def rwkv_prep(xn, z, seq, mu_x, mu_rkv, lw1, lw2, w0a0, g1, g2, k_k, k_a, r_k, tm=512):
    n, d = xn.shape
    w = BRANCH_WIDTH
    tm = min(tm, seq)
    halo = 8
    prev_halo = lambda i: (jnp.maximum(i * (tm // halo) - 1, 0), 0)
    next_halo = lambda i: (jnp.minimum((i + 1) * (tm // halo), n // halo - 1), 0)
    full = lambda a: pl.BlockSpec(a.shape, lambda i: (0,) * a.ndim)
    params = (mu_x, mu_rkv, lw1, lw2, w0a0, g1, g2, k_k.reshape(1, w), k_a.reshape(1, w), r_k.reshape(1, w))
    rows = lambda width: pl.BlockSpec((tm, width), lambda i: (i, 0))
    return pl.pallas_call(
        functools.partial(_rwkv_prep_kernel, tiles_per_seq=seq // tm),
        grid=(n // tm,),
        in_specs=[rows(d), pl.BlockSpec((halo, d), prev_halo), pl.BlockSpec((halo, d), next_halo),
                  pl.BlockSpec((tm, IN_B), lambda i: (i, OFF_B // IN_B)),
                  pl.BlockSpec((halo, IN_B), lambda i: (prev_halo(i)[0], OFF_B // IN_B)),
                  pl.BlockSpec((halo, IN_B), lambda i: (next_halo(i)[0], OFF_B // IN_B))]
                 + [full(p) for p in params],
        out_specs=[rows(WKV_FIELDS * w), rows(WKV_FIELDS * w), rows(w), rows(w)],
        out_shape=[jax.ShapeDtypeStruct((n, WKV_FIELDS * w), F32), jax.ShapeDtypeStruct((n, WKV_FIELDS * w), F32),
                   jax.ShapeDtypeStruct((n, w), F32), jax.ShapeDtypeStruct((n, w), F32)],
        compiler_params=_params("parallel"),
        name="rwkv_prep",
    )(xn, xn, xn, z, z, z, *params)


WKV_CHUNK = 64


def _wkv_constants(reverse):
    c, nh = WKV_CHUNK, N_HEADS
    t = jnp.arange(c)
    before = (t[None, :] > t[:, None]) if reverse else (t[None, :] < t[:, None])
    incl = before | (t[None, :] == t[:, None])
    per_head = lambda m: jnp.kron(jnp.eye(nh, dtype=F32), m.astype(F32))
    return incl.astype(F32), per_head(before), per_head(incl)


def _cumulative_log_decay(tri, logws):
    w = logws[0].shape[1]
    cat = jnp.concatenate(logws, axis=1)
    hi = cat.astype(BF16)
    lo = (cat - hi.astype(F32)).astype(BF16)
    tri = tri.astype(BF16)
    g = jnp.dot(tri, hi, preferred_element_type=F32) + jnp.dot(tri, lo, preferred_element_type=F32)
    return [g[:, i * w:(i + 1) * w] for i in range(len(logws))]


def _wkv_chunks(chains, head_rows, s_mask):
    c, w, nh = WKV_CHUNK, BRANCH_WIDTH, N_HEADS
    hc = nh * c
    every = range(len(chains))

    def per_head(a):
        return jnp.concatenate([jnp.where(head_rows[h:h + 1, :] > 0.0, a, 0.0) for h in range(nh)], axis=0)

    def wide(m):
        return m[0:c] + m[c:2 * c] + m[2 * c:3 * c] + m[3 * c:4 * c]

    gram_lhs, gram_rhs, gam, vs, k_ts, b_ts = [], [], [], [], [], []
    for x, g, s_bd, _, _, _ in chains:
        r, logw, k, v, kap, b = (x[:, j * w:(j + 1) * w] for j in range(WKV_FIELDS))
        gam.append(jnp.exp(g))
        g_inv = jnp.exp(-g)
        kap_t, r_t = kap * jnp.exp(g - logw), r * gam[-1]
        k_ts.append(k * g_inv)
        b_ts.append(b * g_inv)
        vs.append(v)
        gram_lhs.append(jnp.concatenate([per_head(kap_t), per_head(r_t)], axis=0))
        gram_rhs.append(jnp.concatenate([per_head(k_ts[-1]), per_head(b_ts[-1]), s_bd], axis=0))
    gram = [_bdot_nt(gram_lhs[i], gram_rhs[i]) for i in every]
    kk = [gram[i][:hc, :hc] * chains[i][3] for i in every]
    n = [gram[i][:hc, hc:2 * hc] * chains[i][3] for i in every]
    rk = [gram[i][hc:, :hc] * chains[i][4] for i in every]
    rb = [gram[i][hc:, hc:2 * hc] * chains[i][4] for i in every]
    from_kap = [wide(gram[i][:hc, 2 * hc:]) for i in every]
    from_r = [wide(gram[i][hc:, 2 * hc:]) for i in every]
    eye = jnp.where(lax.broadcasted_iota(jnp.int32, (hc, hc), 0) == lax.broadcasted_iota(jnp.int32, (hc, hc), 1),
                    1.0, 0.0)
    inv = [eye - n[i] for i in every]
    power = [_bdot(n[i], n[i]) for i in every]
    levels = c.bit_length() - 2
    for level in range(levels):
        inv = [inv[i] + _bdot(inv[i], power[i]) for i in every]
        if level + 1 < levels:
            power = [_bdot(power[i], power[i]) for i in every]
    v_heads = [per_head(v) for v in vs]
    kk_v = [_bdot(wide(kk[i]), v_heads[i]) for i in every]
    u = [_bdot(wide(inv[i]), per_head(from_kap[i] + kk_v[i])) for i in every]
    y = [from_r[i] + _bdot(jnp.concatenate([wide(rk[i]), -wide(rb[i])], axis=1),
                           jnp.concatenate([v_heads[i], per_head(u[i])], axis=0)) for i in every]
    update = [lax.dot_general(jnp.concatenate([vs[i], u[i]], axis=0).astype(BF16),
                              jnp.concatenate([k_ts[i], -b_ts[i]], axis=0).astype(BF16),
                              (((0,), (0,)), ((), ())), preferred_element_type=F32) for i in every]
    s_new = [(chains[i][2] + update[i] * s_mask) * gam[i][chains[i][5]:chains[i][5] + 1, :] for i in every]
    return y, s_new


def _wkv_chunked_kernel(fw_ref, bw_ref, trif_ref, msf_ref, mif_ref, trib_ref, msb_ref, mib_ref, hr_ref, sm_ref,
                        yf_ref, yb_ref, state_ref, *, batch, rows):
    @pl.when(pl.program_id(0) == 0)
    def _():
        state_ref[...] = jnp.zeros(state_ref.shape, F32)

    c = WKV_CHUNK
    n_chunks = rows // c
    head_rows = hr_ref[...]
    s_mask = sm_ref[...]
    tris = (trif_ref[...], trib_ref[...])
    masks = ((msf_ref[...], mif_ref[...]), (msb_ref[...], mib_ref[...]))
    refs, y_refs = (fw_ref, bw_ref), (yf_ref, yb_ref)
    w = BRANCH_WIDTH

    def body(i, carry):
        bases = (pl.multiple_of(i * c, c), pl.multiple_of((n_chunks - 1 - i) * c, c))
        chains, where = [], []
        for d in range(2):
            xs = [refs[d][bi, pl.ds(bases[d], c), :] for bi in range(batch)]
            gs = _cumulative_log_decay(tris[d], [x[:, w:2 * w] for x in xs])
            for bi in range(batch):
                chains.append((xs[bi], gs[bi], state_ref[2 * bi + d], *masks[d], 0 if d == 1 else c - 1))
                where.append((bi, d))
        ys, states = _wkv_chunks(chains, head_rows, s_mask)
        for (bi, d), y, s_new in zip(where, ys, states):
            y_refs[d][bi, pl.ds(bases[d], c), :] = y
            state_ref[2 * bi + d] = s_new
        return carry

    lax.fori_loop(0, n_chunks, body, 0)


def wkv_chunked(fw, bw, batch, seq, rows=256):
    w = BRANCH_WIDTH
    rows = min(rows, seq)
    nb = seq // rows
    head_of_lane = jnp.arange(w) // HEAD_DIM
    head_rows = (jnp.arange(8)[:, None] == head_of_lane[None, :]).astype(F32)
    s_mask = (head_of_lane[:, None] == head_of_lane[None, :]).astype(F32)
    consts = [*_wkv_constants(False), *_wkv_constants(True), head_rows, s_mask]
    in_f = pl.BlockSpec((batch, rows, WKV_FIELDS * w), lambda c: (0, c, 0))
    in_b = pl.BlockSpec((batch, rows, WKV_FIELDS * w), lambda c: (0, nb - 1 - c, 0))
    out_shape = jax.ShapeDtypeStruct((batch, seq, w), F32)
    yf, yb = pl.pallas_call(
        functools.partial(_wkv_chunked_kernel, batch=batch, rows=rows),
        grid=(nb,),
        in_specs=[in_f, in_b] + [pl.BlockSpec(a.shape, lambda c: (0, 0)) for a in consts],
        out_specs=[pl.BlockSpec((batch, rows, w), lambda c: (0, c, 0)),
                   pl.BlockSpec((batch, rows, w), lambda c: (0, nb - 1 - c, 0))],
        out_shape=[out_shape, out_shape],
        scratch_shapes=[pltpu.VMEM((2 * batch, w, w), F32)],
        compiler_params=_params("arbitrary"),
        name="wkv_chunked",
    )(fw.reshape(batch, seq, -1), bw.reshape(batch, seq, -1), *consts)
    return yf.reshape(batch * seq, w), yb.reshape(batch * seq, w)


S5_T = 64


def _s5_operators(a_re, a_im, log_dt, b_re, b_im, c_re, c_im, n_chunks, reverse):
    g, p, c, t = S5_GROUPS, S5_STATE, S5_GROUP_CH, S5_T
    dt = jnp.exp(log_dt)[:, None]
    mag = jnp.exp(a_re * dt)
    bar_re, bar_im = mag * jnp.cos(a_im * dt), mag * jnp.sin(a_im * dt)
    den = a_re * a_re + a_im * a_im
    f_re = ((bar_re - 1.0) * a_re + bar_im * a_im) / den
    f_im = (bar_im * a_re - (bar_re - 1.0) * a_im) / den
    bb_re = f_re[..., None] * b_re - f_im[..., None] * b_im
    bb_im = f_re[..., None] * b_im + f_im[..., None] * b_re

    def powers(j):
        j = j.astype(F32)[:, None, None]
        m = jnp.exp(j * (a_re * dt)[None])
        return m * jnp.cos(j * (a_im * dt)[None]), m * jnp.sin(j * (a_im * dt)[None])

    pw_re, pw_im = powers(jnp.arange(t + 1))
    z_re = pw_re[..., None] * bb_re[None] - pw_im[..., None] * bb_im[None]
    z_im = pw_re[..., None] * bb_im[None] + pw_im[..., None] * bb_re[None]
    kern = (jnp.einsum('gcp,jgpd->gcdj', c_re, z_re[:t]) - jnp.einsum('gcp,jgpd->gcdj', c_im, z_im[:t]))
    pad = jnp.zeros(kern.shape[:-1] + (t,), F32)
    ramp = (jnp.concatenate([kern, pad], axis=-1) if reverse
            else jnp.concatenate([kern[..., :1], pad, kern[..., :0:-1]], axis=-1))
    ramp = jnp.where((jnp.arange(c) % 2 == 1)[None, None, :, None], jnp.roll(ramp, t, axis=-1), ramp)
    toep = ramp.reshape(g, c * c, 2 * t)
    ti = jnp.arange(t)
    steps_in = (t - ti) if reverse else (ti + 1)
    qr, qi = pw_re[steps_in], pw_im[steps_in]
    out_re = c_re[None] * qr[:, :, None, :] - c_im[None] * qi[:, :, None, :]
    out_im = c_re[None] * qi[:, :, None, :] + c_im[None] * qr[:, :, None, :]
    state_out = jnp.concatenate([out_re, -out_im], axis=-1).transpose(1, 2, 0, 3).reshape(g, c * t, 2 * p)
    steps_left = ti if reverse else (t - 1 - ti)
    in_state = jnp.concatenate([z_re[steps_left], z_im[steps_left]], axis=2)
    in_state = in_state.transpose(1, 2, 3, 0).reshape(g, 2 * p, c * t)
    levels = max(1, (n_chunks - 1).bit_length())
    lr, li = powers(t * (2 ** jnp.arange(levels)))
    chunk_pow = jnp.stack([lr, li], axis=1)[..., None]
    return toep, state_out.astype(BF16), in_state.astype(BF16), chunk_pow


def _s5_chunk_states(fin, pow_ref, n_chunks, reverse):
    p = S5_STATE
    fr, fi = fin[:p], fin[p:]
    lanes = fr.shape[1]
    pos = lax.broadcasted_iota(jnp.int32, fr.shape, 1) % n_chunks
    shift_of = lambda step: (lanes - step) if reverse else step
    reachable = lambda step: (pos < n_chunks - step) if reverse else (pos >= step)
    for level in range(pow_ref.shape[0]):
        step = 2 ** level
        pr, pi = pow_ref[level, 0], pow_ref[level, 1]
        er = pltpu.roll(fr, shift_of(step), 1)
        ei = pltpu.roll(fi, shift_of(step), 1)
        ok = reachable(step)
        fr, fi = (fr + jnp.where(ok, pr * er - pi * ei, 0.0), fi + jnp.where(ok, pr * ei + pi * er, 0.0))
    ok = reachable(1)
    fr = jnp.where(ok, pltpu.roll(fr, shift_of(1), 1), 0.0)
    fi = jnp.where(ok, pltpu.roll(fi, shift_of(1), 1), 0.0)
    return jnp.concatenate([fr, fi], axis=0)


def _s5_toeplitz(ramp_ref, m_ref):
    c, t = S5_GROUP_CH, S5_T
    low = lax.broadcasted_iota(jnp.int32, (t, 2 * t), 1) < t

    def fill(ci, carry):
        for pair in range(c // 2):
            row = ci * c + 2 * pair
            even = jnp.broadcast_to(ramp_ref[0, pl.ds(row, 1), :], (t, 2 * t))
            odd = jnp.broadcast_to(ramp_ref[0, pl.ds(row + 1, 1), :], (t, 2 * t))
            tile = jnp.where(low, pltpu.roll(even, 0, 1, stride=1, stride_axis=0),
                             pltpu.roll(odd, 0, 1, stride=1, stride_axis=0))
            m_ref[pl.ds(pl.multiple_of(ci * t, t), t), pair * 2 * t:(pair + 1) * 2 * t] = tile.astype(BF16)
        return carry

    lax.fori_loop(0, c, fill, 0)


def _s5_conv_kernel(u_ref, rf_ref, sf_ref, ef_ref, pf_ref, rb_ref, sb_ref, eb_ref, pb_ref, y_ref,
                    mf_ref, mb_ref, *, n_chunks):
    u = u_ref[...]
    y = None
    for r_ref, m_ref, s_ref, e_ref, p_ref, reverse in ((rf_ref, mf_ref, sf_ref, ef_ref, pf_ref, False),
                                                       (rb_ref, mb_ref, sb_ref, eb_ref, pb_ref, True)):
        _s5_toeplitz(r_ref, m_ref)
        within = jnp.dot(m_ref[...], u, preferred_element_type=F32)
        fin = jnp.dot(e_ref[0], u, preferred_element_type=F32)
        x_in = _s5_chunk_states(fin, p_ref.at[:, :, 0], n_chunks, reverse)
        part = within + _bdot(s_ref[0], x_in)
        y = part if y is None else y + part
    y_ref[...] = y


def s5_bidirectional_conv(z, batch, seq, ops_fwd, ops_bwd):
    g, c, t, w = S5_GROUPS, S5_GROUP_CH, S5_T, BRANCH_WIDTH
    nc = seq // t
    cols = batch * nc
    u = z[:, OFF_C:OFF_C + w].reshape(cols, t, w).transpose(2, 1, 0).reshape(w * t, cols).astype(BF16)
    ops = (*ops_fwd, *ops_bwd)
    spec = lambda a: pl.BlockSpec((1,) + a.shape[1:], lambda i: (i,) + (0,) * (a.ndim - 1))
    pow_spec = lambda a: pl.BlockSpec(a.shape[:2] + (1,) + a.shape[3:], lambda i: (0, 0, i, 0, 0))
    y = pl.pallas_call(
        functools.partial(_s5_conv_kernel, n_chunks=nc),
        grid=(g,),
        in_specs=[pl.BlockSpec((c * t, cols), lambda i: (i, 0))]
                 + [pow_spec(a) if a.ndim == 5 else spec(a) for a in ops],
        out_specs=pl.BlockSpec((c * t, cols), lambda i: (i, 0)),
        out_shape=jax.ShapeDtypeStruct((w * t, cols), F32),
        scratch_shapes=[pltpu.VMEM((c * t, c * t), BF16), pltpu.VMEM((c * t, c * t), BF16)],
        compiler_params=_params("parallel"),
        name="s5_conv",
    )(u, *ops)
    return y.reshape(w, t, cols).transpose(2, 1, 0).reshape(batch * seq, w)


def _gelu_tanh(y):
    return 0.5 * y * (1.0 + jnp.tanh(math.sqrt(2.0 / math.pi) * (y + 0.044715 * (y * y * y))))


def _merge_kernel(x_ref, *refs, n_parts):
    ya_refs = refs[:n_parts]
    (yd_ref, wf_ref, wb_ref, bonus_ref, rg_ref, s5_ref, u_ref, zg_ref,
     gb_ref, wbr_ref, wout_ref, lnw_ref, lnb_ref, s5d_ref, gluw_ref, glub_ref, o_ref) = refs[n_parts:]
    w = BRANCH_WIDTH
    heads = []
    for h in range(N_HEADS):
        parts = [ref[0, h] for ref in ya_refs]
        log_den = [p[:, HEAD_DIM:HEAD_DIM + 1] for p in parts]
        top = functools.reduce(jnp.maximum, log_den)
        share = [jnp.exp2(ld - top) for ld in log_den]
        heads.append(sum(s * p[:, :HEAD_DIM] for s, p in zip(share, parts)) / sum(share))
    ya = jnp.concatenate(heads, axis=1)
    ys = wf_ref[...] + wb_ref[...]
    cen = ys - _head_sum(ys, N_HEADS) * (1.0 / HEAD_DIM)
    var = _head_sum(cen * cen, N_HEADS) * (1.0 / HEAD_DIM)
    yb = (cen * lax.rsqrt(var + RWKV_GN_EPS) * lnw_ref[...] + lnb_ref[...] + bonus_ref[...]) * rg_ref[...]
    yc = s5_ref[...] + s5d_ref[...] * u_ref[...]
    h = _bdot(_gelu_tanh(yc), gluw_ref[...]) + glub_ref[...]
    yc = h[:, :w] * _sigmoid(h[:, w:])
    proj_d = sum(lax.dot_general(yd_ref[0, h].astype(BF16), wbr_ref[3, h * HEAD_DIM:(h + 1) * HEAD_DIM, :],
                                 (((0,), (0,)), ((), ())), preferred_element_type=F32) for h in range(N_HEADS))
    merged = jnp.zeros(o_ref.shape, F32)
    for i, proj in enumerate((_bdot(ya, wbr_ref[0]), _bdot(yb, wbr_ref[1]), _bdot(yc, wbr_ref[2]), proj_d)):
        gate = _sigmoid(zg_ref[:, i * D_MODEL:(i + 1) * D_MODEL] + gb_ref[i:i + 1, :])
        merged = merged + gate * proj
    o_ref[...] = x_ref[...] + _bdot(merged, wout_ref[...])


def merge_branches(x, ya_parts, yd, wkv_f, wkv_b, bonus, rgate, s5_y, z, zg, gate_b, w_branch, w_out,
                   ln_w, ln_b, s5_d, glu_w, glu_b, batch, seq, tm=512):
    w = BRANCH_WIDTH
    d = D_MODEL
    tm = min(tm, seq)
    nt = seq // tm
    rows = lambda width, col=0: pl.BlockSpec((tm, width), lambda b, i: (b * nt + i, col))
    heads = pl.BlockSpec((1, N_HEADS, tm, 2 * HEAD_DIM), lambda b, i: (b, 0, i, 0))
    heads_t = pl.BlockSpec((1, N_HEADS, HEAD_DIM, tm), lambda b, i: (b, 0, 0, i))
    full = lambda a: pl.BlockSpec(a.shape, lambda b, i: (0,) * a.ndim)
    params = (gate_b, w_branch, w_out, ln_w.reshape(1, w), ln_b.reshape(1, w), s5_d.reshape(1, w),
              glu_w, glu_b.reshape(1, 2 * w))
    return pl.pallas_call(
        functools.partial(_merge_kernel, n_parts=len(ya_parts)),
        grid=(batch, nt),
        in_specs=[rows(d)] + [heads] * len(ya_parts)
                 + [heads_t, rows(w), rows(w), rows(w), rows(w), rows(w),
                    rows(w, OFF_C // w), rows(N_BRANCHES * d)] + [full(p) for p in params],
        out_specs=rows(d),
        out_shape=jax.ShapeDtypeStruct(x.shape, F32),
        compiler_params=_params("parallel", "parallel"),
        name="merge_branches",
    )(x, *ya_parts, yd, wkv_f, wkv_b, bonus, rgate, s5_y, z, zg, *params)


def _silu(x):
    return x * _sigmoid(x)


def _swiglu_tile(x, wg_ref, wu_ref, wd_ref):
    h = (_silu(jnp.dot(x, wg_ref[...], preferred_element_type=F32))
         * jnp.dot(x, wu_ref[...], preferred_element_type=F32))
    return _bdot(h, wd_ref[...])


def _dense_ffn_kernel(x_ref, g_ref, wg_ref, wu_ref, wd_ref, o_ref, xn_ref):
    @pl.when(pl.program_id(1) == 0)
    def _():
        x = x_ref[...]
        xn_ref[...] = _rms(x, g_ref[...]).astype(BF16)
        o_ref[...] = x

    o_ref[...] += _swiglu_tile(xn_ref[...], wg_ref, wu_ref, wd_ref)


def dense_ffn(x, g, w_gate, w_up, w_down, tm=1024, tf=1408):
    n, d = x.shape
    ff = w_gate.shape[1]
    tm = min(tm, n)
    return pl.pallas_call(
        _dense_ffn_kernel,
        grid=(n // tm, ff // tf),
        in_specs=[pl.BlockSpec((tm, d), lambda i, f: (i, 0)),
                  pl.BlockSpec((1, d), lambda i, f: (0, 0)),
                  pl.BlockSpec((d, tf), lambda i, f: (0, f)),
                  pl.BlockSpec((d, tf), lambda i, f: (0, f)),
                  pl.BlockSpec((tf, d), lambda i, f: (f, 0))],
        out_specs=pl.BlockSpec((tm, d), lambda i, f: (i, 0)),
        out_shape=jax.ShapeDtypeStruct((n, d), F32),
        scratch_shapes=[pltpu.VMEM((tm, d), BF16)],
        compiler_params=_params("parallel", "arbitrary"),
        name="dense_ffn",
    )(x, g.reshape(1, d), w_gate, w_up, w_down)


def _router_kernel(x_ref, g_ref, rt_ref, xnb_ref, sel_ref, wt_ref):
    xn = _rms(x_ref[...], g_ref[...])
    xnb_ref[...] = xn.astype(BF16)
    logits = lax.dot_general(rt_ref[...], xn, (((1,), (1,)), ((), ())),
                             precision=lax.Precision.HIGHEST, preferred_element_type=F32)
    e = lax.broadcasted_iota(jnp.int32, logits.shape, 0)
    m1 = jnp.max(logits, axis=0, keepdims=True)
    i1 = jnp.min(jnp.where(logits == m1, e, N_EXPERTS), axis=0, keepdims=True)
    rest = jnp.where(e == i1, NEG_INF, logits)
    m2 = jnp.max(rest, axis=0, keepdims=True)
    i2 = jnp.min(jnp.where(rest == m2, e, N_EXPERTS), axis=0, keepdims=True)
    ratio = jnp.exp(m2 - m1)
    w1 = 1.0 / (1.0 + ratio)
    w2 = ratio / (1.0 + ratio)
    sel_ref[...] = jnp.where((e == i1) | (e == i2), 1.0, 0.0)
    wt_ref[...] = jnp.where(e == i1, w1, jnp.where(e == i2, w2, 0.0))


def moe_route(x, g, router, tm=1024):
    n, d = x.shape
    tm = min(tm, n)
    ne = router.shape[1]
    return pl.pallas_call(
        _router_kernel,
        grid=(n // tm,),
        in_specs=[pl.BlockSpec((tm, d), lambda i: (i, 0)),
                  pl.BlockSpec((1, d), lambda i: (0, 0)),
                  pl.BlockSpec((ne, d), lambda i: (0, 0))],
        out_specs=[pl.BlockSpec((tm, d), lambda i: (i, 0)),
                   pl.BlockSpec((ne, tm), lambda i: (0, i)),
                   pl.BlockSpec((ne, tm), lambda i: (0, i))],
        out_shape=[jax.ShapeDtypeStruct((n, d), BF16), jax.ShapeDtypeStruct((ne, n), F32),
                   jax.ShapeDtypeStruct((ne, n), F32)],
        compiler_params=_params("parallel"),
        name="moe_router",
    )(x, g.reshape(1, d), router.T)


MOE_ROWS = 32
MOE_STATIC_BLOCKS = (8, 9, 10)


def _moe_kernel(x_ref, xnb_ref, sel_ref, wt_ref, wg_ref, wu_ref, wd_ref, o_ref,
                rank_ref, xg_ref, acc_ref, nblk_ref):
    e = pl.program_id(1)
    f = pl.program_id(2)
    nf = pl.num_programs(2)
    tm = x_ref.shape[0]

    @pl.when((e == 0) & (f == 0))
    def _():
        o_ref[...] = x_ref[...]
        before = (lax.broadcasted_iota(jnp.int32, (tm, tm), 0) < lax.broadcasted_iota(jnp.int32, (tm, tm), 1))
        rank_ref[...] = jnp.dot(sel_ref[...].astype(BF16), jnp.where(before, 1.0, 0.0).astype(BF16),
                                preferred_element_type=F32)

    sel_e = sel_ref[pl.ds(e, 1), :]
    rank_e = rank_ref[pl.ds(e, 1), :]
    wt_e = wt_ref[pl.ds(e, 1), :]

    @pl.when(f == 0)
    def _():
        count = jnp.sum(sel_e).astype(jnp.int32)
        nblk_ref[0] = (count + MOE_ROWS - 1) // MOE_ROWS

    nblk = nblk_ref[0]

    def process(rows):
        n_rows = rows.stop - rows.start if isinstance(rows, slice) else rows.size
        first = rows.start

        def one_hot():
            slot = (first + lax.broadcasted_iota(jnp.int32, (n_rows, tm), 0)).astype(F32)
            return jnp.where((rank_e == slot) & (sel_e > 0.0), 1.0, 0.0)

        @pl.when(f == 0)
        def _():
            xg_ref[rows, :] = jnp.dot(one_hot().astype(BF16), xnb_ref[...],
                                      preferred_element_type=F32).astype(BF16)
            acc_ref[rows, :] = jnp.zeros((n_rows, acc_ref.shape[1]), F32)

        xg = xg_ref[rows, :]
        acc_ref[rows, :] += _swiglu_tile(xg, wg_ref.at[0], wu_ref.at[0], wd_ref.at[0])

        @pl.when(f == nf - 1)
        def _():
            hot = one_hot()
            row_w = jnp.sum(hot * wt_e, axis=1, keepdims=True)
            yw = (acc_ref[rows, :] * row_w).astype(BF16)
            o_ref[...] += lax.dot_general(hot.astype(BF16), yw, (((0,), (0,)), ((), ())),
                                          preferred_element_type=F32)

    for n_static in MOE_STATIC_BLOCKS:
        lo = 0 if n_static == MOE_STATIC_BLOCKS[0] else n_static
        hi = n_static if n_static != MOE_STATIC_BLOCKS[-1] else tm // MOE_ROWS
        pl.when((nblk >= lo) & (nblk <= hi))(functools.partial(process, slice(0, n_static * MOE_ROWS)))

    def tail(b, carry):
        process(pl.ds(pl.multiple_of(b * MOE_ROWS, MOE_ROWS), MOE_ROWS))
        return carry

    lax.fori_loop(MOE_STATIC_BLOCKS[-1], nblk, tail, 0)


def moe_ffn(x, xnb, sel, wt, w_gate, w_up, w_down, tm=1024, tf=1792):
    n, d = x.shape
    ne, _, ff = w_gate.shape
    tm = min(tm, n)
    return pl.pallas_call(
        _moe_kernel,
        grid=(n // tm, ne, ff // tf),
        in_specs=[pl.BlockSpec((tm, d), lambda i, e, f: (i, 0)),
                  pl.BlockSpec((tm, d), lambda i, e, f: (i, 0)),
                  pl.BlockSpec((ne, tm), lambda i, e, f: (0, i)),
                  pl.BlockSpec((ne, tm), lambda i, e, f: (0, i)),
                  pl.BlockSpec((1, d, tf), lambda i, e, f: (e, 0, f)),
                  pl.BlockSpec((1, d, tf), lambda i, e, f: (e, 0, f)),
                  pl.BlockSpec((1, tf, d), lambda i, e, f: (e, f, 0))],
        out_specs=pl.BlockSpec((tm, d), lambda i, e, f: (i, 0)),
        out_shape=jax.ShapeDtypeStruct((n, d), F32),
        scratch_shapes=[pltpu.VMEM((ne, tm), F32), pltpu.VMEM((tm, d), BF16), pltpu.VMEM((tm, d), F32),
                        pltpu.SMEM((1,), jnp.int32)],
        compiler_params=_params("parallel", "arbitrary", "arbitrary"),
        name="moe_ffn",
    )(x, xnb, sel, wt, w_gate, w_up, w_down)


def _rms_kernel(x_ref, g_ref, o_ref):
    o_ref[...] = _rms(x_ref[...], g_ref[...])


def rms_norm(x, g, tm=1024):
    n, d = x.shape
    tm = min(tm, n)
    return pl.pallas_call(
        _rms_kernel,
        grid=(n // tm,),
        in_specs=[pl.BlockSpec((tm, d), lambda i: (i, 0)), pl.BlockSpec((1, d), lambda i: (0, 0))],
        out_specs=pl.BlockSpec((tm, d), lambda i: (i, 0)),
        out_shape=jax.ShapeDtypeStruct((n, d), F32),
        compiler_params=_params("parallel"),
        name="final_rms_norm",
    )(x, g.reshape(1, d))


def _rope_angles(pos, n_freq, theta):
    inv_freq = theta ** (-jnp.arange(n_freq, dtype=F32) / n_freq)
    return pos.astype(F32)[:, None] * inv_freq[None, :]


def kernel(x, norm_mix_g, w_in, gate_b, w_branch, w_out, rwkv_mu_rkv, rwkv_mu_x, rwkv_w0, rwkv_w1, rwkv_w2,
           rwkv_a0, rwkv_a1, rwkv_a2, rwkv_g1, rwkv_g2, rwkv_k_k, rwkv_k_a, rwkv_r_k, rwkv_ln_w, rwkv_ln_b,
           s5_a_re, s5_a_im, s5_log_dt, s5_b_re, s5_b_im, s5_c_re, s5_c_im, s5_d, s5_glu_w, s5_glu_b,
           gqa_q_norm, gqa_k_norm, norm_ffn_g, dense_w_gate, dense_w_up, dense_w_down,
           moe_router, moe_w_gate, moe_w_up, moe_w_down, final_norm_g):
    batch, seq, d = x.shape
    depth = w_in.shape[0]
    n = batch * seq
    t = jnp.arange(seq, dtype=jnp.int32)
    rope_tabs = _rotary_tables(_rope_angles(t, ROPE_DIMS // 2, ROPE_THETA), N_HEADS)
    ang_axial = jnp.concatenate([_rope_angles(t // GRID_W, HEAD_DIM // 4, AXIAL_THETA),
                                 _rope_angles(t % GRID_W, HEAD_DIM // 4, AXIAL_THETA)], axis=-1)
    axial_tabs = _rotary_tables(ang_axial, N_HEADS)
    x = x.reshape(n, d)
    for l in range(depth):
        w_small = w_in[l, :, :OFF_GATES].astype(BF16)
        w_gates = w_in[l, :, OFF_GATES:].astype(BF16)
        xn, xnb, z = rms_in_proj(x, norm_mix_g[l], w_small)
        zg = matmul_bf16(xnb, w_gates, tm=1024, tn=1024, out_dtype=BF16)
        qa, ka, va, qd, kd, vd = qkv_prep(z, batch, seq, rope_tabs, axial_tabs, gqa_q_norm[l], gqa_k_norm[l])
        ya = dilated_attention(qa, ka, va)
        yd = gqa_attention(qd, kd, vd)
        lw1 = jnp.concatenate([rwkv_w1[l], rwkv_a1[l]], axis=-1).astype(BF16)
        zeros = jnp.zeros_like(rwkv_w2[l])
        lw2 = jnp.concatenate([jnp.concatenate([rwkv_w2[l], zeros], axis=-1),
                               jnp.concatenate([zeros, rwkv_a2[l]], axis=-1)], axis=1).astype(BF16)
        w0a0 = jnp.concatenate([rwkv_w0[l], rwkv_a0[l]], axis=-1)
        fw, bw, bonus, rgate = rwkv_prep(xn, z, seq, rwkv_mu_x[l], rwkv_mu_rkv[l].reshape(2, IN_B), lw1, lw2, w0a0,
                                         rwkv_g1[l].astype(BF16), rwkv_g2[l].astype(BF16),
                                         rwkv_k_k[l], rwkv_k_a[l], rwkv_r_k[l])
        wkv_f, wkv_b = wkv_chunked(fw, bw, batch, seq)
        s5_ops = [_s5_operators(s5_a_re[l, dr], s5_a_im[l, dr], s5_log_dt[l, dr], s5_b_re[l], s5_b_im[l],
                                s5_c_re[l, dr], s5_c_im[l, dr], seq // S5_T, reverse=(dr == 1)) for dr in range(2)]
        s5_y = s5_bidirectional_conv(z, batch, seq, *s5_ops)
        x = merge_branches(x, ya, yd, wkv_f, wkv_b, bonus, rgate, s5_y, z, zg, gate_b[l],
                           w_branch[l].astype(BF16), w_out[l].astype(BF16), rwkv_ln_w[l], rwkv_ln_b[l],
                           s5_d[l], s5_glu_w[l].astype(BF16), s5_glu_b[l], batch, seq)
        i = l // 2
        if l % 2 == 0:
            x = dense_ffn(x, norm_ffn_g[l], dense_w_gate[i].astype(BF16), dense_w_up[i].astype(BF16),
                          dense_w_down[i].astype(BF16))
        else:
            xnb_f, sel, wt = moe_route(x, norm_ffn_g[l], moe_router[i])
            x = moe_ffn(x, xnb_f, sel, wt, moe_w_gate[i].astype(BF16), moe_w_up[i].astype(BF16),
                        moe_w_down[i].astype(BF16))
    return rms_norm(x, final_norm_g).reshape(batch, seq, d)
```
